```python
import jax, jax.numpy as jnp
from jax import lax
import numpy as np

D_MODEL = 1024
BATCH = 8
SEQ = 4096
DEPTH = 2

D_MIX = D_MODEL
N_ATTN_HEADS = 8
HEAD_DIM = 64
D_ATTN = N_ATTN_HEADS * HEAD_DIM
D_LRU = D_MIX - D_ATTN
N_LRU_BLOCKS = 8
LRU_BLOCK = D_LRU // N_LRU_BLOCKS
CONV_WIDTH = 4
LRU_C = 8.0
DILATED_PATTERNS = ((128, 1), (512, 4), (2048, 16))
N_BUCKETS = 32
MAX_DISTANCE = 2048
D_IN = 3 * D_ATTN + 2 * D_LRU
D_FF_DENSE = 2816
N_EXPERTS = 8
TOP_K = 2
D_FF_EXPERT = 3584
N_DENSE = (DEPTH + 1) // 2
N_MOE = DEPTH // 2
DEEPNORM_ALPHA = (2.0 * DEPTH) ** 0.25
DEEPNORM_BETA = (8.0 * DEPTH) ** -0.25
LN_EPS = 1e-5
RMS_EPS = 1e-6
NEG_INF = -1e30

kernel_name = 'hybrid_rglru_dilated_attn_moe'

f32 = jnp.float32


def _t5_bucket(dist):
    max_exact = N_BUCKETS // 2
    d = np.maximum(dist, 1).astype(np.float32)
    large = max_exact + (np.log(d / max_exact) / np.log(MAX_DISTANCE / max_exact)
                         * (N_BUCKETS - max_exact)).astype(np.int32)
    large = np.minimum(large, N_BUCKETS - 1)
    return np.where(dist < max_exact, dist, large).astype(np.int32)


def _dilated_window_attention(q, k, v, rel_bias, window, dilation):
    B, S, H, Dh = q.shape
    nk = window // dilation
    L = S // dilation
    nb = -(-L // nk)
    Lp = nb * nk

    def by_residue(t):
        t = t.reshape(B, L, dilation, H, Dh).transpose(0, 2, 1, 3, 4)
        return jnp.pad(t, ((0, 0), (0, 0), (0, Lp - L), (0, 0), (0, 0)))

    def kv_windows(t):
        t = jnp.pad(by_residue(t), ((0, 0), (0, 0), (nk, 0), (0, 0), (0, 0)))
        t = t.reshape(B, dilation, nb + 1, nk, H, Dh)
        return jnp.concatenate([t[:, :, :-1], t[:, :, 1:]], axis=3)

    qb = by_residue(q).reshape(B, dilation, nb, nk, H, Dh)
    kw = kv_windows(k)
    vw = kv_windows(v)

    qi = np.arange(nk)[:, None]
    kj = np.arange(2 * nk)[None, :]
    delta = qi + nk - kj
    band = (delta >= 0) & (delta <= nk)
    nonneg = (np.arange(nb)[:, None, None] > 0) | (kj >= nk)[None]
    valid = jnp.asarray(band[None] & nonneg)
    bucket = _t5_bucket(np.clip(delta, 0, nk) * dilation)
    bias = jnp.transpose(rel_bias.astype(f32)[bucket], (2, 0, 1))

    s = jnp.einsum('brnqhc,brnkhc->brnhqk', qb.astype(f32), kw.astype(f32)) * (HEAD_DIM ** -0.5) + bias
    s = jnp.where(valid[:, None], s, NEG_INF)
    lse = jax.nn.logsumexp(s, axis=-1)
    p = jnp.exp(s - lse[..., None])
    o = jnp.einsum('brnhqk,brnkhc->brnqhc', p.astype(v.dtype), vw)

    def back(t):
        t = jnp.swapaxes(t[:, :, :L], 1, 2)
        return t.reshape((B, S) + t.shape[3:])

    o = back(o.reshape(B, dilation, Lp, H, Dh))
    lse = back(jnp.swapaxes(lse, 3, 4).reshape(B, dilation, Lp, H))
    return o, lse


def _mixture_of_dilations(q, k, v, rel_bias):
    outs, lses = [], []
    for window, dilation in DILATED_PATTERNS:
        o, l = _dilated_window_attention(q, k, v, rel_bias, window, dilation)
        outs.append(o.astype(f32))
        lses.append(l)
    w = jax.nn.softmax(jnp.stack(lses), axis=0)
    return jnp.einsum('pbsh,pbshc->bshc', w, jnp.stack(outs))


def _rglru_branch(u, gate, conv_w, conv_b, w_a, b_a, w_x, b_x, lru_lambda):
    B, S, C = u.shape
    u = lax.conv_general_dilated(u, conv_w, window_strides=(1,), padding=[(CONV_WIDTH - 1, 0)],
                                 dimension_numbers=('NWC', 'WIO', 'NWC'),
                                 feature_group_count=C) + conv_b
    ub = u.reshape(B, S, N_LRU_BLOCKS, LRU_BLOCK)
    r = jax.nn.sigmoid(jnp.einsum('bsgi,gij->bsgj', ub, w_a).reshape(B, S, C) + b_a).astype(f32)
    i = jax.nn.sigmoid(jnp.einsum('bsgi,gij->bsgj', ub, w_x).reshape(B, S, C) + b_x).astype(f32)
    log_a = -LRU_C * r * jax.nn.softplus(-lru_lambda.astype(f32))
    a = jnp.exp(log_a)
    beta = jnp.sqrt(-jnp.expm1(2.0 * log_a)) * (i * u.astype(f32))

    def combine(left, right):
        a_l, b_l = left
        a_r, b_r = right
        return a_l * a_r, a_r * b_l + b_r

    _, h = lax.associative_scan(combine, (a, beta), axis=1)
    return jax.nn.gelu(gate.astype(f32)) * h


def _rms_norm(t, g):
    t = t.astype(f32)
    return t * lax.rsqrt(jnp.mean(t * t, axis=-1, keepdims=True) + RMS_EPS) * g.astype(f32)


def _layer_norm(t, g, b):
    tf = t.astype(f32)
    mu = jnp.mean(tf, axis=-1, keepdims=True)
    var = jnp.mean(jnp.square(tf - mu), axis=-1, keepdims=True)
    y = (tf - mu) * lax.rsqrt(var + LN_EPS) * g.astype(f32) + b.astype(f32)
    return y.astype(t.dtype)


def _hybrid_mixer(x, w_in, conv_w, conv_b, w_a, b_a, w_x, b_x, lru_lambda, rel_bias,
                  g_attn, g_lru, w_out):
    B, S, _ = x.shape
    p = jnp.einsum('bsd,de->bse', x, w_in)
    q, k, v, u, gate = jnp.split(p, [D_ATTN, 2 * D_ATTN, 3 * D_ATTN, 3 * D_ATTN + D_LRU], axis=-1)
    heads = lambda t: t.reshape(B, S, N_ATTN_HEADS, HEAD_DIM)
    attn = _mixture_of_dilations(heads(q), heads(k), heads(v), rel_bias).reshape(B, S, D_ATTN)
    lru = _rglru_branch(u, gate, conv_w, conv_b, w_a, b_a, w_x, b_x, lru_lambda)
    y = jnp.concatenate([_rms_norm(attn, g_attn), _rms_norm(lru, g_lru)], axis=-1).astype(x.dtype)
    return jnp.einsum('bsm,md->bsd', y, w_out)


def _swiglu(x, w_gate, w_up, w_down):
    h = jax.nn.silu(jnp.einsum('bsd,df->bsf', x, w_gate)) * jnp.einsum('bsd,df->bsf', x, w_up)
    return jnp.einsum('bsf,fd->bsd', h, w_down)


def _moe_swiglu(x, router_w, w_gate, w_up, w_down):
    logits = jnp.einsum('bsd,de->bse', x, router_w).astype(f32)
    top_val, top_idx = lax.top_k(logits, TOP_K)
    top_p = jax.nn.softmax(top_val, axis=-1)
    comb = jnp.sum(top_p[..., None] * jax.nn.one_hot(top_idx, N_EXPERTS, dtype=f32), axis=-2)
    y = jnp.zeros(x.shape, f32)
    for e in range(N_EXPERTS):
        y = y + comb[..., e:e + 1] * _swiglu(x, w_gate[e], w_up[e], w_down[e]).astype(f32)
    return y.astype(x.dtype)


def setup_inputs(seed: int = 0) -> dict:
    key = jax.random.key(seed)
    ks = jax.random.split(key, 24)
    nrm = lambda k, shape, scale: jax.random.normal(k, shape, f32) * scale
    a0 = jax.random.uniform(ks[9], (DEPTH, D_LRU), f32, minval=0.9, maxval=0.999)
    s0 = a0 ** (1.0 / LRU_C)
    return {
        'x': nrm(ks[0], (BATCH, SEQ, D_MODEL), 1.0),
        'w_in': nrm(ks[1], (DEPTH, D_MODEL, D_IN), D_MODEL ** -0.5),
        'conv_w': nrm(ks[2], (DEPTH, CONV_WIDTH, 1, D_LRU), CONV_WIDTH ** -0.5),
        'conv_b': nrm(ks[3], (DEPTH, D_LRU), 0.01),
        'w_a': nrm(ks[4], (DEPTH, N_LRU_BLOCKS, LRU_BLOCK, LRU_BLOCK), LRU_BLOCK ** -0.5),
        'b_a': nrm(ks[5], (DEPTH, D_LRU), 0.01),
        'w_x': nrm(ks[6], (DEPTH, N_LRU_BLOCKS, LRU_BLOCK, LRU_BLOCK), LRU_BLOCK ** -0.5),
        'b_x': nrm(ks[7], (DEPTH, D_LRU), 0.01),
        'lru_lambda': jnp.log(s0) - jnp.log1p(-s0),
        'rel_bias': nrm(ks[8], (N_BUCKETS, N_ATTN_HEADS), 0.5),
        'g_attn': 1.0 + nrm(ks[10], (DEPTH, D_ATTN), 0.02),
        'g_lru': 1.0 + nrm(ks[11], (DEPTH, D_LRU), 0.02),
        'w_out': nrm(ks[12], (DEPTH, D_MIX, D_MODEL), D_MIX ** -0.5 * DEEPNORM_BETA),
        'ln1_g': 1.0 + nrm(ks[13], (DEPTH, D_MODEL), 0.02),
        'ln1_b': nrm(ks[14], (DEPTH, D_MODEL), 0.01),
        'ln2_g': 1.0 + nrm(ks[15], (DEPTH, D_MODEL), 0.02),
        'ln2_b': nrm(ks[16], (DEPTH, D_MODEL), 0.01),
        'ffn_w_gate': nrm(ks[17], (N_DENSE, D_MODEL, D_FF_DENSE), D_MODEL ** -0.5),
        'ffn_w_up': nrm(ks[18], (N_DENSE, D_MODEL, D_FF_DENSE), D_MODEL ** -0.5),
        'ffn_w_down': nrm(ks[19], (N_DENSE, D_FF_DENSE, D_MODEL), D_FF_DENSE ** -0.5 * DEEPNORM_BETA),
        'router_w': nrm(ks[20], (N_MOE, D_MODEL, N_EXPERTS), D_MODEL ** -0.5),
        'moe_w_gate': nrm(ks[21], (N_MOE, N_EXPERTS, D_MODEL, D_FF_EXPERT), D_MODEL ** -0.5),
        'moe_w_up': nrm(ks[22], (N_MOE, N_EXPERTS, D_MODEL, D_FF_EXPERT), D_MODEL ** -0.5),
        'moe_w_down': nrm(ks[23], (N_MOE, N_EXPERTS, D_FF_EXPERT, D_MODEL), D_FF_EXPERT ** -0.5 * DEEPNORM_BETA),
    }


def reference(x, w_in, conv_w, conv_b, w_a, b_a, w_x, b_x, lru_lambda, rel_bias, g_attn, g_lru,
              w_out, ln1_g, ln1_b, ln2_g, ln2_b, ffn_w_gate, ffn_w_up, ffn_w_down, router_w,
              moe_w_gate, moe_w_up, moe_w_down):
    for layer in range(DEPTH):
        m = _hybrid_mixer(x, w_in[layer], conv_w[layer], conv_b[layer], w_a[layer], b_a[layer],
                          w_x[layer], b_x[layer], lru_lambda[layer], rel_bias,
                          g_attn[layer], g_lru[layer], w_out[layer])
        x = _layer_norm(DEEPNORM_ALPHA * x + m, ln1_g[layer], ln1_b[layer])
        if layer % 2 == 0:
            j = layer // 2
            f = _swiglu(x, ffn_w_gate[j], ffn_w_up[j], ffn_w_down[j])
        else:
            j = layer // 2
            f = _moe_swiglu(x, router_w[j], moe_w_gate[j], moe_w_up[j], moe_w_down[j])
        x = _layer_norm(DEEPNORM_ALPHA * x + f, ln2_g[layer], ln2_b[layer])
    return x
```

```python
import functools

import numpy as np
import jax
import jax.numpy as jnp
from jax import lax
from jax.experimental import pallas as pl
from jax.experimental.pallas import tpu as pltpu

D_MODEL = 1024
N_HEADS = 8
HEAD_DIM = 64
D_ATTN = N_HEADS * HEAD_DIM
D_LRU = 512
N_LRU_BLOCKS = 8
LRU_BLOCK = D_LRU // N_LRU_BLOCKS
CONV_WIDTH = 4
LRU_C = 8.0
DILATED_PATTERNS = ((128, 1), (512, 4), (2048, 16))
ATTN_BLOCK = 128
N_BUCKETS = 32
MAX_DISTANCE = 2048
D_IN = 3 * D_ATTN + 2 * D_LRU
N_EXPERTS = 8
TOP_K = 2
DEPTH = 2
DEEPNORM_ALPHA = (2.0 * DEPTH) ** 0.25
LN_EPS = 1e-5
RMS_EPS = 1e-6
NEG_INF = -1e30

LANES = 128
VMEM_LIMIT_BYTES = 56 * 1024 * 1024

f32 = jnp.float32
bf16 = jnp.bfloat16


def _params(*semantics):
    return pltpu.CompilerParams(dimension_semantics=semantics,
                                vmem_limit_bytes=VMEM_LIMIT_BYTES)


def _in_proj_kernel(x_ref, w_ref, qkv_ref, ug_ref):
    xb = x_ref[...].astype(bf16)
    n_qkv = qkv_ref.shape[1]
    step = 512
    for c in range(0, D_IN, step):
        p = jnp.dot(xb, w_ref[:, c:c + step], preferred_element_type=f32)
        if c < n_qkv:
            qkv_ref[:, c:c + step] = p.astype(bf16)
        else:
            ug_ref[:, c - n_qkv:c - n_qkv + step] = p


def _in_proj(x2d, w_in_bf16, tm=512):
    n = x2d.shape[0]
    return pl.pallas_call(
        _in_proj_kernel,
        grid=(n // tm,),
        in_specs=[pl.BlockSpec((tm, D_MODEL), lambda i: (i, 0)),
                  pl.BlockSpec((D_MODEL, D_IN), lambda i: (0, 0))],
        out_specs=[pl.BlockSpec((tm, 3 * D_ATTN), lambda i: (i, 0)),
                   pl.BlockSpec((tm, 2 * D_LRU), lambda i: (i, 0))],
        out_shape=[jax.ShapeDtypeStruct((n, 3 * D_ATTN), bf16),
                   jax.ShapeDtypeStruct((n, 2 * D_LRU), f32)],
        compiler_params=_params("parallel"),
        name="in_proj",
    )(x2d, w_in_bf16)


def _t5_bucket(dist):
    max_exact = N_BUCKETS // 2
    d = np.maximum(dist, 1).astype(np.float32)
    large = max_exact + (np.log(d / max_exact) / np.log(MAX_DISTANCE / max_exact)
                         * (N_BUCKETS - max_exact)).astype(np.int32)
    large = np.minimum(large, N_BUCKETS - 1)
    return np.where(dist < max_exact, dist, large).astype(np.int32)


def _band_bias(rel_bias, dilation):
    nk = ATTN_BLOCK
    qi = np.arange(nk)[:, None]
    kj = np.arange(2 * nk)[None, :]
    delta = qi + nk - kj
    band = (delta >= 0) & (delta <= nk)
    bucket = _t5_bucket(np.clip(delta, 0, nk) * dilation)
    bias = jnp.transpose(rel_bias.astype(f32)[bucket], (2, 0, 1))
    return jnp.where(jnp.asarray(band)[None], bias, NEG_INF)


def _attn_kernel(q_ref, k_ref, v_ref, bias_ref, o_ref, lse_ref, *, n_blocks):
    nk = ATTN_BLOCK
    lane = lax.broadcasted_iota(jnp.int32, (1, LANES), 1)
    head0 = lane < HEAD_DIM

    def one_block(row0, k, v, biases):
        q = q_ref[pl.ds(row0, nk), :]
        outs, lses = [], []
        for h in range(2):
            keep = head0 if h == 0 else jnp.logical_not(head0)
            qh = jnp.where(keep, q, jnp.zeros_like(q)) * jnp.asarray(HEAD_DIM ** -0.5, bf16)
            s = lax.dot_general(qh, k, (((1,), (1,)), ((), ())),
                                preferred_element_type=f32) + biases[h]
            m = jnp.max(s, axis=-1, keepdims=True)
            p = jnp.exp(s - m)
            l = jnp.sum(p, axis=-1, keepdims=True)
            pv = jnp.dot(p.astype(bf16), v, preferred_element_type=f32)
            outs.append(pv / l)
            lses.append(m + jnp.log(l))
        o_ref[pl.ds(row0, nk), :] = jnp.where(head0, outs[0], outs[1]).astype(o_ref.dtype)
        lse_ref[pl.ds(row0, nk), :] = jnp.where(head0, lses[0], lses[1])

    one_block(0, k_ref[0:nk, :], v_ref[0:nk, :],
              [bias_ref[0, :, nk:], bias_ref[1, :, nk:]])

    def body(i, carry):
        row0 = pl.multiple_of(i * nk, nk)
        prev0 = pl.multiple_of((i - 1) * nk, nk)
        one_block(row0, k_ref[pl.ds(prev0, 2 * nk), :], v_ref[pl.ds(prev0, 2 * nk), :],
                  [bias_ref[0], bias_ref[1]])
        return carry

    if n_blocks > 1:
        lax.fori_loop(1, n_blocks, body, 0)


def _attention_pattern(qkv, band_bias, batch, seq, dilation):
    length = seq // dilation
    n_blocks = length // ATTN_BLOCK
    width = 3 * D_ATTN
    view = qkv.reshape(batch, length, dilation * width)
    pairs = D_ATTN // LANES
    per_row = width // LANES

    def spec(offset):
        return pl.BlockSpec((None, length, LANES),
                            lambda b, r, hp: (b, 0, r * per_row + offset * pairs + hp))

    out_spec = pl.BlockSpec((None, length, LANES), lambda b, r, hp: (b, 0, r * pairs + hp))
    o, lse = pl.pallas_call(
        functools.partial(_attn_kernel, n_blocks=n_blocks),
        grid=(batch, dilation, pairs),
        in_specs=[spec(0), spec(1), spec(2),
                  pl.BlockSpec((2, ATTN_BLOCK, 2 * ATTN_BLOCK), lambda b, r, hp: (hp, 0, 0))],
        out_specs=[out_spec, out_spec],
        out_shape=[jax.ShapeDtypeStruct((batch, length, dilation * D_ATTN), bf16),
                   jax.ShapeDtypeStruct((batch, length, dilation * D_ATTN), f32)],
        compiler_params=_params("parallel", "parallel", "parallel"),
        name=f"attn_d{dilation}",
    )(view, view, view, band_bias)
    return o.reshape(batch * seq, D_ATTN), lse.reshape(batch * seq, D_ATTN)


def _gelu_tanh(x):
    return 0.5 * x * (1.0 + jnp.tanh(np.sqrt(2.0 / np.pi) * (x + 0.044715 * x * x * x)))


def _lru_kernel(ug_ref, convw_ref, convb_ref, wg_ref, ba_ref, bx_ref, lam_ref, g_ref,
                y_ref, ubuf, hcarry, *, ts):
    pad = 8
    t = pl.program_id(1)

    @pl.when(t == 0)
    def _():
        ubuf[0:pad, :] = jnp.zeros((pad, D_LRU), f32)
        hcarry[...] = jnp.zeros_like(hcarry)

    @pl.when(t > 0)
    def _():
        ubuf[0:pad, :] = ubuf[ts:ts + pad, :]

    ubuf[pad:pad + ts, :] = ug_ref[:, 0:D_LRU]
    gate = ug_ref[:, D_LRU:2 * D_LRU]

    u = convb_ref[...] + convw_ref[CONV_WIDTH - 1:CONV_WIDTH, :] * ubuf[pad:pad + ts, :]
    for w in range(CONV_WIDTH - 1):
        back = CONV_WIDTH - 1 - w
        u = u + convw_ref[w:w + 1, :] * ubuf[pad - back:pad - back + ts, :]

    gates = jnp.dot(u.astype(bf16), wg_ref[...], preferred_element_type=f32)
    r = jax.nn.sigmoid(gates[:, 0:D_LRU] + ba_ref[...])
    i = jax.nn.sigmoid(gates[:, D_LRU:2 * D_LRU] + bx_ref[...])
    neg_lam = -lam_ref[...]
    softplus = jnp.maximum(neg_lam, 0.0) + jnp.log1p(jnp.exp(-jnp.abs(neg_lam)))
    log_a = (-LRU_C) * r * softplus
    a = jnp.exp(log_a)
    b = jnp.sqrt(-jnp.tanh(log_a) * (1.0 + a * a)) * (i * u)

    row = lax.broadcasted_iota(jnp.int32, (ts, 1), 0)
    shift = 1
    while shift < ts:
        live = row >= shift
        a_prev = jnp.where(live, pltpu.roll(a, shift, 0), 1.0)
        b_prev = jnp.where(live, pltpu.roll(b, shift, 0), 0.0)
        b = a * b_prev + b
        a = a * a_prev
        shift *= 2
    h = a * hcarry[...] + b
    hcarry[...] = h[ts - 1:ts, :]

    y = _gelu_tanh(gate) * h
    y = y * lax.rsqrt(jnp.mean(y * y, axis=-1, keepdims=True) + RMS_EPS) * g_ref[...]
    y_ref[...] = y.astype(y_ref.dtype)


def _block_diag(w):
    g, i, j = w.shape
    eye = jnp.eye(g, dtype=w.dtype)
    return jnp.einsum('gij,gh->gihj', w, eye).reshape(g * i, g * j)


def _lru_branch(ug, conv_w, conv_b, w_a, b_a, w_x, b_x, lam, g_lru, batch, seq, ts=256):
    wg = jnp.concatenate([_block_diag(w_a), _block_diag(w_x)], axis=1).astype(bf16)
    row = lambda v: v.reshape(1, D_LRU).astype(f32)
    const = lambda shape: pl.BlockSpec(shape, lambda b, t: (0, 0))
    view = ug.reshape(batch, seq, 2 * D_LRU)
    y = pl.pallas_call(
        functools.partial(_lru_kernel, ts=ts),
        grid=(batch, seq // ts),
        in_specs=[pl.BlockSpec((None, ts, 2 * D_LRU), lambda b, t: (b, t, 0)),
                  const((CONV_WIDTH, D_LRU)), const((1, D_LRU)),
                  const((D_LRU, 2 * D_LRU)), const((1, D_LRU)), const((1, D_LRU)),
                  const((1, D_LRU)), const((1, D_LRU))],
        out_specs=pl.BlockSpec((None, ts, D_LRU), lambda b, t: (b, t, 0)),
        out_shape=jax.ShapeDtypeStruct((batch, seq, D_LRU), bf16),
        scratch_shapes=[pltpu.VMEM((ts + 16, D_LRU), f32), pltpu.VMEM((1, D_LRU), f32)],
        compiler_params=_params("parallel", "arbitrary"),
        name="rglru",
    )(view, conv_w.reshape(CONV_WIDTH, D_LRU).astype(f32), row(conv_b), wg, row(b_a), row(b_x),
      row(lam), row(g_lru))
    return y.reshape(batch * seq, D_LRU)


def _layer_norm(z, g, b):
    mu = jnp.mean(z, axis=-1, keepdims=True)
    zc = z - mu
    var = jnp.mean(zc * zc, axis=-1, keepdims=True)
    return zc * lax.rsqrt(var + LN_EPS) * g + b


def _mix_out_kernel(o1, o2, o3, l1, l2, l3, ylru, x_ref, w_ref, gattn, lng, lnb, *rest,
                    with_router):
    if with_router:
        rw_ref, x1_ref, comb_ref = rest
    else:
        (x1_ref,) = rest
    la, lb, lc = l1[...], l2[...], l3[...]
    m = jnp.maximum(jnp.maximum(la, lb), lc)
    ea, eb, ec = jnp.exp(la - m), jnp.exp(lb - m), jnp.exp(lc - m)
    attn = (ea * o1[...].astype(f32) + eb * o2[...].astype(f32) + ec * o3[...].astype(f32)) \
        / (ea + eb + ec)
    attn = attn * lax.rsqrt(jnp.mean(attn * attn, axis=-1, keepdims=True) + RMS_EPS) * gattn[...]
    y = jnp.dot(attn.astype(bf16), w_ref[0:D_ATTN, :], preferred_element_type=f32)
    y = y + jnp.dot(ylru[...], w_ref[D_ATTN:, :], preferred_element_type=f32)
    x1 = _layer_norm(DEEPNORM_ALPHA * x_ref[...] + y, lng[...], lnb[...])
    x1_ref[...] = x1
    if with_router:
        lane = lax.broadcasted_iota(jnp.int32, (1, LANES), 1)
        logits = jnp.dot(x1, rw_ref[...], preferred_element_type=f32,
                         precision=lax.Precision.HIGHEST)
        logits = jnp.where(lane < N_EXPERTS, logits, -jnp.inf)
        v1 = jnp.max(logits, axis=-1, keepdims=True)
        i1 = jnp.min(jnp.where(logits == v1, lane, LANES), axis=-1, keepdims=True)
        rest_logits = jnp.where(lane == i1, -jnp.inf, logits)
        v2 = jnp.max(rest_logits, axis=-1, keepdims=True)
        i2 = jnp.min(jnp.where(rest_logits == v2, lane, LANES), axis=-1, keepdims=True)
        e2 = jnp.exp(v2 - v1)
        p1 = 1.0 / (1.0 + e2)
        p2 = e2 / (1.0 + e2)
        comb_ref[...] = jnp.where(lane == i1, p1, 0.0) + jnp.where(lane == i2, p2, 0.0)


def _mix_out(outs, lses, ylru, x2d, w_out_bf16, g_attn, ln_g, ln_b, router_w=None, tm=512):
    n = x2d.shape[0]
    with_router = router_w is not None
    tile = lambda width: pl.BlockSpec((tm, width), lambda i: (i, 0))
    const = lambda shape: pl.BlockSpec(shape, lambda i: (0, 0))
    in_specs = [tile(D_ATTN)] * 3 + [tile(D_ATTN)] * 3 + [tile(D_LRU), tile(D_MODEL),
                const((D_MODEL, D_MODEL)), const((1, D_ATTN)), const((1, D_MODEL)),
                const((1, D_MODEL))]
    args = [*outs, *lses, ylru, x2d, w_out_bf16, g_attn.reshape(1, D_ATTN).astype(f32),
            ln_g.reshape(1, D_MODEL).astype(f32), ln_b.reshape(1, D_MODEL).astype(f32)]
    out_specs = [tile(D_MODEL)]
    out_shape = [jax.ShapeDtypeStruct((n, D_MODEL), f32)]
    if with_router:
        rw = jnp.zeros((D_MODEL, LANES), f32).at[:, :N_EXPERTS].set(router_w.astype(f32))
        in_specs.append(const((D_MODEL, LANES)))
        args.append(rw)
        out_specs.append(tile(LANES))
        out_shape.append(jax.ShapeDtypeStruct((n, LANES), f32))
    res = pl.pallas_call(
        functools.partial(_mix_out_kernel, with_router=with_router),
        grid=(n // tm,),
        in_specs=in_specs,
        out_specs=out_specs,
        out_shape=out_shape,
        compiler_params=_params("parallel"),
        name="mix_out_router" if with_router else "mix_out",
    )(*args)
    return res if with_router else (res[0], None)


def _ffn_kernel(x_ref, comb_ref, wg_ref, wu_ref, wd_ref, lng, lnb, out_ref, acc_ref, xb_ref,
                *, n_experts, n_chunks):
    e = pl.program_id(1)
    j = pl.program_id(2)

    @pl.when((e == 0) & (j == 0))
    def _():
        acc_ref[...] = jnp.zeros_like(acc_ref)
        xb_ref[...] = x_ref[...].astype(bf16)

    xb = xb_ref[...]
    g = jnp.dot(xb, wg_ref[...], preferred_element_type=f32)
    u = jnp.dot(xb, wu_ref[...], preferred_element_type=f32)
    h = (g * jax.nn.sigmoid(g)) * u
    if n_experts > 1:
        lane = lax.broadcasted_iota(jnp.int32, (1, LANES), 1)
        scale = jnp.sum(jnp.where(lane == e, comb_ref[...], 0.0), axis=-1, keepdims=True)
        h = h * scale
    acc_ref[...] += jnp.dot(h.astype(bf16), wd_ref[...], preferred_element_type=f32)

    @pl.when((e == n_experts - 1) & (j == n_chunks - 1))
    def _():
        out_ref[...] = _layer_norm(DEEPNORM_ALPHA * x_ref[...] + acc_ref[...], lng[...], lnb[...])


def _ffn(x2d, comb, w_gate, w_up, w_down, ln_g, ln_b, tm, tf):
    n = x2d.shape[0]
    n_experts, _, d_ff = w_gate.shape
    n_chunks = d_ff // tf
    return pl.pallas_call(
        functools.partial(_ffn_kernel, n_experts=n_experts, n_chunks=n_chunks),
        grid=(n // tm, n_experts, n_chunks),
        in_specs=[pl.BlockSpec((tm, D_MODEL), lambda i, e, j: (i, 0)),
                  pl.BlockSpec((tm, LANES), lambda i, e, j: (i, 0)),
                  pl.BlockSpec((None, D_MODEL, tf), lambda i, e, j: (e, 0, j)),
                  pl.BlockSpec((None, D_MODEL, tf), lambda i, e, j: (e, 0, j)),
                  pl.BlockSpec((None, tf, D_MODEL), lambda i, e, j: (e, j, 0)),
                  pl.BlockSpec((1, D_MODEL), lambda i, e, j: (0, 0)),
                  pl.BlockSpec((1, D_MODEL), lambda i, e, j: (0, 0))],
        out_specs=pl.BlockSpec((tm, D_MODEL), lambda i, e, j: (i, 0)),
        out_shape=jax.ShapeDtypeStruct((n, D_MODEL), f32),
        scratch_shapes=[pltpu.VMEM((tm, D_MODEL), f32), pltpu.VMEM((tm, D_MODEL), bf16)],
        compiler_params=_params("parallel", "arbitrary", "arbitrary"),
        name="ffn_dense" if n_experts == 1 else "ffn_experts",
    )(x2d, comb, w_gate, w_up, w_down, ln_g.reshape(1, D_MODEL).astype(f32),
      ln_b.reshape(1, D_MODEL).astype(f32))


def kernel(x, w_in, conv_w, conv_b, w_a, b_a, w_x, b_x, lru_lambda, rel_bias, g_attn, g_lru, w_out, ln1_g, ln1_b, ln2_g, ln2_b, ffn_w_gate, ffn_w_up, ffn_w_down, router_w, moe_w_gate, moe_w_up, moe_w_down):
    batch, seq, _ = x.shape
    n = batch * seq
    h = x.reshape(n, D_MODEL).astype(f32)
    biases = [_band_bias(rel_bias, d) for _, d in DILATED_PATTERNS]
    for layer in range(DEPTH):
        qkv, ug = _in_proj(h, w_in[layer].astype(bf16))
        outs, lses = [], []
        for (_, dilation), bias in zip(DILATED_PATTERNS, biases):
            o, lse = _attention_pattern(qkv, bias, batch, seq, dilation)
            outs.append(o)
            lses.append(lse)
        ylru = _lru_branch(ug, conv_w[layer], conv_b[layer], w_a[layer], b_a[layer], w_x[layer],
                           b_x[layer], lru_lambda[layer], g_lru[layer], batch, seq)
        j = layer // 2
        dense = layer % 2 == 0
        x1, comb = _mix_out(outs, lses, ylru, h, w_out[layer].astype(bf16), g_attn[layer],
                            ln1_g[layer], ln1_b[layer], None if dense else router_w[j])
        if dense:
            ones = jnp.ones((n, LANES), f32)
            h = _ffn(x1, ones, ffn_w_gate[j][None].astype(bf16), ffn_w_up[j][None].astype(bf16),
                     ffn_w_down[j][None].astype(bf16), ln2_g[layer], ln2_b[layer],
                     tm=512, tf=1408)
        else:
            h = _ffn(x1, comb, moe_w_gate[j].astype(bf16), moe_w_up[j].astype(bf16),
                     moe_w_down[j].astype(bf16), ln2_g[layer], ln2_b[layer], tm=1024, tf=512)
    return h.reshape(batch, seq, D_MODEL).astype(x.dtype)
```

```python
import functools

import numpy as np
import jax
import jax.numpy as jnp
from jax import lax
from jax.experimental import pallas as pl
from jax.experimental.pallas import tpu as pltpu

D_MODEL = 1024
N_HEADS = 8
HEAD_DIM = 64
D_ATTN = N_HEADS * HEAD_DIM
D_LRU = 512
N_LRU_BLOCKS = 8
LRU_BLOCK = D_LRU // N_LRU_BLOCKS
CONV_WIDTH = 4
LRU_C = 8.0
DILATED_PATTERNS = ((128, 1), (512, 4), (2048, 16))
ATTN_BLOCK = 128
N_BUCKETS = 32
MAX_DISTANCE = 2048
D_IN = 3 * D_ATTN + 2 * D_LRU
N_EXPERTS = 8
TOP_K = 2
DEPTH = 2
DEEPNORM_ALPHA = (2.0 * DEPTH) ** 0.25
LN_EPS = 1e-5
RMS_EPS = 1e-6
NEG_INF = -1e30

LANES = 128
VMEM_LIMIT_BYTES = 56 * 1024 * 1024

f32 = jnp.float32
bf16 = jnp.bfloat16


def _params(*semantics):
    return pltpu.CompilerParams(dimension_semantics=semantics,
                                vmem_limit_bytes=VMEM_LIMIT_BYTES)


def _in_proj_kernel(x_ref, w_ref, qkv_ref, ug_ref):
    xb = x_ref[...].astype(bf16)
    n_qkv = qkv_ref.shape[1]
    step = 512
    for c in range(0, D_IN, step):
        p = jnp.dot(xb, w_ref[:, c:c + step], preferred_element_type=f32)
        if c < n_qkv:
            qkv_ref[:, c:c + step] = p.astype(bf16)
        else:
            ug_ref[:, c - n_qkv:c - n_qkv + step] = p


def _in_proj(x2d, w_in_bf16, tm=512):
    n = x2d.shape[0]
    return pl.pallas_call(
        _in_proj_kernel,
        grid=(n // tm,),
        in_specs=[pl.BlockSpec((tm, D_MODEL), lambda i: (i, 0)),
                  pl.BlockSpec((D_MODEL, D_IN), lambda i: (0, 0))],
        out_specs=[pl.BlockSpec((tm, 3 * D_ATTN), lambda i: (i, 0)),
                   pl.BlockSpec((tm, 2 * D_LRU), lambda i: (i, 0))],
        out_shape=[jax.ShapeDtypeStruct((n, 3 * D_ATTN), bf16),
                   jax.ShapeDtypeStruct((n, 2 * D_LRU), f32)],
        compiler_params=_params("parallel"),
        name="in_proj",
    )(x2d, w_in_bf16)


def _t5_bucket(dist):
    max_exact = N_BUCKETS // 2
    d = np.maximum(dist, 1).astype(np.float32)
    large = max_exact + (np.log(d / max_exact) / np.log(MAX_DISTANCE / max_exact)
                         * (N_BUCKETS - max_exact)).astype(np.int32)
    large = np.minimum(large, N_BUCKETS - 1)
    return np.where(dist < max_exact, dist, large).astype(np.int32)


def _band_bias(rel_bias, dilation):
    nk = ATTN_BLOCK
    qi = np.arange(nk)[:, None]
    kj = np.arange(2 * nk)[None, :]
    delta = qi + nk - kj
    band = (delta >= 0) & (delta <= nk)
    bucket = _t5_bucket(np.clip(delta, 0, nk) * dilation)
    bias = jnp.transpose(rel_bias.astype(f32)[bucket], (2, 0, 1))
    return jnp.where(jnp.asarray(band)[None], bias, NEG_INF)


def _attn_kernel(q_ref, k_ref, v_ref, bias_ref, o_ref, lse_ref, *, n_blocks):
    nk = ATTN_BLOCK
    lane = lax.broadcasted_iota(jnp.int32, (1, LANES), 1)
    head0 = lane < HEAD_DIM

    def one_block(row0, k, v, biases):
        q = q_ref[pl.ds(row0, nk), :]
        outs, lses = [], []
        for h in range(2):
            keep = head0 if h == 0 else jnp.logical_not(head0)
            qh = jnp.where(keep, q, jnp.zeros_like(q)) * jnp.asarray(HEAD_DIM ** -0.5, bf16)
            s = lax.dot_general(qh, k, (((1,), (1,)), ((), ())),
                                preferred_element_type=f32) + biases[h]
            m = jnp.max(s, axis=-1, keepdims=True)
            p = jnp.exp(s - m)
            l = jnp.sum(p, axis=-1, keepdims=True)
            pv = jnp.dot(p.astype(bf16), v, preferred_element_type=f32)
            outs.append(pv / l)
            lses.append(m + jnp.log(l))
        o_ref[pl.ds(row0, nk), :] = jnp.where(head0, outs[0], outs[1]).astype(o_ref.dtype)
        lse_ref[pl.ds(row0, nk), :] = jnp.where(head0, lses[0], lses[1])

    one_block(0, k_ref[0:nk, :], v_ref[0:nk, :],
              [bias_ref[0, :, nk:], bias_ref[1, :, nk:]])

    def body(i, carry):
        row0 = pl.multiple_of(i * nk, nk)
        prev0 = pl.multiple_of((i - 1) * nk, nk)
        one_block(row0, k_ref[pl.ds(prev0, 2 * nk), :], v_ref[pl.ds(prev0, 2 * nk), :],
                  [bias_ref[0], bias_ref[1]])
        return carry

    if n_blocks > 1:
        lax.fori_loop(1, n_blocks, body, 0)


def _attention_pattern(qkv, band_bias, batch, seq, dilation):
    length = seq // dilation
    n_blocks = length // ATTN_BLOCK
    width = 3 * D_ATTN
    view = qkv.reshape(batch, length, dilation * width)
    pairs = D_ATTN // LANES
    per_row = width // LANES

    def spec(offset):
        return pl.BlockSpec((None, length, LANES),
                            lambda b, r, hp: (b, 0, r * per_row + offset * pairs + hp))

    out_spec = pl.BlockSpec((None, length, LANES), lambda b, r, hp: (b, 0, r * pairs + hp))
    o, lse = pl.pallas_call(
        functools.partial(_attn_kernel, n_blocks=n_blocks),
        grid=(batch, dilation, pairs),
        in_specs=[spec(0), spec(1), spec(2),
                  pl.BlockSpec((2, ATTN_BLOCK, 2 * ATTN_BLOCK), lambda b, r, hp: (hp, 0, 0))],
        out_specs=[out_spec, out_spec],
        out_shape=[jax.ShapeDtypeStruct((batch, length, dilation * D_ATTN), bf16),
                   jax.ShapeDtypeStruct((batch, length, dilation * D_ATTN), f32)],
        compiler_params=_params("parallel", "parallel", "parallel"),
        name=f"attn_d{dilation}",
    )(view, view, view, band_bias)
    return o.reshape(batch * seq, D_ATTN), lse.reshape(batch * seq, D_ATTN)


def _gelu_tanh(x):
    return 0.5 * x * (1.0 + jnp.tanh(np.sqrt(2.0 / np.pi) * (x + 0.044715 * x * x * x)))


def _lru_kernel(ug_ref, convw_ref, convb_ref, wg_ref, ba_ref, bx_ref, lam_ref, g_ref,
                y_ref, ubuf, hcarry, *, ts):
    pad = 8
    t = pl.program_id(1)

    @pl.when(t == 0)
    def _():
        ubuf[0:pad, :] = jnp.zeros((pad, D_LRU), f32)
        hcarry[...] = jnp.zeros_like(hcarry)

    @pl.when(t > 0)
    def _():
        ubuf[0:pad, :] = ubuf[ts:ts + pad, :]

    ubuf[pad:pad + ts, :] = ug_ref[:, 0:D_LRU]
    gate = ug_ref[:, D_LRU:2 * D_LRU]

    u = convb_ref[...] + convw_ref[CONV_WIDTH - 1:CONV_WIDTH, :] * ubuf[pad:pad + ts, :]
    for w in range(CONV_WIDTH - 1):
        back = CONV_WIDTH - 1 - w
        u = u + convw_ref[w:w + 1, :] * ubuf[pad - back:pad - back + ts, :]

    gates = jnp.dot(u.astype(bf16), wg_ref[...], preferred_element_type=f32)
    r = jax.nn.sigmoid(gates[:, 0:D_LRU] + ba_ref[...])
    i = jax.nn.sigmoid(gates[:, D_LRU:2 * D_LRU] + bx_ref[...])
    neg_lam = -lam_ref[...]
    softplus = jnp.maximum(neg_lam, 0.0) + jnp.log1p(jnp.exp(-jnp.abs(neg_lam)))
    log_a = (-LRU_C) * r * softplus
    a = jnp.exp(log_a)
    b = jnp.sqrt(-jnp.tanh(log_a) * (1.0 + a * a)) * (i * u)

    row = lax.broadcasted_iota(jnp.int32, (ts, 1), 0)
    shift = 1
    while shift < ts:
        live = row >= shift
        a_prev = jnp.where(live, pltpu.roll(a, shift, 0), 1.0)
        b_prev = jnp.where(live, pltpu.roll(b, shift, 0), 0.0)
        b = a * b_prev + b
        a = a * a_prev
        shift *= 2
    h = a * hcarry[...] + b
    hcarry[...] = h[ts - 1:ts, :]

    y = _gelu_tanh(gate) * h
    y = y * lax.rsqrt(jnp.mean(y * y, axis=-1, keepdims=True) + RMS_EPS) * g_ref[...]
    y_ref[...] = y.astype(y_ref.dtype)


def _block_diag(w):
    g, i, j = w.shape
    eye = jnp.eye(g, dtype=w.dtype)
    return jnp.einsum('gij,gh->gihj', w, eye).reshape(g * i, g * j)


def _lru_branch(ug, conv_w, conv_b, w_a, b_a, w_x, b_x, lam, g_lru, batch, seq, ts=256):
    wg = jnp.concatenate([_block_diag(w_a), _block_diag(w_x)], axis=1).astype(bf16)
    row = lambda v: v.reshape(1, D_LRU).astype(f32)
    const = lambda shape: pl.BlockSpec(shape, lambda b, t: (0, 0))
    view = ug.reshape(batch, seq, 2 * D_LRU)
    y = pl.pallas_call(
        functools.partial(_lru_kernel, ts=ts),
        grid=(batch, seq // ts),
        in_specs=[pl.BlockSpec((None, ts, 2 * D_LRU), lambda b, t: (b, t, 0)),
                  const((CONV_WIDTH, D_LRU)), const((1, D_LRU)),
                  const((D_LRU, 2 * D_LRU)), const((1, D_LRU)), const((1, D_LRU)),
                  const((1, D_LRU)), const((1, D_LRU))],
        out_specs=pl.BlockSpec((None, ts, D_LRU), lambda b, t: (b, t, 0)),
        out_shape=jax.ShapeDtypeStruct((batch, seq, D_LRU), bf16),
        scratch_shapes=[pltpu.VMEM((ts + 16, D_LRU), f32), pltpu.VMEM((1, D_LRU), f32)],
        compiler_params=_params("parallel", "arbitrary"),
        name="rglru",
    )(view, conv_w.reshape(CONV_WIDTH, D_LRU).astype(f32), row(conv_b), wg, row(b_a), row(b_x),
      row(lam), row(g_lru))
    return y.reshape(batch * seq, D_LRU)


ROW_TILE = D_MODEL // LANES


def _store_row_tiles(ref, rows):
    t = rows.shape[0]
    for s in range(ROW_TILE):
        ref[pl.ds(s, t, stride=ROW_TILE), :] = rows[:, s * LANES:(s + 1) * LANES]


def _load_row_tiles(ref, t):
    return jnp.concatenate([ref[pl.ds(s, t, stride=ROW_TILE), :] for s in range(ROW_TILE)],
                           axis=-1)


def _layer_norm(z, g, b):
    mu = jnp.mean(z, axis=-1, keepdims=True)
    zc = z - mu
    var = jnp.mean(zc * zc, axis=-1, keepdims=True)
    return zc * lax.rsqrt(var + LN_EPS) * g + b


def _mix_out_kernel(o1, o2, o3, l1, l2, l3, ylru, x_ref, w_ref, gattn, lng, lnb, *rest,
                    with_router):
    if with_router:
        rw_ref, x1_ref, x1_rows_ref, route_ref, count_ref = rest
    else:
        (x1_ref,) = rest
    la, lb, lc = l1[...], l2[...], l3[...]
    m = jnp.maximum(jnp.maximum(la, lb), lc)
    ea, eb, ec = jnp.exp(la - m), jnp.exp(lb - m), jnp.exp(lc - m)
    attn = (ea * o1[...].astype(f32) + eb * o2[...].astype(f32) + ec * o3[...].astype(f32)) \
        / (ea + eb + ec)
    attn = attn * lax.rsqrt(jnp.mean(attn * attn, axis=-1, keepdims=True) + RMS_EPS) * gattn[...]
    y = jnp.dot(attn.astype(bf16), w_ref[0:D_ATTN, :], preferred_element_type=f32)
    y = y + jnp.dot(ylru[...], w_ref[D_ATTN:, :], preferred_element_type=f32)
    x1 = _layer_norm(DEEPNORM_ALPHA * x_ref[...] + y, lng[...], lnb[...])
    x1_ref[...] = x1
    if with_router:
        _store_row_tiles(x1_rows_ref, x1)
        lane = lax.broadcasted_iota(jnp.int32, (1, LANES), 1)
        logits = jnp.dot(x1, rw_ref[...], preferred_element_type=f32,
                         precision=lax.Precision.HIGHEST)
        logits = jnp.where(lane < N_EXPERTS, logits, -jnp.inf)
        v1 = jnp.max(logits, axis=-1, keepdims=True)
        i1 = jnp.min(jnp.where(logits == v1, lane, LANES), axis=-1, keepdims=True)
        rest_logits = jnp.where(lane == i1, -jnp.inf, logits)
        v2 = jnp.max(rest_logits, axis=-1, keepdims=True)
        i2 = jnp.min(jnp.where(rest_logits == v2, lane, LANES), axis=-1, keepdims=True)
        e2 = jnp.exp(v2 - v1)
        p1 = 1.0 / (1.0 + e2)
        p2 = e2 / (1.0 + e2)

        @pl.when(pl.program_id(0) == 0)
        def _():
            count_ref[...] = jnp.zeros_like(count_ref)

        tm = x1.shape[0]
        chosen = jnp.logical_or(lane == i1, lane == i2)
        tri = (lax.broadcasted_iota(jnp.int32, (tm, tm), 0)
               > lax.broadcasted_iota(jnp.int32, (tm, tm), 1)).astype(bf16)
        rank = count_ref[...] + jnp.dot(tri, chosen.astype(bf16), preferred_element_type=f32)
        count_ref[...] += jnp.sum(chosen.astype(f32), axis=0, keepdims=True)
        r1 = jnp.sum(jnp.where(lane == i1, rank, 0.0), axis=-1, keepdims=True)
        r2 = jnp.sum(jnp.where(lane == i2, rank, 0.0), axis=-1, keepdims=True)
        fields = (i1.astype(f32), i2.astype(f32), r1, r2, p1, p2)
        route = jnp.zeros((tm, LANES), f32)
        for k, val in enumerate(fields):
            route = jnp.where(lane == k, val, route)
        route_ref[...] = route


def _mix_out(outs, lses, ylru, x2d, w_out_bf16, g_attn, ln_g, ln_b, router_w=None, tm=512):
    n = x2d.shape[0]
    with_router = router_w is not None
    tile = lambda width: pl.BlockSpec((tm, width), lambda i: (i, 0))
    const = lambda shape: pl.BlockSpec(shape, lambda i: (0, 0))
    in_specs = [tile(D_ATTN)] * 3 + [tile(D_ATTN)] * 3 + [tile(D_LRU), tile(D_MODEL),
                const((D_MODEL, D_MODEL)), const((1, D_ATTN)), const((1, D_MODEL)),
                const((1, D_MODEL))]
    args = [*outs, *lses, ylru, x2d, w_out_bf16, g_attn.reshape(1, D_ATTN).astype(f32),
            ln_g.reshape(1, D_MODEL).astype(f32), ln_b.reshape(1, D_MODEL).astype(f32)]
    out_specs = [tile(D_MODEL)]
    out_shape = [jax.ShapeDtypeStruct((n, D_MODEL), f32)]
    if with_router:
        rw = jnp.zeros((D_MODEL, LANES), f32).at[:, :N_EXPERTS].set(router_w.astype(f32))
        in_specs.append(const((D_MODEL, LANES)))
        args.append(rw)
        out_specs += [pl.BlockSpec((tm * ROW_TILE, LANES), lambda i: (i, 0)), tile(LANES),
                      const((1, LANES))]
        out_shape += [jax.ShapeDtypeStruct((n * ROW_TILE, LANES), f32),
                      jax.ShapeDtypeStruct((n, LANES), f32), jax.ShapeDtypeStruct((1, LANES), f32)]
    res = pl.pallas_call(
        functools.partial(_mix_out_kernel, with_router=with_router),
        grid=(n // tm,),
        in_specs=in_specs,
        out_specs=out_specs,
        out_shape=out_shape,
        compiler_params=_params("arbitrary" if with_router else "parallel"),
        name="mix_out_router" if with_router else "mix_out",
    )(*args)
    return res if with_router else (res[0], None, None, None)


def _ffn_kernel(tile_expert_ref, n_used_ref, x_ref, wg_ref, wu_ref, wd_ref, *rest,
                n_chunks, fuse_ln):
    if fuse_ln:
        lng, lnb, out_ref, acc_ref, xb_ref = rest
    else:
        out_ref, acc_ref, xb_ref = rest
    del tile_expert_ref
    j = pl.program_id(1)
    used = pl.program_id(0) < n_used_ref[0]

    @pl.when(j == 0)
    def _():
        acc_ref[...] = jnp.zeros_like(acc_ref)

    @pl.when(used & (j == 0))
    def _():
        if fuse_ln:
            xb_ref[...] = x_ref[...].astype(bf16)
        else:
            xb_ref[...] = _load_row_tiles(x_ref, xb_ref.shape[0]).astype(bf16)

    @pl.when(used)
    def _():
        xb = xb_ref[...]
        g = jnp.dot(xb, wg_ref[...], preferred_element_type=f32)
        u = jnp.dot(xb, wu_ref[...], preferred_element_type=f32)
        h = (g * jax.nn.sigmoid(g)) * u
        acc_ref[...] += jnp.dot(h.astype(bf16), wd_ref[...], preferred_element_type=f32)

    @pl.when(j == n_chunks - 1)
    def _():
        if fuse_ln:
            out_ref[...] = _layer_norm(DEEPNORM_ALPHA * x_ref[...] + acc_ref[...],
                                       lng[...], lnb[...])
        else:
            _store_row_tiles(out_ref, acc_ref[...])


def _ffn(x2d, tile_expert, n_used, w_gate, w_up, w_down, ln=None, *, tm, tf, name):
    fuse_ln = ln is not None
    rows = x2d.shape[0] if fuse_ln else x2d.shape[0] // ROW_TILE
    d_ff = w_gate.shape[2]
    n_chunks = d_ff // tf
    io_block = (tm, D_MODEL) if fuse_ln else (tm * ROW_TILE, LANES)

    def chunk(i, j, nu):
        return jnp.where(i < nu[0], j, n_chunks - 1)

    in_specs = [pl.BlockSpec(io_block, lambda i, j, te, nu: (i, 0)),
                pl.BlockSpec((None, D_MODEL, tf), lambda i, j, te, nu: (te[i], 0, chunk(i, j, nu))),
                pl.BlockSpec((None, D_MODEL, tf), lambda i, j, te, nu: (te[i], 0, chunk(i, j, nu))),
                pl.BlockSpec((None, tf, D_MODEL), lambda i, j, te, nu: (te[i], chunk(i, j, nu), 0))]
    args = [x2d, w_gate, w_up, w_down]
    if fuse_ln:
        in_specs += [pl.BlockSpec((1, D_MODEL), lambda i, j, te, nu: (0, 0))] * 2
        args += [v.reshape(1, D_MODEL).astype(f32) for v in ln]
    return pl.pallas_call(
        functools.partial(_ffn_kernel, n_chunks=n_chunks, fuse_ln=fuse_ln),
        grid_spec=pltpu.PrefetchScalarGridSpec(
            num_scalar_prefetch=2,
            grid=(rows // tm, n_chunks),
            in_specs=in_specs,
            out_specs=pl.BlockSpec(io_block, lambda i, j, te, nu: (i, 0)),
            scratch_shapes=[pltpu.VMEM((tm, D_MODEL), f32), pltpu.VMEM((tm, D_MODEL), bf16)]),
        out_shape=jax.ShapeDtypeStruct(x2d.shape, f32),
        compiler_params=_params("parallel", "arbitrary"),
        name=name,
    )(tile_expert, n_used, *args)


def _dispatch_kernel(fill_ref, pos_ref, x_hbm, xs_hbm, zero_buf, sem, *, tt, tm):
    step = pl.program_id(0)

    n_fill = fill_ref.shape[0] // 2

    @pl.when(step == 0)
    def _():
        zero_buf[...] = jnp.zeros_like(zero_buf)

        def fill_copy(f):
            start = pl.multiple_of(fill_ref[f] * ROW_TILE, ROW_TILE)
            return pltpu.make_async_copy(zero_buf, xs_hbm.at[pl.ds(start, tm * ROW_TILE)], sem)

        for f in range(n_fill):
            @pl.when(fill_ref[n_fill + f] > 0)
            def _():
                fill_copy(f).start()
        for f in range(n_fill):
            @pl.when(fill_ref[n_fill + f] > 0)
            def _():
                fill_copy(f).wait()

    base = step * tt

    def row_copy(src_row, dst_row):
        src = pl.multiple_of(src_row * ROW_TILE, ROW_TILE)
        dst = pl.multiple_of(dst_row * ROW_TILE, ROW_TILE)
        return pltpu.make_async_copy(x_hbm.at[pl.ds(src, ROW_TILE)],
                                     xs_hbm.at[pl.ds(dst, ROW_TILE)], sem)

    def issue(t, carry):
        for k in range(TOP_K):
            row_copy(base + t, pos_ref[0, TOP_K * t + k]).start()
        return carry

    def drain(t, carry):
        for k in range(TOP_K):
            row_copy(base + t, pos_ref[0, TOP_K * t + k]).wait()
        return carry

    lax.fori_loop(0, tt, issue, 0)
    lax.fori_loop(0, tt, drain, 0)


def _dispatch(x1_rows, pos_blocks, fill, rows_sorted, *, tt, tm):
    n = x1_rows.shape[0] // ROW_TILE
    return pl.pallas_call(
        functools.partial(_dispatch_kernel, tt=tt, tm=tm),
        grid_spec=pltpu.PrefetchScalarGridSpec(
            num_scalar_prefetch=1,
            grid=(n // tt,),
            in_specs=[pl.BlockSpec((None, 1, TOP_K * tt), lambda i, fill: (i, 0, 0),
                                   memory_space=pltpu.SMEM),
                      pl.BlockSpec(memory_space=pl.ANY)],
            out_specs=pl.BlockSpec(memory_space=pl.ANY),
            scratch_shapes=[pltpu.VMEM((tm * ROW_TILE, LANES), f32), pltpu.SemaphoreType.DMA(())]),
        out_shape=jax.ShapeDtypeStruct((rows_sorted * ROW_TILE, LANES), f32),
        compiler_params=_params("arbitrary"),
        name="moe_dispatch",
    )(fill, pos_blocks, x1_rows)


def _combine_kernel(pos_ref, route_ref, x1_ref, ys_hbm, lng, lnb, out_ref, buf, sem, *, tt):
    def row_copy(t, k):
        src = pl.multiple_of(pos_ref[0, TOP_K * t + k] * ROW_TILE, ROW_TILE)
        dst = pl.multiple_of(t * ROW_TILE, ROW_TILE)
        return pltpu.make_async_copy(ys_hbm.at[pl.ds(src, ROW_TILE)],
                                     buf.at[k, pl.ds(dst, ROW_TILE)], sem)

    def issue(t, carry):
        for k in range(TOP_K):
            row_copy(t, k).start()
        return carry

    def drain(t, carry):
        for k in range(TOP_K):
            row_copy(t, k).wait()
        return carry

    lax.fori_loop(0, tt, issue, 0)
    lax.fori_loop(0, tt, drain, 0)

    lane = lax.broadcasted_iota(jnp.int32, (1, LANES), 1)
    route = route_ref[...]
    p1 = jnp.sum(jnp.where(lane == 4, route, 0.0), axis=-1, keepdims=True)
    p2 = jnp.sum(jnp.where(lane == 5, route, 0.0), axis=-1, keepdims=True)
    y = p1 * _load_row_tiles(buf.at[0], tt) + p2 * _load_row_tiles(buf.at[1], tt)
    out_ref[...] = _layer_norm(DEEPNORM_ALPHA * x1_ref[...] + y, lng[...], lnb[...])


def _combine(pos_blocks, route, x1, y_sorted, ln_g, ln_b, *, tt):
    n = x1.shape[0]
    return pl.pallas_call(
        functools.partial(_combine_kernel, tt=tt),
        grid=(n // tt,),
        in_specs=[pl.BlockSpec((None, 1, TOP_K * tt), lambda i: (i, 0, 0), memory_space=pltpu.SMEM),
                  pl.BlockSpec((tt, LANES), lambda i: (i, 0)),
                  pl.BlockSpec((tt, D_MODEL), lambda i: (i, 0)),
                  pl.BlockSpec(memory_space=pl.ANY),
                  pl.BlockSpec((1, D_MODEL), lambda i: (0, 0)),
                  pl.BlockSpec((1, D_MODEL), lambda i: (0, 0))],
        out_specs=pl.BlockSpec((tt, D_MODEL), lambda i: (i, 0)),
        out_shape=jax.ShapeDtypeStruct((n, D_MODEL), f32),
        scratch_shapes=[pltpu.VMEM((TOP_K, tt * ROW_TILE, LANES), f32),
                        pltpu.SemaphoreType.DMA(())],
        compiler_params=_params("arbitrary"),
        name="moe_combine",
    )(pos_blocks, route, x1, y_sorted, ln_g.reshape(1, D_MODEL).astype(f32),
      ln_b.reshape(1, D_MODEL).astype(f32))


def _moe(x1, x1_rows, route, counts, w_gate, w_up, w_down, ln_g, ln_b, *, tm, tf, tt):
    n = x1.shape[0]
    i32 = jnp.int32
    counts = counts[0, :N_EXPERTS].astype(i32)
    padded = (counts + tm - 1) // tm * tm
    ends = jnp.cumsum(padded)
    offsets = ends - padded
    experts = route[:, 0:TOP_K].astype(i32)
    ranks = route[:, TOP_K:2 * TOP_K].astype(i32)
    pos = (offsets[experts] + ranks).reshape(n // tt, 1, TOP_K * tt)
    n_tiles = TOP_K * n // tm + N_EXPERTS
    n_used = (ends[-1] // tm).astype(i32).reshape(1)
    tile_ids = jnp.arange(n_tiles, dtype=i32)
    tile_expert = jnp.sum((tile_ids[:, None] >= (ends // tm)[None, :]).astype(i32), axis=1)
    last_expert = jnp.max(jnp.where(counts > 0, jnp.arange(N_EXPERTS, dtype=i32), 0))
    tile_expert = jnp.minimum(tile_expert, last_expert).astype(i32)
    tail_tiles = n_used[0] + jnp.arange(N_EXPERTS, dtype=i32)
    fill = jnp.concatenate([ends - tm, tail_tiles * tm,
                            (counts > 0).astype(i32), (tail_tiles < n_tiles).astype(i32)]).astype(i32)

    x_sorted = _dispatch(x1_rows, pos, fill, n_tiles * tm, tt=tt, tm=tm)
    y_sorted = _ffn(x_sorted, tile_expert, n_used, w_gate, w_up, w_down, tm=tm, tf=tf,
                    name="ffn_experts")
    return _combine(pos, route, x1, y_sorted, ln_g, ln_b, tt=tt)


def kernel(x, w_in, conv_w, conv_b, w_a, b_a, w_x, b_x, lru_lambda, rel_bias, g_attn, g_lru, w_out, ln1_g, ln1_b, ln2_g, ln2_b, ffn_w_gate, ffn_w_up, ffn_w_down, router_w, moe_w_gate, moe_w_up, moe_w_down):
    batch, seq, _ = x.shape
    n = batch * seq
    h = x.reshape(n, D_MODEL).astype(f32)
    biases = [_band_bias(rel_bias, d) for _, d in DILATED_PATTERNS]
    for layer in range(DEPTH):
        qkv, ug = _in_proj(h, w_in[layer].astype(bf16))
        outs, lses = [], []
        for (_, dilation), bias in zip(DILATED_PATTERNS, biases):
            o, lse = _attention_pattern(qkv, bias, batch, seq, dilation)
            outs.append(o)
            lses.append(lse)
        ylru = _lru_branch(ug, conv_w[layer], conv_b[layer], w_a[layer], b_a[layer], w_x[layer],
                           b_x[layer], lru_lambda[layer], g_lru[layer], batch, seq)
        j = layer // 2
        dense = layer % 2 == 0
        x1, x1_rows, route, counts = _mix_out(
            outs, lses, ylru, h, w_out[layer].astype(bf16), g_attn[layer], ln1_g[layer],
            ln1_b[layer], None if dense else router_w[j])
        if dense:
            tm = 512
            h = _ffn(x1, jnp.zeros((n // tm,), jnp.int32), jnp.full((1,), n // tm, jnp.int32),
                     ffn_w_gate[j][None].astype(bf16), ffn_w_up[j][None].astype(bf16),
                     ffn_w_down[j][None].astype(bf16), (ln2_g[layer], ln2_b[layer]),
                     tm=tm, tf=1408, name="ffn_dense")
        else:
            h = _moe(x1, x1_rows, route, counts, moe_w_gate[j].astype(bf16), moe_w_up[j].astype(bf16),
                     moe_w_down[j].astype(bf16), ln2_g[layer], ln2_b[layer],
                     tm=1024, tf=896, tt=256)
    return h.reshape(batch, seq, D_MODEL).astype(x.dtype)
```

```python
import functools

import numpy as np
import jax
import jax.numpy as jnp
from jax import lax
from jax.experimental import pallas as pl
from jax.experimental.pallas import tpu as pltpu

D_MODEL = 1024
N_HEADS = 8
HEAD_DIM = 64
D_ATTN = N_HEADS * HEAD_DIM
D_LRU = 512
N_LRU_BLOCKS = 8
LRU_BLOCK = D_LRU // N_LRU_BLOCKS
CONV_WIDTH = 4
LRU_C = 8.0
DILATED_PATTERNS = ((128, 1), (512, 4), (2048, 16))
ATTN_BLOCK = 128
N_BUCKETS = 32
MAX_DISTANCE = 2048
D_IN = 3 * D_ATTN + 2 * D_LRU
N_EXPERTS = 8
TOP_K = 2
DEPTH = 2
DEEPNORM_ALPHA = (2.0 * DEPTH) ** 0.25
LN_EPS = 1e-5
RMS_EPS = 1e-6
NEG_INF = -1e30

LANES = 128
DMA_UNROLL = 8
VMEM_LIMIT_BYTES = 56 * 1024 * 1024

f32 = jnp.float32
bf16 = jnp.bfloat16


def _params(*semantics):
    return pltpu.CompilerParams(dimension_semantics=semantics,
                                vmem_limit_bytes=VMEM_LIMIT_BYTES)


def _in_proj_kernel(x_ref, w_ref, qkv_ref, ug_ref):
    xb = x_ref[...].astype(bf16)
    n_qkv = qkv_ref.shape[1]
    step = 512
    for c in range(0, D_IN, step):
        p = jnp.dot(xb, w_ref[:, c:c + step], preferred_element_type=f32)
        if c < n_qkv:
            qkv_ref[:, c:c + step] = p.astype(bf16)
        else:
            ug_ref[:, c - n_qkv:c - n_qkv + step] = p


def _in_proj(x2d, w_in_bf16, tm=512):
    n = x2d.shape[0]
    return pl.pallas_call(
        _in_proj_kernel,
        grid=(n // tm,),
        in_specs=[pl.BlockSpec((tm, D_MODEL), lambda i: (i, 0)),
                  pl.BlockSpec((D_MODEL, D_IN), lambda i: (0, 0))],
        out_specs=[pl.BlockSpec((tm, 3 * D_ATTN), lambda i: (i, 0)),
                   pl.BlockSpec((tm, 2 * D_LRU), lambda i: (i, 0))],
        out_shape=[jax.ShapeDtypeStruct((n, 3 * D_ATTN), bf16),
                   jax.ShapeDtypeStruct((n, 2 * D_LRU), f32)],
        compiler_params=_params("parallel"),
        name="in_proj",
    )(x2d, w_in_bf16)


def _t5_bucket(dist):
    max_exact = N_BUCKETS // 2
    d = np.maximum(dist, 1).astype(np.float32)
    large = max_exact + (np.log(d / max_exact) / np.log(MAX_DISTANCE / max_exact)
                         * (N_BUCKETS - max_exact)).astype(np.int32)
    large = np.minimum(large, N_BUCKETS - 1)
    return np.where(dist < max_exact, dist, large).astype(np.int32)


def _band_bias(rel_bias, dilation):
    nk = ATTN_BLOCK
    qi = np.arange(nk)[:, None]
    kj = np.arange(2 * nk)[None, :]
    delta = qi + nk - kj
    band = (delta >= 0) & (delta <= nk)
    bucket = _t5_bucket(np.clip(delta, 0, nk) * dilation)
    bias = jnp.transpose(rel_bias.astype(f32)[bucket], (2, 0, 1))
    return jnp.where(jnp.asarray(band)[None], bias, NEG_INF)


def _attn_kernel(q_ref, k_ref, v_ref, bias_ref, o_ref, lse_ref, *, n_blocks):
    nk = ATTN_BLOCK
    lane = lax.broadcasted_iota(jnp.int32, (1, LANES), 1)
    head0 = lane < HEAD_DIM

    def one_block(row0, k, v, biases):
        q = q_ref[pl.ds(row0, nk), :]
        outs, lses = [], []
        for h in range(2):
            keep = head0 if h == 0 else jnp.logical_not(head0)
            qh = jnp.where(keep, q, jnp.zeros_like(q)) * jnp.asarray(HEAD_DIM ** -0.5, bf16)
            s = lax.dot_general(qh, k, (((1,), (1,)), ((), ())),
                                preferred_element_type=f32) + biases[h]
            m = jnp.max(s, axis=-1, keepdims=True)
            p = jnp.exp(s - m)
            l = jnp.sum(p, axis=-1, keepdims=True)
            pv = jnp.dot(p.astype(bf16), v, preferred_element_type=f32)
            outs.append(pv / l)
            lses.append(m + jnp.log(l))
        o_ref[pl.ds(row0, nk), :] = jnp.where(head0, outs[0], outs[1]).astype(o_ref.dtype)
        lse_ref[pl.ds(row0, nk), :] = jnp.where(head0, lses[0], lses[1])

    one_block(0, k_ref[0:nk, :], v_ref[0:nk, :],
              [bias_ref[0, :, nk:], bias_ref[1, :, nk:]])

    def body(i, carry):
        row0 = pl.multiple_of(i * nk, nk)
        prev0 = pl.multiple_of((i - 1) * nk, nk)
        one_block(row0, k_ref[pl.ds(prev0, 2 * nk), :], v_ref[pl.ds(prev0, 2 * nk), :],
                  [bias_ref[0], bias_ref[1]])
        return carry

    if n_blocks > 1:
        lax.fori_loop(1, n_blocks, body, 0)


def _attention_pattern(qkv, band_bias, batch, seq, dilation):
    length = seq // dilation
    n_blocks = length // ATTN_BLOCK
    width = 3 * D_ATTN
    view = qkv.reshape(batch, length, dilation * width)
    pairs = D_ATTN // LANES
    per_row = width // LANES

    def spec(offset):
        return pl.BlockSpec((None, length, LANES),
                            lambda b, r, hp: (b, 0, r * per_row + offset * pairs + hp))

    out_spec = pl.BlockSpec((None, length, LANES), lambda b, r, hp: (b, 0, r * pairs + hp))
    o, lse = pl.pallas_call(
        functools.partial(_attn_kernel, n_blocks=n_blocks),
        grid=(batch, dilation, pairs),
        in_specs=[spec(0), spec(1), spec(2),
                  pl.BlockSpec((2, ATTN_BLOCK, 2 * ATTN_BLOCK), lambda b, r, hp: (hp, 0, 0))],
        out_specs=[out_spec, out_spec],
        out_shape=[jax.ShapeDtypeStruct((batch, length, dilation * D_ATTN), bf16),
                   jax.ShapeDtypeStruct((batch, length, dilation * D_ATTN), f32)],
        compiler_params=_params("parallel", "parallel", "parallel"),
        name=f"attn_d{dilation}",
    )(view, view, view, band_bias)
    return o.reshape(batch * seq, D_ATTN), lse.reshape(batch * seq, D_ATTN)


def _gelu_tanh(x):
    return 0.5 * x * (1.0 + jnp.tanh(np.sqrt(2.0 / np.pi) * (x + 0.044715 * x * x * x)))


def _lru_kernel(ug_ref, convw_ref, convb_ref, wg_ref, ba_ref, bx_ref, lam_ref, g_ref,
                y_ref, ubuf, hcarry, *, ts):
    pad = 8
    t = pl.program_id(1)

    @pl.when(t == 0)
    def _():
        ubuf[0:pad, :] = jnp.zeros((pad, D_LRU), f32)
        hcarry[...] = jnp.zeros_like(hcarry)

    @pl.when(t > 0)
    def _():
        ubuf[0:pad, :] = ubuf[ts:ts + pad, :]

    ubuf[pad:pad + ts, :] = ug_ref[:, 0:D_LRU]
    gate = ug_ref[:, D_LRU:2 * D_LRU]

    u = convb_ref[...] + convw_ref[CONV_WIDTH - 1:CONV_WIDTH, :] * ubuf[pad:pad + ts, :]
    for w in range(CONV_WIDTH - 1):
        back = CONV_WIDTH - 1 - w
        u = u + convw_ref[w:w + 1, :] * ubuf[pad - back:pad - back + ts, :]

    gates = jnp.dot(u.astype(bf16), wg_ref[...], preferred_element_type=f32)
    r = jax.nn.sigmoid(gates[:, 0:D_LRU] + ba_ref[...])
    i = jax.nn.sigmoid(gates[:, D_LRU:2 * D_LRU] + bx_ref[...])
    neg_lam = -lam_ref[...]
    softplus = jnp.maximum(neg_lam, 0.0) + jnp.log1p(jnp.exp(-jnp.abs(neg_lam)))
    log_a = (-LRU_C) * r * softplus
    a = jnp.exp(log_a)
    b = jnp.sqrt(-jnp.tanh(log_a) * (1.0 + a * a)) * (i * u)

    row = lax.broadcasted_iota(jnp.int32, (ts, 1), 0)
    shift = 1
    while shift < ts:
        live = row >= shift
        a_prev = jnp.where(live, pltpu.roll(a, shift, 0), 1.0)
        b_prev = jnp.where(live, pltpu.roll(b, shift, 0), 0.0)
        b = a * b_prev + b
        a = a * a_prev
        shift *= 2
    h = a * hcarry[...] + b
    hcarry[...] = h[ts - 1:ts, :]

    y = _gelu_tanh(gate) * h
    y = y * lax.rsqrt(jnp.mean(y * y, axis=-1, keepdims=True) + RMS_EPS) * g_ref[...]
    y_ref[...] = y.astype(y_ref.dtype)


def _block_diag(w):
    g, i, j = w.shape
    eye = jnp.eye(g, dtype=w.dtype)
    return jnp.einsum('gij,gh->gihj', w, eye).reshape(g * i, g * j)


def _lru_branch(ug, conv_w, conv_b, w_a, b_a, w_x, b_x, lam, g_lru, batch, seq, ts=256):
    wg = jnp.concatenate([_block_diag(w_a), _block_diag(w_x)], axis=1).astype(bf16)
    row = lambda v: v.reshape(1, D_LRU).astype(f32)
    const = lambda shape: pl.BlockSpec(shape, lambda b, t: (0, 0))
    view = ug.reshape(batch, seq, 2 * D_LRU)
    y = pl.pallas_call(
        functools.partial(_lru_kernel, ts=ts),
        grid=(batch, seq // ts),
        in_specs=[pl.BlockSpec((None, ts, 2 * D_LRU), lambda b, t: (b, t, 0)),
                  const((CONV_WIDTH, D_LRU)), const((1, D_LRU)),
                  const((D_LRU, 2 * D_LRU)), const((1, D_LRU)), const((1, D_LRU)),
                  const((1, D_LRU)), const((1, D_LRU))],
        out_specs=pl.BlockSpec((None, ts, D_LRU), lambda b, t: (b, t, 0)),
        out_shape=jax.ShapeDtypeStruct((batch, seq, D_LRU), bf16),
        scratch_shapes=[pltpu.VMEM((ts + 16, D_LRU), f32), pltpu.VMEM((1, D_LRU), f32)],
        compiler_params=_params("parallel", "arbitrary"),
        name="rglru",
    )(view, conv_w.reshape(CONV_WIDTH, D_LRU).astype(f32), row(conv_b), wg, row(b_a), row(b_x),
      row(lam), row(g_lru))
    return y.reshape(batch * seq, D_LRU)


ROW_TILE = D_MODEL // LANES


def _store_row_tiles(ref, rows):
    t = rows.shape[0]
    for s in range(ROW_TILE):
        ref[pl.ds(s, t, stride=ROW_TILE), :] = rows[:, s * LANES:(s + 1) * LANES]


def _load_row_tiles(ref, t):
    return jnp.concatenate([ref[pl.ds(s, t, stride=ROW_TILE), :] for s in range(ROW_TILE)],
                           axis=-1)


def _layer_norm(z, g, b):
    mu = jnp.mean(z, axis=-1, keepdims=True)
    zc = z - mu
    var = jnp.mean(zc * zc, axis=-1, keepdims=True)
    return zc * lax.rsqrt(var + LN_EPS) * g + b


def _mix_out_kernel(o1, o2, o3, l1, l2, l3, ylru, x_ref, w_ref, gattn, lng, lnb, *rest,
                    with_router):
    if with_router:
        rw_ref, x1_ref, x1_rows_ref, route_ref, count_ref = rest
    else:
        (x1_ref,) = rest
    la, lb, lc = l1[...], l2[...], l3[...]
    m = jnp.maximum(jnp.maximum(la, lb), lc)
    ea, eb, ec = jnp.exp(la - m), jnp.exp(lb - m), jnp.exp(lc - m)
    attn = (ea * o1[...].astype(f32) + eb * o2[...].astype(f32) + ec * o3[...].astype(f32)) \
        / (ea + eb + ec)
    attn = attn * lax.rsqrt(jnp.mean(attn * attn, axis=-1, keepdims=True) + RMS_EPS) * gattn[...]
    y = jnp.dot(attn.astype(bf16), w_ref[0:D_ATTN, :], preferred_element_type=f32)
    y = y + jnp.dot(ylru[...], w_ref[D_ATTN:, :], preferred_element_type=f32)
    x1 = _layer_norm(DEEPNORM_ALPHA * x_ref[...] + y, lng[...], lnb[...])
    x1_ref[...] = x1
    if with_router:
        _store_row_tiles(x1_rows_ref, x1)
        lane = lax.broadcasted_iota(jnp.int32, (1, LANES), 1)
        logits = jnp.dot(x1, rw_ref[...], preferred_element_type=f32,
                         precision=lax.Precision.HIGHEST)
        logits = jnp.where(lane < N_EXPERTS, logits, -jnp.inf)
        v1 = jnp.max(logits, axis=-1, keepdims=True)
        i1 = jnp.min(jnp.where(logits == v1, lane, LANES), axis=-1, keepdims=True)
        rest_logits = jnp.where(lane == i1, -jnp.inf, logits)
        v2 = jnp.max(rest_logits, axis=-1, keepdims=True)
        i2 = jnp.min(jnp.where(rest_logits == v2, lane, LANES), axis=-1, keepdims=True)
        e2 = jnp.exp(v2 - v1)
        p1 = 1.0 / (1.0 + e2)
        p2 = e2 / (1.0 + e2)

        @pl.when(pl.program_id(0) == 0)
        def _():
            count_ref[...] = jnp.zeros_like(count_ref)

        tm = x1.shape[0]
        chosen = jnp.logical_or(lane == i1, lane == i2)
        tri = (lax.broadcasted_iota(jnp.int32, (tm, tm), 0)
               > lax.broadcasted_iota(jnp.int32, (tm, tm), 1)).astype(bf16)
        rank = count_ref[...] + jnp.dot(tri, chosen.astype(bf16), preferred_element_type=f32)
        count_ref[...] += jnp.sum(chosen.astype(f32), axis=0, keepdims=True)
        r1 = jnp.sum(jnp.where(lane == i1, rank, 0.0), axis=-1, keepdims=True)
        r2 = jnp.sum(jnp.where(lane == i2, rank, 0.0), axis=-1, keepdims=True)
        fields = (i1.astype(f32), i2.astype(f32), r1, r2, p1, p2)
        route = jnp.zeros((tm, LANES), f32)
        for k, val in enumerate(fields):
            route = jnp.where(lane == k, val, route)
        route_ref[...] = route


def _mix_out(outs, lses, ylru, x2d, w_out_bf16, g_attn, ln_g, ln_b, router_w=None, tm=512):
    n = x2d.shape[0]
    with_router = router_w is not None
    tile = lambda width: pl.BlockSpec((tm, width), lambda i: (i, 0))
    const = lambda shape: pl.BlockSpec(shape, lambda i: (0, 0))
    in_specs = [tile(D_ATTN)] * 3 + [tile(D_ATTN)] * 3 + [tile(D_LRU), tile(D_MODEL),
                const((D_MODEL, D_MODEL)), const((1, D_ATTN)), const((1, D_MODEL)),
                const((1, D_MODEL))]
    args = [*outs, *lses, ylru, x2d, w_out_bf16, g_attn.reshape(1, D_ATTN).astype(f32),
            ln_g.reshape(1, D_MODEL).astype(f32), ln_b.reshape(1, D_MODEL).astype(f32)]
    out_specs = [tile(D_MODEL)]
    out_shape = [jax.ShapeDtypeStruct((n, D_MODEL), f32)]
    if with_router:
        rw = jnp.zeros((D_MODEL, LANES), f32).at[:, :N_EXPERTS].set(router_w.astype(f32))
        in_specs.append(const((D_MODEL, LANES)))
        args.append(rw)
        out_specs += [pl.BlockSpec((tm * ROW_TILE, LANES), lambda i: (i, 0)), tile(LANES),
                      const((1, LANES))]
        out_shape += [jax.ShapeDtypeStruct((n * ROW_TILE, LANES), f32),
                      jax.ShapeDtypeStruct((n, LANES), f32), jax.ShapeDtypeStruct((1, LANES), f32)]
    res = pl.pallas_call(
        functools.partial(_mix_out_kernel, with_router=with_router),
        grid=(n // tm,),
        in_specs=in_specs,
        out_specs=out_specs,
        out_shape=out_shape,
        compiler_params=_params("arbitrary" if with_router else "parallel"),
        name="mix_out_router" if with_router else "mix_out",
    )(*args)
    return res if with_router else (res[0], None, None, None)


def _ffn_kernel(tile_expert_ref, n_used_ref, x_ref, wg_ref, wu_ref, wd_ref, *rest,
                n_chunks, fuse_ln):
    if fuse_ln:
        lng, lnb, out_ref, acc_ref, xb_ref = rest
    else:
        out_ref, acc_ref, xb_ref = rest
    del tile_expert_ref
    j = pl.program_id(1)
    used = pl.program_id(0) < n_used_ref[0]

    @pl.when(j == 0)
    def _():
        acc_ref[...] = jnp.zeros_like(acc_ref)

    @pl.when(used & (j == 0))
    def _():
        if fuse_ln:
            xb_ref[...] = x_ref[...].astype(bf16)
        else:
            xb_ref[...] = _load_row_tiles(x_ref, xb_ref.shape[0]).astype(bf16)

    @pl.when(used)
    def _():
        xb = xb_ref[...]
        g = jnp.dot(xb, wg_ref[...], preferred_element_type=f32)
        u = jnp.dot(xb, wu_ref[...], preferred_element_type=f32)
        h = (g * jax.nn.sigmoid(g)) * u
        acc_ref[...] += jnp.dot(h.astype(bf16), wd_ref[...], preferred_element_type=f32)

    @pl.when(j == n_chunks - 1)
    def _():
        if fuse_ln:
            out_ref[...] = _layer_norm(DEEPNORM_ALPHA * x_ref[...] + acc_ref[...],
                                       lng[...], lnb[...])
        else:
            _store_row_tiles(out_ref, acc_ref[...])


def _ffn(x2d, tile_expert, n_used, w_gate, w_up, w_down, ln=None, *, tm, tf, name):
    fuse_ln = ln is not None
    rows = x2d.shape[0] if fuse_ln else x2d.shape[0] // ROW_TILE
    d_ff = w_gate.shape[2]
    n_chunks = d_ff // tf
    io_block = (tm, D_MODEL) if fuse_ln else (tm * ROW_TILE, LANES)

    def chunk(i, j, nu):
        return jnp.where(i < nu[0], j, n_chunks - 1)

    in_specs = [pl.BlockSpec(io_block, lambda i, j, te, nu: (i, 0)),
                pl.BlockSpec((None, D_MODEL, tf), lambda i, j, te, nu: (te[i], 0, chunk(i, j, nu))),
                pl.BlockSpec((None, D_MODEL, tf), lambda i, j, te, nu: (te[i], 0, chunk(i, j, nu))),
                pl.BlockSpec((None, tf, D_MODEL), lambda i, j, te, nu: (te[i], chunk(i, j, nu), 0))]
    args = [x2d, w_gate, w_up, w_down]
    if fuse_ln:
        in_specs += [pl.BlockSpec((1, D_MODEL), lambda i, j, te, nu: (0, 0))] * 2
        args += [v.reshape(1, D_MODEL).astype(f32) for v in ln]
    return pl.pallas_call(
        functools.partial(_ffn_kernel, n_chunks=n_chunks, fuse_ln=fuse_ln),
        grid_spec=pltpu.PrefetchScalarGridSpec(
            num_scalar_prefetch=2,
            grid=(rows // tm, n_chunks),
            in_specs=in_specs,
            out_specs=pl.BlockSpec(io_block, lambda i, j, te, nu: (i, 0)),
            scratch_shapes=[pltpu.VMEM((tm, D_MODEL), f32), pltpu.VMEM((tm, D_MODEL), bf16)]),
        out_shape=jax.ShapeDtypeStruct(x2d.shape, f32),
        compiler_params=_params("parallel", "arbitrary"),
        name=name,
    )(tile_expert, n_used, *args)


def _dispatch_kernel(fill_ref, pos_ref, x_ref, xs_hbm, zero_buf, sem, *, tt, tm):
    step = pl.program_id(0)

    n_fill = fill_ref.shape[0] // 2

    @pl.when(step == 0)
    def _():
        zero_buf[...] = jnp.zeros_like(zero_buf)

        def fill_copy(f):
            start = pl.multiple_of(fill_ref[f] * ROW_TILE, ROW_TILE)
            return pltpu.make_async_copy(zero_buf, xs_hbm.at[pl.ds(start, tm * ROW_TILE)], sem)

        for f in range(n_fill):
            @pl.when(fill_ref[n_fill + f] > 0)
            def _():
                fill_copy(f).start()
        for f in range(n_fill):
            @pl.when(fill_ref[n_fill + f] > 0)
            def _():
                fill_copy(f).wait()

    def issue(t, carry):
        src = pl.multiple_of(t * ROW_TILE, ROW_TILE)
        for k in range(TOP_K):
            dst = pl.multiple_of(pos_ref[0, TOP_K * t + k] * ROW_TILE, ROW_TILE)
            pltpu.make_async_copy(x_ref.at[pl.ds(src, ROW_TILE)],
                                  xs_hbm.at[pl.ds(dst, ROW_TILE)], sem).start()
        return carry

    lax.fori_loop(0, tt, issue, 0, unroll=DMA_UNROLL)
    for _ in range(TOP_K):
        pltpu.make_async_copy(x_ref, xs_hbm.at[pl.ds(0, tt * ROW_TILE)], sem).wait()


def _dispatch(x1_rows, pos_blocks, fill, rows_sorted, *, tt, tm):
    n = x1_rows.shape[0] // ROW_TILE
    return pl.pallas_call(
        functools.partial(_dispatch_kernel, tt=tt, tm=tm),
        grid_spec=pltpu.PrefetchScalarGridSpec(
            num_scalar_prefetch=1,
            grid=(n // tt,),
            in_specs=[pl.BlockSpec((None, 1, TOP_K * tt), lambda i, fill: (i, 0, 0),
                                   memory_space=pltpu.SMEM),
                      pl.BlockSpec((tt * ROW_TILE, LANES), lambda i, fill: (i, 0))],
            out_specs=pl.BlockSpec(memory_space=pl.ANY),
            scratch_shapes=[pltpu.VMEM((tm * ROW_TILE, LANES), f32), pltpu.SemaphoreType.DMA(())]),
        out_shape=jax.ShapeDtypeStruct((rows_sorted * ROW_TILE, LANES), f32),
        compiler_params=_params("arbitrary"),
        name="moe_dispatch",
    )(fill, pos_blocks, x1_rows)


def _combine_kernel(pos_ref, route_ref, x1_ref, ys_hbm, lng, lnb, out_ref, buf, sem, *, tt):
    def issue(t, carry):
        dst = pl.multiple_of(t * ROW_TILE, ROW_TILE)
        for k in range(TOP_K):
            src = pl.multiple_of(pos_ref[0, TOP_K * t + k] * ROW_TILE, ROW_TILE)
            pltpu.make_async_copy(ys_hbm.at[pl.ds(src, ROW_TILE)],
                                  buf.at[k, pl.ds(dst, ROW_TILE)], sem).start()
        return carry

    lax.fori_loop(0, tt, issue, 0, unroll=DMA_UNROLL)
    for k in range(TOP_K):
        pltpu.make_async_copy(ys_hbm.at[pl.ds(0, tt * ROW_TILE)], buf.at[k], sem).wait()

    lane = lax.broadcasted_iota(jnp.int32, (1, LANES), 1)
    route = route_ref[...]
    p1 = jnp.sum(jnp.where(lane == 4, route, 0.0), axis=-1, keepdims=True)
    p2 = jnp.sum(jnp.where(lane == 5, route, 0.0), axis=-1, keepdims=True)
    y = p1 * _load_row_tiles(buf.at[0], tt) + p2 * _load_row_tiles(buf.at[1], tt)
    out_ref[...] = _layer_norm(DEEPNORM_ALPHA * x1_ref[...] + y, lng[...], lnb[...])


def _combine(pos_blocks, route, x1, y_sorted, ln_g, ln_b, *, tt):
    n = x1.shape[0]
    return pl.pallas_call(
        functools.partial(_combine_kernel, tt=tt),
        grid=(n // tt,),
        in_specs=[pl.BlockSpec((None, 1, TOP_K * tt), lambda i: (i, 0, 0), memory_space=pltpu.SMEM),
                  pl.BlockSpec((tt, LANES), lambda i: (i, 0)),
                  pl.BlockSpec((tt, D_MODEL), lambda i: (i, 0)),
                  pl.BlockSpec(memory_space=pl.ANY),
                  pl.BlockSpec((1, D_MODEL), lambda i: (0, 0)),
                  pl.BlockSpec((1, D_MODEL), lambda i: (0, 0))],
        out_specs=pl.BlockSpec((tt, D_MODEL), lambda i: (i, 0)),
        out_shape=jax.ShapeDtypeStruct((n, D_MODEL), f32),
        scratch_shapes=[pltpu.VMEM((TOP_K, tt * ROW_TILE, LANES), f32),
                        pltpu.SemaphoreType.DMA(())],
        compiler_params=_params("arbitrary"),
        name="moe_combine",
    )(pos_blocks, route, x1, y_sorted, ln_g.reshape(1, D_MODEL).astype(f32),
      ln_b.reshape(1, D_MODEL).astype(f32))


def _moe(x1, x1_rows, route, counts, w_gate, w_up, w_down, ln_g, ln_b, *, tm, tf, tt):
    n = x1.shape[0]
    i32 = jnp.int32
    counts = counts[0, :N_EXPERTS].astype(i32)
    padded = (counts + tm - 1) // tm * tm
    ends = jnp.cumsum(padded)
    offsets = ends - padded
    experts = route[:, 0:TOP_K].astype(i32)
    ranks = route[:, TOP_K:2 * TOP_K].astype(i32)
    pos = (offsets[experts] + ranks).reshape(n // tt, 1, TOP_K * tt)
    n_tiles = TOP_K * n // tm + N_EXPERTS
    n_used = (ends[-1] // tm).astype(i32).reshape(1)
    tile_ids = jnp.arange(n_tiles, dtype=i32)
    tile_expert = jnp.sum((tile_ids[:, None] >= (ends // tm)[None, :]).astype(i32), axis=1)
    last_expert = jnp.max(jnp.where(counts > 0, jnp.arange(N_EXPERTS, dtype=i32), 0))
    tile_expert = jnp.minimum(tile_expert, last_expert).astype(i32)
    tail_tiles = n_used[0] + jnp.arange(N_EXPERTS, dtype=i32)
    fill = jnp.concatenate([ends - tm, tail_tiles * tm,
                            (counts > 0).astype(i32), (tail_tiles < n_tiles).astype(i32)]).astype(i32)

    x_sorted = _dispatch(x1_rows, pos, fill, n_tiles * tm, tt=tt, tm=tm)
    y_sorted = _ffn(x_sorted, tile_expert, n_used, w_gate, w_up, w_down, tm=tm, tf=tf,
                    name="ffn_experts")
    return _combine(pos, route, x1, y_sorted, ln_g, ln_b, tt=tt)


def kernel(x, w_in, conv_w, conv_b, w_a, b_a, w_x, b_x, lru_lambda, rel_bias, g_attn, g_lru, w_out, ln1_g, ln1_b, ln2_g, ln2_b, ffn_w_gate, ffn_w_up, ffn_w_down, router_w, moe_w_gate, moe_w_up, moe_w_down):
    batch, seq, _ = x.shape
    n = batch * seq
    h = x.reshape(n, D_MODEL).astype(f32)
    biases = [_band_bias(rel_bias, d) for _, d in DILATED_PATTERNS]
    for layer in range(DEPTH):
        qkv, ug = _in_proj(h, w_in[layer].astype(bf16))
        outs, lses = [], []
        for (_, dilation), bias in zip(DILATED_PATTERNS, biases):
            o, lse = _attention_pattern(qkv, bias, batch, seq, dilation)
            outs.append(o)
            lses.append(lse)
        ylru = _lru_branch(ug, conv_w[layer], conv_b[layer], w_a[layer], b_a[layer], w_x[layer],
                           b_x[layer], lru_lambda[layer], g_lru[layer], batch, seq)
        j = layer // 2
        dense = layer % 2 == 0
        x1, x1_rows, route, counts = _mix_out(
            outs, lses, ylru, h, w_out[layer].astype(bf16), g_attn[layer], ln1_g[layer],
            ln1_b[layer], None if dense else router_w[j])
        if dense:
            tm = 512
            h = _ffn(x1, jnp.zeros((n // tm,), jnp.int32), jnp.full((1,), n // tm, jnp.int32),
                     ffn_w_gate[j][None].astype(bf16), ffn_w_up[j][None].astype(bf16),
                     ffn_w_down[j][None].astype(bf16), (ln2_g[layer], ln2_b[layer]),
                     tm=tm, tf=1408, name="ffn_dense")
        else:
            h = _moe(x1, x1_rows, route, counts, moe_w_gate[j].astype(bf16), moe_w_up[j].astype(bf16),
                     moe_w_down[j].astype(bf16), ln2_g[layer], ln2_b[layer],
                     tm=1024, tf=896, tt=256)
    return h.reshape(batch, seq, D_MODEL).astype(x.dtype)
```

```python
import functools

import numpy as np
import jax
import jax.numpy as jnp
from jax import lax
from jax.experimental import pallas as pl
from jax.experimental.pallas import tpu as pltpu

D_MODEL = 1024
N_HEADS = 8
HEAD_DIM = 64
D_ATTN = N_HEADS * HEAD_DIM
D_LRU = 512
N_LRU_BLOCKS = 8
LRU_BLOCK = D_LRU // N_LRU_BLOCKS
CONV_WIDTH = 4
LRU_C = 8.0
DILATED_PATTERNS = ((128, 1), (512, 4), (2048, 16))
ATTN_BLOCK = 128
N_BUCKETS = 32
MAX_DISTANCE = 2048
D_IN = 3 * D_ATTN + 2 * D_LRU
N_EXPERTS = 8
TOP_K = 2
DEPTH = 2
DEEPNORM_ALPHA = (2.0 * DEPTH) ** 0.25
LN_EPS = 1e-5
RMS_EPS = 1e-6
NEG_INF = -1e30

LANES = 128
DMA_UNROLL = 8
ATTN_UNROLL = 2
VMEM_LIMIT_BYTES = 56 * 1024 * 1024

f32 = jnp.float32
bf16 = jnp.bfloat16


def _params(*semantics):
    return pltpu.CompilerParams(dimension_semantics=semantics,
                                vmem_limit_bytes=VMEM_LIMIT_BYTES)


def _in_proj_kernel(x_ref, w_ref, qkv_ref, ug_ref):
    xb = x_ref[...].astype(bf16)
    n_qkv = qkv_ref.shape[1]
    step = 512
    for c in range(0, D_IN, step):
        p = jnp.dot(xb, w_ref[:, c:c + step], preferred_element_type=f32)
        if c < n_qkv:
            qkv_ref[:, c:c + step] = p
        else:
            ug_ref[:, c - n_qkv:c - n_qkv + step] = p


def _in_proj(x2d, w_in_bf16, tm=512):
    n = x2d.shape[0]
    return pl.pallas_call(
        _in_proj_kernel,
        grid=(n // tm,),
        in_specs=[pl.BlockSpec((tm, D_MODEL), lambda i: (i, 0)),
                  pl.BlockSpec((D_MODEL, D_IN), lambda i: (0, 0))],
        out_specs=[pl.BlockSpec((tm, 3 * D_ATTN), lambda i: (i, 0)),
                   pl.BlockSpec((tm, 2 * D_LRU), lambda i: (i, 0))],
        out_shape=[jax.ShapeDtypeStruct((n, 3 * D_ATTN), f32),
                   jax.ShapeDtypeStruct((n, 2 * D_LRU), f32)],
        compiler_params=_params("parallel"),
        name="in_proj",
    )(x2d, w_in_bf16)


def _t5_bucket(dist):
    max_exact = N_BUCKETS // 2
    d = np.maximum(dist, 1).astype(np.float32)
    large = max_exact + (np.log(d / max_exact) / np.log(MAX_DISTANCE / max_exact)
                         * (N_BUCKETS - max_exact)).astype(np.int32)
    large = np.minimum(large, N_BUCKETS - 1)
    return np.where(dist < max_exact, dist, large).astype(np.int32)


def _band_bias(rel_bias, dilation):
    nk = ATTN_BLOCK
    qi = np.arange(nk)[:, None]
    kj = np.arange(2 * nk)[None, :]
    delta = qi + nk - kj
    band = (delta >= 0) & (delta <= nk)
    bucket = _t5_bucket(np.clip(delta, 0, nk) * dilation)
    bias = jnp.transpose(rel_bias.astype(f32)[bucket], (2, 0, 1))
    valid = np.stack([band, band & (kj >= nk)])[:, None]
    return jnp.where(jnp.asarray(valid), bias[None], NEG_INF)


def _attn_kernel(q_ref, k_ref, v_ref, bias_ref, o_ref, qs, ks, vs, m_s, l_s, acc_s, *, seq):
    nk = ATTN_BLOCK
    n_blocks = seq // nk
    chunk = 2 * nk
    lane = lax.broadcasted_iota(jnp.int32, (1, LANES), 1)
    head0 = lane < HEAD_DIM

    m_s[...] = jnp.full_like(m_s, NEG_INF)
    l_s[...] = jnp.zeros_like(l_s)
    acc_s[...] = jnp.zeros_like(acc_s)
    ks[0:nk, :] = jnp.zeros((nk, LANES), bf16)
    vs[0:nk, :] = jnp.zeros((nk, LANES), bf16)

    for p, (_, d) in enumerate(DILATED_PATTERNS):
        length = seq // d
        nb = length // nk
        chunks_per_residue = length // chunk

        def gather(t, carry):
            r = t // chunks_per_residue
            c = t % chunks_per_residue
            src = pl.ds(r + d * chunk * c, chunk, stride=d)
            dst = pl.multiple_of(t * chunk, chunk)
            qs[pl.ds(dst, chunk), :] = (q_ref[src, :] * (HEAD_DIM ** -0.5)).astype(bf16)
            ks[pl.ds(nk + dst, chunk), :] = k_ref[src, :].astype(bf16)
            vs[pl.ds(nk + dst, chunk), :] = v_ref[src, :].astype(bf16)
            return carry

        lax.fori_loop(0, seq // chunk, gather, 0)

        def block(b, carry):
            r = b // nb
            i = b % nb
            first = jnp.asarray(i == 0, jnp.int32)
            row0 = pl.multiple_of(b * nk, nk)
            q = qs[pl.ds(row0, nk), :]
            k = ks[pl.ds(row0, 2 * nk), :]
            v = vs[pl.ds(row0, 2 * nk), :]
            ms, ls, pvs = [], [], []
            for h in range(2):
                keep = head0 if h == 0 else jnp.logical_not(head0)
                qh = jnp.where(keep, q, jnp.zeros_like(q))
                s = lax.dot_general(qh, k, (((1,), (1,)), ((), ())),
                                    preferred_element_type=f32) + bias_ref[p, first, h]
                m = jnp.max(s, axis=-1, keepdims=True)
                e = jnp.exp(s - m)
                ms.append(m)
                ls.append(jnp.sum(e, axis=-1, keepdims=True))
                pvs.append(jnp.dot(e.astype(bf16), v, preferred_element_type=f32))
            m_blk = jnp.where(head0, ms[0], ms[1])
            l_blk = jnp.where(head0, ls[0], ls[1])
            pv_blk = jnp.where(head0, pvs[0], pvs[1])
            tok = pl.ds(r + d * nk * i, nk, stride=d)
            m_old = m_s[tok, :]
            m_new = jnp.maximum(m_old, m_blk)
            w_old = jnp.exp(m_old - m_new)
            w_blk = jnp.exp(m_blk - m_new)
            l_s[tok, :] = w_old * l_s[tok, :] + w_blk * l_blk
            acc_s[tok, :] = w_old * acc_s[tok, :] + w_blk * pv_blk
            m_s[tok, :] = m_new
            return carry

        lax.fori_loop(0, n_blocks, block, 0, unroll=ATTN_UNROLL)

    def finish(t, carry):
        rows = pl.ds(pl.multiple_of(t * chunk, chunk), chunk)
        o_ref[rows, :] = (acc_s[rows, :] / l_s[rows, :]).astype(o_ref.dtype)
        return carry

    lax.fori_loop(0, seq // chunk, finish, 0)


def _attention(qkv, band_bias, batch, seq):
    view = qkv.reshape(batch, seq, 3 * D_ATTN)
    pairs = D_ATTN // LANES

    def spec(offset):
        return pl.BlockSpec((None, seq, LANES), lambda b, hp: (b, 0, offset * pairs + hp))

    n_pat = len(DILATED_PATTERNS)
    o = pl.pallas_call(
        functools.partial(_attn_kernel, seq=seq),
        grid=(batch, pairs),
        in_specs=[spec(0), spec(1), spec(2),
                  pl.BlockSpec((n_pat, 2, 2, ATTN_BLOCK, 2 * ATTN_BLOCK),
                               lambda b, hp: (0, 0, hp, 0, 0))],
        out_specs=pl.BlockSpec((None, seq, LANES), lambda b, hp: (b, 0, hp)),
        out_shape=jax.ShapeDtypeStruct((batch, seq, D_ATTN), bf16),
        scratch_shapes=[pltpu.VMEM((seq, LANES), bf16),
                        pltpu.VMEM((seq + ATTN_BLOCK, LANES), bf16),
                        pltpu.VMEM((seq + ATTN_BLOCK, LANES), bf16),
                        pltpu.VMEM((seq, LANES), f32), pltpu.VMEM((seq, LANES), f32),
                        pltpu.VMEM((seq, LANES), f32)],
        compiler_params=_params("parallel", "parallel"),
        name="attention",
    )(view, view, view, band_bias)
    return o.reshape(batch * seq, D_ATTN)


def _gelu_tanh(x):
    return 0.5 * x * (1.0 + jnp.tanh(np.sqrt(2.0 / np.pi) * (x + 0.044715 * x * x * x)))


def _lru_kernel(ug_ref, convw_ref, convb_ref, wg_ref, ba_ref, bx_ref, lam_ref, g_ref,
                y_ref, ubuf, hcarry, *, ts):
    pad = 8
    t = pl.program_id(1)

    @pl.when(t == 0)
    def _():
        ubuf[0:pad, :] = jnp.zeros((pad, D_LRU), f32)
        hcarry[...] = jnp.zeros_like(hcarry)

    @pl.when(t > 0)
    def _():
        ubuf[0:pad, :] = ubuf[ts:ts + pad, :]

    ubuf[pad:pad + ts, :] = ug_ref[:, 0:D_LRU]
    gate = ug_ref[:, D_LRU:2 * D_LRU]

    u = convb_ref[...] + convw_ref[CONV_WIDTH - 1:CONV_WIDTH, :] * ubuf[pad:pad + ts, :]
    for w in range(CONV_WIDTH - 1):
        back = CONV_WIDTH - 1 - w
        u = u + convw_ref[w:w + 1, :] * ubuf[pad - back:pad - back + ts, :]

    gates = jnp.dot(u.astype(bf16), wg_ref[...], preferred_element_type=f32)
    r = jax.nn.sigmoid(gates[:, 0:D_LRU] + ba_ref[...])
    i = jax.nn.sigmoid(gates[:, D_LRU:2 * D_LRU] + bx_ref[...])
    neg_lam = -lam_ref[...]
    softplus = jnp.maximum(neg_lam, 0.0) + jnp.log1p(jnp.exp(-jnp.abs(neg_lam)))
    log_a = (-LRU_C) * r * softplus
    a = jnp.exp(log_a)
    b = jnp.sqrt(-jnp.tanh(log_a) * (1.0 + a * a)) * (i * u)

    row = lax.broadcasted_iota(jnp.int32, (ts, 1), 0)
    shift = 1
    while shift < ts:
        live = row >= shift
        a_prev = jnp.where(live, pltpu.roll(a, shift, 0), 1.0)
        b_prev = jnp.where(live, pltpu.roll(b, shift, 0), 0.0)
        b = a * b_prev + b
        a = a * a_prev
        shift *= 2
    h = a * hcarry[...] + b
    hcarry[...] = h[ts - 1:ts, :]

    y = _gelu_tanh(gate) * h
    y = y * lax.rsqrt(jnp.mean(y * y, axis=-1, keepdims=True) + RMS_EPS) * g_ref[...]
    y_ref[...] = y.astype(y_ref.dtype)


def _block_diag(w):
    g, i, j = w.shape
    eye = jnp.eye(g, dtype=w.dtype)
    return jnp.einsum('gij,gh->gihj', w, eye).reshape(g * i, g * j)


def _lru_branch(ug, conv_w, conv_b, w_a, b_a, w_x, b_x, lam, g_lru, batch, seq, ts=256):
    wg = jnp.concatenate([_block_diag(w_a), _block_diag(w_x)], axis=1).astype(bf16)
    row = lambda v: v.reshape(1, D_LRU).astype(f32)
    const = lambda shape: pl.BlockSpec(shape, lambda b, t: (0, 0))
    view = ug.reshape(batch, seq, 2 * D_LRU)
    y = pl.pallas_call(
        functools.partial(_lru_kernel, ts=ts),
        grid=(batch, seq // ts),
        in_specs=[pl.BlockSpec((None, ts, 2 * D_LRU), lambda b, t: (b, t, 0)),
                  const((CONV_WIDTH, D_LRU)), const((1, D_LRU)),
                  const((D_LRU, 2 * D_LRU)), const((1, D_LRU)), const((1, D_LRU)),
                  const((1, D_LRU)), const((1, D_LRU))],
        out_specs=pl.BlockSpec((None, ts, D_LRU), lambda b, t: (b, t, 0)),
        out_shape=jax.ShapeDtypeStruct((batch, seq, D_LRU), bf16),
        scratch_shapes=[pltpu.VMEM((ts + 16, D_LRU), f32), pltpu.VMEM((1, D_LRU), f32)],
        compiler_params=_params("parallel", "arbitrary"),
        name="rglru",
    )(view, conv_w.reshape(CONV_WIDTH, D_LRU).astype(f32), row(conv_b), wg, row(b_a), row(b_x),
      row(lam), row(g_lru))
    return y.reshape(batch * seq, D_LRU)


ROW_TILE = D_MODEL // LANES


def _store_row_tiles(ref, rows):
    t = rows.shape[0]
    for s in range(ROW_TILE):
        ref[pl.ds(s, t, stride=ROW_TILE), :] = rows[:, s * LANES:(s + 1) * LANES]


def _load_row_tiles(ref, t):
    return jnp.concatenate([ref[pl.ds(s, t, stride=ROW_TILE), :] for s in range(ROW_TILE)],
                           axis=-1)


def _layer_norm(z, g, b):
    mu = jnp.mean(z, axis=-1, keepdims=True)
    zc = z - mu
    var = jnp.mean(zc * zc, axis=-1, keepdims=True)
    return zc * lax.rsqrt(var + LN_EPS) * g + b


def _mix_out_kernel(attn_ref, ylru, x_ref, w_ref, gattn, lng, lnb, *rest, with_router):
    if with_router:
        rw_ref, x1_ref, x1_rows_ref, route_ref, count_ref = rest
    else:
        (x1_ref,) = rest
    attn = attn_ref[...].astype(f32)
    attn = attn * lax.rsqrt(jnp.mean(attn * attn, axis=-1, keepdims=True) + RMS_EPS) * gattn[...]
    y = jnp.dot(attn.astype(bf16), w_ref[0:D_ATTN, :], preferred_element_type=f32)
    y = y + jnp.dot(ylru[...], w_ref[D_ATTN:, :], preferred_element_type=f32)
    x1 = _layer_norm(DEEPNORM_ALPHA * x_ref[...] + y, lng[...], lnb[...])
    x1_ref[...] = x1
    if with_router:
        _store_row_tiles(x1_rows_ref, x1)
        lane = lax.broadcasted_iota(jnp.int32, (1, LANES), 1)
        logits = jnp.dot(x1, rw_ref[...], preferred_element_type=f32,
                         precision=lax.Precision.HIGHEST)
        logits = jnp.where(lane < N_EXPERTS, logits, -jnp.inf)
        v1 = jnp.max(logits, axis=-1, keepdims=True)
        i1 = jnp.min(jnp.where(logits == v1, lane, LANES), axis=-1, keepdims=True)
        rest_logits = jnp.where(lane == i1, -jnp.inf, logits)
        v2 = jnp.max(rest_logits, axis=-1, keepdims=True)
        i2 = jnp.min(jnp.where(rest_logits == v2, lane, LANES), axis=-1, keepdims=True)
        e2 = jnp.exp(v2 - v1)
        p1 = 1.0 / (1.0 + e2)
        p2 = e2 / (1.0 + e2)

        @pl.when(pl.program_id(0) == 0)
        def _():
            count_ref[...] = jnp.zeros_like(count_ref)

        tm = x1.shape[0]
        chosen = jnp.logical_or(lane == i1, lane == i2)
        tri = (lax.broadcasted_iota(jnp.int32, (tm, tm), 0)
               > lax.broadcasted_iota(jnp.int32, (tm, tm), 1)).astype(bf16)
        rank = count_ref[...] + jnp.dot(tri, chosen.astype(bf16), preferred_element_type=f32)
        count_ref[...] += jnp.sum(chosen.astype(f32), axis=0, keepdims=True)
        r1 = jnp.sum(jnp.where(lane == i1, rank, 0.0), axis=-1, keepdims=True)
        r2 = jnp.sum(jnp.where(lane == i2, rank, 0.0), axis=-1, keepdims=True)
        fields = (i1.astype(f32), i2.astype(f32), r1, r2, p1, p2)
        route = jnp.zeros((tm, LANES), f32)
        for k, val in enumerate(fields):
            route = jnp.where(lane == k, val, route)
        route_ref[...] = route


def _mix_out(attn, ylru, x2d, w_out_bf16, g_attn, ln_g, ln_b, router_w=None, tm=512):
    n = x2d.shape[0]
    with_router = router_w is not None
    tile = lambda width: pl.BlockSpec((tm, width), lambda i: (i, 0))
    const = lambda shape: pl.BlockSpec(shape, lambda i: (0, 0))
    in_specs = [tile(D_ATTN), tile(D_LRU), tile(D_MODEL), const((D_MODEL, D_MODEL)),
                const((1, D_ATTN)), const((1, D_MODEL)), const((1, D_MODEL))]
    args = [attn, ylru, x2d, w_out_bf16, g_attn.reshape(1, D_ATTN).astype(f32),
            ln_g.reshape(1, D_MODEL).astype(f32), ln_b.reshape(1, D_MODEL).astype(f32)]
    out_specs = [tile(D_MODEL)]
    out_shape = [jax.ShapeDtypeStruct((n, D_MODEL), f32)]
    if with_router:
        rw = jnp.zeros((D_MODEL, LANES), f32).at[:, :N_EXPERTS].set(router_w.astype(f32))
        in_specs.append(const((D_MODEL, LANES)))
        args.append(rw)
        out_specs += [pl.BlockSpec((tm * ROW_TILE, LANES), lambda i: (i, 0)), tile(LANES),
                      const((1, LANES))]
        out_shape += [jax.ShapeDtypeStruct((n * ROW_TILE, LANES), f32),
                      jax.ShapeDtypeStruct((n, LANES), f32), jax.ShapeDtypeStruct((1, LANES), f32)]
    res = pl.pallas_call(
        functools.partial(_mix_out_kernel, with_router=with_router),
        grid=(n // tm,),
        in_specs=in_specs,
        out_specs=out_specs,
        out_shape=out_shape,
        compiler_params=_params("arbitrary" if with_router else "parallel"),
        name="mix_out_router" if with_router else "mix_out",
    )(*args)
    return res if with_router else (res[0], None, None, None)


def _ffn_kernel(tile_expert_ref, n_used_ref, x_ref, wg_ref, wu_ref, wd_ref, *rest,
                n_chunks, fuse_ln):
    if fuse_ln:
        lng, lnb, out_ref, acc_ref, xb_ref = rest
    else:
        out_ref, acc_ref, xb_ref = rest
    del tile_expert_ref
    j = pl.program_id(1)
    used = pl.program_id(0) < n_used_ref[0]

    @pl.when(j == 0)
    def _():
        acc_ref[...] = jnp.zeros_like(acc_ref)

    @pl.when(used & (j == 0))
    def _():
        if fuse_ln:
            xb_ref[...] = x_ref[...].astype(bf16)
        else:
            xb_ref[...] = _load_row_tiles(x_ref, xb_ref.shape[0]).astype(bf16)

    @pl.when(used)
    def _():
        xb = xb_ref[...]
        g = jnp.dot(xb, wg_ref[...], preferred_element_type=f32)
        u = jnp.dot(xb, wu_ref[...], preferred_element_type=f32)
        h = (g * jax.nn.sigmoid(g)) * u
        acc_ref[...] += jnp.dot(h.astype(bf16), wd_ref[...], preferred_element_type=f32)

    @pl.when(j == n_chunks - 1)
    def _():
        if fuse_ln:
            out_ref[...] = _layer_norm(DEEPNORM_ALPHA * x_ref[...] + acc_ref[...],
                                       lng[...], lnb[...])
        else:
            _store_row_tiles(out_ref, acc_ref[...])


def _ffn(x2d, tile_expert, n_used, w_gate, w_up, w_down, ln=None, *, tm, tf, name):
    fuse_ln = ln is not None
    rows = x2d.shape[0] if fuse_ln else x2d.shape[0] // ROW_TILE
    d_ff = w_gate.shape[2]
    n_chunks = d_ff // tf
    io_block = (tm, D_MODEL) if fuse_ln else (tm * ROW_TILE, LANES)

    def chunk(i, j, nu):
        return jnp.where(i < nu[0], j, n_chunks - 1)

    in_specs = [pl.BlockSpec(io_block, lambda i, j, te, nu: (i, 0)),
                pl.BlockSpec((None, D_MODEL, tf), lambda i, j, te, nu: (te[i], 0, chunk(i, j, nu))),
                pl.BlockSpec((None, D_MODEL, tf), lambda i, j, te, nu: (te[i], 0, chunk(i, j, nu))),
                pl.BlockSpec((None, tf, D_MODEL), lambda i, j, te, nu: (te[i], chunk(i, j, nu), 0))]
    args = [x2d, w_gate, w_up, w_down]
    if fuse_ln:
        in_specs += [pl.BlockSpec((1, D_MODEL), lambda i, j, te, nu: (0, 0))] * 2
        args += [v.reshape(1, D_MODEL).astype(f32) for v in ln]
    return pl.pallas_call(
        functools.partial(_ffn_kernel, n_chunks=n_chunks, fuse_ln=fuse_ln),
        grid_spec=pltpu.PrefetchScalarGridSpec(
            num_scalar_prefetch=2,
            grid=(rows // tm, n_chunks),
            in_specs=in_specs,
            out_specs=pl.BlockSpec(io_block, lambda i, j, te, nu: (i, 0)),
            scratch_shapes=[pltpu.VMEM((tm, D_MODEL), f32), pltpu.VMEM((tm, D_MODEL), bf16)]),
        out_shape=jax.ShapeDtypeStruct(x2d.shape, f32),
        compiler_params=_params("parallel", "arbitrary"),
        name=name,
    )(tile_expert, n_used, *args)


def _dispatch_kernel(fill_ref, pos_ref, x_ref, xs_hbm, zero_buf, sem, *, tt, tm):
    step = pl.program_id(0)

    n_fill = fill_ref.shape[0] // 2

    @pl.when(step == 0)
    def _():
        zero_buf[...] = jnp.zeros_like(zero_buf)

        def fill_copy(f):
            start = pl.multiple_of(fill_ref[f] * ROW_TILE, ROW_TILE)
            return pltpu.make_async_copy(zero_buf, xs_hbm.at[pl.ds(start, tm * ROW_TILE)], sem)

        for f in range(n_fill):
            @pl.when(fill_ref[n_fill + f] > 0)
            def _():
                fill_copy(f).start()
        for f in range(n_fill):
            @pl.when(fill_ref[n_fill + f] > 0)
            def _():
                fill_copy(f).wait()

    def issue(t, carry):
        src = pl.multiple_of(t * ROW_TILE, ROW_TILE)
        for k in range(TOP_K):
            dst = pl.multiple_of(pos_ref[0, TOP_K * t + k] * ROW_TILE, ROW_TILE)
            pltpu.make_async_copy(x_ref.at[pl.ds(src, ROW_TILE)],
                                  xs_hbm.at[pl.ds(dst, ROW_TILE)], sem).start()
        return carry

    lax.fori_loop(0, tt, issue, 0, unroll=DMA_UNROLL)
    for _ in range(TOP_K):
        pltpu.make_async_copy(x_ref, xs_hbm.at[pl.ds(0, tt * ROW_TILE)], sem).wait()


def _dispatch(x1_rows, pos_blocks, fill, rows_sorted, *, tt, tm):
    n = x1_rows.shape[0] // ROW_TILE
    return pl.pallas_call(
        functools.partial(_dispatch_kernel, tt=tt, tm=tm),
        grid_spec=pltpu.PrefetchScalarGridSpec(
            num_scalar_prefetch=1,
            grid=(n // tt,),
            in_specs=[pl.BlockSpec((None, 1, TOP_K * tt), lambda i, fill: (i, 0, 0),
                                   memory_space=pltpu.SMEM),
                      pl.BlockSpec((tt * ROW_TILE, LANES), lambda i, fill: (i, 0))],
            out_specs=pl.BlockSpec(memory_space=pl.ANY),
            scratch_shapes=[pltpu.VMEM((tm * ROW_TILE, LANES), f32), pltpu.SemaphoreType.DMA(())]),
        out_shape=jax.ShapeDtypeStruct((rows_sorted * ROW_TILE, LANES), f32),
        compiler_params=_params("arbitrary"),
        name="moe_dispatch",
    )(fill, pos_blocks, x1_rows)


def _combine_kernel(pos_ref, route_ref, x1_ref, ys_hbm, lng, lnb, out_ref, buf, sem, *, tt):
    def issue(t, carry):
        dst = pl.multiple_of(t * ROW_TILE, ROW_TILE)
        for k in range(TOP_K):
            src = pl.multiple_of(pos_ref[0, TOP_K * t + k] * ROW_TILE, ROW_TILE)
            pltpu.make_async_copy(ys_hbm.at[pl.ds(src, ROW_TILE)],
                                  buf.at[k, pl.ds(dst, ROW_TILE)], sem).start()
        return carry

    lax.fori_loop(0, tt, issue, 0, unroll=DMA_UNROLL)
    for k in range(TOP_K):
        pltpu.make_async_copy(ys_hbm.at[pl.ds(0, tt * ROW_TILE)], buf.at[k], sem).wait()

    lane = lax.broadcasted_iota(jnp.int32, (1, LANES), 1)
    route = route_ref[...]
    p1 = jnp.sum(jnp.where(lane == 4, route, 0.0), axis=-1, keepdims=True)
    p2 = jnp.sum(jnp.where(lane == 5, route, 0.0), axis=-1, keepdims=True)
    y = p1 * _load_row_tiles(buf.at[0], tt) + p2 * _load_row_tiles(buf.at[1], tt)
    out_ref[...] = _layer_norm(DEEPNORM_ALPHA * x1_ref[...] + y, lng[...], lnb[...])


def _combine(pos_blocks, route, x1, y_sorted, ln_g, ln_b, *, tt):
    n = x1.shape[0]
    return pl.pallas_call(
        functools.partial(_combine_kernel, tt=tt),
        grid=(n // tt,),
        in_specs=[pl.BlockSpec((None, 1, TOP_K * tt), lambda i: (i, 0, 0), memory_space=pltpu.SMEM),
                  pl.BlockSpec((tt, LANES), lambda i: (i, 0)),
                  pl.BlockSpec((tt, D_MODEL), lambda i: (i, 0)),
                  pl.BlockSpec(memory_space=pl.ANY),
                  pl.BlockSpec((1, D_MODEL), lambda i: (0, 0)),
                  pl.BlockSpec((1, D_MODEL), lambda i: (0, 0))],
        out_specs=pl.BlockSpec((tt, D_MODEL), lambda i: (i, 0)),
        out_shape=jax.ShapeDtypeStruct((n, D_MODEL), f32),
        scratch_shapes=[pltpu.VMEM((TOP_K, tt * ROW_TILE, LANES), f32),
                        pltpu.SemaphoreType.DMA(())],
        compiler_params=_params("arbitrary"),
        name="moe_combine",
    )(pos_blocks, route, x1, y_sorted, ln_g.reshape(1, D_MODEL).astype(f32),
      ln_b.reshape(1, D_MODEL).astype(f32))


def _moe(x1, x1_rows, route, counts, w_gate, w_up, w_down, ln_g, ln_b, *, tm, tf, tt):
    n = x1.shape[0]
    i32 = jnp.int32
    counts = counts[0, :N_EXPERTS].astype(i32)
    padded = (counts + tm - 1) // tm * tm
    ends = jnp.cumsum(padded)
    offsets = ends - padded
    experts = route[:, 0:TOP_K].astype(i32)
    ranks = route[:, TOP_K:2 * TOP_K].astype(i32)
    pos = (offsets[experts] + ranks).reshape(n // tt, 1, TOP_K * tt)
    n_tiles = TOP_K * n // tm + N_EXPERTS
    n_used = (ends[-1] // tm).astype(i32).reshape(1)
    tile_ids = jnp.arange(n_tiles, dtype=i32)
    tile_expert = jnp.sum((tile_ids[:, None] >= (ends // tm)[None, :]).astype(i32), axis=1)
    last_expert = jnp.max(jnp.where(counts > 0, jnp.arange(N_EXPERTS, dtype=i32), 0))
    tile_expert = jnp.minimum(tile_expert, last_expert).astype(i32)
    tail_tiles = n_used[0] + jnp.arange(N_EXPERTS, dtype=i32)
    fill = jnp.concatenate([ends - tm, tail_tiles * tm,
                            (counts > 0).astype(i32), (tail_tiles < n_tiles).astype(i32)]).astype(i32)

    x_sorted = _dispatch(x1_rows, pos, fill, n_tiles * tm, tt=tt, tm=tm)
    y_sorted = _ffn(x_sorted, tile_expert, n_used, w_gate, w_up, w_down, tm=tm, tf=tf,
                    name="ffn_experts")
    return _combine(pos, route, x1, y_sorted, ln_g, ln_b, tt=tt)


def kernel(x, w_in, conv_w, conv_b, w_a, b_a, w_x, b_x, lru_lambda, rel_bias, g_attn, g_lru, w_out, ln1_g, ln1_b, ln2_g, ln2_b, ffn_w_gate, ffn_w_up, ffn_w_down, router_w, moe_w_gate, moe_w_up, moe_w_down):
    batch, seq, _ = x.shape
    n = batch * seq
    h = x.reshape(n, D_MODEL).astype(f32)
    band_bias = jnp.stack([_band_bias(rel_bias, d) for _, d in DILATED_PATTERNS])
    for layer in range(DEPTH):
        qkv, ug = _in_proj(h, w_in[layer].astype(bf16))
        attn = _attention(qkv, band_bias, batch, seq)
        ylru = _lru_branch(ug, conv_w[layer], conv_b[layer], w_a[layer], b_a[layer], w_x[layer],
                           b_x[layer], lru_lambda[layer], g_lru[layer], batch, seq)
        j = layer // 2
        dense = layer % 2 == 0
        x1, x1_rows, route, counts = _mix_out(
            attn, ylru, h, w_out[layer].astype(bf16), g_attn[layer], ln1_g[layer],
            ln1_b[layer], None if dense else router_w[j])
        if dense:
            tm = 512
            h = _ffn(x1, jnp.zeros((n // tm,), jnp.int32), jnp.full((1,), n // tm, jnp.int32),
                     ffn_w_gate[j][None].astype(bf16), ffn_w_up[j][None].astype(bf16),
                     ffn_w_down[j][None].astype(bf16), (ln2_g[layer], ln2_b[layer]),
                     tm=tm, tf=1408, name="ffn_dense")
        else:
            h = _moe(x1, x1_rows, route, counts, moe_w_gate[j].astype(bf16), moe_w_up[j].astype(bf16),
                     moe_w_down[j].astype(bf16), ln2_g[layer], ln2_b[layer],
                     tm=1024, tf=896, tt=256)
    return h.reshape(batch, seq, D_MODEL).astype(x.dtype)
```

```python
import functools

import numpy as np
import jax
import jax.numpy as jnp
from jax import lax
from jax.experimental import pallas as pl
from jax.experimental.pallas import tpu as pltpu

D_MODEL = 1024
N_HEADS = 8
HEAD_DIM = 64
D_ATTN = N_HEADS * HEAD_DIM
D_LRU = 512
N_LRU_BLOCKS = 8
LRU_BLOCK = D_LRU // N_LRU_BLOCKS
CONV_WIDTH = 4
LRU_C = 8.0
DILATED_PATTERNS = ((128, 1), (512, 4), (2048, 16))
ATTN_BLOCK = 128
N_BUCKETS = 32
MAX_DISTANCE = 2048
D_IN = 3 * D_ATTN + 2 * D_LRU
N_EXPERTS = 8
TOP_K = 2
DEPTH = 2
DEEPNORM_ALPHA = (2.0 * DEPTH) ** 0.25
LN_EPS = 1e-5
RMS_EPS = 1e-6
NEG_INF = -1e30
LOG2_E = float(np.log2(np.e))

LANES = 128
DMA_UNROLL = 8
SCORE_UNROLL = 8
ATTN_UNROLL = 4
VMEM_LIMIT_BYTES = 56 * 1024 * 1024

f32 = jnp.float32
bf16 = jnp.bfloat16


def _params(*semantics):
    return pltpu.CompilerParams(dimension_semantics=semantics,
                                vmem_limit_bytes=VMEM_LIMIT_BYTES)


def _in_proj_kernel(x_ref, w_ref, qkv_ref, ug_ref):
    xb = x_ref[...].astype(bf16)
    n_qkv = qkv_ref.shape[1]
    step = 512
    for c in range(0, D_IN, step):
        p = jnp.dot(xb, w_ref[:, c:c + step], preferred_element_type=f32)
        if c < n_qkv:
            qkv_ref[:, c:c + step] = p
        else:
            ug_ref[:, c - n_qkv:c - n_qkv + step] = p


def _in_proj(x2d, w_in_bf16, tm=512):
    n = x2d.shape[0]
    return pl.pallas_call(
        _in_proj_kernel,
        grid=(n // tm,),
        in_specs=[pl.BlockSpec((tm, D_MODEL), lambda i: (i, 0)),
                  pl.BlockSpec((D_MODEL, D_IN), lambda i: (0, 0))],
        out_specs=[pl.BlockSpec((tm, 3 * D_ATTN), lambda i: (i, 0)),
                   pl.BlockSpec((tm, 2 * D_LRU), lambda i: (i, 0))],
        out_shape=[jax.ShapeDtypeStruct((n, 3 * D_ATTN), f32),
                   jax.ShapeDtypeStruct((n, 2 * D_LRU), f32)],
        compiler_params=_params("parallel"),
        name="in_proj",
    )(x2d, w_in_bf16)


def _t5_bucket(dist):
    max_exact = N_BUCKETS // 2
    d = np.maximum(dist, 1).astype(np.float32)
    large = max_exact + (np.log(d / max_exact) / np.log(MAX_DISTANCE / max_exact)
                         * (N_BUCKETS - max_exact)).astype(np.int32)
    large = np.minimum(large, N_BUCKETS - 1)
    return np.where(dist < max_exact, dist, large).astype(np.int32)


def _band_bias(rel_bias, dilation):
    nk = ATTN_BLOCK
    qi = np.arange(nk)[:, None]
    kj = np.arange(2 * nk)[None, :]
    delta = qi + nk - kj
    band = (delta >= 0) & (delta <= nk)
    bucket = _t5_bucket(np.clip(delta, 0, nk) * dilation)
    onehot = np.eye(N_BUCKETS, dtype=np.float32)[bucket.reshape(-1)]
    bias = jnp.dot(jnp.asarray(onehot), rel_bias.astype(f32), precision=lax.Precision.HIGHEST)
    bias = jnp.transpose(bias.reshape(nk, 2 * nk, N_HEADS), (2, 0, 1))
    valid = np.stack([band, band & (kj >= nk)])[:, None]
    bias = jnp.where(jnp.asarray(valid), bias[None], NEG_INF)
    return (bias * LOG2_E).reshape(2, N_HEADS // 2, 2 * nk, 2 * nk)


def _attn_kernel(q_ref, k_ref, v_ref, bias_ref, o_ref, qs, ks, vs, s_buf, m_buf, m_s, l_s, acc_s,
                 *, seq):
    nk = ATTN_BLOCK
    n_blocks = seq // nk
    chunk = 2 * nk
    lane = lax.broadcasted_iota(jnp.int32, (1, LANES), 1)
    head0 = lane < HEAD_DIM

    m_s[...] = jnp.full_like(m_s, NEG_INF)
    l_s[...] = jnp.zeros_like(l_s)
    acc_s[...] = jnp.zeros_like(acc_s)
    ks[0:nk, :] = jnp.zeros((nk, LANES), bf16)
    vs[0:nk, 0:LANES] = jnp.zeros((nk, LANES), bf16)
    vs[:, LANES:2 * LANES] = jnp.ones((seq + nk, LANES), bf16)

    for p, (_, d) in enumerate(DILATED_PATTERNS):
        length = seq // d
        nb = length // nk
        chunks_per_residue = length // chunk

        def gather(t, carry):
            r = t // chunks_per_residue
            c = t % chunks_per_residue
            src = pl.ds(r + d * chunk * c, chunk, stride=d)
            dst = pl.multiple_of(t * chunk, chunk)
            q = q_ref[src, :] * (HEAD_DIM ** -0.5 * LOG2_E)
            q0 = jnp.where(head0, q, 0.0).astype(bf16)
            q1 = jnp.where(head0, 0.0, q).astype(bf16)
            for half in range(2):
                base = pl.multiple_of(2 * dst + half * chunk, chunk)
                qs[pl.ds(base, nk), :] = q0[half * nk:(half + 1) * nk]
                qs[pl.ds(base + nk, nk), :] = q1[half * nk:(half + 1) * nk]
            ks[pl.ds(nk + dst, chunk), :] = k_ref[src, :].astype(bf16)
            vs[pl.ds(nk + dst, chunk), 0:LANES] = v_ref[src, :].astype(bf16)
            return carry

        lax.fori_loop(0, seq // chunk, gather, 0)

        def scores(b, carry):
            first = jnp.asarray(b % nb == 0, jnp.int32)
            rows = pl.ds(pl.multiple_of(b * chunk, chunk), chunk)
            k = ks[pl.ds(pl.multiple_of(b * nk, nk), chunk), :]
            s = lax.dot_general(qs[rows, :], k, (((1,), (1,)), ((), ())),
                                preferred_element_type=f32) + bias_ref[p, first]
            s_buf[rows, :] = s
            m_buf[rows, :] = jnp.broadcast_to(jnp.max(s, axis=-1, keepdims=True), (chunk, LANES))
            return carry

        lax.fori_loop(0, n_blocks, scores, 0, unroll=SCORE_UNROLL)

        def block(b, carry):
            r = b // nb
            i = b % nb
            rows = pl.ds(pl.multiple_of(b * chunk, chunk), chunk)
            v = vs[pl.ds(pl.multiple_of(b * nk, nk), chunk), :]
            m = m_buf[rows, :]
            e = jnp.exp2(s_buf[rows, :] - jnp.concatenate([m, m], axis=1))
            pv = jnp.dot(e.astype(bf16), v, preferred_element_type=f32)
            m_blk = jnp.where(head0, m[0:nk], m[nk:chunk])
            l_blk = jnp.where(head0, pv[0:nk, LANES:], pv[nk:chunk, LANES:])
            pv_blk = jnp.where(head0, pv[0:nk, 0:LANES], pv[nk:chunk, 0:LANES])
            tok = pl.ds(r + d * nk * i, nk, stride=d)
            m_old = m_s[tok, :]
            m_new = jnp.maximum(m_old, m_blk)
            w_old = jnp.exp2(m_old - m_new)
            w_blk = jnp.exp2(m_blk - m_new)
            l_s[tok, :] = w_old * l_s[tok, :] + w_blk * l_blk
            acc_s[tok, :] = w_old * acc_s[tok, :] + w_blk * pv_blk
            m_s[tok, :] = m_new
            return carry

        lax.fori_loop(0, n_blocks, block, 0, unroll=ATTN_UNROLL)

    def finish(t, carry):
        rows = pl.ds(pl.multiple_of(t * chunk, chunk), chunk)
        o_ref[rows, :] = (acc_s[rows, :] / l_s[rows, :]).astype(o_ref.dtype)
        return carry

    lax.fori_loop(0, seq // chunk, finish, 0)


def _attention(qkv, band_bias, batch, seq):
    view = qkv.reshape(batch, seq, 3 * D_ATTN)
    pairs = D_ATTN // LANES

    def spec(offset):
        return pl.BlockSpec((None, seq, LANES), lambda b, hp: (b, 0, offset * pairs + hp))

    n_pat = len(DILATED_PATTERNS)
    o = pl.pallas_call(
        functools.partial(_attn_kernel, seq=seq),
        grid=(batch, pairs),
        in_specs=[spec(0), spec(1), spec(2),
                  pl.BlockSpec((n_pat, 2, None, 2 * ATTN_BLOCK, 2 * ATTN_BLOCK),
                               lambda b, hp: (0, 0, hp, 0, 0))],
        out_specs=pl.BlockSpec((None, seq, LANES), lambda b, hp: (b, 0, hp)),
        out_shape=jax.ShapeDtypeStruct((batch, seq, D_ATTN), bf16),
        scratch_shapes=[pltpu.VMEM((2 * seq, LANES), bf16),
                        pltpu.VMEM((seq + ATTN_BLOCK, LANES), bf16),
                        pltpu.VMEM((seq + ATTN_BLOCK, 2 * LANES), bf16),
                        pltpu.VMEM((2 * seq, 2 * ATTN_BLOCK), f32),
                        pltpu.VMEM((2 * seq, LANES), f32),
                        pltpu.VMEM((seq, LANES), f32), pltpu.VMEM((seq, LANES), f32),
                        pltpu.VMEM((seq, LANES), f32)],
        compiler_params=_params("parallel", "parallel"),
        name="attention",
    )(view, view, view, band_bias)
    return o.reshape(batch * seq, D_ATTN)


def _gelu_tanh(x):
    return 0.5 * x * (1.0 + jnp.tanh(np.sqrt(2.0 / np.pi) * (x + 0.044715 * x * x * x)))


def _lru_kernel(ug_ref, convw_ref, convb_ref, wg_ref, ba_ref, bx_ref, lam_ref, g_ref,
                y_ref, ubuf, hcarry, *, ts):
    pad = 8
    t = pl.program_id(1)

    @pl.when(t == 0)
    def _():
        ubuf[0:pad, :] = jnp.zeros((pad, D_LRU), f32)
        hcarry[...] = jnp.zeros_like(hcarry)

    @pl.when(t > 0)
    def _():
        ubuf[0:pad, :] = ubuf[ts:ts + pad, :]

    ubuf[pad:pad + ts, :] = ug_ref[:, 0:D_LRU]
    gate = ug_ref[:, D_LRU:2 * D_LRU]

    u = convb_ref[...] + convw_ref[CONV_WIDTH - 1:CONV_WIDTH, :] * ubuf[pad:pad + ts, :]
    for w in range(CONV_WIDTH - 1):
        back = CONV_WIDTH - 1 - w
        u = u + convw_ref[w:w + 1, :] * ubuf[pad - back:pad - back + ts, :]

    gates = jnp.dot(u.astype(bf16), wg_ref[...], preferred_element_type=f32)
    r = jax.nn.sigmoid(gates[:, 0:D_LRU] + ba_ref[...])
    i = jax.nn.sigmoid(gates[:, D_LRU:2 * D_LRU] + bx_ref[...])
    neg_lam = -lam_ref[...]
    softplus = jnp.maximum(neg_lam, 0.0) + jnp.log1p(jnp.exp(-jnp.abs(neg_lam)))
    log_a = (-LRU_C) * r * softplus
    a = jnp.exp(log_a)
    b = jnp.sqrt(-jnp.tanh(log_a) * (1.0 + a * a)) * (i * u)

    row = lax.broadcasted_iota(jnp.int32, (ts, 1), 0)
    shift = 1
    while shift < ts:
        live = row >= shift
        a_prev = jnp.where(live, pltpu.roll(a, shift, 0), 1.0)
        b_prev = jnp.where(live, pltpu.roll(b, shift, 0), 0.0)
        b = a * b_prev + b
        a = a * a_prev
        shift *= 2
    h = a * hcarry[...] + b
    hcarry[...] = h[ts - 1:ts, :]

    y = _gelu_tanh(gate) * h
    y = y * lax.rsqrt(jnp.mean(y * y, axis=-1, keepdims=True) + RMS_EPS) * g_ref[...]
    y_ref[...] = y.astype(y_ref.dtype)


def _block_diag(w):
    g, i, j = w.shape
    eye = jnp.eye(g, dtype=w.dtype)
    return jnp.einsum('gij,gh->gihj', w, eye).reshape(g * i, g * j)


def _lru_branch(ug, conv_w, conv_b, w_a, b_a, w_x, b_x, lam, g_lru, batch, seq, ts=256):
    wg = jnp.concatenate([_block_diag(w_a), _block_diag(w_x)], axis=1).astype(bf16)
    row = lambda v: v.reshape(1, D_LRU).astype(f32)
    const = lambda shape: pl.BlockSpec(shape, lambda b, t: (0, 0))
    view = ug.reshape(batch, seq, 2 * D_LRU)
    y = pl.pallas_call(
        functools.partial(_lru_kernel, ts=ts),
        grid=(batch, seq // ts),
        in_specs=[pl.BlockSpec((None, ts, 2 * D_LRU), lambda b, t: (b, t, 0)),
                  const((CONV_WIDTH, D_LRU)), const((1, D_LRU)),
                  const((D_LRU, 2 * D_LRU)), const((1, D_LRU)), const((1, D_LRU)),
                  const((1, D_LRU)), const((1, D_LRU))],
        out_specs=pl.BlockSpec((None, ts, D_LRU), lambda b, t: (b, t, 0)),
        out_shape=jax.ShapeDtypeStruct((batch, seq, D_LRU), bf16),
        scratch_shapes=[pltpu.VMEM((ts + 16, D_LRU), f32), pltpu.VMEM((1, D_LRU), f32)],
        compiler_params=_params("parallel", "arbitrary"),
        name="rglru",
    )(view, conv_w.reshape(CONV_WIDTH, D_LRU).astype(f32), row(conv_b), wg, row(b_a), row(b_x),
      row(lam), row(g_lru))
    return y.reshape(batch * seq, D_LRU)


ROW_TILE = D_MODEL // LANES


def _store_row_tiles(ref, rows):
    t = rows.shape[0]
    for s in range(ROW_TILE):
        ref[pl.ds(s, t, stride=ROW_TILE), :] = rows[:, s * LANES:(s + 1) * LANES]


def _load_row_tiles(ref, t):
    return jnp.concatenate([ref[pl.ds(s, t, stride=ROW_TILE), :] for s in range(ROW_TILE)],
                           axis=-1)


def _layer_norm(z, g, b):
    mu = jnp.mean(z, axis=-1, keepdims=True)
    zc = z - mu
    var = jnp.mean(zc * zc, axis=-1, keepdims=True)
    return zc * lax.rsqrt(var + LN_EPS) * g + b


def _mix_out_kernel(attn_ref, ylru, x_ref, w_ref, gattn, lng, lnb, *rest, with_router):
    if with_router:
        rw_ref, x1_ref, x1_rows_ref, route_ref, count_ref = rest
    else:
        (x1_ref,) = rest
    attn = attn_ref[...].astype(f32)
    attn = attn * lax.rsqrt(jnp.mean(attn * attn, axis=-1, keepdims=True) + RMS_EPS) * gattn[...]
    y = jnp.dot(attn.astype(bf16), w_ref[0:D_ATTN, :], preferred_element_type=f32)
    y = y + jnp.dot(ylru[...], w_ref[D_ATTN:, :], preferred_element_type=f32)
    x1 = _layer_norm(DEEPNORM_ALPHA * x_ref[...] + y, lng[...], lnb[...])
    x1_ref[...] = x1
    if with_router:
        _store_row_tiles(x1_rows_ref, x1)
        lane = lax.broadcasted_iota(jnp.int32, (1, LANES), 1)
        logits = jnp.dot(x1, rw_ref[...], preferred_element_type=f32,
                         precision=lax.Precision.HIGHEST)
        logits = jnp.where(lane < N_EXPERTS, logits, -jnp.inf)
        v1 = jnp.max(logits, axis=-1, keepdims=True)
        i1 = jnp.min(jnp.where(logits == v1, lane, LANES), axis=-1, keepdims=True)
        rest_logits = jnp.where(lane == i1, -jnp.inf, logits)
        v2 = jnp.max(rest_logits, axis=-1, keepdims=True)
        i2 = jnp.min(jnp.where(rest_logits == v2, lane, LANES), axis=-1, keepdims=True)
        e2 = jnp.exp(v2 - v1)
        p1 = 1.0 / (1.0 + e2)
        p2 = e2 / (1.0 + e2)

        @pl.when(pl.program_id(0) == 0)
        def _():
            count_ref[...] = jnp.zeros_like(count_ref)

        tm = x1.shape[0]
        chosen = jnp.logical_or(lane == i1, lane == i2)
        tri = (lax.broadcasted_iota(jnp.int32, (tm, tm), 0)
               > lax.broadcasted_iota(jnp.int32, (tm, tm), 1)).astype(bf16)
        rank = count_ref[...] + jnp.dot(tri, chosen.astype(bf16), preferred_element_type=f32)
        count_ref[...] += jnp.sum(chosen.astype(f32), axis=0, keepdims=True)
        r1 = jnp.sum(jnp.where(lane == i1, rank, 0.0), axis=-1, keepdims=True)
        r2 = jnp.sum(jnp.where(lane == i2, rank, 0.0), axis=-1, keepdims=True)
        fields = (i1.astype(f32), i2.astype(f32), r1, r2, p1, p2)
        route = jnp.zeros((tm, LANES), f32)
        for k, val in enumerate(fields):
            route = jnp.where(lane == k, val, route)
        route_ref[...] = route


def _mix_out(attn, ylru, x2d, w_out_bf16, g_attn, ln_g, ln_b, router_w=None, tm=512):
    n = x2d.shape[0]
    with_router = router_w is not None
    tile = lambda width: pl.BlockSpec((tm, width), lambda i: (i, 0))
    const = lambda shape: pl.BlockSpec(shape, lambda i: (0, 0))
    in_specs = [tile(D_ATTN), tile(D_LRU), tile(D_MODEL), const((D_MODEL, D_MODEL)),
                const((1, D_ATTN)), const((1, D_MODEL)), const((1, D_MODEL))]
    args = [attn, ylru, x2d, w_out_bf16, g_attn.reshape(1, D_ATTN).astype(f32),
            ln_g.reshape(1, D_MODEL).astype(f32), ln_b.reshape(1, D_MODEL).astype(f32)]
    out_specs = [tile(D_MODEL)]
    out_shape = [jax.ShapeDtypeStruct((n, D_MODEL), f32)]
    if with_router:
        rw = jnp.zeros((D_MODEL, LANES), f32).at[:, :N_EXPERTS].set(router_w.astype(f32))
        in_specs.append(const((D_MODEL, LANES)))
        args.append(rw)
        out_specs += [pl.BlockSpec((tm * ROW_TILE, LANES), lambda i: (i, 0)), tile(LANES),
                      const((1, LANES))]
        out_shape += [jax.ShapeDtypeStruct((n * ROW_TILE, LANES), f32),
                      jax.ShapeDtypeStruct((n, LANES), f32), jax.ShapeDtypeStruct((1, LANES), f32)]
    res = pl.pallas_call(
        functools.partial(_mix_out_kernel, with_router=with_router),
        grid=(n // tm,),
        in_specs=in_specs,
        out_specs=out_specs,
        out_shape=out_shape,
        compiler_params=_params("arbitrary" if with_router else "parallel"),
        name="mix_out_router" if with_router else "mix_out",
    )(*args)
    return res if with_router else (res[0], None, None, None)


def _ffn_kernel(tile_expert_ref, n_used_ref, x_ref, wg_ref, wu_ref, wd_ref, *rest,
                n_chunks, fuse_ln):
    if fuse_ln:
        lng, lnb, out_ref, acc_ref, xb_ref = rest
    else:
        out_ref, acc_ref, xb_ref = rest
    del tile_expert_ref
    j = pl.program_id(1)
    used = pl.program_id(0) < n_used_ref[0]

    @pl.when(j == 0)
    def _():
        acc_ref[...] = jnp.zeros_like(acc_ref)

    @pl.when(used & (j == 0))
    def _():
        if fuse_ln:
            xb_ref[...] = x_ref[...].astype(bf16)
        else:
            xb_ref[...] = _load_row_tiles(x_ref, xb_ref.shape[0]).astype(bf16)

    @pl.when(used)
    def _():
        xb = xb_ref[...]
        g = jnp.dot(xb, wg_ref[...], preferred_element_type=f32)
        u = jnp.dot(xb, wu_ref[...], preferred_element_type=f32)
        h = (g * jax.nn.sigmoid(g)) * u
        acc_ref[...] += jnp.dot(h.astype(bf16), wd_ref[...], preferred_element_type=f32)

    @pl.when(j == n_chunks - 1)
    def _():
        if fuse_ln:
            out_ref[...] = _layer_norm(DEEPNORM_ALPHA * x_ref[...] + acc_ref[...],
                                       lng[...], lnb[...])
        else:
            _store_row_tiles(out_ref, acc_ref[...])


def _ffn(x2d, tile_expert, n_used, w_gate, w_up, w_down, ln=None, *, tm, tf, name):
    fuse_ln = ln is not None
    rows = x2d.shape[0] if fuse_ln else x2d.shape[0] // ROW_TILE
    d_ff = w_gate.shape[2]
    n_chunks = d_ff // tf
    io_block = (tm, D_MODEL) if fuse_ln else (tm * ROW_TILE, LANES)

    def chunk(i, j, nu):
        return jnp.where(i < nu[0], j, n_chunks - 1)

    in_specs = [pl.BlockSpec(io_block, lambda i, j, te, nu: (i, 0)),
                pl.BlockSpec((None, D_MODEL, tf), lambda i, j, te, nu: (te[i], 0, chunk(i, j, nu))),
                pl.BlockSpec((None, D_MODEL, tf), lambda i, j, te, nu: (te[i], 0, chunk(i, j, nu))),
                pl.BlockSpec((None, tf, D_MODEL), lambda i, j, te, nu: (te[i], chunk(i, j, nu), 0))]
    args = [x2d, w_gate, w_up, w_down]
    if fuse_ln:
        in_specs += [pl.BlockSpec((1, D_MODEL), lambda i, j, te, nu: (0, 0))] * 2
        args += [v.reshape(1, D_MODEL).astype(f32) for v in ln]
    return pl.pallas_call(
        functools.partial(_ffn_kernel, n_chunks=n_chunks, fuse_ln=fuse_ln),
        grid_spec=pltpu.PrefetchScalarGridSpec(
            num_scalar_prefetch=2,
            grid=(rows // tm, n_chunks),
            in_specs=in_specs,
            out_specs=pl.BlockSpec(io_block, lambda i, j, te, nu: (i, 0)),
            scratch_shapes=[pltpu.VMEM((tm, D_MODEL), f32), pltpu.VMEM((tm, D_MODEL), bf16)]),
        out_shape=jax.ShapeDtypeStruct(x2d.shape, f32),
        compiler_params=_params("parallel", "arbitrary"),
        name=name,
    )(tile_expert, n_used, *args)


def _dispatch_kernel(fill_ref, pos_ref, x_ref, xs_hbm, zero_buf, sem, *, tt, tm):
    step = pl.program_id(0)

    n_fill = fill_ref.shape[0] // 2

    @pl.when(step == 0)
    def _():
        zero_buf[...] = jnp.zeros_like(zero_buf)

        def fill_copy(f):
            start = pl.multiple_of(fill_ref[f] * ROW_TILE, ROW_TILE)
            return pltpu.make_async_copy(zero_buf, xs_hbm.at[pl.ds(start, tm * ROW_TILE)], sem)

        for f in range(n_fill):
            @pl.when(fill_ref[n_fill + f] > 0)
            def _():
                fill_copy(f).start()
        for f in range(n_fill):
            @pl.when(fill_ref[n_fill + f] > 0)
            def _():
                fill_copy(f).wait()

    def issue(t, carry):
        src = pl.multiple_of(t * ROW_TILE, ROW_TILE)
        for k in range(TOP_K):
            dst = pl.multiple_of(pos_ref[0, TOP_K * t + k] * ROW_TILE, ROW_TILE)
            pltpu.make_async_copy(x_ref.at[pl.ds(src, ROW_TILE)],
                                  xs_hbm.at[pl.ds(dst, ROW_TILE)], sem).start()
        return carry

    lax.fori_loop(0, tt, issue, 0, unroll=DMA_UNROLL)
    for _ in range(TOP_K):
        pltpu.make_async_copy(x_ref, xs_hbm.at[pl.ds(0, tt * ROW_TILE)], sem).wait()


def _dispatch(x1_rows, pos_blocks, fill, rows_sorted, *, tt, tm):
    n = x1_rows.shape[0] // ROW_TILE
    return pl.pallas_call(
        functools.partial(_dispatch_kernel, tt=tt, tm=tm),
        grid_spec=pltpu.PrefetchScalarGridSpec(
            num_scalar_prefetch=1,
            grid=(n // tt,),
            in_specs=[pl.BlockSpec((None, 1, TOP_K * tt), lambda i, fill: (i, 0, 0),
                                   memory_space=pltpu.SMEM),
                      pl.BlockSpec((tt * ROW_TILE, LANES), lambda i, fill: (i, 0))],
            out_specs=pl.BlockSpec(memory_space=pl.ANY),
            scratch_shapes=[pltpu.VMEM((tm * ROW_TILE, LANES), f32), pltpu.SemaphoreType.DMA(())]),
        out_shape=jax.ShapeDtypeStruct((rows_sorted * ROW_TILE, LANES), f32),
        compiler_params=_params("arbitrary"),
        name="moe_dispatch",
    )(fill, pos_blocks, x1_rows)


def _combine_kernel(pos_ref, route_ref, x1_ref, ys_hbm, lng, lnb, out_ref, buf, sem, *, tt):
    def issue(t, carry):
        dst = pl.multiple_of(t * ROW_TILE, ROW_TILE)
        for k in range(TOP_K):
            src = pl.multiple_of(pos_ref[0, TOP_K * t + k] * ROW_TILE, ROW_TILE)
            pltpu.make_async_copy(ys_hbm.at[pl.ds(src, ROW_TILE)],
                                  buf.at[k, pl.ds(dst, ROW_TILE)], sem).start()
        return carry

    lax.fori_loop(0, tt, issue, 0, unroll=DMA_UNROLL)
    for k in range(TOP_K):
        pltpu.make_async_copy(ys_hbm.at[pl.ds(0, tt * ROW_TILE)], buf.at[k], sem).wait()

    lane = lax.broadcasted_iota(jnp.int32, (1, LANES), 1)
    route = route_ref[...]
    p1 = jnp.sum(jnp.where(lane == 4, route, 0.0), axis=-1, keepdims=True)
    p2 = jnp.sum(jnp.where(lane == 5, route, 0.0), axis=-1, keepdims=True)
    y = p1 * _load_row_tiles(buf.at[0], tt) + p2 * _load_row_tiles(buf.at[1], tt)
    out_ref[...] = _layer_norm(DEEPNORM_ALPHA * x1_ref[...] + y, lng[...], lnb[...])


def _combine(pos_blocks, route, x1, y_sorted, ln_g, ln_b, *, tt):
    n = x1.shape[0]
    return pl.pallas_call(
        functools.partial(_combine_kernel, tt=tt),
        grid=(n // tt,),
        in_specs=[pl.BlockSpec((None, 1, TOP_K * tt), lambda i: (i, 0, 0), memory_space=pltpu.SMEM),
                  pl.BlockSpec((tt, LANES), lambda i: (i, 0)),
                  pl.BlockSpec((tt, D_MODEL), lambda i: (i, 0)),
                  pl.BlockSpec(memory_space=pl.ANY),
                  pl.BlockSpec((1, D_MODEL), lambda i: (0, 0)),
                  pl.BlockSpec((1, D_MODEL), lambda i: (0, 0))],
        out_specs=pl.BlockSpec((tt, D_MODEL), lambda i: (i, 0)),
        out_shape=jax.ShapeDtypeStruct((n, D_MODEL), f32),
        scratch_shapes=[pltpu.VMEM((TOP_K, tt * ROW_TILE, LANES), f32),
                        pltpu.SemaphoreType.DMA(())],
        compiler_params=_params("arbitrary"),
        name="moe_combine",
    )(pos_blocks, route, x1, y_sorted, ln_g.reshape(1, D_MODEL).astype(f32),
      ln_b.reshape(1, D_MODEL).astype(f32))


def _moe(x1, x1_rows, route, counts, w_gate, w_up, w_down, ln_g, ln_b, *, tm, tf, tt):
    n = x1.shape[0]
    i32 = jnp.int32
    counts = counts[0, :N_EXPERTS].astype(i32)
    padded = (counts + tm - 1) // tm * tm
    ends = jnp.cumsum(padded)
    offsets = ends - padded
    experts = route[:, 0:TOP_K].astype(i32)
    ranks = route[:, TOP_K:2 * TOP_K].astype(i32)
    pos = (offsets[experts] + ranks).reshape(n // tt, 1, TOP_K * tt)
    n_tiles = TOP_K * n // tm + N_EXPERTS
    n_used = (ends[-1] // tm).astype(i32).reshape(1)
    tile_ids = jnp.arange(n_tiles, dtype=i32)
    tile_expert = jnp.sum((tile_ids[:, None] >= (ends // tm)[None, :]).astype(i32), axis=1)
    last_expert = jnp.max(jnp.where(counts > 0, jnp.arange(N_EXPERTS, dtype=i32), 0))
    tile_expert = jnp.minimum(tile_expert, last_expert).astype(i32)
    tail_tiles = n_used[0] + jnp.arange(N_EXPERTS, dtype=i32)
    fill = jnp.concatenate([ends - tm, tail_tiles * tm,
                            (counts > 0).astype(i32), (tail_tiles < n_tiles).astype(i32)]).astype(i32)

    x_sorted = _dispatch(x1_rows, pos, fill, n_tiles * tm, tt=tt, tm=tm)
    y_sorted = _ffn(x_sorted, tile_expert, n_used, w_gate, w_up, w_down, tm=tm, tf=tf,
                    name="ffn_experts")
    return _combine(pos, route, x1, y_sorted, ln_g, ln_b, tt=tt)


def kernel(x, w_in, conv_w, conv_b, w_a, b_a, w_x, b_x, lru_lambda, rel_bias, g_attn, g_lru, w_out, ln1_g, ln1_b, ln2_g, ln2_b, ffn_w_gate, ffn_w_up, ffn_w_down, router_w, moe_w_gate, moe_w_up, moe_w_down):
    batch, seq, _ = x.shape
    n = batch * seq
    h = x.reshape(n, D_MODEL).astype(f32)
    band_bias = jnp.stack([_band_bias(rel_bias, d) for _, d in DILATED_PATTERNS])
    for layer in range(DEPTH):
        qkv, ug = _in_proj(h, w_in[layer].astype(bf16))
        attn = _attention(qkv, band_bias, batch, seq)
        ylru = _lru_branch(ug, conv_w[layer], conv_b[layer], w_a[layer], b_a[layer], w_x[layer],
                           b_x[layer], lru_lambda[layer], g_lru[layer], batch, seq)
        j = layer // 2
        dense = layer % 2 == 0
        x1, x1_rows, route, counts = _mix_out(
            attn, ylru, h, w_out[layer].astype(bf16), g_attn[layer], ln1_g[layer],
            ln1_b[layer], None if dense else router_w[j])
        if dense:
            tm = 512
            h = _ffn(x1, jnp.zeros((n // tm,), jnp.int32), jnp.full((1,), n // tm, jnp.int32),
                     ffn_w_gate[j][None].astype(bf16), ffn_w_up[j][None].astype(bf16),
                     ffn_w_down[j][None].astype(bf16), (ln2_g[layer], ln2_b[layer]),
                     tm=tm, tf=1408, name="ffn_dense")
        else:
            h = _moe(x1, x1_rows, route, counts, moe_w_gate[j].astype(bf16), moe_w_up[j].astype(bf16),
                     moe_w_down[j].astype(bf16), ln2_g[layer], ln2_b[layer],
                     tm=1024, tf=896, tt=256)
    return h.reshape(batch, seq, D_MODEL).astype(x.dtype)
```

```python
import functools

import numpy as np
import jax
import jax.numpy as jnp
from jax import lax
from jax.experimental import pallas as pl
from jax.experimental.pallas import tpu as pltpu

D_MODEL = 1024
N_HEADS = 8
HEAD_DIM = 64
D_ATTN = N_HEADS * HEAD_DIM
D_LRU = 512
N_LRU_BLOCKS = 8
LRU_BLOCK = D_LRU // N_LRU_BLOCKS
CONV_WIDTH = 4
LRU_C = 8.0
DILATED_PATTERNS = ((128, 1), (512, 4), (2048, 16))
ATTN_BLOCK = 128
N_BUCKETS = 32
MAX_DISTANCE = 2048
D_IN = 3 * D_ATTN + 2 * D_LRU
N_EXPERTS = 8
TOP_K = 2
DEPTH = 2
DEEPNORM_ALPHA = (2.0 * DEPTH) ** 0.25
LN_EPS = 1e-5
RMS_EPS = 1e-6
NEG_INF = -1e30
LOG2_E = float(np.log2(np.e))

LANES = 128
DMA_UNROLL = 8
SCORE_UNROLL = 8
ATTN_UNROLL = 4
VMEM_LIMIT_BYTES = 56 * 1024 * 1024

f32 = jnp.float32
bf16 = jnp.bfloat16


def _params(*semantics):
    return pltpu.CompilerParams(dimension_semantics=semantics,
                                vmem_limit_bytes=VMEM_LIMIT_BYTES)


def _in_proj_kernel(x_ref, w_ref, qkv_ref, ug_ref):
    xb = x_ref[...].astype(bf16)
    n_qkv = qkv_ref.shape[1]
    step = 512
    for c in range(0, D_IN, step):
        p = jnp.dot(xb, w_ref[:, c:c + step], preferred_element_type=f32)
        if c < n_qkv:
            qkv_ref[:, c:c + step] = p
        else:
            ug_ref[:, c - n_qkv:c - n_qkv + step] = p


def _in_proj(x2d, w_in_bf16, tm=512):
    n = x2d.shape[0]
    return pl.pallas_call(
        _in_proj_kernel,
        grid=(n // tm,),
        in_specs=[pl.BlockSpec((tm, D_MODEL), lambda i: (i, 0)),
                  pl.BlockSpec((D_MODEL, D_IN), lambda i: (0, 0))],
        out_specs=[pl.BlockSpec((tm, 3 * D_ATTN), lambda i: (i, 0)),
                   pl.BlockSpec((tm, 2 * D_LRU), lambda i: (i, 0))],
        out_shape=[jax.ShapeDtypeStruct((n, 3 * D_ATTN), f32),
                   jax.ShapeDtypeStruct((n, 2 * D_LRU), f32)],
        compiler_params=_params("parallel"),
        name="in_proj",
    )(x2d, w_in_bf16)


def _t5_bucket(dist):
    max_exact = N_BUCKETS // 2
    d = np.maximum(dist, 1).astype(np.float32)
    large = max_exact + (np.log(d / max_exact) / np.log(MAX_DISTANCE / max_exact)
                         * (N_BUCKETS - max_exact)).astype(np.int32)
    large = np.minimum(large, N_BUCKETS - 1)
    return np.where(dist < max_exact, dist, large).astype(np.int32)


def _band_bias(rel_bias, dilation):
    nk = ATTN_BLOCK
    qi = np.arange(nk)[:, None]
    kj = np.arange(2 * nk)[None, :]
    delta = qi + nk - kj
    band = (delta >= 0) & (delta <= nk)
    bucket = _t5_bucket(np.clip(delta, 0, nk) * dilation)
    onehot = np.eye(N_BUCKETS, dtype=np.float32)[bucket.reshape(-1)]
    bias = jnp.dot(jnp.asarray(onehot), rel_bias.astype(f32), precision=lax.Precision.HIGHEST)
    bias = jnp.transpose(bias.reshape(nk, 2 * nk, N_HEADS), (2, 0, 1))
    valid = np.stack([band, band & (kj >= nk)])[:, None]
    bias = jnp.where(jnp.asarray(valid), bias[None], NEG_INF)
    return (bias * LOG2_E).reshape(2, N_HEADS // 2, 2 * nk, 2 * nk)


def _attn_kernel(q_ref, k_ref, v_ref, bias_ref, o_ref, qs, ks, vs, s_buf, m_buf, m_s, l_s, acc_s,
                 *, seq):
    nk = ATTN_BLOCK
    n_blocks = seq // nk
    chunk = 2 * nk
    lane = lax.broadcasted_iota(jnp.int32, (1, LANES), 1)
    head0 = lane < HEAD_DIM

    m_s[...] = jnp.full_like(m_s, NEG_INF)
    l_s[...] = jnp.zeros_like(l_s)
    acc_s[...] = jnp.zeros_like(acc_s)
    ks[0:nk, :] = jnp.zeros((nk, LANES), bf16)
    vs[0:nk, 0:LANES] = jnp.zeros((nk, LANES), bf16)
    vs[:, LANES:2 * LANES] = jnp.ones((seq + nk, LANES), bf16)

    for p, (_, d) in enumerate(DILATED_PATTERNS):
        length = seq // d
        nb = length // nk
        chunks_per_residue = length // chunk

        def gather(t, carry):
            r = t // chunks_per_residue
            c = t % chunks_per_residue
            src = pl.ds(r + d * chunk * c, chunk, stride=d)
            dst = pl.multiple_of(t * chunk, chunk)
            q = q_ref[src, :] * (HEAD_DIM ** -0.5 * LOG2_E)
            q0 = jnp.where(head0, q, 0.0).astype(bf16)
            q1 = jnp.where(head0, 0.0, q).astype(bf16)
            for half in range(2):
                base = pl.multiple_of(2 * dst + half * chunk, chunk)
                qs[pl.ds(base, nk), :] = q0[half * nk:(half + 1) * nk]
                qs[pl.ds(base + nk, nk), :] = q1[half * nk:(half + 1) * nk]
            ks[pl.ds(nk + dst, chunk), :] = k_ref[src, :].astype(bf16)
            vs[pl.ds(nk + dst, chunk), 0:LANES] = v_ref[src, :].astype(bf16)
            return carry

        lax.fori_loop(0, seq // chunk, gather, 0)

        def scores(b, carry):
            first = jnp.asarray(b % nb == 0, jnp.int32)
            rows = pl.ds(pl.multiple_of(b * chunk, chunk), chunk)
            k = ks[pl.ds(pl.multiple_of(b * nk, nk), chunk), :]
            s = lax.dot_general(qs[rows, :], k, (((1,), (1,)), ((), ())),
                                preferred_element_type=f32) + bias_ref[p, first]
            s_buf[rows, :] = s
            m_buf[rows, :] = jnp.broadcast_to(jnp.max(s, axis=-1, keepdims=True), (chunk, LANES))
            return carry

        lax.fori_loop(0, n_blocks, scores, 0, unroll=SCORE_UNROLL)

        def block(b, carry):
            r = b // nb
            i = b % nb
            rows = pl.ds(pl.multiple_of(b * chunk, chunk), chunk)
            v = vs[pl.ds(pl.multiple_of(b * nk, nk), chunk), :]
            m = m_buf[rows, :]
            e = jnp.exp2(s_buf[rows, :] - jnp.concatenate([m, m], axis=1))
            pv = jnp.dot(e.astype(bf16), v, preferred_element_type=f32)
            m_blk = jnp.where(head0, m[0:nk], m[nk:chunk])
            l_blk = jnp.where(head0, pv[0:nk, LANES:], pv[nk:chunk, LANES:])
            pv_blk = jnp.where(head0, pv[0:nk, 0:LANES], pv[nk:chunk, 0:LANES])
            tok = pl.ds(r + d * nk * i, nk, stride=d)
            m_old = m_s[tok, :]
            m_new = jnp.maximum(m_old, m_blk)
            w_old = jnp.exp2(m_old - m_new)
            w_blk = jnp.exp2(m_blk - m_new)
            l_s[tok, :] = w_old * l_s[tok, :] + w_blk * l_blk
            acc_s[tok, :] = w_old * acc_s[tok, :] + w_blk * pv_blk
            m_s[tok, :] = m_new
            return carry

        lax.fori_loop(0, n_blocks, block, 0, unroll=ATTN_UNROLL)

    def finish(t, carry):
        rows = pl.ds(pl.multiple_of(t * chunk, chunk), chunk)
        o_ref[rows, :] = (acc_s[rows, :] / l_s[rows, :]).astype(o_ref.dtype)
        return carry

    lax.fori_loop(0, seq // chunk, finish, 0)


def _attention(qkv, band_bias, batch, seq):
    view = qkv.reshape(batch, seq, 3 * D_ATTN)
    pairs = D_ATTN // LANES

    def spec(offset):
        return pl.BlockSpec((None, seq, LANES), lambda b, hp: (b, 0, offset * pairs + hp))

    n_pat = len(DILATED_PATTERNS)
    o = pl.pallas_call(
        functools.partial(_attn_kernel, seq=seq),
        grid=(batch, pairs),
        in_specs=[spec(0), spec(1), spec(2),
                  pl.BlockSpec((n_pat, 2, None, 2 * ATTN_BLOCK, 2 * ATTN_BLOCK),
                               lambda b, hp: (0, 0, hp, 0, 0))],
        out_specs=pl.BlockSpec((None, seq, LANES), lambda b, hp: (b, 0, hp)),
        out_shape=jax.ShapeDtypeStruct((batch, seq, D_ATTN), bf16),
        scratch_shapes=[pltpu.VMEM((2 * seq, LANES), bf16),
                        pltpu.VMEM((seq + ATTN_BLOCK, LANES), bf16),
                        pltpu.VMEM((seq + ATTN_BLOCK, 2 * LANES), bf16),
                        pltpu.VMEM((2 * seq, 2 * ATTN_BLOCK), f32),
                        pltpu.VMEM((2 * seq, LANES), f32),
                        pltpu.VMEM((seq, LANES), f32), pltpu.VMEM((seq, LANES), f32),
                        pltpu.VMEM((seq, LANES), f32)],
        compiler_params=_params("parallel", "parallel"),
        name="attention",
    )(view, view, view, band_bias)
    return o.reshape(batch * seq, D_ATTN)


def _gelu_tanh(x):
    return 0.5 * x * (1.0 + jnp.tanh(np.sqrt(2.0 / np.pi) * (x + 0.044715 * x * x * x)))


def _lru_kernel(ug_ref, convw_ref, convb_ref, wg_ref, ba_ref, bx_ref, lam_ref, g_ref,
                y_ref, ubuf, hcarry, *, ts):
    pad = 8
    t = pl.program_id(1)

    @pl.when(t == 0)
    def _():
        ubuf[0:pad, :] = jnp.zeros((pad, D_LRU), f32)
        hcarry[...] = jnp.zeros_like(hcarry)

    @pl.when(t > 0)
    def _():
        ubuf[0:pad, :] = ubuf[ts:ts + pad, :]

    ubuf[pad:pad + ts, :] = ug_ref[:, 0:D_LRU]
    gate = ug_ref[:, D_LRU:2 * D_LRU]

    u = convb_ref[...] + convw_ref[CONV_WIDTH - 1:CONV_WIDTH, :] * ubuf[pad:pad + ts, :]
    for w in range(CONV_WIDTH - 1):
        back = CONV_WIDTH - 1 - w
        u = u + convw_ref[w:w + 1, :] * ubuf[pad - back:pad - back + ts, :]

    gates = jnp.dot(u.astype(bf16), wg_ref[...], preferred_element_type=f32)
    r = jax.nn.sigmoid(gates[:, 0:D_LRU] + ba_ref[...])
    i = jax.nn.sigmoid(gates[:, D_LRU:2 * D_LRU] + bx_ref[...])
    neg_lam = -lam_ref[...]
    softplus = jnp.maximum(neg_lam, 0.0) + jnp.log1p(jnp.exp(-jnp.abs(neg_lam)))
    log_a = (-LRU_C) * r * softplus
    a = jnp.exp(log_a)
    b = jnp.sqrt(-jnp.tanh(log_a) * (1.0 + a * a)) * (i * u)

    row = lax.broadcasted_iota(jnp.int32, (ts, 1), 0)
    shift = 1
    while shift < ts:
        live = row >= shift
        a_prev = jnp.where(live, pltpu.roll(a, shift, 0), 1.0)
        b_prev = jnp.where(live, pltpu.roll(b, shift, 0), 0.0)
        b = a * b_prev + b
        a = a * a_prev
        shift *= 2
    h = a * hcarry[...] + b
    hcarry[...] = h[ts - 1:ts, :]

    y = _gelu_tanh(gate) * h
    y = y * lax.rsqrt(jnp.mean(y * y, axis=-1, keepdims=True) + RMS_EPS) * g_ref[...]
    y_ref[...] = y.astype(y_ref.dtype)


def _block_diag(w):
    g, i, j = w.shape
    eye = jnp.eye(g, dtype=w.dtype)
    return jnp.einsum('gij,gh->gihj', w, eye).reshape(g * i, g * j)


def _lru_branch(ug, conv_w, conv_b, w_a, b_a, w_x, b_x, lam, g_lru, batch, seq, ts=256):
    wg = jnp.concatenate([_block_diag(w_a), _block_diag(w_x)], axis=1).astype(bf16)
    row = lambda v: v.reshape(1, D_LRU).astype(f32)
    const = lambda shape: pl.BlockSpec(shape, lambda b, t: (0, 0))
    view = ug.reshape(batch, seq, 2 * D_LRU)
    y = pl.pallas_call(
        functools.partial(_lru_kernel, ts=ts),
        grid=(batch, seq // ts),
        in_specs=[pl.BlockSpec((None, ts, 2 * D_LRU), lambda b, t: (b, t, 0)),
                  const((CONV_WIDTH, D_LRU)), const((1, D_LRU)),
                  const((D_LRU, 2 * D_LRU)), const((1, D_LRU)), const((1, D_LRU)),
                  const((1, D_LRU)), const((1, D_LRU))],
        out_specs=pl.BlockSpec((None, ts, D_LRU), lambda b, t: (b, t, 0)),
        out_shape=jax.ShapeDtypeStruct((batch, seq, D_LRU), bf16),
        scratch_shapes=[pltpu.VMEM((ts + 16, D_LRU), f32), pltpu.VMEM((1, D_LRU), f32)],
        compiler_params=_params("parallel", "arbitrary"),
        name="rglru",
    )(view, conv_w.reshape(CONV_WIDTH, D_LRU).astype(f32), row(conv_b), wg, row(b_a), row(b_x),
      row(lam), row(g_lru))
    return y.reshape(batch * seq, D_LRU)


ROW_TILE = D_MODEL // LANES


def _store_row_tiles(ref, rows):
    t = rows.shape[0]
    for s in range(ROW_TILE):
        ref[pl.ds(s, t, stride=ROW_TILE), :] = rows[:, s * LANES:(s + 1) * LANES]


def _load_row_tiles(ref, t):
    return jnp.concatenate([ref[pl.ds(s, t, stride=ROW_TILE), :] for s in range(ROW_TILE)],
                           axis=-1)


def _layer_norm(z, g, b):
    mu = jnp.mean(z, axis=-1, keepdims=True)
    zc = z - mu
    var = jnp.mean(zc * zc, axis=-1, keepdims=True)
    return zc * lax.rsqrt(var + LN_EPS) * g + b


def _mix_out_kernel(attn_ref, ylru, x_ref, w_ref, gattn, lng, lnb, *rest, with_router):
    if with_router:
        rw_ref, tri_ref, x1_ref, x1_rows_ref, route_ref, count_ref = rest
    else:
        (x1_ref,) = rest
    attn = attn_ref[...].astype(f32)
    attn = attn * lax.rsqrt(jnp.mean(attn * attn, axis=-1, keepdims=True) + RMS_EPS) * gattn[...]
    y = jnp.dot(attn.astype(bf16), w_ref[0:D_ATTN, :], preferred_element_type=f32)
    y = y + jnp.dot(ylru[...], w_ref[D_ATTN:, :], preferred_element_type=f32)
    x1 = _layer_norm(DEEPNORM_ALPHA * x_ref[...] + y, lng[...], lnb[...])
    x1_ref[...] = x1
    if with_router:
        _store_row_tiles(x1_rows_ref, x1)
        lane = lax.broadcasted_iota(jnp.int32, (1, LANES), 1)
        x_hi = x1.astype(bf16)
        x_lo = (x1 - x_hi.astype(f32)).astype(bf16)
        logits = (jnp.dot(x_hi, rw_ref[0], preferred_element_type=f32)
                  + jnp.dot(x_lo, rw_ref[0], preferred_element_type=f32)
                  + jnp.dot(x_hi, rw_ref[1], preferred_element_type=f32))
        logits = jnp.where(lane < N_EXPERTS, logits, -jnp.inf)
        v1 = jnp.max(logits, axis=-1, keepdims=True)
        i1 = jnp.min(jnp.where(logits == v1, lane, LANES), axis=-1, keepdims=True)
        rest_logits = jnp.where(lane == i1, -jnp.inf, logits)
        v2 = jnp.max(rest_logits, axis=-1, keepdims=True)
        i2 = jnp.min(jnp.where(rest_logits == v2, lane, LANES), axis=-1, keepdims=True)
        e2 = jnp.exp(v2 - v1)
        p1 = 1.0 / (1.0 + e2)
        p2 = e2 / (1.0 + e2)

        @pl.when(pl.program_id(0) == 0)
        def _():
            count_ref[...] = jnp.zeros_like(count_ref)

        tm = x1.shape[0]
        chosen = jnp.logical_or(lane == i1, lane == i2)
        rank = count_ref[...] + jnp.dot(tri_ref[...], chosen.astype(bf16),
                                        preferred_element_type=f32)
        count_ref[...] += jnp.sum(chosen.astype(f32), axis=0, keepdims=True)
        r1 = jnp.sum(jnp.where(lane == i1, rank, 0.0), axis=-1, keepdims=True)
        r2 = jnp.sum(jnp.where(lane == i2, rank, 0.0), axis=-1, keepdims=True)
        fields = (i1.astype(f32), i2.astype(f32), r1, r2, p1, p2)
        route = jnp.zeros((tm, LANES), f32)
        for k, val in enumerate(fields):
            route = jnp.where(lane == k, val, route)
        route_ref[...] = route


def _mix_out(attn, ylru, x2d, w_out_bf16, g_attn, ln_g, ln_b, router_w=None, tm=512):
    n = x2d.shape[0]
    with_router = router_w is not None
    tile = lambda width: pl.BlockSpec((tm, width), lambda i: (i, 0))
    const = lambda shape: pl.BlockSpec(shape, lambda i: (0, 0))
    in_specs = [tile(D_ATTN), tile(D_LRU), tile(D_MODEL), const((D_MODEL, D_MODEL)),
                const((1, D_ATTN)), const((1, D_MODEL)), const((1, D_MODEL))]
    args = [attn, ylru, x2d, w_out_bf16, g_attn.reshape(1, D_ATTN).astype(f32),
            ln_g.reshape(1, D_MODEL).astype(f32), ln_b.reshape(1, D_MODEL).astype(f32)]
    out_specs = [tile(D_MODEL)]
    out_shape = [jax.ShapeDtypeStruct((n, D_MODEL), f32)]
    if with_router:
        rw = jnp.zeros((D_MODEL, LANES), f32).at[:, :N_EXPERTS].set(router_w.astype(f32))
        rw_hi = rw.astype(bf16)
        rw_lo = (rw - rw_hi.astype(f32)).astype(bf16)
        strictly_lower = jnp.asarray(np.tri(tm, k=-1), bf16)
        in_specs += [pl.BlockSpec((2, D_MODEL, LANES), lambda i: (0, 0, 0)), const((tm, tm))]
        args += [jnp.stack([rw_hi, rw_lo]), strictly_lower]
        out_specs += [pl.BlockSpec((tm * ROW_TILE, LANES), lambda i: (i, 0)), tile(LANES),
                      const((1, LANES))]
        out_shape += [jax.ShapeDtypeStruct((n * ROW_TILE, LANES), f32),
                      jax.ShapeDtypeStruct((n, LANES), f32), jax.ShapeDtypeStruct((1, LANES), f32)]
    res = pl.pallas_call(
        functools.partial(_mix_out_kernel, with_router=with_router),
        grid=(n // tm,),
        in_specs=in_specs,
        out_specs=out_specs,
        out_shape=out_shape,
        compiler_params=_params("arbitrary" if with_router else "parallel"),
        name="mix_out_router" if with_router else "mix_out",
    )(*args)
    return res if with_router else (res[0], None, None, None)


def _ffn_kernel(tile_expert_ref, n_used_ref, x_ref, wg_ref, wu_ref, wd_ref, *rest,
                n_chunks, fuse_ln):
    if fuse_ln:
        lng, lnb, out_ref, acc_ref, xb_ref = rest
    else:
        out_ref, acc_ref, xb_ref = rest
    del tile_expert_ref
    j = pl.program_id(1)
    used = pl.program_id(0) < n_used_ref[0]

    @pl.when(j == 0)
    def _():
        acc_ref[...] = jnp.zeros_like(acc_ref)

    @pl.when(used & (j == 0))
    def _():
        if fuse_ln:
            xb_ref[...] = x_ref[...].astype(bf16)
        else:
            xb_ref[...] = _load_row_tiles(x_ref, xb_ref.shape[0]).astype(bf16)

    @pl.when(used)
    def _():
        xb = xb_ref[...]
        g = jnp.dot(xb, wg_ref[...], preferred_element_type=f32)
        u = jnp.dot(xb, wu_ref[...], preferred_element_type=f32)
        h = (g * jax.nn.sigmoid(g)) * u
        acc_ref[...] += jnp.dot(h.astype(bf16), wd_ref[...], preferred_element_type=f32)

    @pl.when(j == n_chunks - 1)
    def _():
        if fuse_ln:
            out_ref[...] = _layer_norm(DEEPNORM_ALPHA * x_ref[...] + acc_ref[...],
                                       lng[...], lnb[...])
        else:
            _store_row_tiles(out_ref, acc_ref[...])


def _ffn(x2d, tile_expert, n_used, w_gate, w_up, w_down, ln=None, *, tm, tf, name):
    fuse_ln = ln is not None
    rows = x2d.shape[0] if fuse_ln else x2d.shape[0] // ROW_TILE
    d_ff = w_gate.shape[2]
    n_chunks = d_ff // tf
    io_block = (tm, D_MODEL) if fuse_ln else (tm * ROW_TILE, LANES)

    def chunk(i, j, nu):
        return jnp.where(i < nu[0], j, n_chunks - 1)

    in_specs = [pl.BlockSpec(io_block, lambda i, j, te, nu: (i, 0)),
                pl.BlockSpec((None, D_MODEL, tf), lambda i, j, te, nu: (te[i], 0, chunk(i, j, nu))),
                pl.BlockSpec((None, D_MODEL, tf), lambda i, j, te, nu: (te[i], 0, chunk(i, j, nu))),
                pl.BlockSpec((None, tf, D_MODEL), lambda i, j, te, nu: (te[i], chunk(i, j, nu), 0))]
    args = [x2d, w_gate, w_up, w_down]
    if fuse_ln:
        in_specs += [pl.BlockSpec((1, D_MODEL), lambda i, j, te, nu: (0, 0))] * 2
        args += [v.reshape(1, D_MODEL).astype(f32) for v in ln]
    return pl.pallas_call(
        functools.partial(_ffn_kernel, n_chunks=n_chunks, fuse_ln=fuse_ln),
        grid_spec=pltpu.PrefetchScalarGridSpec(
            num_scalar_prefetch=2,
            grid=(rows // tm, n_chunks),
            in_specs=in_specs,
            out_specs=pl.BlockSpec(io_block, lambda i, j, te, nu: (i, 0)),
            scratch_shapes=[pltpu.VMEM((tm, D_MODEL), f32), pltpu.VMEM((tm, D_MODEL), bf16)]),
        out_shape=jax.ShapeDtypeStruct(x2d.shape, f32),
        compiler_params=_params("parallel", "arbitrary"),
        name=name,
    )(tile_expert, n_used, *args)


def _dispatch_kernel(fill_ref, pos_ref, x_ref, xs_hbm, zero_buf, sem, *, tt, tm):
    step = pl.program_id(0)

    n_fill = fill_ref.shape[0] // 2

    @pl.when(step == 0)
    def _():
        zero_buf[...] = jnp.zeros_like(zero_buf)

        def fill_copy(f):
            start = pl.multiple_of(fill_ref[f] * ROW_TILE, ROW_TILE)
            return pltpu.make_async_copy(zero_buf, xs_hbm.at[pl.ds(start, tm * ROW_TILE)], sem)

        for f in range(n_fill):
            @pl.when(fill_ref[n_fill + f] > 0)
            def _():
                fill_copy(f).start()
        for f in range(n_fill):
            @pl.when(fill_ref[n_fill + f] > 0)
            def _():
                fill_copy(f).wait()

    def issue(t, carry):
        src = pl.multiple_of(t * ROW_TILE, ROW_TILE)
        for k in range(TOP_K):
            dst = pl.multiple_of(pos_ref[0, TOP_K * t + k] * ROW_TILE, ROW_TILE)
            pltpu.make_async_copy(x_ref.at[pl.ds(src, ROW_TILE)],
                                  xs_hbm.at[pl.ds(dst, ROW_TILE)], sem).start()
        return carry

    lax.fori_loop(0, tt, issue, 0, unroll=DMA_UNROLL)
    for _ in range(TOP_K):
        pltpu.make_async_copy(x_ref, xs_hbm.at[pl.ds(0, tt * ROW_TILE)], sem).wait()


def _dispatch(x1_rows, pos_blocks, fill, rows_sorted, *, tt, tm):
    n = x1_rows.shape[0] // ROW_TILE
    return pl.pallas_call(
        functools.partial(_dispatch_kernel, tt=tt, tm=tm),
        grid_spec=pltpu.PrefetchScalarGridSpec(
            num_scalar_prefetch=1,
            grid=(n // tt,),
            in_specs=[pl.BlockSpec((None, 1, TOP_K * tt), lambda i, fill: (i, 0, 0),
                                   memory_space=pltpu.SMEM),
                      pl.BlockSpec((tt * ROW_TILE, LANES), lambda i, fill: (i, 0))],
            out_specs=pl.BlockSpec(memory_space=pl.ANY),
            scratch_shapes=[pltpu.VMEM((tm * ROW_TILE, LANES), f32), pltpu.SemaphoreType.DMA(())]),
        out_shape=jax.ShapeDtypeStruct((rows_sorted * ROW_TILE, LANES), f32),
        compiler_params=_params("arbitrary"),
        name="moe_dispatch",
    )(fill, pos_blocks, x1_rows)


def _combine_kernel(pos_ref, route_ref, x1_ref, ys_hbm, lng, lnb, out_ref, buf, sem, *, tt):
    def issue(t, carry):
        dst = pl.multiple_of(t * ROW_TILE, ROW_TILE)
        for k in range(TOP_K):
            src = pl.multiple_of(pos_ref[0, TOP_K * t + k] * ROW_TILE, ROW_TILE)
            pltpu.make_async_copy(ys_hbm.at[pl.ds(src, ROW_TILE)],
                                  buf.at[k, pl.ds(dst, ROW_TILE)], sem).start()
        return carry

    lax.fori_loop(0, tt, issue, 0, unroll=DMA_UNROLL)
    for k in range(TOP_K):
        pltpu.make_async_copy(ys_hbm.at[pl.ds(0, tt * ROW_TILE)], buf.at[k], sem).wait()

    lane = lax.broadcasted_iota(jnp.int32, (1, LANES), 1)
    route = route_ref[...]
    p1 = jnp.sum(jnp.where(lane == 4, route, 0.0), axis=-1, keepdims=True)
    p2 = jnp.sum(jnp.where(lane == 5, route, 0.0), axis=-1, keepdims=True)
    y = p1 * _load_row_tiles(buf.at[0], tt) + p2 * _load_row_tiles(buf.at[1], tt)
    out_ref[...] = _layer_norm(DEEPNORM_ALPHA * x1_ref[...] + y, lng[...], lnb[...])


def _combine(pos_blocks, route, x1, y_sorted, ln_g, ln_b, *, tt):
    n = x1.shape[0]
    return pl.pallas_call(
        functools.partial(_combine_kernel, tt=tt),
        grid=(n // tt,),
        in_specs=[pl.BlockSpec((None, 1, TOP_K * tt), lambda i: (i, 0, 0), memory_space=pltpu.SMEM),
                  pl.BlockSpec((tt, LANES), lambda i: (i, 0)),
                  pl.BlockSpec((tt, D_MODEL), lambda i: (i, 0)),
                  pl.BlockSpec(memory_space=pl.ANY),
                  pl.BlockSpec((1, D_MODEL), lambda i: (0, 0)),
                  pl.BlockSpec((1, D_MODEL), lambda i: (0, 0))],
        out_specs=pl.BlockSpec((tt, D_MODEL), lambda i: (i, 0)),
        out_shape=jax.ShapeDtypeStruct((n, D_MODEL), f32),
        scratch_shapes=[pltpu.VMEM((TOP_K, tt * ROW_TILE, LANES), f32),
                        pltpu.SemaphoreType.DMA(())],
        compiler_params=_params("arbitrary"),
        name="moe_combine",
    )(pos_blocks, route, x1, y_sorted, ln_g.reshape(1, D_MODEL).astype(f32),
      ln_b.reshape(1, D_MODEL).astype(f32))


def _moe(x1, x1_rows, route, counts, w_gate, w_up, w_down, ln_g, ln_b, *, tm, tf, tt):
    n = x1.shape[0]
    i32 = jnp.int32
    counts = counts[0, :N_EXPERTS].astype(i32)
    padded = (counts + tm - 1) // tm * tm
    ends = jnp.cumsum(padded)
    offsets = ends - padded
    experts = route[:, 0:TOP_K].astype(i32)
    ranks = route[:, TOP_K:2 * TOP_K].astype(i32)
    pos = (offsets[experts] + ranks).reshape(n // tt, 1, TOP_K * tt)
    n_tiles = TOP_K * n // tm + N_EXPERTS
    n_used = (ends[-1] // tm).astype(i32).reshape(1)
    tile_ids = jnp.arange(n_tiles, dtype=i32)
    tile_expert = jnp.sum((tile_ids[:, None] >= (ends // tm)[None, :]).astype(i32), axis=1)
    last_expert = jnp.max(jnp.where(counts > 0, jnp.arange(N_EXPERTS, dtype=i32), 0))
    tile_expert = jnp.minimum(tile_expert, last_expert).astype(i32)
    tail_tiles = n_used[0] + jnp.arange(N_EXPERTS, dtype=i32)
    fill = jnp.concatenate([ends - tm, tail_tiles * tm,
                            (counts > 0).astype(i32), (tail_tiles < n_tiles).astype(i32)]).astype(i32)

    x_sorted = _dispatch(x1_rows, pos, fill, n_tiles * tm, tt=tt, tm=tm)
    y_sorted = _ffn(x_sorted, tile_expert, n_used, w_gate, w_up, w_down, tm=tm, tf=tf,
                    name="ffn_experts")
    return _combine(pos, route, x1, y_sorted, ln_g, ln_b, tt=tt)


def kernel(x, w_in, conv_w, conv_b, w_a, b_a, w_x, b_x, lru_lambda, rel_bias, g_attn, g_lru, w_out, ln1_g, ln1_b, ln2_g, ln2_b, ffn_w_gate, ffn_w_up, ffn_w_down, router_w, moe_w_gate, moe_w_up, moe_w_down):
    batch, seq, _ = x.shape
    n = batch * seq
    h = x.reshape(n, D_MODEL).astype(f32)
    band_bias = jnp.stack([_band_bias(rel_bias, d) for _, d in DILATED_PATTERNS])
    for layer in range(DEPTH):
        qkv, ug = _in_proj(h, w_in[layer].astype(bf16))
        attn = _attention(qkv, band_bias, batch, seq)
        ylru = _lru_branch(ug, conv_w[layer], conv_b[layer], w_a[layer], b_a[layer], w_x[layer],
                           b_x[layer], lru_lambda[layer], g_lru[layer], batch, seq)
        j = layer // 2
        dense = layer % 2 == 0
        x1, x1_rows, route, counts = _mix_out(
            attn, ylru, h, w_out[layer].astype(bf16), g_attn[layer], ln1_g[layer],
            ln1_b[layer], None if dense else router_w[j])
        if dense:
            tm = 512
            h = _ffn(x1, jnp.zeros((n // tm,), jnp.int32), jnp.full((1,), n // tm, jnp.int32),
                     ffn_w_gate[j][None].astype(bf16), ffn_w_up[j][None].astype(bf16),
                     ffn_w_down[j][None].astype(bf16), (ln2_g[layer], ln2_b[layer]),
                     tm=tm, tf=1408, name="ffn_dense")
        else:
            h = _moe(x1, x1_rows, route, counts, moe_w_gate[j].astype(bf16), moe_w_up[j].astype(bf16),
                     moe_w_down[j].astype(bf16), ln2_g[layer], ln2_b[layer],
                     tm=512, tf=1792, tt=256)
    return h.reshape(batch, seq, D_MODEL).astype(x.dtype)
```

```python
import functools

import numpy as np
import jax
import jax.numpy as jnp
from jax import lax
from jax.experimental import pallas as pl
from jax.experimental.pallas import tpu as pltpu

D_MODEL = 1024
N_HEADS = 8
HEAD_DIM = 64
D_ATTN = N_HEADS * HEAD_DIM
D_LRU = 512
N_LRU_BLOCKS = 8
LRU_BLOCK = D_LRU // N_LRU_BLOCKS
CONV_WIDTH = 4
LRU_C = 8.0
DILATED_PATTERNS = ((128, 1), (512, 4), (2048, 16))
ATTN_BLOCK = 128
N_BUCKETS = 32
MAX_DISTANCE = 2048
D_IN = 3 * D_ATTN + 2 * D_LRU
N_EXPERTS = 8
TOP_K = 2
DEPTH = 2
DEEPNORM_ALPHA = (2.0 * DEPTH) ** 0.25
LN_EPS = 1e-5
RMS_EPS = 1e-6
NEG_INF = -1e30
LOG2_E = float(np.log2(np.e))

LANES = 128
DMA_UNROLL = 8
SCORE_UNROLL = 8
ATTN_UNROLL = 4
VMEM_LIMIT_BYTES = 56 * 1024 * 1024

f32 = jnp.float32
bf16 = jnp.bfloat16


def _params(*semantics):
    return pltpu.CompilerParams(dimension_semantics=semantics,
                                vmem_limit_bytes=VMEM_LIMIT_BYTES)


def _in_proj_kernel(x_ref, w_ref, qkv_ref, ug_ref):
    xb = x_ref[...].astype(bf16)
    n_qkv = qkv_ref.shape[1]
    step = 512
    for c in range(0, D_IN, step):
        p = jnp.dot(xb, w_ref[:, c:c + step], preferred_element_type=f32)
        if c < n_qkv:
            qkv_ref[:, c:c + step] = p
        else:
            ug_ref[:, c - n_qkv:c - n_qkv + step] = p


def _in_proj(x2d, w_in_bf16, tm=512):
    n = x2d.shape[0]
    return pl.pallas_call(
        _in_proj_kernel,
        grid=(n // tm,),
        in_specs=[pl.BlockSpec((tm, D_MODEL), lambda i: (i, 0)),
                  pl.BlockSpec((D_MODEL, D_IN), lambda i: (0, 0))],
        out_specs=[pl.BlockSpec((tm, 3 * D_ATTN), lambda i: (i, 0)),
                   pl.BlockSpec((tm, 2 * D_LRU), lambda i: (i, 0))],
        out_shape=[jax.ShapeDtypeStruct((n, 3 * D_ATTN), f32),
                   jax.ShapeDtypeStruct((n, 2 * D_LRU), f32)],
        compiler_params=_params("parallel"),
        name="in_proj",
    )(x2d, w_in_bf16)


def _t5_bucket(dist):
    max_exact = N_BUCKETS // 2
    d = np.maximum(dist, 1).astype(np.float32)
    large = max_exact + (np.log(d / max_exact) / np.log(MAX_DISTANCE / max_exact)
                         * (N_BUCKETS - max_exact)).astype(np.int32)
    large = np.minimum(large, N_BUCKETS - 1)
    return np.where(dist < max_exact, dist, large).astype(np.int32)


def _band_bias(rel_bias, dilation):
    nk = ATTN_BLOCK
    qi = np.arange(nk)[:, None]
    kj = np.arange(2 * nk)[None, :]
    delta = qi + nk - kj
    band = (delta >= 0) & (delta <= nk)
    bucket = _t5_bucket(np.clip(delta, 0, nk) * dilation)
    onehot = np.eye(N_BUCKETS, dtype=np.float32)[bucket.reshape(-1)]
    bias = jnp.dot(jnp.asarray(onehot), rel_bias.astype(f32), precision=lax.Precision.HIGHEST)
    bias = jnp.transpose(bias.reshape(nk, 2 * nk, N_HEADS), (2, 0, 1))
    valid = np.stack([band, band & (kj >= nk)])[:, None]
    bias = jnp.where(jnp.asarray(valid), bias[None], NEG_INF)
    return (bias * LOG2_E).reshape(2, N_HEADS // 2, 2 * nk, 2 * nk)


def _attn_kernel(q_ref, k_ref, v_ref, bias_ref, o_ref, qs, ks, vs, s_buf, m_buf, m_s, l_s, acc_s,
                 *, seq):
    nk = ATTN_BLOCK
    n_blocks = seq // nk
    chunk = 2 * nk
    lane = lax.broadcasted_iota(jnp.int32, (1, LANES), 1)
    head0 = lane < HEAD_DIM

    m_s[...] = jnp.full_like(m_s, NEG_INF)
    l_s[...] = jnp.zeros_like(l_s)
    acc_s[...] = jnp.zeros_like(acc_s)
    ks[0:nk, :] = jnp.zeros((nk, LANES), bf16)
    vs[0:nk, 0:LANES] = jnp.zeros((nk, LANES), bf16)
    vs[:, LANES:2 * LANES] = jnp.ones((seq + nk, LANES), bf16)

    for p, (_, d) in enumerate(DILATED_PATTERNS):
        length = seq // d
        nb = length // nk
        chunks_per_residue = length // chunk

        def gather(t, carry):
            r = t // chunks_per_residue
            c = t % chunks_per_residue
            src = pl.ds(r + d * chunk * c, chunk, stride=d)
            dst = pl.multiple_of(t * chunk, chunk)
            q = q_ref[src, :] * (HEAD_DIM ** -0.5 * LOG2_E)
            q0 = jnp.where(head0, q, 0.0).astype(bf16)
            q1 = jnp.where(head0, 0.0, q).astype(bf16)
            for half in range(2):
                base = pl.multiple_of(2 * dst + half * chunk, chunk)
                qs[pl.ds(base, nk), :] = q0[half * nk:(half + 1) * nk]
                qs[pl.ds(base + nk, nk), :] = q1[half * nk:(half + 1) * nk]
            ks[pl.ds(nk + dst, chunk), :] = k_ref[src, :].astype(bf16)
            vs[pl.ds(nk + dst, chunk), 0:LANES] = v_ref[src, :].astype(bf16)
            return carry

        lax.fori_loop(0, seq // chunk, gather, 0)

        def scores(b, carry):
            first = jnp.asarray(b % nb == 0, jnp.int32)
            rows = pl.ds(pl.multiple_of(b * chunk, chunk), chunk)
            k = ks[pl.ds(pl.multiple_of(b * nk, nk), chunk), :]
            s = lax.dot_general(qs[rows, :], k, (((1,), (1,)), ((), ())),
                                preferred_element_type=f32) + bias_ref[p, first]
            s_buf[rows, :] = s
            m_buf[rows, :] = jnp.broadcast_to(jnp.max(s, axis=-1, keepdims=True), (chunk, LANES))
            return carry

        lax.fori_loop(0, n_blocks, scores, 0, unroll=SCORE_UNROLL)

        def block(b, carry):
            r = b // nb
            i = b % nb
            rows = pl.ds(pl.multiple_of(b * chunk, chunk), chunk)
            v = vs[pl.ds(pl.multiple_of(b * nk, nk), chunk), :]
            m = m_buf[rows, :]
            e = jnp.exp2(s_buf[rows, :] - jnp.concatenate([m, m], axis=1))
            pv = jnp.dot(e.astype(bf16), v, preferred_element_type=f32)
            m_blk = jnp.where(head0, m[0:nk], m[nk:chunk])
            l_blk = jnp.where(head0, pv[0:nk, LANES:], pv[nk:chunk, LANES:])
            pv_blk = jnp.where(head0, pv[0:nk, 0:LANES], pv[nk:chunk, 0:LANES])
            tok = pl.ds(r + d * nk * i, nk, stride=d)
            m_old = m_s[tok, :]
            m_new = jnp.maximum(m_old, m_blk)
            w_old = jnp.exp2(m_old - m_new)
            w_blk = jnp.exp2(m_blk - m_new)
            l_s[tok, :] = w_old * l_s[tok, :] + w_blk * l_blk
            acc_s[tok, :] = w_old * acc_s[tok, :] + w_blk * pv_blk
            m_s[tok, :] = m_new
            return carry

        lax.fori_loop(0, n_blocks, block, 0, unroll=ATTN_UNROLL)

    def finish(t, carry):
        rows = pl.ds(pl.multiple_of(t * chunk, chunk), chunk)
        o_ref[rows, :] = (acc_s[rows, :] / l_s[rows, :]).astype(o_ref.dtype)
        return carry

    lax.fori_loop(0, seq // chunk, finish, 0)


def _attention(qkv, band_bias, batch, seq):
    view = qkv.reshape(batch, seq, 3 * D_ATTN)
    pairs = D_ATTN // LANES

    def spec(offset):
        return pl.BlockSpec((None, seq, LANES), lambda b, hp: (b, 0, offset * pairs + hp))

    n_pat = len(DILATED_PATTERNS)
    o = pl.pallas_call(
        functools.partial(_attn_kernel, seq=seq),
        grid=(batch, pairs),
        in_specs=[spec(0), spec(1), spec(2),
                  pl.BlockSpec((n_pat, 2, None, 2 * ATTN_BLOCK, 2 * ATTN_BLOCK),
                               lambda b, hp: (0, 0, hp, 0, 0))],
        out_specs=pl.BlockSpec((None, seq, LANES), lambda b, hp: (b, 0, hp)),
        out_shape=jax.ShapeDtypeStruct((batch, seq, D_ATTN), bf16),
        scratch_shapes=[pltpu.VMEM((2 * seq, LANES), bf16),
                        pltpu.VMEM((seq + ATTN_BLOCK, LANES), bf16),
                        pltpu.VMEM((seq + ATTN_BLOCK, 2 * LANES), bf16),
                        pltpu.VMEM((2 * seq, 2 * ATTN_BLOCK), f32),
                        pltpu.VMEM((2 * seq, LANES), f32),
                        pltpu.VMEM((seq, LANES), f32), pltpu.VMEM((seq, LANES), f32),
                        pltpu.VMEM((seq, LANES), f32)],
        compiler_params=_params("parallel", "parallel"),
        name="attention",
    )(view, view, view, band_bias)
    return o.reshape(batch * seq, D_ATTN)


def _gelu_tanh(x):
    return 0.5 * x * (1.0 + jnp.tanh(np.sqrt(2.0 / np.pi) * (x + 0.044715 * x * x * x)))


def _lru_kernel(ug_ref, convw_ref, convb_ref, wg_ref, ba_ref, bx_ref, lam_ref, g_ref,
                y_ref, ubuf, hcarry, *, ts):
    pad = 8
    t = pl.program_id(1)

    @pl.when(t == 0)
    def _():
        ubuf[0:pad, :] = jnp.zeros((pad, D_LRU), f32)
        hcarry[...] = jnp.zeros_like(hcarry)

    @pl.when(t > 0)
    def _():
        ubuf[0:pad, :] = ubuf[ts:ts + pad, :]

    ubuf[pad:pad + ts, :] = ug_ref[:, 0:D_LRU]
    gate = ug_ref[:, D_LRU:2 * D_LRU]

    u = convb_ref[...] + convw_ref[CONV_WIDTH - 1:CONV_WIDTH, :] * ubuf[pad:pad + ts, :]
    for w in range(CONV_WIDTH - 1):
        back = CONV_WIDTH - 1 - w
        u = u + convw_ref[w:w + 1, :] * ubuf[pad - back:pad - back + ts, :]

    gates = jnp.dot(u.astype(bf16), wg_ref[...], preferred_element_type=f32)
    r = jax.nn.sigmoid(gates[:, 0:D_LRU] + ba_ref[...])
    i = jax.nn.sigmoid(gates[:, D_LRU:2 * D_LRU] + bx_ref[...])
    neg_lam = -lam_ref[...]
    softplus = jnp.maximum(neg_lam, 0.0) + jnp.log1p(jnp.exp(-jnp.abs(neg_lam)))
    log_a = (-LRU_C) * r * softplus
    a = jnp.exp(log_a)
    b = jnp.sqrt(-jnp.tanh(log_a) * (1.0 + a * a)) * (i * u)

    row = lax.broadcasted_iota(jnp.int32, (ts, 1), 0)
    shift = 1
    while shift < ts:
        live = row >= shift
        a_prev = jnp.where(live, pltpu.roll(a, shift, 0), 1.0)
        b_prev = jnp.where(live, pltpu.roll(b, shift, 0), 0.0)
        b = a * b_prev + b
        a = a * a_prev
        shift *= 2
    h = a * hcarry[...] + b
    hcarry[...] = h[ts - 1:ts, :]

    y = _gelu_tanh(gate) * h
    y = y * lax.rsqrt(jnp.mean(y * y, axis=-1, keepdims=True) + RMS_EPS) * g_ref[...]
    y_ref[...] = y.astype(y_ref.dtype)


def _block_diag(w):
    g, i, j = w.shape
    eye = jnp.eye(g, dtype=w.dtype)
    return jnp.einsum('gij,gh->gihj', w, eye).reshape(g * i, g * j)


def _lru_branch(ug, conv_w, conv_b, w_a, b_a, w_x, b_x, lam, g_lru, batch, seq, ts=256):
    wg = jnp.concatenate([_block_diag(w_a), _block_diag(w_x)], axis=1).astype(bf16)
    row = lambda v: v.reshape(1, D_LRU).astype(f32)
    const = lambda shape: pl.BlockSpec(shape, lambda b, t: (0, 0))
    view = ug.reshape(batch, seq, 2 * D_LRU)
    y = pl.pallas_call(
        functools.partial(_lru_kernel, ts=ts),
        grid=(batch, seq // ts),
        in_specs=[pl.BlockSpec((None, ts, 2 * D_LRU), lambda b, t: (b, t, 0)),
                  const((CONV_WIDTH, D_LRU)), const((1, D_LRU)),
                  const((D_LRU, 2 * D_LRU)), const((1, D_LRU)), const((1, D_LRU)),
                  const((1, D_LRU)), const((1, D_LRU))],
        out_specs=pl.BlockSpec((None, ts, D_LRU), lambda b, t: (b, t, 0)),
        out_shape=jax.ShapeDtypeStruct((batch, seq, D_LRU), bf16),
        scratch_shapes=[pltpu.VMEM((ts + 16, D_LRU), f32), pltpu.VMEM((1, D_LRU), f32)],
        compiler_params=_params("parallel", "arbitrary"),
        name="rglru",
    )(view, conv_w.reshape(CONV_WIDTH, D_LRU).astype(f32), row(conv_b), wg, row(b_a), row(b_x),
      row(lam), row(g_lru))
    return y.reshape(batch * seq, D_LRU)


ROW_TILE = D_MODEL // LANES


def _store_row_tiles(ref, rows):
    t = rows.shape[0]
    for s in range(ROW_TILE):
        ref[pl.ds(s, t, stride=ROW_TILE), :] = rows[:, s * LANES:(s + 1) * LANES]


def _load_row_tiles(ref, t):
    return jnp.concatenate([ref[pl.ds(s, t, stride=ROW_TILE), :] for s in range(ROW_TILE)],
                           axis=-1)


def _layer_norm(z, g, b):
    mu = jnp.mean(z, axis=-1, keepdims=True)
    zc = z - mu
    var = jnp.mean(zc * zc, axis=-1, keepdims=True)
    return zc * lax.rsqrt(var + LN_EPS) * g + b


def _mix_out_kernel(attn_ref, ylru, x_ref, w_ref, gattn, lng, lnb, *rest, with_router):
    if with_router:
        rw_ref, tri_ref, x1_ref, x1_rows_ref, route_ref, count_ref = rest
    else:
        (x1_ref,) = rest
    attn = attn_ref[...].astype(f32)
    attn = attn * lax.rsqrt(jnp.mean(attn * attn, axis=-1, keepdims=True) + RMS_EPS) * gattn[...]
    y = jnp.dot(attn.astype(bf16), w_ref[0:D_ATTN, :], preferred_element_type=f32)
    y = y + jnp.dot(ylru[...], w_ref[D_ATTN:, :], preferred_element_type=f32)
    x1 = _layer_norm(DEEPNORM_ALPHA * x_ref[...] + y, lng[...], lnb[...])
    x1_ref[...] = x1
    if with_router:
        _store_row_tiles(x1_rows_ref, x1)
        lane = lax.broadcasted_iota(jnp.int32, (1, LANES), 1)
        x_hi = x1.astype(bf16)
        x_lo = (x1 - x_hi.astype(f32)).astype(bf16)
        logits = (jnp.dot(x_hi, rw_ref[0], preferred_element_type=f32)
                  + jnp.dot(x_lo, rw_ref[0], preferred_element_type=f32)
                  + jnp.dot(x_hi, rw_ref[1], preferred_element_type=f32))
        logits = jnp.where(lane < N_EXPERTS, logits, -jnp.inf)
        v1 = jnp.max(logits, axis=-1, keepdims=True)
        i1 = jnp.min(jnp.where(logits == v1, lane, LANES), axis=-1, keepdims=True)
        rest_logits = jnp.where(lane == i1, -jnp.inf, logits)
        v2 = jnp.max(rest_logits, axis=-1, keepdims=True)
        i2 = jnp.min(jnp.where(rest_logits == v2, lane, LANES), axis=-1, keepdims=True)
        e2 = jnp.exp(v2 - v1)
        p1 = 1.0 / (1.0 + e2)
        p2 = e2 / (1.0 + e2)

        @pl.when(pl.program_id(0) == 0)
        def _():
            count_ref[...] = jnp.zeros_like(count_ref)

        tm = x1.shape[0]
        chosen = jnp.logical_or(lane == i1, lane == i2)
        rank = count_ref[...] + jnp.dot(tri_ref[...], chosen.astype(bf16),
                                        preferred_element_type=f32)
        count_ref[...] += jnp.sum(chosen.astype(f32), axis=0, keepdims=True)
        r1 = jnp.sum(jnp.where(lane == i1, rank, 0.0), axis=-1, keepdims=True)
        r2 = jnp.sum(jnp.where(lane == i2, rank, 0.0), axis=-1, keepdims=True)
        fields = (i1.astype(f32), i2.astype(f32), r1, r2, p1, p2)
        route = jnp.zeros((tm, LANES), f32)
        for k, val in enumerate(fields):
            route = jnp.where(lane == k, val, route)
        route_ref[...] = route


def _mix_out(attn, ylru, x2d, w_out_bf16, g_attn, ln_g, ln_b, router_w=None, tm=512):
    n = x2d.shape[0]
    with_router = router_w is not None
    tile = lambda width: pl.BlockSpec((tm, width), lambda i: (i, 0))
    const = lambda shape: pl.BlockSpec(shape, lambda i: (0, 0))
    in_specs = [tile(D_ATTN), tile(D_LRU), tile(D_MODEL), const((D_MODEL, D_MODEL)),
                const((1, D_ATTN)), const((1, D_MODEL)), const((1, D_MODEL))]
    args = [attn, ylru, x2d, w_out_bf16, g_attn.reshape(1, D_ATTN).astype(f32),
            ln_g.reshape(1, D_MODEL).astype(f32), ln_b.reshape(1, D_MODEL).astype(f32)]
    out_specs = [tile(D_MODEL)]
    out_shape = [jax.ShapeDtypeStruct((n, D_MODEL), f32)]
    if with_router:
        rw = jnp.zeros((D_MODEL, LANES), f32).at[:, :N_EXPERTS].set(router_w.astype(f32))
        rw_hi = rw.astype(bf16)
        rw_lo = (rw - rw_hi.astype(f32)).astype(bf16)
        strictly_lower = jnp.asarray(np.tri(tm, k=-1), bf16)
        in_specs += [pl.BlockSpec((2, D_MODEL, LANES), lambda i: (0, 0, 0)), const((tm, tm))]
        args += [jnp.stack([rw_hi, rw_lo]), strictly_lower]
        out_specs += [pl.BlockSpec((tm * ROW_TILE, LANES), lambda i: (i, 0)), tile(LANES),
                      const((1, LANES))]
        out_shape += [jax.ShapeDtypeStruct((n * ROW_TILE, LANES), f32),
                      jax.ShapeDtypeStruct((n, LANES), f32), jax.ShapeDtypeStruct((1, LANES), f32)]
    res = pl.pallas_call(
        functools.partial(_mix_out_kernel, with_router=with_router),
        grid=(n // tm,),
        in_specs=in_specs,
        out_specs=out_specs,
        out_shape=out_shape,
        compiler_params=_params("arbitrary" if with_router else "parallel"),
        name="mix_out_router" if with_router else "mix_out",
    )(*args)
    return res if with_router else (res[0], None, None, None)


def _ffn_kernel(tile_expert_ref, n_used_ref, x_ref, wg_ref, wu_ref, wd_ref, *rest,
                n_chunks, fuse_ln):
    if fuse_ln:
        lng, lnb, out_ref, acc_ref, xb_ref = rest
    else:
        out_ref, acc_ref, xb_ref = rest
    del tile_expert_ref
    j = pl.program_id(1)
    used = pl.program_id(0) < n_used_ref[0]

    @pl.when(j == 0)
    def _():
        acc_ref[...] = jnp.zeros_like(acc_ref)

    @pl.when(used & (j == 0))
    def _():
        if fuse_ln:
            xb_ref[...] = x_ref[...].astype(bf16)
        else:
            xb_ref[...] = _load_row_tiles(x_ref, xb_ref.shape[0]).astype(bf16)

    @pl.when(used)
    def _():
        xb = xb_ref[...]
        g = jnp.dot(xb, wg_ref[...], preferred_element_type=f32)
        u = jnp.dot(xb, wu_ref[...], preferred_element_type=f32)
        h = (g * jax.nn.sigmoid(g)) * u
        acc_ref[...] += jnp.dot(h.astype(bf16), wd_ref[...], preferred_element_type=f32)

    @pl.when(j == n_chunks - 1)
    def _():
        if fuse_ln:
            out_ref[...] = _layer_norm(DEEPNORM_ALPHA * x_ref[...] + acc_ref[...],
                                       lng[...], lnb[...])
        else:
            _store_row_tiles(out_ref, acc_ref[...])


def _ffn(x2d, tile_expert, n_used, w_gate, w_up, w_down, ln=None, *, tm, tf, name):
    fuse_ln = ln is not None
    rows = x2d.shape[0] if fuse_ln else x2d.shape[0] // ROW_TILE
    d_ff = w_gate.shape[2]
    n_chunks = d_ff // tf
    io_block = (tm, D_MODEL) if fuse_ln else (tm * ROW_TILE, LANES)

    def chunk(i, j, nu):
        return jnp.where(i < nu[0], j, n_chunks - 1)

    in_specs = [pl.BlockSpec(io_block, lambda i, j, te, nu: (i, 0)),
                pl.BlockSpec((None, D_MODEL, tf), lambda i, j, te, nu: (te[i], 0, chunk(i, j, nu))),
                pl.BlockSpec((None, D_MODEL, tf), lambda i, j, te, nu: (te[i], 0, chunk(i, j, nu))),
                pl.BlockSpec((None, tf, D_MODEL), lambda i, j, te, nu: (te[i], chunk(i, j, nu), 0))]
    args = [x2d, w_gate, w_up, w_down]
    if fuse_ln:
        in_specs += [pl.BlockSpec((1, D_MODEL), lambda i, j, te, nu: (0, 0))] * 2
        args += [v.reshape(1, D_MODEL).astype(f32) for v in ln]
    return pl.pallas_call(
        functools.partial(_ffn_kernel, n_chunks=n_chunks, fuse_ln=fuse_ln),
        grid_spec=pltpu.PrefetchScalarGridSpec(
            num_scalar_prefetch=2,
            grid=(rows // tm, n_chunks),
            in_specs=in_specs,
            out_specs=pl.BlockSpec(io_block, lambda i, j, te, nu: (i, 0)),
            scratch_shapes=[pltpu.VMEM((tm, D_MODEL), f32), pltpu.VMEM((tm, D_MODEL), bf16)]),
        out_shape=jax.ShapeDtypeStruct(x2d.shape, f32),
        compiler_params=_params("parallel", "arbitrary"),
        name=name,
    )(tile_expert, n_used, *args)


def _dispatch_kernel(fill_ref, pos_ref, x_ref, xs_hbm, zero_buf, sem, *, tt, tm):
    step = pl.program_id(0)

    n_fill = fill_ref.shape[0] // 2

    @pl.when(step == 0)
    def _():
        zero_buf[...] = jnp.zeros_like(zero_buf)

        def fill_copy(f):
            start = pl.multiple_of(fill_ref[f] * ROW_TILE, ROW_TILE)
            return pltpu.make_async_copy(zero_buf, xs_hbm.at[pl.ds(start, tm * ROW_TILE)], sem)

        for f in range(n_fill):
            @pl.when(fill_ref[n_fill + f] > 0)
            def _():
                fill_copy(f).start()
        for f in range(n_fill):
            @pl.when(fill_ref[n_fill + f] > 0)
            def _():
                fill_copy(f).wait()

    def issue(t, carry):
        src = pl.multiple_of(t * ROW_TILE, ROW_TILE)
        for k in range(TOP_K):
            dst = pl.multiple_of(pos_ref[0, TOP_K * t + k] * ROW_TILE, ROW_TILE)
            pltpu.make_async_copy(x_ref.at[pl.ds(src, ROW_TILE)],
                                  xs_hbm.at[pl.ds(dst, ROW_TILE)], sem).start(priority=k)
        return carry

    lax.fori_loop(0, tt, issue, 0, unroll=DMA_UNROLL)
    for _ in range(TOP_K):
        pltpu.make_async_copy(x_ref, xs_hbm.at[pl.ds(0, tt * ROW_TILE)], sem).wait()


def _dispatch(x1_rows, pos_blocks, fill, rows_sorted, *, tt, tm):
    n = x1_rows.shape[0] // ROW_TILE
    return pl.pallas_call(
        functools.partial(_dispatch_kernel, tt=tt, tm=tm),
        grid_spec=pltpu.PrefetchScalarGridSpec(
            num_scalar_prefetch=1,
            grid=(n // tt,),
            in_specs=[pl.BlockSpec((None, 1, TOP_K * tt), lambda i, fill: (i, 0, 0),
                                   memory_space=pltpu.SMEM),
                      pl.BlockSpec((tt * ROW_TILE, LANES), lambda i, fill: (i, 0))],
            out_specs=pl.BlockSpec(memory_space=pl.ANY),
            scratch_shapes=[pltpu.VMEM((tm * ROW_TILE, LANES), f32), pltpu.SemaphoreType.DMA(())]),
        out_shape=jax.ShapeDtypeStruct((rows_sorted * ROW_TILE, LANES), f32),
        compiler_params=_params("arbitrary"),
        name="moe_dispatch",
    )(fill, pos_blocks, x1_rows)


def _combine_kernel(pos_ref, pos_next_ref, route_ref, x1_ref, ys_hbm, lng, lnb, out_ref, buf, sems,
                    *, tt, n_steps):
    step = pl.program_id(0)
    slot = step % 2

    def issue_tile(tile_pos_ref, into):
        def issue(t, carry):
            dst = pl.multiple_of(t * ROW_TILE, ROW_TILE)
            for k in range(TOP_K):
                src = pl.multiple_of(tile_pos_ref[0, TOP_K * t + k] * ROW_TILE, ROW_TILE)
                pltpu.make_async_copy(ys_hbm.at[pl.ds(src, ROW_TILE)],
                                      buf.at[into, k, pl.ds(dst, ROW_TILE)],
                                      sems.at[into]).start(priority=k)
            return carry

        lax.fori_loop(0, tt, issue, 0, unroll=DMA_UNROLL)

    @pl.when(step == 0)
    def _():
        issue_tile(pos_ref, 0)

    @pl.when(step + 1 < n_steps)
    def _():
        issue_tile(pos_next_ref, 1 - slot)

    for k in range(TOP_K):
        pltpu.make_async_copy(ys_hbm.at[pl.ds(0, tt * ROW_TILE)], buf.at[slot, k],
                              sems.at[slot]).wait()

    lane = lax.broadcasted_iota(jnp.int32, (1, LANES), 1)
    route = route_ref[...]
    p1 = jnp.sum(jnp.where(lane == 4, route, 0.0), axis=-1, keepdims=True)
    p2 = jnp.sum(jnp.where(lane == 5, route, 0.0), axis=-1, keepdims=True)
    y = p1 * _load_row_tiles(buf.at[slot, 0], tt) + p2 * _load_row_tiles(buf.at[slot, 1], tt)
    out_ref[...] = _layer_norm(DEEPNORM_ALPHA * x1_ref[...] + y, lng[...], lnb[...])


def _combine(pos_blocks, route, x1, y_sorted, ln_g, ln_b, *, tt):
    n = x1.shape[0]
    n_steps = n // tt
    pos_spec = lambda index: pl.BlockSpec((None, 1, TOP_K * tt), index, memory_space=pltpu.SMEM)
    return pl.pallas_call(
        functools.partial(_combine_kernel, tt=tt, n_steps=n_steps),
        grid=(n_steps,),
        in_specs=[pos_spec(lambda i: (i, 0, 0)),
                  pos_spec(lambda i: (jnp.minimum(i + 1, n_steps - 1), 0, 0)),
                  pl.BlockSpec((tt, LANES), lambda i: (i, 0)),
                  pl.BlockSpec((tt, D_MODEL), lambda i: (i, 0)),
                  pl.BlockSpec(memory_space=pl.ANY),
                  pl.BlockSpec((1, D_MODEL), lambda i: (0, 0)),
                  pl.BlockSpec((1, D_MODEL), lambda i: (0, 0))],
        out_specs=pl.BlockSpec((tt, D_MODEL), lambda i: (i, 0)),
        out_shape=jax.ShapeDtypeStruct((n, D_MODEL), f32),
        scratch_shapes=[pltpu.VMEM((2, TOP_K, tt * ROW_TILE, LANES), f32),
                        pltpu.SemaphoreType.DMA((2,))],
        compiler_params=_params("arbitrary"),
        name="moe_combine",
    )(pos_blocks, pos_blocks, route, x1, y_sorted, ln_g.reshape(1, D_MODEL).astype(f32),
      ln_b.reshape(1, D_MODEL).astype(f32))


def _moe(x1, x1_rows, route, counts, w_gate, w_up, w_down, ln_g, ln_b, *, tm, tf, tt):
    n = x1.shape[0]
    i32 = jnp.int32
    counts = counts[0, :N_EXPERTS].astype(i32)
    padded = (counts + tm - 1) // tm * tm
    ends = jnp.cumsum(padded)
    offsets = ends - padded
    experts = route[:, 0:TOP_K].astype(i32)
    ranks = route[:, TOP_K:2 * TOP_K].astype(i32)
    pos = (offsets[experts] + ranks).reshape(n // tt, 1, TOP_K * tt)
    n_tiles = TOP_K * n // tm + N_EXPERTS
    n_used = (ends[-1] // tm).astype(i32).reshape(1)
    tile_ids = jnp.arange(n_tiles, dtype=i32)
    tile_expert = jnp.sum((tile_ids[:, None] >= (ends // tm)[None, :]).astype(i32), axis=1)
    last_expert = jnp.max(jnp.where(counts > 0, jnp.arange(N_EXPERTS, dtype=i32), 0))
    tile_expert = jnp.minimum(tile_expert, last_expert).astype(i32)
    tail_tiles = n_used[0] + jnp.arange(N_EXPERTS, dtype=i32)
    fill = jnp.concatenate([ends - tm, tail_tiles * tm,
                            (counts > 0).astype(i32), (tail_tiles < n_tiles).astype(i32)]).astype(i32)

    x_sorted = _dispatch(x1_rows, pos, fill, n_tiles * tm, tt=tt, tm=tm)
    y_sorted = _ffn(x_sorted, tile_expert, n_used, w_gate, w_up, w_down, tm=tm, tf=tf,
                    name="ffn_experts")
    return _combine(pos, route, x1, y_sorted, ln_g, ln_b, tt=tt)


def kernel(x, w_in, conv_w, conv_b, w_a, b_a, w_x, b_x, lru_lambda, rel_bias, g_attn, g_lru, w_out, ln1_g, ln1_b, ln2_g, ln2_b, ffn_w_gate, ffn_w_up, ffn_w_down, router_w, moe_w_gate, moe_w_up, moe_w_down):
    batch, seq, _ = x.shape
    n = batch * seq
    h = x.reshape(n, D_MODEL).astype(f32)
    band_bias = jnp.stack([_band_bias(rel_bias, d) for _, d in DILATED_PATTERNS])
    for layer in range(DEPTH):
        qkv, ug = _in_proj(h, w_in[layer].astype(bf16))
        attn = _attention(qkv, band_bias, batch, seq)
        ylru = _lru_branch(ug, conv_w[layer], conv_b[layer], w_a[layer], b_a[layer], w_x[layer],
                           b_x[layer], lru_lambda[layer], g_lru[layer], batch, seq)
        j = layer // 2
        dense = layer % 2 == 0
        x1, x1_rows, route, counts = _mix_out(
            attn, ylru, h, w_out[layer].astype(bf16), g_attn[layer], ln1_g[layer],
            ln1_b[layer], None if dense else router_w[j])
        if dense:
            tm = 512
            h = _ffn(x1, jnp.zeros((n // tm,), jnp.int32), jnp.full((1,), n // tm, jnp.int32),
                     ffn_w_gate[j][None].astype(bf16), ffn_w_up[j][None].astype(bf16),
                     ffn_w_down[j][None].astype(bf16), (ln2_g[layer], ln2_b[layer]),
                     tm=tm, tf=1408, name="ffn_dense")
        else:
            h = _moe(x1, x1_rows, route, counts, moe_w_gate[j].astype(bf16), moe_w_up[j].astype(bf16),
                     moe_w_down[j].astype(bf16), ln2_g[layer], ln2_b[layer],
                     tm=512, tf=1792, tt=256)
    return h.reshape(batch, seq, D_MODEL).astype(x.dtype)
```

```python
import functools

import numpy as np
import jax
import jax.numpy as jnp
from jax import lax
from jax.experimental import pallas as pl
from jax.experimental.pallas import tpu as pltpu

D_MODEL = 1024
N_HEADS = 8
HEAD_DIM = 64
D_ATTN = N_HEADS * HEAD_DIM
D_LRU = 512
N_LRU_BLOCKS = 8
LRU_BLOCK = D_LRU // N_LRU_BLOCKS
CONV_WIDTH = 4
LRU_C = 8.0
DILATED_PATTERNS = ((128, 1), (512, 4), (2048, 16))
ATTN_BLOCK = 128
N_BUCKETS = 32
MAX_DISTANCE = 2048
D_IN = 3 * D_ATTN + 2 * D_LRU
N_EXPERTS = 8
TOP_K = 2
DEPTH = 2
DEEPNORM_ALPHA = (2.0 * DEPTH) ** 0.25
LN_EPS = 1e-5
RMS_EPS = 1e-6
NEG_INF = -1e30
LOG2_E = float(np.log2(np.e))

LANES = 128
DMA_UNROLL = 8
SCORE_UNROLL = 8
ATTN_UNROLL = 4
VMEM_LIMIT_BYTES = 56 * 1024 * 1024

f32 = jnp.float32
bf16 = jnp.bfloat16


def _params(*semantics):
    return pltpu.CompilerParams(dimension_semantics=semantics,
                                vmem_limit_bytes=VMEM_LIMIT_BYTES)


def _in_proj_kernel(x_ref, w_ref, qkv_ref, ug_ref):
    xb = x_ref[...].astype(bf16)
    n_qkv = qkv_ref.shape[1]
    step = 512
    for c in range(0, D_IN, step):
        p = jnp.dot(xb, w_ref[:, c:c + step], preferred_element_type=f32)
        if c < n_qkv:
            qkv_ref[:, c:c + step] = p
        else:
            ug_ref[:, c - n_qkv:c - n_qkv + step] = p


def _in_proj(x2d, w_in_bf16, tm=512):
    n = x2d.shape[0]
    return pl.pallas_call(
        _in_proj_kernel,
        grid=(n // tm,),
        in_specs=[pl.BlockSpec((tm, D_MODEL), lambda i: (i, 0)),
                  pl.BlockSpec((D_MODEL, D_IN), lambda i: (0, 0))],
        out_specs=[pl.BlockSpec((tm, 3 * D_ATTN), lambda i: (i, 0)),
                   pl.BlockSpec((tm, 2 * D_LRU), lambda i: (i, 0))],
        out_shape=[jax.ShapeDtypeStruct((n, 3 * D_ATTN), f32),
                   jax.ShapeDtypeStruct((n, 2 * D_LRU), f32)],
        compiler_params=_params("parallel"),
        name="in_proj",
    )(x2d, w_in_bf16)


def _t5_bucket(dist):
    max_exact = N_BUCKETS // 2
    d = np.maximum(dist, 1).astype(np.float32)
    large = max_exact + (np.log(d / max_exact) / np.log(MAX_DISTANCE / max_exact)
                         * (N_BUCKETS - max_exact)).astype(np.int32)
    large = np.minimum(large, N_BUCKETS - 1)
    return np.where(dist < max_exact, dist, large).astype(np.int32)


def _band_bias(rel_bias, dilation):
    nk = ATTN_BLOCK
    qi = np.arange(nk)[:, None]
    kj = np.arange(2 * nk)[None, :]
    delta = qi + nk - kj
    band = (delta >= 0) & (delta <= nk)
    bucket = _t5_bucket(np.clip(delta, 0, nk) * dilation)
    onehot = np.eye(N_BUCKETS, dtype=np.float32)[bucket.reshape(-1)]
    bias = jnp.dot(jnp.asarray(onehot), rel_bias.astype(f32), precision=lax.Precision.HIGHEST)
    bias = jnp.transpose(bias.reshape(nk, 2 * nk, N_HEADS), (2, 0, 1))
    valid = np.stack([band, band & (kj >= nk)])[:, None]
    bias = jnp.where(jnp.asarray(valid), bias[None], NEG_INF)
    return (bias * LOG2_E).reshape(2, N_HEADS // 2, 2 * nk, 2 * nk)


def _attn_kernel(q_ref, k_ref, v_ref, bias_ref, o_ref, qs, ks, vs, s_buf, m_buf, m_s, l_s, acc_s,
                 *, seq):
    nk = ATTN_BLOCK
    n_blocks = seq // nk
    chunk = 2 * nk
    lane = lax.broadcasted_iota(jnp.int32, (1, LANES), 1)
    head0 = lane < HEAD_DIM

    ks[0:nk, :] = jnp.zeros((nk, LANES), bf16)
    vs[0:nk, 0:LANES] = jnp.zeros((nk, LANES), bf16)
    vs[:, LANES:2 * LANES] = jnp.ones((seq + nk, LANES), bf16)

    order = sorted(range(len(DILATED_PATTERNS)), key=lambda p: -DILATED_PATTERNS[p][1])
    assert DILATED_PATTERNS[order[-1]][1] == 1
    for p in order:
        d = DILATED_PATTERNS[p][1]
        is_first, is_last = p == order[0], p == order[-1]
        length = seq // d
        nb = length // nk
        chunks_per_residue = length // chunk

        def gather(t, carry):
            r = t // chunks_per_residue
            c = t % chunks_per_residue
            src = pl.ds(r + d * chunk * c, chunk, stride=d)
            dst = pl.multiple_of(t * chunk, chunk)
            q = q_ref[src, :] * (HEAD_DIM ** -0.5 * LOG2_E)
            q0 = jnp.where(head0, q, 0.0).astype(bf16)
            q1 = jnp.where(head0, 0.0, q).astype(bf16)
            for half in range(2):
                base = pl.multiple_of(2 * dst + half * chunk, chunk)
                qs[pl.ds(base, nk), :] = q0[half * nk:(half + 1) * nk]
                qs[pl.ds(base + nk, nk), :] = q1[half * nk:(half + 1) * nk]
            ks[pl.ds(nk + dst, chunk), :] = k_ref[src, :].astype(bf16)
            vs[pl.ds(nk + dst, chunk), 0:LANES] = v_ref[src, :].astype(bf16)
            return carry

        lax.fori_loop(0, seq // chunk, gather, 0)

        def scores(b, carry):
            first = jnp.asarray(b % nb == 0, jnp.int32)
            rows = pl.ds(pl.multiple_of(b * chunk, chunk), chunk)
            k = ks[pl.ds(pl.multiple_of(b * nk, nk), chunk), :]
            s = lax.dot_general(qs[rows, :], k, (((1,), (1,)), ((), ())),
                                preferred_element_type=f32) + bias_ref[p, first]
            s_buf[rows, :] = s
            m_buf[rows, :] = jnp.broadcast_to(jnp.max(s, axis=-1, keepdims=True), (chunk, LANES))
            return carry

        def block(b, carry):
            r = b // nb
            i = b % nb
            rows = pl.ds(pl.multiple_of(b * chunk, chunk), chunk)
            v = vs[pl.ds(pl.multiple_of(b * nk, nk), chunk), :]
            m = m_buf[rows, :]
            e = jnp.exp2(s_buf[rows, :] - jnp.concatenate([m, m], axis=1))
            pv = jnp.dot(e.astype(bf16), v, preferred_element_type=f32)
            m_blk = jnp.where(head0, m[0:nk], m[nk:chunk])
            l_blk = jnp.where(head0, pv[0:nk, LANES:], pv[nk:chunk, LANES:])
            pv_blk = jnp.where(head0, pv[0:nk, 0:LANES], pv[nk:chunk, 0:LANES])
            if d == 1:
                tok = pl.ds(pl.multiple_of(b * nk, nk), nk)
            else:
                tok = pl.ds(r + d * nk * i, nk, stride=d)
            if is_first:
                m_s[tok, :] = m_blk
                l_s[tok, :] = l_blk
                acc_s[tok, :] = pv_blk
                return carry
            m_old = m_s[tok, :]
            m_new = jnp.maximum(m_old, m_blk)
            w_old = jnp.exp2(m_old - m_new)
            w_blk = jnp.exp2(m_blk - m_new)
            l_new = w_old * l_s[tok, :] + w_blk * l_blk
            acc_new = w_old * acc_s[tok, :] + w_blk * pv_blk
            if is_last:
                o_ref[tok, :] = (acc_new / l_new).astype(o_ref.dtype)
            else:
                l_s[tok, :] = l_new
                acc_s[tok, :] = acc_new
                m_s[tok, :] = m_new
            return carry

        n_groups = n_blocks // ATTN_UNROLL

        def group(fn, g):
            for j in range(ATTN_UNROLL):
                fn(g * ATTN_UNROLL + j, 0)

        def pipelined(g, carry):
            group(block, g)
            group(scores, g + 1)
            return carry

        group(scores, 0)
        lax.fori_loop(0, n_groups - 1, pipelined, 0)
        group(block, n_groups - 1)


def _attention(qkv, band_bias, batch, seq):
    view = qkv.reshape(batch, seq, 3 * D_ATTN)
    pairs = D_ATTN // LANES

    def spec(offset):
        return pl.BlockSpec((None, seq, LANES), lambda b, hp: (b, 0, offset * pairs + hp))

    n_pat = len(DILATED_PATTERNS)
    o = pl.pallas_call(
        functools.partial(_attn_kernel, seq=seq),
        grid=(batch, pairs),
        in_specs=[spec(0), spec(1), spec(2),
                  pl.BlockSpec((n_pat, 2, None, 2 * ATTN_BLOCK, 2 * ATTN_BLOCK),
                               lambda b, hp: (0, 0, hp, 0, 0))],
        out_specs=pl.BlockSpec((None, seq, LANES), lambda b, hp: (b, 0, hp)),
        out_shape=jax.ShapeDtypeStruct((batch, seq, D_ATTN), bf16),
        scratch_shapes=[pltpu.VMEM((2 * seq, LANES), bf16),
                        pltpu.VMEM((seq + ATTN_BLOCK, LANES), bf16),
                        pltpu.VMEM((seq + ATTN_BLOCK, 2 * LANES), bf16),
                        pltpu.VMEM((2 * seq, 2 * ATTN_BLOCK), f32),
                        pltpu.VMEM((2 * seq, LANES), f32),
                        pltpu.VMEM((seq, LANES), f32), pltpu.VMEM((seq, LANES), f32),
                        pltpu.VMEM((seq, LANES), f32)],
        compiler_params=_params("parallel", "parallel"),
        name="attention",
    )(view, view, view, band_bias)
    return o.reshape(batch * seq, D_ATTN)


def _gelu_tanh(x):
    return 0.5 * x * (1.0 + jnp.tanh(np.sqrt(2.0 / np.pi) * (x + 0.044715 * x * x * x)))


def _lru_kernel(ug_ref, convw_ref, convb_ref, wg_ref, ba_ref, bx_ref, lam_ref, g_ref,
                y_ref, ubuf, hcarry, *, ts):
    pad = 8
    t = pl.program_id(1)

    @pl.when(t == 0)
    def _():
        ubuf[0:pad, :] = jnp.zeros((pad, D_LRU), f32)
        hcarry[...] = jnp.zeros_like(hcarry)

    @pl.when(t > 0)
    def _():
        ubuf[0:pad, :] = ubuf[ts:ts + pad, :]

    ubuf[pad:pad + ts, :] = ug_ref[:, 0:D_LRU]
    gate = ug_ref[:, D_LRU:2 * D_LRU]

    u = convb_ref[...] + convw_ref[CONV_WIDTH - 1:CONV_WIDTH, :] * ubuf[pad:pad + ts, :]
    for w in range(CONV_WIDTH - 1):
        back = CONV_WIDTH - 1 - w
        u = u + convw_ref[w:w + 1, :] * ubuf[pad - back:pad - back + ts, :]

    gates = jnp.dot(u.astype(bf16), wg_ref[...], preferred_element_type=f32)
    r = jax.nn.sigmoid(gates[:, 0:D_LRU] + ba_ref[...])
    i = jax.nn.sigmoid(gates[:, D_LRU:2 * D_LRU] + bx_ref[...])
    neg_lam = -lam_ref[...]
    softplus = jnp.maximum(neg_lam, 0.0) + jnp.log1p(jnp.exp(-jnp.abs(neg_lam)))
    log_a = (-LRU_C) * r * softplus
    a = jnp.exp(log_a)
    b = jnp.sqrt(-jnp.tanh(log_a) * (1.0 + a * a)) * (i * u)

    row = lax.broadcasted_iota(jnp.int32, (ts, 1), 0)
    shift = 1
    while shift < ts:
        live = row >= shift
        a_prev = jnp.where(live, pltpu.roll(a, shift, 0), 1.0)
        b_prev = jnp.where(live, pltpu.roll(b, shift, 0), 0.0)
        b = a * b_prev + b
        a = a * a_prev
        shift *= 2
    h = a * hcarry[...] + b
    hcarry[...] = h[ts - 1:ts, :]

    y = _gelu_tanh(gate) * h
    y = y * lax.rsqrt(jnp.mean(y * y, axis=-1, keepdims=True) + RMS_EPS) * g_ref[...]
    y_ref[...] = y.astype(y_ref.dtype)


def _block_diag(w):
    g, i, j = w.shape
    eye = jnp.eye(g, dtype=w.dtype)
    return jnp.einsum('gij,gh->gihj', w, eye).reshape(g * i, g * j)


def _lru_branch(ug, conv_w, conv_b, w_a, b_a, w_x, b_x, lam, g_lru, batch, seq, ts=256):
    wg = jnp.concatenate([_block_diag(w_a), _block_diag(w_x)], axis=1).astype(bf16)
    row = lambda v: v.reshape(1, D_LRU).astype(f32)
    const = lambda shape: pl.BlockSpec(shape, lambda b, t: (0, 0))
    view = ug.reshape(batch, seq, 2 * D_LRU)
    y = pl.pallas_call(
        functools.partial(_lru_kernel, ts=ts),
        grid=(batch, seq // ts),
        in_specs=[pl.BlockSpec((None, ts, 2 * D_LRU), lambda b, t: (b, t, 0)),
                  const((CONV_WIDTH, D_LRU)), const((1, D_LRU)),
                  const((D_LRU, 2 * D_LRU)), const((1, D_LRU)), const((1, D_LRU)),
                  const((1, D_LRU)), const((1, D_LRU))],
        out_specs=pl.BlockSpec((None, ts, D_LRU), lambda b, t: (b, t, 0)),
        out_shape=jax.ShapeDtypeStruct((batch, seq, D_LRU), bf16),
        scratch_shapes=[pltpu.VMEM((ts + 16, D_LRU), f32), pltpu.VMEM((1, D_LRU), f32)],
        compiler_params=_params("parallel", "arbitrary"),
        name="rglru",
    )(view, conv_w.reshape(CONV_WIDTH, D_LRU).astype(f32), row(conv_b), wg, row(b_a), row(b_x),
      row(lam), row(g_lru))
    return y.reshape(batch * seq, D_LRU)


ROW_TILE = D_MODEL // LANES


def _store_row_tiles(ref, rows):
    t = rows.shape[0]
    for s in range(ROW_TILE):
        ref[pl.ds(s, t, stride=ROW_TILE), :] = rows[:, s * LANES:(s + 1) * LANES]


def _load_row_tiles(ref, t):
    return jnp.concatenate([ref[pl.ds(s, t, stride=ROW_TILE), :] for s in range(ROW_TILE)],
                           axis=-1)


def _layer_norm(z, g, b):
    mu = jnp.mean(z, axis=-1, keepdims=True)
    zc = z - mu
    var = jnp.mean(zc * zc, axis=-1, keepdims=True)
    return zc * lax.rsqrt(var + LN_EPS) * g + b


def _mix_out_kernel(attn_ref, ylru, x_ref, w_ref, gattn, lng, lnb, *rest, with_router):
    if with_router:
        rw_ref, tri_ref, x1_ref, x1_rows_ref, route_ref, count_ref = rest
    else:
        (x1_ref,) = rest
    attn = attn_ref[...].astype(f32)
    attn = attn * lax.rsqrt(jnp.mean(attn * attn, axis=-1, keepdims=True) + RMS_EPS) * gattn[...]
    y = jnp.dot(attn.astype(bf16), w_ref[0:D_ATTN, :], preferred_element_type=f32)
    y = y + jnp.dot(ylru[...], w_ref[D_ATTN:, :], preferred_element_type=f32)
    x1 = _layer_norm(DEEPNORM_ALPHA * x_ref[...] + y, lng[...], lnb[...])
    x1_ref[...] = x1
    if with_router:
        _store_row_tiles(x1_rows_ref, x1)
        lane = lax.broadcasted_iota(jnp.int32, (1, LANES), 1)
        x_hi = x1.astype(bf16)
        x_lo = (x1 - x_hi.astype(f32)).astype(bf16)
        logits = (jnp.dot(x_hi, rw_ref[0], preferred_element_type=f32)
                  + jnp.dot(x_lo, rw_ref[0], preferred_element_type=f32)
                  + jnp.dot(x_hi, rw_ref[1], preferred_element_type=f32))
        logits = jnp.where(lane < N_EXPERTS, logits, -jnp.inf)
        v1 = jnp.max(logits, axis=-1, keepdims=True)
        i1 = jnp.min(jnp.where(logits == v1, lane, LANES), axis=-1, keepdims=True)
        rest_logits = jnp.where(lane == i1, -jnp.inf, logits)
        v2 = jnp.max(rest_logits, axis=-1, keepdims=True)
        i2 = jnp.min(jnp.where(rest_logits == v2, lane, LANES), axis=-1, keepdims=True)
        e2 = jnp.exp(v2 - v1)
        p1 = 1.0 / (1.0 + e2)
        p2 = e2 / (1.0 + e2)

        @pl.when(pl.program_id(0) == 0)
        def _():
            count_ref[...] = jnp.zeros_like(count_ref)

        tm = x1.shape[0]
        chosen = jnp.logical_or(lane == i1, lane == i2)
        rank = count_ref[...] + jnp.dot(tri_ref[...], chosen.astype(bf16),
                                        preferred_element_type=f32)
        count_ref[...] += jnp.sum(chosen.astype(f32), axis=0, keepdims=True)
        r1 = jnp.sum(jnp.where(lane == i1, rank, 0.0), axis=-1, keepdims=True)
        r2 = jnp.sum(jnp.where(lane == i2, rank, 0.0), axis=-1, keepdims=True)
        fields = (i1.astype(f32), i2.astype(f32), r1, r2, p1, p2)
        route = jnp.zeros((tm, LANES), f32)
        for k, val in enumerate(fields):
            route = jnp.where(lane == k, val, route)
        route_ref[...] = route


def _mix_out(attn, ylru, x2d, w_out_bf16, g_attn, ln_g, ln_b, router_w=None, tm=512):
    n = x2d.shape[0]
    with_router = router_w is not None
    tile = lambda width: pl.BlockSpec((tm, width), lambda i: (i, 0))
    const = lambda shape: pl.BlockSpec(shape, lambda i: (0, 0))
    in_specs = [tile(D_ATTN), tile(D_LRU), tile(D_MODEL), const((D_MODEL, D_MODEL)),
                const((1, D_ATTN)), const((1, D_MODEL)), const((1, D_MODEL))]
    args = [attn, ylru, x2d, w_out_bf16, g_attn.reshape(1, D_ATTN).astype(f32),
            ln_g.reshape(1, D_MODEL).astype(f32), ln_b.reshape(1, D_MODEL).astype(f32)]
    out_specs = [tile(D_MODEL)]
    out_shape = [jax.ShapeDtypeStruct((n, D_MODEL), f32)]
    if with_router:
        rw = jnp.zeros((D_MODEL, LANES), f32).at[:, :N_EXPERTS].set(router_w.astype(f32))
        rw_hi = rw.astype(bf16)
        rw_lo = (rw - rw_hi.astype(f32)).astype(bf16)
        strictly_lower = jnp.asarray(np.tri(tm, k=-1), bf16)
        in_specs += [pl.BlockSpec((2, D_MODEL, LANES), lambda i: (0, 0, 0)), const((tm, tm))]
        args += [jnp.stack([rw_hi, rw_lo]), strictly_lower]
        out_specs += [pl.BlockSpec((tm * ROW_TILE, LANES), lambda i: (i, 0)), tile(LANES),
                      const((1, LANES))]
        out_shape += [jax.ShapeDtypeStruct((n * ROW_TILE, LANES), f32),
                      jax.ShapeDtypeStruct((n, LANES), f32), jax.ShapeDtypeStruct((1, LANES), f32)]
    res = pl.pallas_call(
        functools.partial(_mix_out_kernel, with_router=with_router),
        grid=(n // tm,),
        in_specs=in_specs,
        out_specs=out_specs,
        out_shape=out_shape,
        compiler_params=_params("arbitrary" if with_router else "parallel"),
        name="mix_out_router" if with_router else "mix_out",
    )(*args)
    return res if with_router else (res[0], None, None, None)


def _ffn_kernel(tile_expert_ref, n_used_ref, x_ref, wg_ref, wu_ref, wd_ref, *rest,
                n_chunks, fuse_ln):
    if fuse_ln:
        lng, lnb, out_ref, acc_ref, xb_ref = rest
    else:
        out_ref, acc_ref, xb_ref = rest
    del tile_expert_ref
    j = pl.program_id(1)
    used = pl.program_id(0) < n_used_ref[0]

    @pl.when(j == 0)
    def _():
        acc_ref[...] = jnp.zeros_like(acc_ref)

    @pl.when(used & (j == 0))
    def _():
        if fuse_ln:
            xb_ref[...] = x_ref[...].astype(bf16)
        else:
            xb_ref[...] = _load_row_tiles(x_ref, xb_ref.shape[0]).astype(bf16)

    @pl.when(used)
    def _():
        xb = xb_ref[...]
        g = jnp.dot(xb, wg_ref[...], preferred_element_type=f32)
        u = jnp.dot(xb, wu_ref[...], preferred_element_type=f32)
        h = (g * jax.nn.sigmoid(g)) * u
        acc_ref[...] += jnp.dot(h.astype(bf16), wd_ref[...], preferred_element_type=f32)

    @pl.when(j == n_chunks - 1)
    def _():
        if fuse_ln:
            out_ref[...] = _layer_norm(DEEPNORM_ALPHA * x_ref[...] + acc_ref[...],
                                       lng[...], lnb[...])
        else:
            _store_row_tiles(out_ref, acc_ref[...])


def _ffn(x2d, tile_expert, n_used, w_gate, w_up, w_down, ln=None, *, tm, tf, name):
    fuse_ln = ln is not None
    rows = x2d.shape[0] if fuse_ln else x2d.shape[0] // ROW_TILE
    d_ff = w_gate.shape[2]
    n_chunks = d_ff // tf
    io_block = (tm, D_MODEL) if fuse_ln else (tm * ROW_TILE, LANES)

    def chunk(i, j, nu):
        return jnp.where(i < nu[0], j, n_chunks - 1)

    in_specs = [pl.BlockSpec(io_block, lambda i, j, te, nu: (i, 0)),
                pl.BlockSpec((None, D_MODEL, tf), lambda i, j, te, nu: (te[i], 0, chunk(i, j, nu))),
                pl.BlockSpec((None, D_MODEL, tf), lambda i, j, te, nu: (te[i], 0, chunk(i, j, nu))),
                pl.BlockSpec((None, tf, D_MODEL), lambda i, j, te, nu: (te[i], chunk(i, j, nu), 0))]
    args = [x2d, w_gate, w_up, w_down]
    if fuse_ln:
        in_specs += [pl.BlockSpec((1, D_MODEL), lambda i, j, te, nu: (0, 0))] * 2
        args += [v.reshape(1, D_MODEL).astype(f32) for v in ln]
    return pl.pallas_call(
        functools.partial(_ffn_kernel, n_chunks=n_chunks, fuse_ln=fuse_ln),
        grid_spec=pltpu.PrefetchScalarGridSpec(
            num_scalar_prefetch=2,
            grid=(rows // tm, n_chunks),
            in_specs=in_specs,
            out_specs=pl.BlockSpec(io_block, lambda i, j, te, nu: (i, 0)),
            scratch_shapes=[pltpu.VMEM((tm, D_MODEL), f32), pltpu.VMEM((tm, D_MODEL), bf16)]),
        out_shape=jax.ShapeDtypeStruct(x2d.shape, f32),
        compiler_params=_params("parallel", "arbitrary"),
        name=name,
    )(tile_expert, n_used, *args)


def _dispatch_kernel(fill_ref, pos_ref, x_ref, xs_hbm, zero_buf, sem, *, tt, tm):
    step = pl.program_id(0)

    n_fill = fill_ref.shape[0] // 2

    @pl.when(step == 0)
    def _():
        zero_buf[...] = jnp.zeros_like(zero_buf)

        def fill_copy(f):
            start = pl.multiple_of(fill_ref[f] * ROW_TILE, ROW_TILE)
            return pltpu.make_async_copy(zero_buf, xs_hbm.at[pl.ds(start, tm * ROW_TILE)], sem)

        for f in range(n_fill):
            @pl.when(fill_ref[n_fill + f] > 0)
            def _():
                fill_copy(f).start()
        for f in range(n_fill):
            @pl.when(fill_ref[n_fill + f] > 0)
            def _():
                fill_copy(f).wait()

    def issue(t, carry):
        src = pl.multiple_of(t * ROW_TILE, ROW_TILE)
        for k in range(TOP_K):
            dst = pl.multiple_of(pos_ref[0, TOP_K * t + k] * ROW_TILE, ROW_TILE)
            pltpu.make_async_copy(x_ref.at[pl.ds(src, ROW_TILE)],
                                  xs_hbm.at[pl.ds(dst, ROW_TILE)], sem).start(priority=k)
        return carry

    lax.fori_loop(0, tt, issue, 0, unroll=DMA_UNROLL)
    for _ in range(TOP_K):
        pltpu.make_async_copy(x_ref, xs_hbm.at[pl.ds(0, tt * ROW_TILE)], sem).wait()


def _dispatch(x1_rows, pos_blocks, fill, rows_sorted, *, tt, tm):
    n = x1_rows.shape[0] // ROW_TILE
    return pl.pallas_call(
        functools.partial(_dispatch_kernel, tt=tt, tm=tm),
        grid_spec=pltpu.PrefetchScalarGridSpec(
            num_scalar_prefetch=1,
            grid=(n // tt,),
            in_specs=[pl.BlockSpec((None, 1, TOP_K * tt), lambda i, fill: (i, 0, 0),
                                   memory_space=pltpu.SMEM),
                      pl.BlockSpec((tt * ROW_TILE, LANES), lambda i, fill: (i, 0))],
            out_specs=pl.BlockSpec(memory_space=pl.ANY),
            scratch_shapes=[pltpu.VMEM((tm * ROW_TILE, LANES), f32), pltpu.SemaphoreType.DMA(())]),
        out_shape=jax.ShapeDtypeStruct((rows_sorted * ROW_TILE, LANES), f32),
        compiler_params=_params("arbitrary"),
        name="moe_dispatch",
    )(fill, pos_blocks, x1_rows)


def _combine_kernel(pos_ref, pos_next_ref, route_ref, x1_ref, ys_hbm, lng, lnb, out_ref, buf, sems,
                    *, tt, n_steps):
    step = pl.program_id(0)
    slot = step % 2

    def issue_tile(tile_pos_ref, into):
        def issue(t, carry):
            dst = pl.multiple_of(t * ROW_TILE, ROW_TILE)
            for k in range(TOP_K):
                src = pl.multiple_of(tile_pos_ref[0, TOP_K * t + k] * ROW_TILE, ROW_TILE)
                pltpu.make_async_copy(ys_hbm.at[pl.ds(src, ROW_TILE)],
                                      buf.at[into, k, pl.ds(dst, ROW_TILE)],
                                      sems.at[into]).start(priority=k)
            return carry

        lax.fori_loop(0, tt, issue, 0, unroll=DMA_UNROLL)

    @pl.when(step == 0)
    def _():
        issue_tile(pos_ref, 0)

    @pl.when(step + 1 < n_steps)
    def _():
        issue_tile(pos_next_ref, 1 - slot)

    for k in range(TOP_K):
        pltpu.make_async_copy(ys_hbm.at[pl.ds(0, tt * ROW_TILE)], buf.at[slot, k],
                              sems.at[slot]).wait()

    lane = lax.broadcasted_iota(jnp.int32, (1, LANES), 1)
    route = route_ref[...]
    p1 = jnp.sum(jnp.where(lane == 4, route, 0.0), axis=-1, keepdims=True)
    p2 = jnp.sum(jnp.where(lane == 5, route, 0.0), axis=-1, keepdims=True)
    y = p1 * _load_row_tiles(buf.at[slot, 0], tt) + p2 * _load_row_tiles(buf.at[slot, 1], tt)
    out_ref[...] = _layer_norm(DEEPNORM_ALPHA * x1_ref[...] + y, lng[...], lnb[...])


def _combine(pos_blocks, route, x1, y_sorted, ln_g, ln_b, *, tt):
    n = x1.shape[0]
    n_steps = n // tt
    pos_spec = lambda index: pl.BlockSpec((None, 1, TOP_K * tt), index, memory_space=pltpu.SMEM)
    return pl.pallas_call(
        functools.partial(_combine_kernel, tt=tt, n_steps=n_steps),
        grid=(n_steps,),
        in_specs=[pos_spec(lambda i: (i, 0, 0)),
                  pos_spec(lambda i: (jnp.minimum(i + 1, n_steps - 1), 0, 0)),
                  pl.BlockSpec((tt, LANES), lambda i: (i, 0)),
                  pl.BlockSpec((tt, D_MODEL), lambda i: (i, 0)),
                  pl.BlockSpec(memory_space=pl.ANY),
                  pl.BlockSpec((1, D_MODEL), lambda i: (0, 0)),
                  pl.BlockSpec((1, D_MODEL), lambda i: (0, 0))],
        out_specs=pl.BlockSpec((tt, D_MODEL), lambda i: (i, 0)),
        out_shape=jax.ShapeDtypeStruct((n, D_MODEL), f32),
        scratch_shapes=[pltpu.VMEM((2, TOP_K, tt * ROW_TILE, LANES), f32),
                        pltpu.SemaphoreType.DMA((2,))],
        compiler_params=_params("arbitrary"),
        name="moe_combine",
    )(pos_blocks, pos_blocks, route, x1, y_sorted, ln_g.reshape(1, D_MODEL).astype(f32),
      ln_b.reshape(1, D_MODEL).astype(f32))


def _moe(x1, x1_rows, route, counts, w_gate, w_up, w_down, ln_g, ln_b, *, tm, tf, tt):
    n = x1.shape[0]
    i32 = jnp.int32
    counts = counts[0, :N_EXPERTS].astype(i32)
    padded = (counts + tm - 1) // tm * tm
    ends = jnp.cumsum(padded)
    offsets = ends - padded
    experts = route[:, 0:TOP_K].astype(i32)
    ranks = route[:, TOP_K:2 * TOP_K].astype(i32)
    pos = (offsets[experts] + ranks).reshape(n // tt, 1, TOP_K * tt)
    n_tiles = TOP_K * n // tm + N_EXPERTS
    n_used = (ends[-1] // tm).astype(i32).reshape(1)
    tile_ids = jnp.arange(n_tiles, dtype=i32)
    tile_expert = jnp.sum((tile_ids[:, None] >= (ends // tm)[None, :]).astype(i32), axis=1)
    last_expert = jnp.max(jnp.where(counts > 0, jnp.arange(N_EXPERTS, dtype=i32), 0))
    tile_expert = jnp.minimum(tile_expert, last_expert).astype(i32)
    tail_tiles = n_used[0] + jnp.arange(N_EXPERTS, dtype=i32)
    fill = jnp.concatenate([ends - tm, tail_tiles * tm,
                            (counts > 0).astype(i32), (tail_tiles < n_tiles).astype(i32)]).astype(i32)

    x_sorted = _dispatch(x1_rows, pos, fill, n_tiles * tm, tt=tt, tm=tm)
    y_sorted = _ffn(x_sorted, tile_expert, n_used, w_gate, w_up, w_down, tm=tm, tf=tf,
                    name="ffn_experts")
    return _combine(pos, route, x1, y_sorted, ln_g, ln_b, tt=tt)


def kernel(x, w_in, conv_w, conv_b, w_a, b_a, w_x, b_x, lru_lambda, rel_bias, g_attn, g_lru, w_out, ln1_g, ln1_b, ln2_g, ln2_b, ffn_w_gate, ffn_w_up, ffn_w_down, router_w, moe_w_gate, moe_w_up, moe_w_down):
    batch, seq, _ = x.shape
    n = batch * seq
    h = x.reshape(n, D_MODEL).astype(f32)
    band_bias = jnp.stack([_band_bias(rel_bias, d) for _, d in DILATED_PATTERNS])
    for layer in range(DEPTH):
        qkv, ug = _in_proj(h, w_in[layer].astype(bf16))
        attn = _attention(qkv, band_bias, batch, seq)
        ylru = _lru_branch(ug, conv_w[layer], conv_b[layer], w_a[layer], b_a[layer], w_x[layer],
                           b_x[layer], lru_lambda[layer], g_lru[layer], batch, seq)
        j = layer // 2
        dense = layer % 2 == 0
        x1, x1_rows, route, counts = _mix_out(
            attn, ylru, h, w_out[layer].astype(bf16), g_attn[layer], ln1_g[layer],
            ln1_b[layer], None if dense else router_w[j])
        if dense:
            tm = 512
            h = _ffn(x1, jnp.zeros((n // tm,), jnp.int32), jnp.full((1,), n // tm, jnp.int32),
                     ffn_w_gate[j][None].astype(bf16), ffn_w_up[j][None].astype(bf16),
                     ffn_w_down[j][None].astype(bf16), (ln2_g[layer], ln2_b[layer]),
                     tm=tm, tf=1408, name="ffn_dense")
        else:
            h = _moe(x1, x1_rows, route, counts, moe_w_gate[j].astype(bf16), moe_w_up[j].astype(bf16),
                     moe_w_down[j].astype(bf16), ln2_g[layer], ln2_b[layer],
                     tm=512, tf=1792, tt=256)
    return h.reshape(batch, seq, D_MODEL).astype(x.dtype)
```

```python
import functools

import numpy as np
import jax
import jax.numpy as jnp
from jax import lax
from jax.experimental import pallas as pl
from jax.experimental.pallas import tpu as pltpu

D_MODEL = 1024
N_HEADS = 8
HEAD_DIM = 64
D_ATTN = N_HEADS * HEAD_DIM
D_LRU = 512
N_LRU_BLOCKS = 8
LRU_BLOCK = D_LRU // N_LRU_BLOCKS
CONV_WIDTH = 4
LRU_C = 8.0
DILATED_PATTERNS = ((128, 1), (512, 4), (2048, 16))
ATTN_BLOCK = 128
N_BUCKETS = 32
MAX_DISTANCE = 2048
D_IN = 3 * D_ATTN + 2 * D_LRU
N_EXPERTS = 8
TOP_K = 2
DEPTH = 2
DEEPNORM_ALPHA = (2.0 * DEPTH) ** 0.25
LN_EPS = 1e-5
RMS_EPS = 1e-6
NEG_INF = -1e30
LOG2_E = float(np.log2(np.e))

LANES = 128
SUBLANES = 8
DMA_UNROLL = 8
SCORE_UNROLL = 8
ATTN_UNROLL = 4
VMEM_LIMIT_BYTES = 56 * 1024 * 1024

f32 = jnp.float32
bf16 = jnp.bfloat16


def _params(*semantics):
    return pltpu.CompilerParams(dimension_semantics=semantics,
                                vmem_limit_bytes=VMEM_LIMIT_BYTES)


def _in_proj_kernel(x_ref, w_ref, qkv_ref, ug_ref):
    xb = x_ref[...].astype(bf16)
    n_qkv = qkv_ref.shape[1]
    step = 512
    for c in range(0, D_IN, step):
        p = jnp.dot(xb, w_ref[:, c:c + step], preferred_element_type=f32)
        if c < n_qkv:
            qkv_ref[:, c:c + step] = p
        else:
            ug_ref[:, c - n_qkv:c - n_qkv + step] = p


def _in_proj(x2d, w_in_bf16, tm=512):
    n = x2d.shape[0]
    return pl.pallas_call(
        _in_proj_kernel,
        grid=(n // tm,),
        in_specs=[pl.BlockSpec((tm, D_MODEL), lambda i: (i, 0)),
                  pl.BlockSpec((D_MODEL, D_IN), lambda i: (0, 0))],
        out_specs=[pl.BlockSpec((tm, 3 * D_ATTN), lambda i: (i, 0)),
                   pl.BlockSpec((tm, 2 * D_LRU), lambda i: (i, 0))],
        out_shape=[jax.ShapeDtypeStruct((n, 3 * D_ATTN), f32),
                   jax.ShapeDtypeStruct((n, 2 * D_LRU), f32)],
        compiler_params=_params("parallel"),
        name="in_proj",
    )(x2d, w_in_bf16)


def _t5_bucket(dist):
    max_exact = N_BUCKETS // 2
    d = np.maximum(dist, 1).astype(np.float32)
    large = max_exact + (np.log(d / max_exact) / np.log(MAX_DISTANCE / max_exact)
                         * (N_BUCKETS - max_exact)).astype(np.int32)
    large = np.minimum(large, N_BUCKETS - 1)
    return np.where(dist < max_exact, dist, large).astype(np.int32)


def _band_bias(rel_bias, dilation):
    nk = ATTN_BLOCK
    qi = np.arange(nk)[:, None]
    kj = np.arange(2 * nk)[None, :]
    delta = qi + nk - kj
    band = (delta >= 0) & (delta <= nk)
    bucket = _t5_bucket(np.clip(delta, 0, nk) * dilation)
    onehot = np.eye(N_BUCKETS, dtype=np.float32)[bucket.reshape(-1)]
    bias = jnp.dot(jnp.asarray(onehot), rel_bias.astype(f32), precision=lax.Precision.HIGHEST)
    bias = jnp.transpose(bias.reshape(nk, 2 * nk, N_HEADS), (2, 0, 1))
    valid = np.stack([band, band & (kj >= nk)])[:, None]
    bias = jnp.where(jnp.asarray(valid), bias[None], NEG_INF)
    return (bias * LOG2_E).reshape(2, N_HEADS // 2, 2 * nk, 2 * nk)


def _attn_kernel(q_ref, k_ref, v_ref, bias_ref, o_ref, qs, ks, vs, s_buf, m_buf, m_s, l_s, acc_s,
                 *, seq):
    nk = ATTN_BLOCK
    n_blocks = seq // nk
    chunk = 2 * nk
    lane = lax.broadcasted_iota(jnp.int32, (1, LANES), 1)
    head0 = lane < HEAD_DIM

    ks[0:nk, :] = jnp.zeros((nk, LANES), bf16)
    vs[0:nk, 0:LANES] = jnp.zeros((nk, LANES), bf16)
    vs[:, LANES:2 * LANES] = jnp.ones((seq + nk, LANES), bf16)

    order = sorted(range(len(DILATED_PATTERNS)), key=lambda p: -DILATED_PATTERNS[p][1])
    assert DILATED_PATTERNS[order[-1]][1] == 1
    for p in order:
        d = DILATED_PATTERNS[p][1]
        is_first, is_last = p == order[0], p == order[-1]
        length = seq // d
        nb = length // nk
        chunks_per_residue = length // chunk

        def gather(t, carry):
            r = t // chunks_per_residue
            c = t % chunks_per_residue
            src = pl.ds(r + d * chunk * c, chunk, stride=d)
            dst = pl.multiple_of(t * chunk, chunk)
            q = q_ref[src, :] * (HEAD_DIM ** -0.5 * LOG2_E)
            q0 = jnp.where(head0, q, 0.0).astype(bf16)
            q1 = jnp.where(head0, 0.0, q).astype(bf16)
            for half in range(2):
                base = pl.multiple_of(2 * dst + half * chunk, chunk)
                qs[pl.ds(base, nk), :] = q0[half * nk:(half + 1) * nk]
                qs[pl.ds(base + nk, nk), :] = q1[half * nk:(half + 1) * nk]
            ks[pl.ds(nk + dst, chunk), :] = k_ref[src, :].astype(bf16)
            vs[pl.ds(nk + dst, chunk), 0:LANES] = v_ref[src, :].astype(bf16)
            return carry

        lax.fori_loop(0, seq // chunk, gather, 0)

        def scores(b, carry):
            first = jnp.asarray(b % nb == 0, jnp.int32)
            rows = pl.ds(pl.multiple_of(b * chunk, chunk), chunk)
            k = ks[pl.ds(pl.multiple_of(b * nk, nk), chunk), :]
            s = lax.dot_general(qs[rows, :], k, (((1,), (1,)), ((), ())),
                                preferred_element_type=f32) + bias_ref[p, first]
            s_buf[rows, :] = s
            m_buf[rows, :] = jnp.broadcast_to(jnp.max(s, axis=-1, keepdims=True), (chunk, LANES))
            return carry

        def block(b, carry):
            r = b // nb
            i = b % nb
            rows = pl.ds(pl.multiple_of(b * chunk, chunk), chunk)
            v = vs[pl.ds(pl.multiple_of(b * nk, nk), chunk), :]
            m = m_buf[rows, :]
            e = jnp.exp2(s_buf[rows, :] - jnp.concatenate([m, m], axis=1))
            pv = jnp.dot(e.astype(bf16), v, preferred_element_type=f32)
            m_blk = jnp.where(head0, m[0:nk], m[nk:chunk])
            l_blk = jnp.where(head0, pv[0:nk, LANES:], pv[nk:chunk, LANES:])
            pv_blk = jnp.where(head0, pv[0:nk, 0:LANES], pv[nk:chunk, 0:LANES])
            if d == 1:
                tok = pl.ds(pl.multiple_of(b * nk, nk), nk)
            else:
                tok = pl.ds(r + d * nk * i, nk, stride=d)
            if is_first:
                m_s[tok, :] = m_blk
                l_s[tok, :] = l_blk
                acc_s[tok, :] = pv_blk
                return carry
            m_old = m_s[tok, :]
            m_new = jnp.maximum(m_old, m_blk)
            w_old = jnp.exp2(m_old - m_new)
            w_blk = jnp.exp2(m_blk - m_new)
            l_new = w_old * l_s[tok, :] + w_blk * l_blk
            acc_new = w_old * acc_s[tok, :] + w_blk * pv_blk
            if is_last:
                o_ref[tok, :] = (acc_new / l_new).astype(o_ref.dtype)
            else:
                l_s[tok, :] = l_new
                acc_s[tok, :] = acc_new
                m_s[tok, :] = m_new
            return carry

        n_groups = n_blocks // ATTN_UNROLL

        def group(fn, g):
            for j in range(ATTN_UNROLL):
                fn(g * ATTN_UNROLL + j, 0)

        def pipelined(g, carry):
            group(block, g)
            group(scores, g + 1)
            return carry

        group(scores, 0)
        lax.fori_loop(0, n_groups - 1, pipelined, 0)
        group(block, n_groups - 1)


def _attention(qkv, band_bias, batch, seq):
    view = qkv.reshape(batch, seq, 3 * D_ATTN)
    pairs = D_ATTN // LANES

    def spec(offset):
        return pl.BlockSpec((None, seq, LANES), lambda b, hp: (b, 0, offset * pairs + hp))

    n_pat = len(DILATED_PATTERNS)
    o = pl.pallas_call(
        functools.partial(_attn_kernel, seq=seq),
        grid=(batch, pairs),
        in_specs=[spec(0), spec(1), spec(2),
                  pl.BlockSpec((n_pat, 2, None, 2 * ATTN_BLOCK, 2 * ATTN_BLOCK),
                               lambda b, hp: (0, 0, hp, 0, 0))],
        out_specs=pl.BlockSpec((None, seq, LANES), lambda b, hp: (b, 0, hp)),
        out_shape=jax.ShapeDtypeStruct((batch, seq, D_ATTN), bf16),
        scratch_shapes=[pltpu.VMEM((2 * seq, LANES), bf16),
                        pltpu.VMEM((seq + ATTN_BLOCK, LANES), bf16),
                        pltpu.VMEM((seq + ATTN_BLOCK, 2 * LANES), bf16),
                        pltpu.VMEM((2 * seq, 2 * ATTN_BLOCK), f32),
                        pltpu.VMEM((2 * seq, LANES), f32),
                        pltpu.VMEM((seq, LANES), f32), pltpu.VMEM((seq, LANES), f32),
                        pltpu.VMEM((seq, LANES), f32)],
        compiler_params=_params("parallel", "parallel"),
        name="attention",
    )(view, view, view, band_bias)
    return o.reshape(batch * seq, D_ATTN)


def _gelu_tanh(x):
    return 0.5 * x * (1.0 + jnp.tanh(np.sqrt(2.0 / np.pi) * (x + 0.044715 * x * x * x)))


def _lru_kernel(ug_ref, convw_ref, convb_ref, wg_ref, ba_ref, bx_ref, lam_ref, g_ref,
                y_ref, ubuf, hcarry, *, ts):
    pad = SUBLANES
    t = pl.program_id(1)

    @pl.when(t == 0)
    def _():
        ubuf[0:pad, :] = jnp.zeros((pad, D_LRU), f32)
        hcarry[...] = jnp.zeros_like(hcarry)

    @pl.when(t > 0)
    def _():
        ubuf[0:pad, :] = ubuf[ts:ts + pad, :]

    ubuf[pad:pad + ts, :] = ug_ref[:, 0:D_LRU]
    gate = ug_ref[:, D_LRU:2 * D_LRU]

    u = convb_ref[...] + convw_ref[CONV_WIDTH - 1:CONV_WIDTH, :] * ubuf[pad:pad + ts, :]
    for w in range(CONV_WIDTH - 1):
        back = CONV_WIDTH - 1 - w
        u = u + convw_ref[w:w + 1, :] * ubuf[pad - back:pad - back + ts, :]

    gates = jnp.dot(u.astype(bf16), wg_ref[...], preferred_element_type=f32)
    r = jax.nn.sigmoid(gates[:, 0:D_LRU] + ba_ref[...])
    i = jax.nn.sigmoid(gates[:, D_LRU:2 * D_LRU] + bx_ref[...])
    neg_lam = -lam_ref[...]
    softplus = jnp.maximum(neg_lam, 0.0) + jnp.log1p(jnp.exp(-jnp.abs(neg_lam)))
    log_a = (-LRU_C) * r * softplus
    a = jnp.exp(log_a)
    b = jnp.sqrt(-jnp.tanh(log_a) * (1.0 + a * a)) * (i * u)

    groups = ts // SUBLANES
    a = a.reshape(groups, SUBLANES, D_LRU)
    b = b.reshape(groups, SUBLANES, D_LRU)
    sub = lax.broadcasted_iota(jnp.int32, (1, SUBLANES, 1), 1)
    shift = 1
    while shift < SUBLANES:
        live = sub >= shift
        a_prev = jnp.where(live, pltpu.roll(a, shift, 1), 1.0)
        b_prev = jnp.where(live, pltpu.roll(b, shift, 1), 0.0)
        b = a * b_prev + b
        a = a * a_prev
        shift *= 2
    h_prev = hcarry[...]
    h_groups = []
    for g in range(groups):
        h_g = a[g] * h_prev + b[g]
        h_prev = h_g[SUBLANES - 1:SUBLANES, :]
        h_groups.append(h_g)
    hcarry[...] = h_prev
    h = jnp.concatenate(h_groups, axis=0)

    y = _gelu_tanh(gate) * h
    y = y * lax.rsqrt(jnp.mean(y * y, axis=-1, keepdims=True) + RMS_EPS) * g_ref[...]
    y_ref[...] = y.astype(y_ref.dtype)


def _block_diag(w):
    g, i, j = w.shape
    eye = jnp.eye(g, dtype=w.dtype)
    return jnp.einsum('gij,gh->gihj', w, eye).reshape(g * i, g * j)


def _lru_branch(ug, conv_w, conv_b, w_a, b_a, w_x, b_x, lam, g_lru, batch, seq, ts=512):
    wg = jnp.concatenate([_block_diag(w_a), _block_diag(w_x)], axis=1).astype(bf16)
    row = lambda v: v.reshape(1, D_LRU).astype(f32)
    const = lambda shape: pl.BlockSpec(shape, lambda b, t: (0, 0))
    view = ug.reshape(batch, seq, 2 * D_LRU)
    y = pl.pallas_call(
        functools.partial(_lru_kernel, ts=ts),
        grid=(batch, seq // ts),
        in_specs=[pl.BlockSpec((None, ts, 2 * D_LRU), lambda b, t: (b, t, 0)),
                  const((CONV_WIDTH, D_LRU)), const((1, D_LRU)),
                  const((D_LRU, 2 * D_LRU)), const((1, D_LRU)), const((1, D_LRU)),
                  const((1, D_LRU)), const((1, D_LRU))],
        out_specs=pl.BlockSpec((None, ts, D_LRU), lambda b, t: (b, t, 0)),
        out_shape=jax.ShapeDtypeStruct((batch, seq, D_LRU), bf16),
        scratch_shapes=[pltpu.VMEM((ts + 16, D_LRU), f32), pltpu.VMEM((1, D_LRU), f32)],
        compiler_params=_params("parallel", "arbitrary"),
        name="rglru",
    )(view, conv_w.reshape(CONV_WIDTH, D_LRU).astype(f32), row(conv_b), wg, row(b_a), row(b_x),
      row(lam), row(g_lru))
    return y.reshape(batch * seq, D_LRU)


ROW_TILE = D_MODEL // LANES


def _store_row_tiles(ref, rows):
    t = rows.shape[0]
    for s in range(ROW_TILE):
        ref[pl.ds(s, t, stride=ROW_TILE), :] = rows[:, s * LANES:(s + 1) * LANES]


def _load_row_tiles(ref, t):
    return jnp.concatenate([ref[pl.ds(s, t, stride=ROW_TILE), :] for s in range(ROW_TILE)],
                           axis=-1)


def _layer_norm(z, g, b):
    mu = jnp.mean(z, axis=-1, keepdims=True)
    zc = z - mu
    var = jnp.mean(zc * zc, axis=-1, keepdims=True)
    return zc * lax.rsqrt(var + LN_EPS) * g + b


def _mix_out_kernel(attn_ref, ylru, x_ref, w_ref, gattn, lng, lnb, *rest, with_router):
    if with_router:
        rw_ref, tri_ref, x1_ref, x1_rows_ref, route_ref, count_ref = rest
    else:
        (x1_ref,) = rest
    attn = attn_ref[...].astype(f32)
    attn = attn * lax.rsqrt(jnp.mean(attn * attn, axis=-1, keepdims=True) + RMS_EPS) * gattn[...]
    y = jnp.dot(attn.astype(bf16), w_ref[0:D_ATTN, :], preferred_element_type=f32)
    y = y + jnp.dot(ylru[...], w_ref[D_ATTN:, :], preferred_element_type=f32)
    x1 = _layer_norm(DEEPNORM_ALPHA * x_ref[...] + y, lng[...], lnb[...])
    x1_ref[...] = x1
    if with_router:
        _store_row_tiles(x1_rows_ref, x1)
        lane = lax.broadcasted_iota(jnp.int32, (1, LANES), 1)
        x_hi = x1.astype(bf16)
        x_lo = (x1 - x_hi.astype(f32)).astype(bf16)
        logits = (jnp.dot(x_hi, rw_ref[0], preferred_element_type=f32)
                  + jnp.dot(x_lo, rw_ref[0], preferred_element_type=f32)
                  + jnp.dot(x_hi, rw_ref[1], preferred_element_type=f32))
        logits = jnp.where(lane < N_EXPERTS, logits, -jnp.inf)
        v1 = jnp.max(logits, axis=-1, keepdims=True)
        i1 = jnp.min(jnp.where(logits == v1, lane, LANES), axis=-1, keepdims=True)
        rest_logits = jnp.where(lane == i1, -jnp.inf, logits)
        v2 = jnp.max(rest_logits, axis=-1, keepdims=True)
        i2 = jnp.min(jnp.where(rest_logits == v2, lane, LANES), axis=-1, keepdims=True)
        e2 = jnp.exp(v2 - v1)
        p1 = 1.0 / (1.0 + e2)
        p2 = e2 / (1.0 + e2)

        @pl.when(pl.program_id(0) == 0)
        def _():
            count_ref[...] = jnp.zeros_like(count_ref)

        tm = x1.shape[0]
        chosen = jnp.logical_or(lane == i1, lane == i2)
        rank = count_ref[...] + jnp.dot(tri_ref[...], chosen.astype(bf16),
                                        preferred_element_type=f32)
        count_ref[...] += jnp.sum(chosen.astype(f32), axis=0, keepdims=True)
        r1 = jnp.sum(jnp.where(lane == i1, rank, 0.0), axis=-1, keepdims=True)
        r2 = jnp.sum(jnp.where(lane == i2, rank, 0.0), axis=-1, keepdims=True)
        fields = (i1.astype(f32), i2.astype(f32), r1, r2, p1, p2)
        route = jnp.zeros((tm, LANES), f32)
        for k, val in enumerate(fields):
            route = jnp.where(lane == k, val, route)
        route_ref[...] = route


def _mix_out(attn, ylru, x2d, w_out_bf16, g_attn, ln_g, ln_b, router_w=None, tm=512):
    n = x2d.shape[0]
    with_router = router_w is not None
    tile = lambda width: pl.BlockSpec((tm, width), lambda i: (i, 0))
    const = lambda shape: pl.BlockSpec(shape, lambda i: (0, 0))
    in_specs = [tile(D_ATTN), tile(D_LRU), tile(D_MODEL), const((D_MODEL, D_MODEL)),
                const((1, D_ATTN)), const((1, D_MODEL)), const((1, D_MODEL))]
    args = [attn, ylru, x2d, w_out_bf16, g_attn.reshape(1, D_ATTN).astype(f32),
            ln_g.reshape(1, D_MODEL).astype(f32), ln_b.reshape(1, D_MODEL).astype(f32)]
    out_specs = [tile(D_MODEL)]
    out_shape = [jax.ShapeDtypeStruct((n, D_MODEL), f32)]
    if with_router:
        rw = jnp.zeros((D_MODEL, LANES), f32).at[:, :N_EXPERTS].set(router_w.astype(f32))
        rw_hi = rw.astype(bf16)
        rw_lo = (rw - rw_hi.astype(f32)).astype(bf16)
        strictly_lower = jnp.asarray(np.tri(tm, k=-1), bf16)
        in_specs += [pl.BlockSpec((2, D_MODEL, LANES), lambda i: (0, 0, 0)), const((tm, tm))]
        args += [jnp.stack([rw_hi, rw_lo]), strictly_lower]
        out_specs += [pl.BlockSpec((tm * ROW_TILE, LANES), lambda i: (i, 0)), tile(LANES),
                      const((1, LANES))]
        out_shape += [jax.ShapeDtypeStruct((n * ROW_TILE, LANES), f32),
                      jax.ShapeDtypeStruct((n, LANES), f32), jax.ShapeDtypeStruct((1, LANES), f32)]
    res = pl.pallas_call(
        functools.partial(_mix_out_kernel, with_router=with_router),
        grid=(n // tm,),
        in_specs=in_specs,
        out_specs=out_specs,
        out_shape=out_shape,
        compiler_params=_params("arbitrary" if with_router else "parallel"),
        name="mix_out_router" if with_router else "mix_out",
    )(*args)
    return res if with_router else (res[0], None, None, None)


def _ffn_kernel(tile_expert_ref, n_used_ref, x_ref, wg_ref, wu_ref, wd_ref, *rest,
                n_chunks, fuse_ln):
    if fuse_ln:
        lng, lnb, out_ref, acc_ref, xb_ref = rest
    else:
        out_ref, acc_ref, xb_ref = rest
    del tile_expert_ref
    j = pl.program_id(1)
    used = pl.program_id(0) < n_used_ref[0]

    @pl.when(j == 0)
    def _():
        acc_ref[...] = jnp.zeros_like(acc_ref)

    @pl.when(used & (j == 0))
    def _():
        if fuse_ln:
            xb_ref[...] = x_ref[...].astype(bf16)
        else:
            xb_ref[...] = _load_row_tiles(x_ref, xb_ref.shape[0]).astype(bf16)

    @pl.when(used)
    def _():
        xb = xb_ref[...]
        g = jnp.dot(xb, wg_ref[...], preferred_element_type=f32)
        u = jnp.dot(xb, wu_ref[...], preferred_element_type=f32)
        h = (g * jax.nn.sigmoid(g)) * u
        acc_ref[...] += jnp.dot(h.astype(bf16), wd_ref[...], preferred_element_type=f32)

    @pl.when(j == n_chunks - 1)
    def _():
        if fuse_ln:
            out_ref[...] = _layer_norm(DEEPNORM_ALPHA * x_ref[...] + acc_ref[...],
                                       lng[...], lnb[...])
        else:
            _store_row_tiles(out_ref, acc_ref[...])


def _ffn(x2d, tile_expert, n_used, w_gate, w_up, w_down, ln=None, *, tm, tf, name):
    fuse_ln = ln is not None
    rows = x2d.shape[0] if fuse_ln else x2d.shape[0] // ROW_TILE
    d_ff = w_gate.shape[2]
    n_chunks = d_ff // tf
    io_block = (tm, D_MODEL) if fuse_ln else (tm * ROW_TILE, LANES)

    def chunk(i, j, nu):
        return jnp.where(i < nu[0], j, n_chunks - 1)

    resident = w_gate.shape[0] == 1 and n_chunks == 1
    mode = dict(pipeline_mode=pl.Buffered(1)) if resident else {}
    in_specs = [pl.BlockSpec(io_block, lambda i, j, te, nu: (i, 0)),
                pl.BlockSpec((None, D_MODEL, tf),
                             lambda i, j, te, nu: (te[i], 0, chunk(i, j, nu)), **mode),
                pl.BlockSpec((None, D_MODEL, tf),
                             lambda i, j, te, nu: (te[i], 0, chunk(i, j, nu)), **mode),
                pl.BlockSpec((None, tf, D_MODEL),
                             lambda i, j, te, nu: (te[i], chunk(i, j, nu), 0), **mode)]
    args = [x2d, w_gate, w_up, w_down]
    if fuse_ln:
        in_specs += [pl.BlockSpec((1, D_MODEL), lambda i, j, te, nu: (0, 0))] * 2
        args += [v.reshape(1, D_MODEL).astype(f32) for v in ln]
    return pl.pallas_call(
        functools.partial(_ffn_kernel, n_chunks=n_chunks, fuse_ln=fuse_ln),
        grid_spec=pltpu.PrefetchScalarGridSpec(
            num_scalar_prefetch=2,
            grid=(rows // tm, n_chunks),
            in_specs=in_specs,
            out_specs=pl.BlockSpec(io_block, lambda i, j, te, nu: (i, 0)),
            scratch_shapes=[pltpu.VMEM((tm, D_MODEL), f32), pltpu.VMEM((tm, D_MODEL), bf16)]),
        out_shape=jax.ShapeDtypeStruct(x2d.shape, f32),
        compiler_params=_params("parallel", "arbitrary"),
        name=name,
    )(tile_expert, n_used, *args)


def _dispatch_kernel(fill_ref, pos_ref, x_ref, xs_hbm, zero_buf, sem, *, tt, tm):
    step = pl.program_id(0)

    n_fill = fill_ref.shape[0] // 2

    @pl.when(step == 0)
    def _():
        zero_buf[...] = jnp.zeros_like(zero_buf)

        def fill_copy(f):
            start = pl.multiple_of(fill_ref[f] * ROW_TILE, ROW_TILE)
            return pltpu.make_async_copy(zero_buf, xs_hbm.at[pl.ds(start, tm * ROW_TILE)], sem)

        for f in range(n_fill):
            @pl.when(fill_ref[n_fill + f] > 0)
            def _():
                fill_copy(f).start()
        for f in range(n_fill):
            @pl.when(fill_ref[n_fill + f] > 0)
            def _():
                fill_copy(f).wait()

    def issue(t, carry):
        src = pl.multiple_of(t * ROW_TILE, ROW_TILE)
        for k in range(TOP_K):
            dst = pl.multiple_of(pos_ref[0, TOP_K * t + k] * ROW_TILE, ROW_TILE)
            pltpu.make_async_copy(x_ref.at[pl.ds(src, ROW_TILE)],
                                  xs_hbm.at[pl.ds(dst, ROW_TILE)], sem).start(priority=k)
        return carry

    lax.fori_loop(0, tt, issue, 0, unroll=DMA_UNROLL)
    for _ in range(TOP_K):
        pltpu.make_async_copy(x_ref, xs_hbm.at[pl.ds(0, tt * ROW_TILE)], sem).wait()


def _dispatch(x1_rows, pos_blocks, fill, rows_sorted, *, tt, tm):
    n = x1_rows.shape[0] // ROW_TILE
    return pl.pallas_call(
        functools.partial(_dispatch_kernel, tt=tt, tm=tm),
        grid_spec=pltpu.PrefetchScalarGridSpec(
            num_scalar_prefetch=1,
            grid=(n // tt,),
            in_specs=[pl.BlockSpec((None, 1, TOP_K * tt), lambda i, fill: (i, 0, 0),
                                   memory_space=pltpu.SMEM),
                      pl.BlockSpec((tt * ROW_TILE, LANES), lambda i, fill: (i, 0))],
            out_specs=pl.BlockSpec(memory_space=pl.ANY),
            scratch_shapes=[pltpu.VMEM((tm * ROW_TILE, LANES), f32), pltpu.SemaphoreType.DMA(())]),
        out_shape=jax.ShapeDtypeStruct((rows_sorted * ROW_TILE, LANES), f32),
        compiler_params=_params("arbitrary"),
        name="moe_dispatch",
    )(fill, pos_blocks, x1_rows)


def _combine_kernel(pos_ref, pos_next_ref, route_ref, x1_ref, ys_hbm, lng, lnb, out_ref, buf, sems,
                    *, tt, n_steps):
    step = pl.program_id(0)
    slot = step % 2

    def issue_tile(tile_pos_ref, into):
        def issue(t, carry):
            dst = pl.multiple_of(t * ROW_TILE, ROW_TILE)
            for k in range(TOP_K):
                src = pl.multiple_of(tile_pos_ref[0, TOP_K * t + k] * ROW_TILE, ROW_TILE)
                pltpu.make_async_copy(ys_hbm.at[pl.ds(src, ROW_TILE)],
                                      buf.at[into, k, pl.ds(dst, ROW_TILE)],
                                      sems.at[into]).start(priority=k)
            return carry

        lax.fori_loop(0, tt, issue, 0, unroll=DMA_UNROLL)

    @pl.when(step == 0)
    def _():
        issue_tile(pos_ref, 0)

    @pl.when(step + 1 < n_steps)
    def _():
        issue_tile(pos_next_ref, 1 - slot)

    for k in range(TOP_K):
        pltpu.make_async_copy(ys_hbm.at[pl.ds(0, tt * ROW_TILE)], buf.at[slot, k],
                              sems.at[slot]).wait()

    lane = lax.broadcasted_iota(jnp.int32, (1, LANES), 1)
    route = route_ref[...]
    p1 = jnp.sum(jnp.where(lane == 4, route, 0.0), axis=-1, keepdims=True)
    p2 = jnp.sum(jnp.where(lane == 5, route, 0.0), axis=-1, keepdims=True)
    y = p1 * _load_row_tiles(buf.at[slot, 0], tt) + p2 * _load_row_tiles(buf.at[slot, 1], tt)
    out_ref[...] = _layer_norm(DEEPNORM_ALPHA * x1_ref[...] + y, lng[...], lnb[...])


def _combine(pos_blocks, route, x1, y_sorted, ln_g, ln_b, *, tt):
    n = x1.shape[0]
    n_steps = n // tt
    pos_spec = lambda index: pl.BlockSpec((None, 1, TOP_K * tt), index, memory_space=pltpu.SMEM)
    return pl.pallas_call(
        functools.partial(_combine_kernel, tt=tt, n_steps=n_steps),
        grid=(n_steps,),
        in_specs=[pos_spec(lambda i: (i, 0, 0)),
                  pos_spec(lambda i: (jnp.minimum(i + 1, n_steps - 1), 0, 0)),
                  pl.BlockSpec((tt, LANES), lambda i: (i, 0)),
                  pl.BlockSpec((tt, D_MODEL), lambda i: (i, 0)),
                  pl.BlockSpec(memory_space=pl.ANY),
                  pl.BlockSpec((1, D_MODEL), lambda i: (0, 0)),
                  pl.BlockSpec((1, D_MODEL), lambda i: (0, 0))],
        out_specs=pl.BlockSpec((tt, D_MODEL), lambda i: (i, 0)),
        out_shape=jax.ShapeDtypeStruct((n, D_MODEL), f32),
        scratch_shapes=[pltpu.VMEM((2, TOP_K, tt * ROW_TILE, LANES), f32),
                        pltpu.SemaphoreType.DMA((2,))],
        compiler_params=_params("arbitrary"),
        name="moe_combine",
    )(pos_blocks, pos_blocks, route, x1, y_sorted, ln_g.reshape(1, D_MODEL).astype(f32),
      ln_b.reshape(1, D_MODEL).astype(f32))


def _moe(x1, x1_rows, route, counts, w_gate, w_up, w_down, ln_g, ln_b, *, tm, tf, tt):
    n = x1.shape[0]
    i32 = jnp.int32
    counts = counts[0, :N_EXPERTS].astype(i32)
    padded = (counts + tm - 1) // tm * tm
    ends = jnp.cumsum(padded)
    offsets = ends - padded
    experts = route[:, 0:TOP_K].astype(i32)
    ranks = route[:, TOP_K:2 * TOP_K].astype(i32)
    pos = (offsets[experts] + ranks).reshape(n // tt, 1, TOP_K * tt)
    n_tiles = TOP_K * n // tm + N_EXPERTS
    n_used = (ends[-1] // tm).astype(i32).reshape(1)
    tile_ids = jnp.arange(n_tiles, dtype=i32)
    tile_expert = jnp.sum((tile_ids[:, None] >= (ends // tm)[None, :]).astype(i32), axis=1)
    last_expert = jnp.max(jnp.where(counts > 0, jnp.arange(N_EXPERTS, dtype=i32), 0))
    tile_expert = jnp.minimum(tile_expert, last_expert).astype(i32)
    tail_tiles = n_used[0] + jnp.arange(N_EXPERTS, dtype=i32)
    fill = jnp.concatenate([ends - tm, tail_tiles * tm,
                            (counts > 0).astype(i32), (tail_tiles < n_tiles).astype(i32)]).astype(i32)

    x_sorted = _dispatch(x1_rows, pos, fill, n_tiles * tm, tt=tt, tm=tm)
    y_sorted = _ffn(x_sorted, tile_expert, n_used, w_gate, w_up, w_down, tm=tm, tf=tf,
                    name="ffn_experts")
    return _combine(pos, route, x1, y_sorted, ln_g, ln_b, tt=tt)


def kernel(x, w_in, conv_w, conv_b, w_a, b_a, w_x, b_x, lru_lambda, rel_bias, g_attn, g_lru, w_out, ln1_g, ln1_b, ln2_g, ln2_b, ffn_w_gate, ffn_w_up, ffn_w_down, router_w, moe_w_gate, moe_w_up, moe_w_down):
    batch, seq, _ = x.shape
    n = batch * seq
    h = x.reshape(n, D_MODEL).astype(f32)
    band_bias = jnp.stack([_band_bias(rel_bias, d) for _, d in DILATED_PATTERNS])
    for layer in range(DEPTH):
        qkv, ug = _in_proj(h, w_in[layer].astype(bf16))
        attn = _attention(qkv, band_bias, batch, seq)
        ylru = _lru_branch(ug, conv_w[layer], conv_b[layer], w_a[layer], b_a[layer], w_x[layer],
                           b_x[layer], lru_lambda[layer], g_lru[layer], batch, seq)
        j = layer // 2
        dense = layer % 2 == 0
        x1, x1_rows, route, counts = _mix_out(
            attn, ylru, h, w_out[layer].astype(bf16), g_attn[layer], ln1_g[layer],
            ln1_b[layer], None if dense else router_w[j])
        if dense:
            tm = 512
            h = _ffn(x1, jnp.zeros((n // tm,), jnp.int32), jnp.full((1,), n // tm, jnp.int32),
                     ffn_w_gate[j][None].astype(bf16), ffn_w_up[j][None].astype(bf16),
                     ffn_w_down[j][None].astype(bf16), (ln2_g[layer], ln2_b[layer]),
                     tm=tm, tf=ffn_w_gate.shape[2], name="ffn_dense")
        else:
            h = _moe(x1, x1_rows, route, counts, moe_w_gate[j].astype(bf16), moe_w_up[j].astype(bf16),
                     moe_w_down[j].astype(bf16), ln2_g[layer], ln2_b[layer],
                     tm=512, tf=1792, tt=256)
    return h.reshape(batch, seq, D_MODEL).astype(x.dtype)
```

```python
import functools

import numpy as np
import jax
import jax.numpy as jnp
from jax import lax
from jax.experimental import pallas as pl
from jax.experimental.pallas import tpu as pltpu

D_MODEL = 1024
N_HEADS = 8
HEAD_DIM = 64
D_ATTN = N_HEADS * HEAD_DIM
D_LRU = 512
N_LRU_BLOCKS = 8
LRU_BLOCK = D_LRU // N_LRU_BLOCKS
CONV_WIDTH = 4
LRU_C = 8.0
DILATED_PATTERNS = ((128, 1), (512, 4), (2048, 16))
ATTN_BLOCK = 128
N_BUCKETS = 32
MAX_DISTANCE = 2048
D_IN = 3 * D_ATTN + 2 * D_LRU
N_EXPERTS = 8
TOP_K = 2
DEPTH = 2
DEEPNORM_ALPHA = (2.0 * DEPTH) ** 0.25
LN_EPS = 1e-5
RMS_EPS = 1e-6
NEG_INF = -1e30
LOG2_E = float(np.log2(np.e))

LANES = 128
SUBLANES = 8
DMA_UNROLL = 8
SCORE_UNROLL = 8
ATTN_UNROLL = 4
VMEM_LIMIT_BYTES = 56 * 1024 * 1024

f32 = jnp.float32
bf16 = jnp.bfloat16


def _params(*semantics):
    return pltpu.CompilerParams(dimension_semantics=semantics,
                                vmem_limit_bytes=VMEM_LIMIT_BYTES)


def _in_proj_kernel(x_ref, w_ref, qkv_ref, ug_ref):
    xb = x_ref[...].astype(bf16)
    n_qkv = qkv_ref.shape[1]
    step = 512
    for c in range(0, D_IN, step):
        p = jnp.dot(xb, w_ref[:, c:c + step], preferred_element_type=f32)
        if c < n_qkv:
            qkv_ref[:, c:c + step] = p
        else:
            ug_ref[:, c - n_qkv:c - n_qkv + step] = p


def _in_proj(x2d, w_in_bf16, tm=1024):
    n = x2d.shape[0]
    return pl.pallas_call(
        _in_proj_kernel,
        grid=(n // tm,),
        in_specs=[pl.BlockSpec((tm, D_MODEL), lambda i: (i, 0)),
                  pl.BlockSpec((D_MODEL, D_IN), lambda i: (0, 0))],
        out_specs=[pl.BlockSpec((tm, 3 * D_ATTN), lambda i: (i, 0)),
                   pl.BlockSpec((tm, 2 * D_LRU), lambda i: (i, 0))],
        out_shape=[jax.ShapeDtypeStruct((n, 3 * D_ATTN), f32),
                   jax.ShapeDtypeStruct((n, 2 * D_LRU), f32)],
        compiler_params=_params("parallel"),
        name="in_proj",
    )(x2d, w_in_bf16)


def _t5_bucket(dist):
    max_exact = N_BUCKETS // 2
    d = np.maximum(dist, 1).astype(np.float32)
    large = max_exact + (np.log(d / max_exact) / np.log(MAX_DISTANCE / max_exact)
                         * (N_BUCKETS - max_exact)).astype(np.int32)
    large = np.minimum(large, N_BUCKETS - 1)
    return np.where(dist < max_exact, dist, large).astype(np.int32)


def _band_bias(rel_bias, dilation):
    nk = ATTN_BLOCK
    qi = np.arange(nk)[:, None]
    kj = np.arange(2 * nk)[None, :]
    delta = qi + nk - kj
    band = (delta >= 0) & (delta <= nk)
    bucket = _t5_bucket(np.clip(delta, 0, nk) * dilation)
    onehot = np.eye(N_BUCKETS, dtype=np.float32)[bucket.reshape(-1)]
    bias = jnp.dot(jnp.asarray(onehot), rel_bias.astype(f32), precision=lax.Precision.HIGHEST)
    bias = jnp.transpose(bias.reshape(nk, 2 * nk, N_HEADS), (2, 0, 1))
    valid = np.stack([band, band & (kj >= nk)])[:, None]
    bias = jnp.where(jnp.asarray(valid), bias[None], NEG_INF)
    return (bias * LOG2_E).reshape(2, N_HEADS // 2, 2 * nk, 2 * nk)


def _attn_kernel(q_ref, k_ref, v_ref, bias_ref, o_ref, qs, ks, vs, s_buf, m_buf, m_s, l_s, acc_s,
                 *, seq):
    nk = ATTN_BLOCK
    n_blocks = seq // nk
    chunk = 2 * nk
    lane = lax.broadcasted_iota(jnp.int32, (1, LANES), 1)
    head0 = lane < HEAD_DIM

    ks[0:nk, :] = jnp.zeros((nk, LANES), bf16)
    vs[0:nk, 0:LANES] = jnp.zeros((nk, LANES), bf16)
    vs[:, LANES:2 * LANES] = jnp.ones((seq + nk, LANES), bf16)

    order = sorted(range(len(DILATED_PATTERNS)), key=lambda p: -DILATED_PATTERNS[p][1])
    assert DILATED_PATTERNS[order[-1]][1] == 1
    for p in order:
        d = DILATED_PATTERNS[p][1]
        is_first, is_last = p == order[0], p == order[-1]
        length = seq // d
        nb = length // nk
        chunks_per_residue = length // chunk

        def gather(t, carry):
            r = t // chunks_per_residue
            c = t % chunks_per_residue
            src = pl.ds(r + d * chunk * c, chunk, stride=d)
            dst = pl.multiple_of(t * chunk, chunk)
            q = q_ref[src, :] * (HEAD_DIM ** -0.5 * LOG2_E)
            q0 = jnp.where(head0, q, 0.0).astype(bf16)
            q1 = jnp.where(head0, 0.0, q).astype(bf16)
            for half in range(2):
                base = pl.multiple_of(2 * dst + half * chunk, chunk)
                qs[pl.ds(base, nk), :] = q0[half * nk:(half + 1) * nk]
                qs[pl.ds(base + nk, nk), :] = q1[half * nk:(half + 1) * nk]
            ks[pl.ds(nk + dst, chunk), :] = k_ref[src, :].astype(bf16)
            vs[pl.ds(nk + dst, chunk), 0:LANES] = v_ref[src, :].astype(bf16)
            return carry

        lax.fori_loop(0, seq // chunk, gather, 0)

        def scores(b, carry):
            first = jnp.asarray(b % nb == 0, jnp.int32)
            rows = pl.ds(pl.multiple_of(b * chunk, chunk), chunk)
            k = ks[pl.ds(pl.multiple_of(b * nk, nk), chunk), :]
            s = lax.dot_general(qs[rows, :], k, (((1,), (1,)), ((), ())),
                                preferred_element_type=f32) + bias_ref[p, first]
            s_buf[rows, :] = s
            m_buf[rows, :] = jnp.broadcast_to(jnp.max(s, axis=-1, keepdims=True), (chunk, LANES))
            return carry

        def block(b, carry):
            r = b // nb
            i = b % nb
            rows = pl.ds(pl.multiple_of(b * chunk, chunk), chunk)
            v = vs[pl.ds(pl.multiple_of(b * nk, nk), chunk), :]
            m = m_buf[rows, :]
            e = jnp.exp2(s_buf[rows, :] - jnp.concatenate([m, m], axis=1))
            pv = jnp.dot(e.astype(bf16), v, preferred_element_type=f32)
            m_blk = jnp.where(head0, m[0:nk], m[nk:chunk])
            l_blk = jnp.where(head0, pv[0:nk, LANES:], pv[nk:chunk, LANES:])
            pv_blk = jnp.where(head0, pv[0:nk, 0:LANES], pv[nk:chunk, 0:LANES])
            if d == 1:
                tok = pl.ds(pl.multiple_of(b * nk, nk), nk)
            else:
                tok = pl.ds(r + d * nk * i, nk, stride=d)
            if is_first:
                m_s[tok, :] = m_blk
                l_s[tok, :] = l_blk
                acc_s[tok, :] = pv_blk
                return carry
            m_old = m_s[tok, :]
            m_new = jnp.maximum(m_old, m_blk)
            w_old = jnp.exp2(m_old - m_new)
            w_blk = jnp.exp2(m_blk - m_new)
            l_new = w_old * l_s[tok, :] + w_blk * l_blk
            acc_new = w_old * acc_s[tok, :] + w_blk * pv_blk
            if is_last:
                o_ref[tok, :] = (acc_new / l_new).astype(o_ref.dtype)
            else:
                l_s[tok, :] = l_new
                acc_s[tok, :] = acc_new
                m_s[tok, :] = m_new
            return carry

        n_groups = n_blocks // ATTN_UNROLL

        def group(fn, g):
            for j in range(ATTN_UNROLL):
                fn(g * ATTN_UNROLL + j, 0)

        def pipelined(g, carry):
            group(block, g)
            group(scores, g + 1)
            return carry

        group(scores, 0)
        lax.fori_loop(0, n_groups - 1, pipelined, 0)
        group(block, n_groups - 1)


def _attention(qkv, band_bias, batch, seq):
    view = qkv.reshape(batch, seq, 3 * D_ATTN)
    pairs = D_ATTN // LANES

    def spec(offset):
        return pl.BlockSpec((None, seq, LANES), lambda b, hp: (b, 0, offset * pairs + hp))

    n_pat = len(DILATED_PATTERNS)
    o = pl.pallas_call(
        functools.partial(_attn_kernel, seq=seq),
        grid=(batch, pairs),
        in_specs=[spec(0), spec(1), spec(2),
                  pl.BlockSpec((n_pat, 2, None, 2 * ATTN_BLOCK, 2 * ATTN_BLOCK),
                               lambda b, hp: (0, 0, hp, 0, 0))],
        out_specs=pl.BlockSpec((None, seq, LANES), lambda b, hp: (b, 0, hp)),
        out_shape=jax.ShapeDtypeStruct((batch, seq, D_ATTN), bf16),
        scratch_shapes=[pltpu.VMEM((2 * seq, LANES), bf16),
                        pltpu.VMEM((seq + ATTN_BLOCK, LANES), bf16),
                        pltpu.VMEM((seq + ATTN_BLOCK, 2 * LANES), bf16),
                        pltpu.VMEM((2 * seq, 2 * ATTN_BLOCK), f32),
                        pltpu.VMEM((2 * seq, LANES), f32),
                        pltpu.VMEM((seq, LANES), f32), pltpu.VMEM((seq, LANES), f32),
                        pltpu.VMEM((seq, LANES), f32)],
        compiler_params=_params("parallel", "parallel"),
        name="attention",
    )(view, view, view, band_bias)
    return o.reshape(batch * seq, D_ATTN)


def _gelu_tanh(x):
    return 0.5 * x * (1.0 + jnp.tanh(np.sqrt(2.0 / np.pi) * (x + 0.044715 * x * x * x)))


def _lru_kernel(ug_ref, convw_ref, convb_ref, wg_ref, ba_ref, bx_ref, lam_ref, g_ref,
                y_ref, ubuf, hcarry, *, ts):
    pad = SUBLANES
    t = pl.program_id(1)

    @pl.when(t == 0)
    def _():
        ubuf[0:pad, :] = jnp.zeros((pad, D_LRU), f32)
        hcarry[...] = jnp.zeros_like(hcarry)

    @pl.when(t > 0)
    def _():
        ubuf[0:pad, :] = ubuf[ts:ts + pad, :]

    ubuf[pad:pad + ts, :] = ug_ref[:, 0:D_LRU]
    gate = ug_ref[:, D_LRU:2 * D_LRU]

    u = convb_ref[...] + convw_ref[CONV_WIDTH - 1:CONV_WIDTH, :] * ubuf[pad:pad + ts, :]
    for w in range(CONV_WIDTH - 1):
        back = CONV_WIDTH - 1 - w
        u = u + convw_ref[w:w + 1, :] * ubuf[pad - back:pad - back + ts, :]

    gates = jnp.dot(u.astype(bf16), wg_ref[...], preferred_element_type=f32)
    r = jax.nn.sigmoid(gates[:, 0:D_LRU] + ba_ref[...])
    i = jax.nn.sigmoid(gates[:, D_LRU:2 * D_LRU] + bx_ref[...])
    neg_lam = -lam_ref[...]
    softplus = jnp.maximum(neg_lam, 0.0) + jnp.log1p(jnp.exp(-jnp.abs(neg_lam)))
    log_a = (-LRU_C) * r * softplus
    a = jnp.exp(log_a)
    b = jnp.sqrt(-jnp.tanh(log_a) * (1.0 + a * a)) * (i * u)

    groups = ts // SUBLANES
    a = a.reshape(groups, SUBLANES, D_LRU)
    b = b.reshape(groups, SUBLANES, D_LRU)
    sub = lax.broadcasted_iota(jnp.int32, (1, SUBLANES, 1), 1)
    shift = 1
    while shift < SUBLANES:
        live = sub >= shift
        a_prev = jnp.where(live, pltpu.roll(a, shift, 1), 1.0)
        b_prev = jnp.where(live, pltpu.roll(b, shift, 1), 0.0)
        b = a * b_prev + b
        a = a * a_prev
        shift *= 2
    h_prev = hcarry[...]
    h_groups = []
    for g in range(groups):
        h_g = a[g] * h_prev + b[g]
        h_prev = h_g[SUBLANES - 1:SUBLANES, :]
        h_groups.append(h_g)
    hcarry[...] = h_prev
    h = jnp.concatenate(h_groups, axis=0)

    y = _gelu_tanh(gate) * h
    y = y * lax.rsqrt(jnp.mean(y * y, axis=-1, keepdims=True) + RMS_EPS) * g_ref[...]
    y_ref[...] = y.astype(y_ref.dtype)


def _block_diag(w):
    g, i, j = w.shape
    eye = jnp.eye(g, dtype=w.dtype)
    return jnp.einsum('gij,gh->gihj', w, eye).reshape(g * i, g * j)


def _lru_branch(ug, conv_w, conv_b, w_a, b_a, w_x, b_x, lam, g_lru, batch, seq, ts=512):
    wg = jnp.concatenate([_block_diag(w_a), _block_diag(w_x)], axis=1).astype(bf16)
    row = lambda v: v.reshape(1, D_LRU).astype(f32)
    const = lambda shape: pl.BlockSpec(shape, lambda b, t: (0, 0))
    view = ug.reshape(batch, seq, 2 * D_LRU)
    y = pl.pallas_call(
        functools.partial(_lru_kernel, ts=ts),
        grid=(batch, seq // ts),
        in_specs=[pl.BlockSpec((None, ts, 2 * D_LRU), lambda b, t: (b, t, 0)),
                  const((CONV_WIDTH, D_LRU)), const((1, D_LRU)),
                  const((D_LRU, 2 * D_LRU)), const((1, D_LRU)), const((1, D_LRU)),
                  const((1, D_LRU)), const((1, D_LRU))],
        out_specs=pl.BlockSpec((None, ts, D_LRU), lambda b, t: (b, t, 0)),
        out_shape=jax.ShapeDtypeStruct((batch, seq, D_LRU), bf16),
        scratch_shapes=[pltpu.VMEM((ts + 16, D_LRU), f32), pltpu.VMEM((1, D_LRU), f32)],
        compiler_params=_params("parallel", "arbitrary"),
        name="rglru",
    )(view, conv_w.reshape(CONV_WIDTH, D_LRU).astype(f32), row(conv_b), wg, row(b_a), row(b_x),
      row(lam), row(g_lru))
    return y.reshape(batch * seq, D_LRU)


ROW_TILE = D_MODEL // LANES


def _store_row_tiles(ref, rows):
    t = rows.shape[0]
    for s in range(ROW_TILE):
        ref[pl.ds(s, t, stride=ROW_TILE), :] = rows[:, s * LANES:(s + 1) * LANES]


def _load_row_tiles(ref, t):
    return jnp.concatenate([ref[pl.ds(s, t, stride=ROW_TILE), :] for s in range(ROW_TILE)],
                           axis=-1)


def _layer_norm(z, g, b):
    mu = jnp.mean(z, axis=-1, keepdims=True)
    zc = z - mu
    var = jnp.mean(zc * zc, axis=-1, keepdims=True)
    return zc * lax.rsqrt(var + LN_EPS) * g + b


def _mix_out_kernel(attn_ref, ylru, x_ref, w_ref, gattn, lng, lnb, *rest, with_router):
    if with_router:
        rw_ref, tri_ref, x1_ref, route_ref, count_ref = rest
    else:
        (x1_ref,) = rest
    attn = attn_ref[...].astype(f32)
    attn = attn * lax.rsqrt(jnp.mean(attn * attn, axis=-1, keepdims=True) + RMS_EPS) * gattn[...]
    y = jnp.dot(attn.astype(bf16), w_ref[0:D_ATTN, :], preferred_element_type=f32)
    y = y + jnp.dot(ylru[...], w_ref[D_ATTN:, :], preferred_element_type=f32)
    x1 = _layer_norm(DEEPNORM_ALPHA * x_ref[...] + y, lng[...], lnb[...])
    x1_ref[...] = x1
    if with_router:
        lane = lax.broadcasted_iota(jnp.int32, (1, LANES), 1)
        x_hi = x1.astype(bf16)
        x_lo = (x1 - x_hi.astype(f32)).astype(bf16)
        logits = (jnp.dot(x_hi, rw_ref[0], preferred_element_type=f32)
                  + jnp.dot(x_lo, rw_ref[0], preferred_element_type=f32)
                  + jnp.dot(x_hi, rw_ref[1], preferred_element_type=f32))
        logits = jnp.where(lane < N_EXPERTS, logits, -jnp.inf)
        v1 = jnp.max(logits, axis=-1, keepdims=True)
        i1 = jnp.min(jnp.where(logits == v1, lane, LANES), axis=-1, keepdims=True)
        rest_logits = jnp.where(lane == i1, -jnp.inf, logits)
        v2 = jnp.max(rest_logits, axis=-1, keepdims=True)
        i2 = jnp.min(jnp.where(rest_logits == v2, lane, LANES), axis=-1, keepdims=True)
        e2 = jnp.exp(v2 - v1)
        p1 = 1.0 / (1.0 + e2)
        p2 = e2 / (1.0 + e2)

        @pl.when(pl.program_id(0) == 0)
        def _():
            count_ref[...] = jnp.zeros_like(count_ref)

        tm = x1.shape[0]
        chosen = jnp.logical_or(lane == i1, lane == i2)
        rank = count_ref[...] + jnp.dot(tri_ref[...], chosen.astype(bf16),
                                        preferred_element_type=f32)
        count_ref[...] += jnp.sum(chosen.astype(f32), axis=0, keepdims=True)
        r1 = jnp.sum(jnp.where(lane == i1, rank, 0.0), axis=-1, keepdims=True)
        r2 = jnp.sum(jnp.where(lane == i2, rank, 0.0), axis=-1, keepdims=True)
        fields = (i1.astype(f32), i2.astype(f32), r1, r2, p1, p2)
        route = jnp.zeros((tm, LANES), f32)
        for k, val in enumerate(fields):
            route = jnp.where(lane == k, val, route)
        route_ref[...] = route


def _mix_out(attn, ylru, x2d, w_out_bf16, g_attn, ln_g, ln_b, router_w=None):
    n = x2d.shape[0]
    with_router = router_w is not None
    tm = 512 if with_router else 1024
    tile = lambda width: pl.BlockSpec((tm, width), lambda i: (i, 0))
    const = lambda shape: pl.BlockSpec(shape, lambda i: (0, 0))
    in_specs = [tile(D_ATTN), tile(D_LRU), tile(D_MODEL), const((D_MODEL, D_MODEL)),
                const((1, D_ATTN)), const((1, D_MODEL)), const((1, D_MODEL))]
    args = [attn, ylru, x2d, w_out_bf16, g_attn.reshape(1, D_ATTN).astype(f32),
            ln_g.reshape(1, D_MODEL).astype(f32), ln_b.reshape(1, D_MODEL).astype(f32)]
    out_specs = [tile(D_MODEL)]
    out_shape = [jax.ShapeDtypeStruct((n, D_MODEL), f32)]
    if with_router:
        rw = jnp.zeros((D_MODEL, LANES), f32).at[:, :N_EXPERTS].set(router_w.astype(f32))
        rw_hi = rw.astype(bf16)
        rw_lo = (rw - rw_hi.astype(f32)).astype(bf16)
        strictly_lower = jnp.asarray(np.tri(tm, k=-1), bf16)
        in_specs += [pl.BlockSpec((2, D_MODEL, LANES), lambda i: (0, 0, 0)), const((tm, tm))]
        args += [jnp.stack([rw_hi, rw_lo]), strictly_lower]
        out_specs += [tile(LANES), const((1, LANES))]
        out_shape += [jax.ShapeDtypeStruct((n, LANES), f32), jax.ShapeDtypeStruct((1, LANES), f32)]
    res = pl.pallas_call(
        functools.partial(_mix_out_kernel, with_router=with_router),
        grid=(n // tm,),
        in_specs=in_specs,
        out_specs=out_specs,
        out_shape=out_shape,
        compiler_params=_params("arbitrary" if with_router else "parallel"),
        name="mix_out_router" if with_router else "mix_out",
    )(*args)
    return res if with_router else (res[0], None, None)


def _ffn_kernel(tile_expert_ref, n_used_ref, x_ref, wg_ref, wu_ref, wd_ref, *rest,
                n_chunks, fuse_ln):
    if fuse_ln:
        lng, lnb, out_ref, acc_ref, xb_ref = rest
    else:
        out_ref, acc_ref, xb_ref = rest
    del tile_expert_ref
    j = pl.program_id(1)
    used = pl.program_id(0) < n_used_ref[0]

    @pl.when(j == 0)
    def _():
        acc_ref[...] = jnp.zeros_like(acc_ref)

    @pl.when(used & (j == 0))
    def _():
        if fuse_ln:
            xb_ref[...] = x_ref[...].astype(bf16)
        else:
            xb_ref[...] = _load_row_tiles(x_ref, xb_ref.shape[0]).astype(bf16)

    @pl.when(used)
    def _():
        xb = xb_ref[...]
        g = jnp.dot(xb, wg_ref[...], preferred_element_type=f32)
        u = jnp.dot(xb, wu_ref[...], preferred_element_type=f32)
        h = (g * jax.nn.sigmoid(g)) * u
        acc_ref[...] += jnp.dot(h.astype(bf16), wd_ref[...], preferred_element_type=f32)

    @pl.when(j == n_chunks - 1)
    def _():
        if fuse_ln:
            out_ref[...] = _layer_norm(DEEPNORM_ALPHA * x_ref[...] + acc_ref[...],
                                       lng[...], lnb[...])
        else:
            _store_row_tiles(out_ref, acc_ref[...])


def _ffn(x2d, tile_expert, n_used, w_gate, w_up, w_down, ln=None, *, tm, tf, name):
    fuse_ln = ln is not None
    rows = x2d.shape[0] if fuse_ln else x2d.shape[0] // ROW_TILE
    d_ff = w_gate.shape[2]
    n_chunks = d_ff // tf
    io_block = (tm, D_MODEL) if fuse_ln else (tm * ROW_TILE, LANES)

    def chunk(i, j, nu):
        return jnp.where(i < nu[0], j, n_chunks - 1)

    resident = w_gate.shape[0] == 1 and n_chunks == 1
    mode = dict(pipeline_mode=pl.Buffered(1)) if resident else {}
    in_specs = [pl.BlockSpec(io_block, lambda i, j, te, nu: (i, 0)),
                pl.BlockSpec((None, D_MODEL, tf),
                             lambda i, j, te, nu: (te[i], 0, chunk(i, j, nu)), **mode),
                pl.BlockSpec((None, D_MODEL, tf),
                             lambda i, j, te, nu: (te[i], 0, chunk(i, j, nu)), **mode),
                pl.BlockSpec((None, tf, D_MODEL),
                             lambda i, j, te, nu: (te[i], chunk(i, j, nu), 0), **mode)]
    args = [x2d, w_gate, w_up, w_down]
    if fuse_ln:
        in_specs += [pl.BlockSpec((1, D_MODEL), lambda i, j, te, nu: (0, 0))] * 2
        args += [v.reshape(1, D_MODEL).astype(f32) for v in ln]
    return pl.pallas_call(
        functools.partial(_ffn_kernel, n_chunks=n_chunks, fuse_ln=fuse_ln),
        grid_spec=pltpu.PrefetchScalarGridSpec(
            num_scalar_prefetch=2,
            grid=(rows // tm, n_chunks),
            in_specs=in_specs,
            out_specs=pl.BlockSpec(io_block, lambda i, j, te, nu: (i, 0)),
            scratch_shapes=[pltpu.VMEM((tm, D_MODEL), f32), pltpu.VMEM((tm, D_MODEL), bf16)]),
        out_shape=jax.ShapeDtypeStruct(x2d.shape, f32),
        compiler_params=_params("parallel", "arbitrary"),
        name=name,
    )(tile_expert, n_used, *args)


def _dispatch_kernel(fill_ref, pos_ref, x_ref, xs_hbm, zero_buf, rows_buf, sem, *, tt, tm):
    step = pl.program_id(0)

    n_fill = fill_ref.shape[0] // 2

    @pl.when(step == 0)
    def _():
        zero_buf[...] = jnp.zeros_like(zero_buf)

        def fill_copy(f):
            start = pl.multiple_of(fill_ref[f] * ROW_TILE, ROW_TILE)
            return pltpu.make_async_copy(zero_buf, xs_hbm.at[pl.ds(start, tm * ROW_TILE)], sem)

        for f in range(n_fill):
            @pl.when(fill_ref[n_fill + f] > 0)
            def _():
                fill_copy(f).start()
        for f in range(n_fill):
            @pl.when(fill_ref[n_fill + f] > 0)
            def _():
                fill_copy(f).wait()

    _store_row_tiles(rows_buf, x_ref[...])

    def issue(t, carry):
        src = pl.multiple_of(t * ROW_TILE, ROW_TILE)
        for k in range(TOP_K):
            dst = pl.multiple_of(pos_ref[0, TOP_K * t + k] * ROW_TILE, ROW_TILE)
            pltpu.make_async_copy(rows_buf.at[pl.ds(src, ROW_TILE)],
                                  xs_hbm.at[pl.ds(dst, ROW_TILE)], sem).start(priority=k)
        return carry

    lax.fori_loop(0, tt, issue, 0, unroll=DMA_UNROLL)
    for _ in range(TOP_K):
        pltpu.make_async_copy(rows_buf, xs_hbm.at[pl.ds(0, tt * ROW_TILE)], sem).wait()


def _dispatch(x1, pos_blocks, fill, rows_sorted, *, tt, tm):
    n = x1.shape[0]
    return pl.pallas_call(
        functools.partial(_dispatch_kernel, tt=tt, tm=tm),
        grid_spec=pltpu.PrefetchScalarGridSpec(
            num_scalar_prefetch=1,
            grid=(n // tt,),
            in_specs=[pl.BlockSpec((None, 1, TOP_K * tt), lambda i, fill: (i, 0, 0),
                                   memory_space=pltpu.SMEM),
                      pl.BlockSpec((tt, D_MODEL), lambda i, fill: (i, 0))],
            out_specs=pl.BlockSpec(memory_space=pl.ANY),
            scratch_shapes=[pltpu.VMEM((tm * ROW_TILE, LANES), f32),
                            pltpu.VMEM((tt * ROW_TILE, LANES), f32),
                            pltpu.SemaphoreType.DMA(())]),
        out_shape=jax.ShapeDtypeStruct((rows_sorted * ROW_TILE, LANES), f32),
        compiler_params=_params("arbitrary"),
        name="moe_dispatch",
    )(fill, pos_blocks, x1)


def _combine_kernel(pos_ref, pos_next_ref, route_ref, x1_ref, ys_hbm, lng, lnb, out_ref, buf, sems,
                    *, tt, n_steps):
    step = pl.program_id(0)
    slot = step % 2

    def issue_tile(tile_pos_ref, into):
        def issue(t, carry):
            dst = pl.multiple_of(t * ROW_TILE, ROW_TILE)
            for k in range(TOP_K):
                src = pl.multiple_of(tile_pos_ref[0, TOP_K * t + k] * ROW_TILE, ROW_TILE)
                pltpu.make_async_copy(ys_hbm.at[pl.ds(src, ROW_TILE)],
                                      buf.at[into, k, pl.ds(dst, ROW_TILE)],
                                      sems.at[into]).start(priority=k)
            return carry

        lax.fori_loop(0, tt, issue, 0, unroll=DMA_UNROLL)

    @pl.when(step == 0)
    def _():
        issue_tile(pos_ref, 0)

    @pl.when(step + 1 < n_steps)
    def _():
        issue_tile(pos_next_ref, 1 - slot)

    for k in range(TOP_K):
        pltpu.make_async_copy(ys_hbm.at[pl.ds(0, tt * ROW_TILE)], buf.at[slot, k],
                              sems.at[slot]).wait()

    lane = lax.broadcasted_iota(jnp.int32, (1, LANES), 1)
    route = route_ref[...]
    p1 = jnp.sum(jnp.where(lane == 4, route, 0.0), axis=-1, keepdims=True)
    p2 = jnp.sum(jnp.where(lane == 5, route, 0.0), axis=-1, keepdims=True)
    y = p1 * _load_row_tiles(buf.at[slot, 0], tt) + p2 * _load_row_tiles(buf.at[slot, 1], tt)
    out_ref[...] = _layer_norm(DEEPNORM_ALPHA * x1_ref[...] + y, lng[...], lnb[...])


def _combine(pos_blocks, route, x1, y_sorted, ln_g, ln_b, *, tt):
    n = x1.shape[0]
    n_steps = n // tt
    pos_spec = lambda index: pl.BlockSpec((None, 1, TOP_K * tt), index, memory_space=pltpu.SMEM)
    return pl.pallas_call(
        functools.partial(_combine_kernel, tt=tt, n_steps=n_steps),
        grid=(n_steps,),
        in_specs=[pos_spec(lambda i: (i, 0, 0)),
                  pos_spec(lambda i: (jnp.minimum(i + 1, n_steps - 1), 0, 0)),
                  pl.BlockSpec((tt, LANES), lambda i: (i, 0)),
                  pl.BlockSpec((tt, D_MODEL), lambda i: (i, 0)),
                  pl.BlockSpec(memory_space=pl.ANY),
                  pl.BlockSpec((1, D_MODEL), lambda i: (0, 0)),
                  pl.BlockSpec((1, D_MODEL), lambda i: (0, 0))],
        out_specs=pl.BlockSpec((tt, D_MODEL), lambda i: (i, 0)),
        out_shape=jax.ShapeDtypeStruct((n, D_MODEL), f32),
        scratch_shapes=[pltpu.VMEM((2, TOP_K, tt * ROW_TILE, LANES), f32),
                        pltpu.SemaphoreType.DMA((2,))],
        compiler_params=_params("arbitrary"),
        name="moe_combine",
    )(pos_blocks, pos_blocks, route, x1, y_sorted, ln_g.reshape(1, D_MODEL).astype(f32),
      ln_b.reshape(1, D_MODEL).astype(f32))


def _moe(x1, route, counts, w_gate, w_up, w_down, ln_g, ln_b, *, tm, tf, tt):
    n = x1.shape[0]
    i32 = jnp.int32
    counts = counts[0, :N_EXPERTS].astype(i32)
    padded = (counts + tm - 1) // tm * tm
    ends = jnp.cumsum(padded)
    offsets = ends - padded
    experts = route[:, 0:TOP_K].astype(i32)
    ranks = route[:, TOP_K:2 * TOP_K].astype(i32)
    pos = (offsets[experts] + ranks).reshape(n // tt, 1, TOP_K * tt)
    n_tiles = TOP_K * n // tm + N_EXPERTS
    n_used = (ends[-1] // tm).astype(i32).reshape(1)
    tile_ids = jnp.arange(n_tiles, dtype=i32)
    tile_expert = jnp.sum((tile_ids[:, None] >= (ends // tm)[None, :]).astype(i32), axis=1)
    last_expert = jnp.max(jnp.where(counts > 0, jnp.arange(N_EXPERTS, dtype=i32), 0))
    tile_expert = jnp.minimum(tile_expert, last_expert).astype(i32)
    tail_tiles = n_used[0] + jnp.arange(N_EXPERTS, dtype=i32)
    fill = jnp.concatenate([ends - tm, tail_tiles * tm,
                            (counts > 0).astype(i32), (tail_tiles < n_tiles).astype(i32)]).astype(i32)

    x_sorted = _dispatch(x1, pos, fill, n_tiles * tm, tt=tt, tm=tm)
    y_sorted = _ffn(x_sorted, tile_expert, n_used, w_gate, w_up, w_down, tm=tm, tf=tf,
                    name="ffn_experts")
    return _combine(pos, route, x1, y_sorted, ln_g, ln_b, tt=tt)


def kernel(x, w_in, conv_w, conv_b, w_a, b_a, w_x, b_x, lru_lambda, rel_bias, g_attn, g_lru, w_out, ln1_g, ln1_b, ln2_g, ln2_b, ffn_w_gate, ffn_w_up, ffn_w_down, router_w, moe_w_gate, moe_w_up, moe_w_down):
    batch, seq, _ = x.shape
    n = batch * seq
    h = x.reshape(n, D_MODEL).astype(f32)
    band_bias = jnp.stack([_band_bias(rel_bias, d) for _, d in DILATED_PATTERNS])
    for layer in range(DEPTH):
        qkv, ug = _in_proj(h, w_in[layer].astype(bf16))
        attn = _attention(qkv, band_bias, batch, seq)
        ylru = _lru_branch(ug, conv_w[layer], conv_b[layer], w_a[layer], b_a[layer], w_x[layer],
                           b_x[layer], lru_lambda[layer], g_lru[layer], batch, seq)
        j = layer // 2
        dense = layer % 2 == 0
        x1, route, counts = _mix_out(
            attn, ylru, h, w_out[layer].astype(bf16), g_attn[layer], ln1_g[layer],
            ln1_b[layer], None if dense else router_w[j])
        if dense:
            tm = 512
            h = _ffn(x1, jnp.zeros((n // tm,), jnp.int32), jnp.full((1,), n // tm, jnp.int32),
                     ffn_w_gate[j][None].astype(bf16), ffn_w_up[j][None].astype(bf16),
                     ffn_w_down[j][None].astype(bf16), (ln2_g[layer], ln2_b[layer]),
                     tm=tm, tf=ffn_w_gate.shape[2], name="ffn_dense")
        else:
            h = _moe(x1, route, counts, moe_w_gate[j].astype(bf16), moe_w_up[j].astype(bf16),
                     moe_w_down[j].astype(bf16), ln2_g[layer], ln2_b[layer],
                     tm=512, tf=1792, tt=512)
    return h.reshape(batch, seq, D_MODEL).astype(x.dtype)
```

```python
import functools

import numpy as np
import jax
import jax.numpy as jnp
from jax import lax
from jax.experimental import pallas as pl
from jax.experimental.pallas import tpu as pltpu

D_MODEL = 1024
N_HEADS = 8
HEAD_DIM = 64
D_ATTN = N_HEADS * HEAD_DIM
D_LRU = 512
N_LRU_BLOCKS = 8
LRU_BLOCK = D_LRU // N_LRU_BLOCKS
CONV_WIDTH = 4
LRU_C = 8.0
DILATED_PATTERNS = ((128, 1), (512, 4), (2048, 16))
ATTN_BLOCK = 128
N_BUCKETS = 32
MAX_DISTANCE = 2048
D_IN = 3 * D_ATTN + 2 * D_LRU
N_EXPERTS = 8
TOP_K = 2
DEPTH = 2
DEEPNORM_ALPHA = (2.0 * DEPTH) ** 0.25
LN_EPS = 1e-5
RMS_EPS = 1e-6
NEG_INF = -1e30
LOG2_E = float(np.log2(np.e))

LANES = 128
SUBLANES = 8
DMA_UNROLL = 8
SCORE_UNROLL = 8
ATTN_UNROLL = 4
VMEM_LIMIT_BYTES = 56 * 1024 * 1024

f32 = jnp.float32
bf16 = jnp.bfloat16


def _params(*semantics):
    return pltpu.CompilerParams(dimension_semantics=semantics,
                                vmem_limit_bytes=VMEM_LIMIT_BYTES)


def _in_proj_kernel(x_ref, w_ref, qkv_ref, ug_ref):
    xb = x_ref[...].astype(bf16)
    n_qkv = qkv_ref.shape[1]
    step = 512
    for c in range(0, D_IN, step):
        p = jnp.dot(xb, w_ref[:, c:c + step], preferred_element_type=f32)
        if c < n_qkv:
            qkv_ref[:, c:c + step] = p
        else:
            ug_ref[:, c - n_qkv:c - n_qkv + step] = p


def _in_proj(x2d, w_in_bf16, tm=1024):
    n = x2d.shape[0]
    return pl.pallas_call(
        _in_proj_kernel,
        grid=(n // tm,),
        in_specs=[pl.BlockSpec((tm, D_MODEL), lambda i: (i, 0)),
                  pl.BlockSpec((D_MODEL, D_IN), lambda i: (0, 0))],
        out_specs=[pl.BlockSpec((tm, 3 * D_ATTN), lambda i: (i, 0)),
                   pl.BlockSpec((tm, 2 * D_LRU), lambda i: (i, 0))],
        out_shape=[jax.ShapeDtypeStruct((n, 3 * D_ATTN), f32),
                   jax.ShapeDtypeStruct((n, 2 * D_LRU), f32)],
        compiler_params=_params("parallel"),
        name="in_proj",
    )(x2d, w_in_bf16)


def _t5_bucket(dist):
    max_exact = N_BUCKETS // 2
    d = np.maximum(dist, 1).astype(np.float32)
    large = max_exact + (np.log(d / max_exact) / np.log(MAX_DISTANCE / max_exact)
                         * (N_BUCKETS - max_exact)).astype(np.int32)
    large = np.minimum(large, N_BUCKETS - 1)
    return np.where(dist < max_exact, dist, large).astype(np.int32)


def _band_bias(rel_bias, dilation):
    nk = ATTN_BLOCK
    qi = np.arange(nk)[:, None]
    kj = np.arange(2 * nk)[None, :]
    delta = qi + nk - kj
    band = (delta >= 0) & (delta <= nk)
    bucket = _t5_bucket(np.clip(delta, 0, nk) * dilation)
    onehot = np.eye(N_BUCKETS, dtype=np.float32)[bucket.reshape(-1)]
    bias = jnp.dot(jnp.asarray(onehot), rel_bias.astype(f32), precision=lax.Precision.HIGHEST)
    bias = jnp.transpose(bias.reshape(nk, 2 * nk, N_HEADS), (2, 0, 1))
    valid = np.stack([band, band & (kj >= nk)])[:, None]
    bias = jnp.where(jnp.asarray(valid), bias[None], NEG_INF)
    return (bias * LOG2_E).reshape(2, N_HEADS // 2, 2 * nk, 2 * nk)


def _attn_kernel(q_ref, k_ref, v_ref, bias_ref, o_ref, qs, ks, vs, s_buf, m_buf, m_s, l_s, acc_s,
                 *, seq):
    nk = ATTN_BLOCK
    n_blocks = seq // nk
    chunk = 2 * nk
    lane = lax.broadcasted_iota(jnp.int32, (1, LANES), 1)
    head0 = lane < HEAD_DIM

    ks[0:nk, :] = jnp.zeros((nk, LANES), bf16)
    vs[0:nk, 0:LANES] = jnp.zeros((nk, LANES), bf16)
    vs[:, LANES:2 * LANES] = jnp.ones((seq + nk, LANES), bf16)

    order = sorted(range(len(DILATED_PATTERNS)), key=lambda p: -DILATED_PATTERNS[p][1])
    assert DILATED_PATTERNS[order[-1]][1] == 1
    for p in order:
        d = DILATED_PATTERNS[p][1]
        is_first, is_last = p == order[0], p == order[-1]
        length = seq // d
        nb = length // nk
        chunks_per_residue = length // chunk

        def gather(t, carry):
            r = t // chunks_per_residue
            c = t % chunks_per_residue
            src = pl.ds(r + d * chunk * c, chunk, stride=d)
            dst = pl.multiple_of(t * chunk, chunk)
            q = q_ref[src, :] * (HEAD_DIM ** -0.5 * LOG2_E)
            q0 = jnp.where(head0, q, 0.0).astype(bf16)
            q1 = jnp.where(head0, 0.0, q).astype(bf16)
            for half in range(2):
                base = pl.multiple_of(2 * dst + half * chunk, chunk)
                qs[pl.ds(base, nk), :] = q0[half * nk:(half + 1) * nk]
                qs[pl.ds(base + nk, nk), :] = q1[half * nk:(half + 1) * nk]
            ks[pl.ds(nk + dst, chunk), :] = k_ref[src, :].astype(bf16)
            vs[pl.ds(nk + dst, chunk), 0:LANES] = v_ref[src, :].astype(bf16)
            return carry

        lax.fori_loop(0, seq // chunk, gather, 0)

        def scores(b, carry):
            first = jnp.asarray(b % nb == 0, jnp.int32)
            rows = pl.ds(pl.multiple_of(b * chunk, chunk), chunk)
            k = ks[pl.ds(pl.multiple_of(b * nk, nk), chunk), :]
            s_buf[rows, :] = lax.dot_general(qs[rows, :], k, (((1,), (1,)), ((), ())),
                                             preferred_element_type=f32) + bias_ref[p, first]
            return carry

        def rowmax(b, carry):
            rows = pl.ds(pl.multiple_of(b * chunk, chunk), chunk)
            m_buf[rows, :] = jnp.broadcast_to(jnp.max(s_buf[rows, :], axis=-1, keepdims=True),
                                              (chunk, LANES))
            return carry

        def block(b, carry):
            r = b // nb
            i = b % nb
            rows = pl.ds(pl.multiple_of(b * chunk, chunk), chunk)
            v = vs[pl.ds(pl.multiple_of(b * nk, nk), chunk), :]
            m = m_buf[rows, :]
            e = jnp.exp2(s_buf[rows, :] - jnp.concatenate([m, m], axis=1))
            pv = jnp.dot(e.astype(bf16), v, preferred_element_type=f32)
            m_blk = jnp.where(head0, m[0:nk], m[nk:chunk])
            l_blk = jnp.where(head0, pv[0:nk, LANES:], pv[nk:chunk, LANES:])
            pv_blk = jnp.where(head0, pv[0:nk, 0:LANES], pv[nk:chunk, 0:LANES])
            if d == 1:
                tok = pl.ds(pl.multiple_of(b * nk, nk), nk)
            else:
                tok = pl.ds(r + d * nk * i, nk, stride=d)
            if is_first:
                m_s[tok, :] = m_blk
                l_s[tok, :] = l_blk
                acc_s[tok, :] = pv_blk
                return carry
            m_old = m_s[tok, :]
            m_new = jnp.maximum(m_old, m_blk)
            w_old = jnp.exp2(m_old - m_new)
            w_blk = jnp.exp2(m_blk - m_new)
            l_new = w_old * l_s[tok, :] + w_blk * l_blk
            acc_new = w_old * acc_s[tok, :] + w_blk * pv_blk
            if is_last:
                o_ref[tok, :] = (acc_new / l_new).astype(o_ref.dtype)
            else:
                l_s[tok, :] = l_new
                acc_s[tok, :] = acc_new
                m_s[tok, :] = m_new
            return carry

        n_groups = n_blocks // ATTN_UNROLL
        assert n_groups >= 3

        def group(fn, g):
            for j in range(ATTN_UNROLL):
                fn(g * ATTN_UNROLL + j, 0)

        def pipelined(g, carry):
            group(block, g)
            group(rowmax, g + 1)
            group(scores, g + 2)
            return carry

        group(scores, 0)
        group(scores, 1)
        group(rowmax, 0)
        lax.fori_loop(0, n_groups - 2, pipelined, 0)
        group(block, n_groups - 2)
        group(rowmax, n_groups - 1)
        group(block, n_groups - 1)


def _attention(qkv, band_bias, batch, seq):
    view = qkv.reshape(batch, seq, 3 * D_ATTN)
    pairs = D_ATTN // LANES

    def spec(offset):
        return pl.BlockSpec((None, seq, LANES), lambda b, hp: (b, 0, offset * pairs + hp))

    n_pat = len(DILATED_PATTERNS)
    o = pl.pallas_call(
        functools.partial(_attn_kernel, seq=seq),
        grid=(batch, pairs),
        in_specs=[spec(0), spec(1), spec(2),
                  pl.BlockSpec((n_pat, 2, None, 2 * ATTN_BLOCK, 2 * ATTN_BLOCK),
                               lambda b, hp: (0, 0, hp, 0, 0))],
        out_specs=pl.BlockSpec((None, seq, LANES), lambda b, hp: (b, 0, hp)),
        out_shape=jax.ShapeDtypeStruct((batch, seq, D_ATTN), bf16),
        scratch_shapes=[pltpu.VMEM((2 * seq, LANES), bf16),
                        pltpu.VMEM((seq + ATTN_BLOCK, LANES), bf16),
                        pltpu.VMEM((seq + ATTN_BLOCK, 2 * LANES), bf16),
                        pltpu.VMEM((2 * seq, 2 * ATTN_BLOCK), f32),
                        pltpu.VMEM((2 * seq, LANES), f32),
                        pltpu.VMEM((seq, LANES), f32), pltpu.VMEM((seq, LANES), f32),
                        pltpu.VMEM((seq, LANES), f32)],
        compiler_params=_params("parallel", "parallel"),
        name="attention",
    )(view, view, view, band_bias)
    return o.reshape(batch * seq, D_ATTN)


def _gelu_tanh(x):
    return 0.5 * x * (1.0 + jnp.tanh(np.sqrt(2.0 / np.pi) * (x + 0.044715 * x * x * x)))


def _lru_kernel(ug_ref, convw_ref, convb_ref, wg_ref, ba_ref, bx_ref, lam_ref, g_ref,
                y_ref, ubuf, hcarry, *, ts):
    pad = SUBLANES
    t = pl.program_id(1)

    @pl.when(t == 0)
    def _():
        ubuf[0:pad, :] = jnp.zeros((pad, D_LRU), f32)
        hcarry[...] = jnp.zeros_like(hcarry)

    @pl.when(t > 0)
    def _():
        ubuf[0:pad, :] = ubuf[ts:ts + pad, :]

    ubuf[pad:pad + ts, :] = ug_ref[:, 0:D_LRU]
    gate = ug_ref[:, D_LRU:2 * D_LRU]

    u = convb_ref[...] + convw_ref[CONV_WIDTH - 1:CONV_WIDTH, :] * ubuf[pad:pad + ts, :]
    for w in range(CONV_WIDTH - 1):
        back = CONV_WIDTH - 1 - w
        u = u + convw_ref[w:w + 1, :] * ubuf[pad - back:pad - back + ts, :]

    gates = jnp.dot(u.astype(bf16), wg_ref[...], preferred_element_type=f32)
    r = jax.nn.sigmoid(gates[:, 0:D_LRU] + ba_ref[...])
    i = jax.nn.sigmoid(gates[:, D_LRU:2 * D_LRU] + bx_ref[...])
    neg_lam = -lam_ref[...]
    softplus = jnp.maximum(neg_lam, 0.0) + jnp.log1p(jnp.exp(-jnp.abs(neg_lam)))
    log_a = (-LRU_C) * r * softplus
    a = jnp.exp(log_a)
    b = jnp.sqrt(-jnp.tanh(log_a) * (1.0 + a * a)) * (i * u)

    groups = ts // SUBLANES
    a = a.reshape(groups, SUBLANES, D_LRU)
    b = b.reshape(groups, SUBLANES, D_LRU)
    sub = lax.broadcasted_iota(jnp.int32, (1, SUBLANES, 1), 1)
    shift = 1
    while shift < SUBLANES:
        live = sub >= shift
        a_prev = jnp.where(live, pltpu.roll(a, shift, 1), 1.0)
        b_prev = jnp.where(live, pltpu.roll(b, shift, 1), 0.0)
        b = a * b_prev + b
        a = a * a_prev
        shift *= 2
    h_prev = hcarry[...]
    h_groups = []
    for g in range(groups):
        h_g = a[g] * h_prev + b[g]
        h_prev = h_g[SUBLANES - 1:SUBLANES, :]
        h_groups.append(h_g)
    hcarry[...] = h_prev
    h = jnp.concatenate(h_groups, axis=0)

    y = _gelu_tanh(gate) * h
    y = y * lax.rsqrt(jnp.mean(y * y, axis=-1, keepdims=True) + RMS_EPS) * g_ref[...]
    y_ref[...] = y.astype(y_ref.dtype)


def _block_diag(w):
    g, i, j = w.shape
    eye = jnp.eye(g, dtype=w.dtype)
    return jnp.einsum('gij,gh->gihj', w, eye).reshape(g * i, g * j)


def _lru_branch(ug, conv_w, conv_b, w_a, b_a, w_x, b_x, lam, g_lru, batch, seq, ts=512):
    wg = jnp.concatenate([_block_diag(w_a), _block_diag(w_x)], axis=1).astype(bf16)
    row = lambda v: v.reshape(1, D_LRU).astype(f32)
    const = lambda shape: pl.BlockSpec(shape, lambda b, t: (0, 0))
    view = ug.reshape(batch, seq, 2 * D_LRU)
    y = pl.pallas_call(
        functools.partial(_lru_kernel, ts=ts),
        grid=(batch, seq // ts),
        in_specs=[pl.BlockSpec((None, ts, 2 * D_LRU), lambda b, t: (b, t, 0)),
                  const((CONV_WIDTH, D_LRU)), const((1, D_LRU)),
                  const((D_LRU, 2 * D_LRU)), const((1, D_LRU)), const((1, D_LRU)),
                  const((1, D_LRU)), const((1, D_LRU))],
        out_specs=pl.BlockSpec((None, ts, D_LRU), lambda b, t: (b, t, 0)),
        out_shape=jax.ShapeDtypeStruct((batch, seq, D_LRU), bf16),
        scratch_shapes=[pltpu.VMEM((ts + 16, D_LRU), f32), pltpu.VMEM((1, D_LRU), f32)],
        compiler_params=_params("parallel", "arbitrary"),
        name="rglru",
    )(view, conv_w.reshape(CONV_WIDTH, D_LRU).astype(f32), row(conv_b), wg, row(b_a), row(b_x),
      row(lam), row(g_lru))
    return y.reshape(batch * seq, D_LRU)


ROW_TILE = D_MODEL // LANES


def _store_row_tiles(ref, rows):
    t = rows.shape[0]
    for s in range(ROW_TILE):
        ref[pl.ds(s, t, stride=ROW_TILE), :] = rows[:, s * LANES:(s + 1) * LANES]


def _load_row_tiles(ref, t):
    return jnp.concatenate([ref[pl.ds(s, t, stride=ROW_TILE), :] for s in range(ROW_TILE)],
                           axis=-1)


def _layer_norm(z, g, b):
    mu = jnp.mean(z, axis=-1, keepdims=True)
    zc = z - mu
    var = jnp.mean(zc * zc, axis=-1, keepdims=True)
    return zc * lax.rsqrt(var + LN_EPS) * g + b


def _mix_out_kernel(attn_ref, ylru, x_ref, w_ref, gattn, lng, lnb, *rest, with_router):
    if with_router:
        rw_ref, tri_ref, x1_ref, route_ref, count_ref = rest
    else:
        (x1_ref,) = rest
    attn = attn_ref[...].astype(f32)
    attn = attn * lax.rsqrt(jnp.mean(attn * attn, axis=-1, keepdims=True) + RMS_EPS) * gattn[...]
    y = jnp.dot(attn.astype(bf16), w_ref[0:D_ATTN, :], preferred_element_type=f32)
    y = y + jnp.dot(ylru[...], w_ref[D_ATTN:, :], preferred_element_type=f32)
    x1 = _layer_norm(DEEPNORM_ALPHA * x_ref[...] + y, lng[...], lnb[...])
    x1_ref[...] = x1
    if with_router:
        lane = lax.broadcasted_iota(jnp.int32, (1, LANES), 1).astype(f32)
        x_hi = x1.astype(bf16)
        x_lo = (x1 - x_hi.astype(f32)).astype(bf16)
        logits = (jnp.dot(x_hi, rw_ref[0], preferred_element_type=f32)
                  + jnp.dot(x_lo, rw_ref[0], preferred_element_type=f32)
                  + jnp.dot(x_hi, rw_ref[1], preferred_element_type=f32))
        logits = jnp.where(lane < N_EXPERTS, logits, -jnp.inf)
        v1 = jnp.max(logits, axis=-1, keepdims=True)
        i1 = jnp.min(jnp.where(logits == v1, lane, float(LANES)), axis=-1, keepdims=True)
        rest_logits = jnp.where(lane == i1, -jnp.inf, logits)
        v2 = jnp.max(rest_logits, axis=-1, keepdims=True)
        i2 = jnp.min(jnp.where(rest_logits == v2, lane, float(LANES)), axis=-1, keepdims=True)
        e2 = jnp.exp(v2 - v1)
        p1 = 1.0 / (1.0 + e2)
        p2 = e2 / (1.0 + e2)

        @pl.when(pl.program_id(0) == 0)
        def _():
            count_ref[...] = jnp.zeros_like(count_ref)

        tm = x1.shape[0]
        chosen = jnp.logical_or(lane == i1, lane == i2)
        rank = count_ref[...] + jnp.dot(tri_ref[...], chosen.astype(bf16),
                                        preferred_element_type=f32)
        count_ref[...] += jnp.sum(chosen.astype(f32), axis=0, keepdims=True)
        r1 = jnp.sum(jnp.where(lane == i1, rank, 0.0), axis=-1, keepdims=True)
        r2 = jnp.sum(jnp.where(lane == i2, rank, 0.0), axis=-1, keepdims=True)
        fields = (i1, i2, r1, r2, p1, p2)
        route = jnp.zeros((tm, LANES), f32)
        for k, val in enumerate(fields):
            route = jnp.where(lane == k, val, route)
        route_ref[...] = route


def _mix_out(attn, ylru, x2d, w_out_bf16, g_attn, ln_g, ln_b, router_w=None):
    n = x2d.shape[0]
    with_router = router_w is not None
    tm = 512 if with_router else 1024
    tile = lambda width: pl.BlockSpec((tm, width), lambda i: (i, 0))
    const = lambda shape: pl.BlockSpec(shape, lambda i: (0, 0))
    in_specs = [tile(D_ATTN), tile(D_LRU), tile(D_MODEL), const((D_MODEL, D_MODEL)),
                const((1, D_ATTN)), const((1, D_MODEL)), const((1, D_MODEL))]
    args = [attn, ylru, x2d, w_out_bf16, g_attn.reshape(1, D_ATTN).astype(f32),
            ln_g.reshape(1, D_MODEL).astype(f32), ln_b.reshape(1, D_MODEL).astype(f32)]
    out_specs = [tile(D_MODEL)]
    out_shape = [jax.ShapeDtypeStruct((n, D_MODEL), f32)]
    if with_router:
        rw = jnp.zeros((D_MODEL, LANES), f32).at[:, :N_EXPERTS].set(router_w.astype(f32))
        rw_hi = rw.astype(bf16)
        rw_lo = (rw - rw_hi.astype(f32)).astype(bf16)
        strictly_lower = jnp.asarray(np.tri(tm, k=-1), bf16)
        in_specs += [pl.BlockSpec((2, D_MODEL, LANES), lambda i: (0, 0, 0)), const((tm, tm))]
        args += [jnp.stack([rw_hi, rw_lo]), strictly_lower]
        out_specs += [tile(LANES), const((1, LANES))]
        out_shape += [jax.ShapeDtypeStruct((n, LANES), f32), jax.ShapeDtypeStruct((1, LANES), f32)]
    res = pl.pallas_call(
        functools.partial(_mix_out_kernel, with_router=with_router),
        grid=(n // tm,),
        in_specs=in_specs,
        out_specs=out_specs,
        out_shape=out_shape,
        compiler_params=_params("arbitrary" if with_router else "parallel"),
        name="mix_out_router" if with_router else "mix_out",
    )(*args)
    return res if with_router else (res[0], None, None)


def _ffn_kernel(tile_expert_ref, n_used_ref, x_ref, wg_ref, wu_ref, wd_ref, *rest,
                n_chunks, fuse_ln):
    if fuse_ln:
        lng, lnb, out_ref, acc_ref, xb_ref = rest
    else:
        out_ref, acc_ref, xb_ref = rest
    del tile_expert_ref
    j = pl.program_id(1)
    used = pl.program_id(0) < n_used_ref[0]

    @pl.when(j == 0)
    def _():
        acc_ref[...] = jnp.zeros_like(acc_ref)

    @pl.when(used & (j == 0))
    def _():
        if fuse_ln:
            xb_ref[...] = x_ref[...].astype(bf16)
        else:
            xb_ref[...] = _load_row_tiles(x_ref, xb_ref.shape[0]).astype(bf16)

    @pl.when(used)
    def _():
        xb = xb_ref[...]
        g = jnp.dot(xb, wg_ref[...], preferred_element_type=f32)
        u = jnp.dot(xb, wu_ref[...], preferred_element_type=f32)
        h = (g * jax.nn.sigmoid(g)) * u
        acc_ref[...] += jnp.dot(h.astype(bf16), wd_ref[...], preferred_element_type=f32)

    @pl.when(j == n_chunks - 1)
    def _():
        if fuse_ln:
            out_ref[...] = _layer_norm(DEEPNORM_ALPHA * x_ref[...] + acc_ref[...],
                                       lng[...], lnb[...])
        else:
            _store_row_tiles(out_ref, acc_ref[...])


def _ffn(x2d, tile_expert, n_used, w_gate, w_up, w_down, ln=None, *, tm, tf, name):
    fuse_ln = ln is not None
    rows = x2d.shape[0] if fuse_ln else x2d.shape[0] // ROW_TILE
    d_ff = w_gate.shape[2]
    n_chunks = d_ff // tf
    io_block = (tm, D_MODEL) if fuse_ln else (tm * ROW_TILE, LANES)

    def chunk(i, j, nu):
        return jnp.where(i < nu[0], j, n_chunks - 1)

    resident = w_gate.shape[0] == 1 and n_chunks == 1
    mode = dict(pipeline_mode=pl.Buffered(1)) if resident else {}
    in_specs = [pl.BlockSpec(io_block, lambda i, j, te, nu: (i, 0)),
                pl.BlockSpec((None, D_MODEL, tf),
                             lambda i, j, te, nu: (te[i], 0, chunk(i, j, nu)), **mode),
                pl.BlockSpec((None, D_MODEL, tf),
                             lambda i, j, te, nu: (te[i], 0, chunk(i, j, nu)), **mode),
                pl.BlockSpec((None, tf, D_MODEL),
                             lambda i, j, te, nu: (te[i], chunk(i, j, nu), 0), **mode)]
    args = [x2d, w_gate, w_up, w_down]
    if fuse_ln:
        in_specs += [pl.BlockSpec((1, D_MODEL), lambda i, j, te, nu: (0, 0))] * 2
        args += [v.reshape(1, D_MODEL).astype(f32) for v in ln]
    return pl.pallas_call(
        functools.partial(_ffn_kernel, n_chunks=n_chunks, fuse_ln=fuse_ln),
        grid_spec=pltpu.PrefetchScalarGridSpec(
            num_scalar_prefetch=2,
            grid=(rows // tm, n_chunks),
            in_specs=in_specs,
            out_specs=pl.BlockSpec(io_block, lambda i, j, te, nu: (i, 0)),
            scratch_shapes=[pltpu.VMEM((tm, D_MODEL), f32), pltpu.VMEM((tm, D_MODEL), bf16)]),
        out_shape=jax.ShapeDtypeStruct(x2d.shape, f32),
        compiler_params=_params("parallel", "arbitrary"),
        name=name,
    )(tile_expert, n_used, *args)


def _dispatch_kernel(fill_ref, pos_ref, x_ref, xs_hbm, zero_buf, rows_buf, sem, *, tt, tm):
    step = pl.program_id(0)

    n_fill = fill_ref.shape[0] // 2

    @pl.when(step == 0)
    def _():
        zero_buf[...] = jnp.zeros_like(zero_buf)

        def fill_copy(f):
            start = pl.multiple_of(fill_ref[f] * ROW_TILE, ROW_TILE)
            return pltpu.make_async_copy(zero_buf, xs_hbm.at[pl.ds(start, tm * ROW_TILE)], sem)

        for f in range(n_fill):
            @pl.when(fill_ref[n_fill + f] > 0)
            def _():
                fill_copy(f).start()
        for f in range(n_fill):
            @pl.when(fill_ref[n_fill + f] > 0)
            def _():
                fill_copy(f).wait()

    _store_row_tiles(rows_buf, x_ref[...])

    def issue(t, carry):
        src = pl.multiple_of(t * ROW_TILE, ROW_TILE)
        for k in range(TOP_K):
            dst = pl.multiple_of(pos_ref[0, TOP_K * t + k] * ROW_TILE, ROW_TILE)
            pltpu.make_async_copy(rows_buf.at[pl.ds(src, ROW_TILE)],
                                  xs_hbm.at[pl.ds(dst, ROW_TILE)], sem).start(priority=k)
        return carry

    lax.fori_loop(0, tt, issue, 0, unroll=DMA_UNROLL)
    for _ in range(TOP_K):
        pltpu.make_async_copy(rows_buf, xs_hbm.at[pl.ds(0, tt * ROW_TILE)], sem).wait()


def _dispatch(x1, pos_blocks, fill, rows_sorted, *, tt, tm):
    n = x1.shape[0]
    return pl.pallas_call(
        functools.partial(_dispatch_kernel, tt=tt, tm=tm),
        grid_spec=pltpu.PrefetchScalarGridSpec(
            num_scalar_prefetch=1,
            grid=(n // tt,),
            in_specs=[pl.BlockSpec((None, 1, TOP_K * tt), lambda i, fill: (i, 0, 0),
                                   memory_space=pltpu.SMEM),
                      pl.BlockSpec((tt, D_MODEL), lambda i, fill: (i, 0))],
            out_specs=pl.BlockSpec(memory_space=pl.ANY),
            scratch_shapes=[pltpu.VMEM((tm * ROW_TILE, LANES), f32),
                            pltpu.VMEM((tt * ROW_TILE, LANES), f32),
                            pltpu.SemaphoreType.DMA(())]),
        out_shape=jax.ShapeDtypeStruct((rows_sorted * ROW_TILE, LANES), f32),
        compiler_params=_params("arbitrary"),
        name="moe_dispatch",
    )(fill, pos_blocks, x1)


def _combine_kernel(pos_ref, pos_next_ref, route_ref, x1_ref, ys_hbm, lng, lnb, out_ref, buf, sems,
                    *, tt, n_steps):
    step = pl.program_id(0)
    slot = step % 2

    def issue_tile(tile_pos_ref, into):
        def issue(t, carry):
            dst = pl.multiple_of(t * ROW_TILE, ROW_TILE)
            for k in range(TOP_K):
                src = pl.multiple_of(tile_pos_ref[0, TOP_K * t + k] * ROW_TILE, ROW_TILE)
                pltpu.make_async_copy(ys_hbm.at[pl.ds(src, ROW_TILE)],
                                      buf.at[into, k, pl.ds(dst, ROW_TILE)],
                                      sems.at[into]).start(priority=k)
            return carry

        lax.fori_loop(0, tt, issue, 0, unroll=DMA_UNROLL)

    @pl.when(step == 0)
    def _():
        issue_tile(pos_ref, 0)

    @pl.when(step + 1 < n_steps)
    def _():
        issue_tile(pos_next_ref, 1 - slot)

    for k in range(TOP_K):
        pltpu.make_async_copy(ys_hbm.at[pl.ds(0, tt * ROW_TILE)], buf.at[slot, k],
                              sems.at[slot]).wait()

    lane = lax.broadcasted_iota(jnp.int32, (1, LANES), 1)
    route = route_ref[...]
    p1 = jnp.sum(jnp.where(lane == 4, route, 0.0), axis=-1, keepdims=True)
    p2 = jnp.sum(jnp.where(lane == 5, route, 0.0), axis=-1, keepdims=True)
    y = p1 * _load_row_tiles(buf.at[slot, 0], tt) + p2 * _load_row_tiles(buf.at[slot, 1], tt)
    out_ref[...] = _layer_norm(DEEPNORM_ALPHA * x1_ref[...] + y, lng[...], lnb[...])


def _combine(pos_blocks, route, x1, y_sorted, ln_g, ln_b, *, tt):
    n = x1.shape[0]
    n_steps = n // tt
    pos_spec = lambda index: pl.BlockSpec((None, 1, TOP_K * tt), index, memory_space=pltpu.SMEM)
    return pl.pallas_call(
        functools.partial(_combine_kernel, tt=tt, n_steps=n_steps),
        grid=(n_steps,),
        in_specs=[pos_spec(lambda i: (i, 0, 0)),
                  pos_spec(lambda i: (jnp.minimum(i + 1, n_steps - 1), 0, 0)),
                  pl.BlockSpec((tt, LANES), lambda i: (i, 0)),
                  pl.BlockSpec((tt, D_MODEL), lambda i: (i, 0)),
                  pl.BlockSpec(memory_space=pl.ANY),
                  pl.BlockSpec((1, D_MODEL), lambda i: (0, 0)),
                  pl.BlockSpec((1, D_MODEL), lambda i: (0, 0))],
        out_specs=pl.BlockSpec((tt, D_MODEL), lambda i: (i, 0)),
        out_shape=jax.ShapeDtypeStruct((n, D_MODEL), f32),
        scratch_shapes=[pltpu.VMEM((2, TOP_K, tt * ROW_TILE, LANES), f32),
                        pltpu.SemaphoreType.DMA((2,))],
        compiler_params=_params("arbitrary"),
        name="moe_combine",
    )(pos_blocks, pos_blocks, route, x1, y_sorted, ln_g.reshape(1, D_MODEL).astype(f32),
      ln_b.reshape(1, D_MODEL).astype(f32))


def _moe(x1, route, counts, w_gate, w_up, w_down, ln_g, ln_b, *, tm, tf, tt):
    n = x1.shape[0]
    i32 = jnp.int32
    counts = counts[0, :N_EXPERTS].astype(i32)
    padded = (counts + tm - 1) // tm * tm
    ends = jnp.cumsum(padded)
    offsets = ends - padded
    experts = route[:, 0:TOP_K].astype(i32)
    ranks = route[:, TOP_K:2 * TOP_K].astype(i32)
    pos = (offsets[experts] + ranks).reshape(n // tt, 1, TOP_K * tt)
    n_tiles = TOP_K * n // tm + N_EXPERTS
    n_used = (ends[-1] // tm).astype(i32).reshape(1)
    tile_ids = jnp.arange(n_tiles, dtype=i32)
    tile_expert = jnp.sum((tile_ids[:, None] >= (ends // tm)[None, :]).astype(i32), axis=1)
    last_expert = jnp.max(jnp.where(counts > 0, jnp.arange(N_EXPERTS, dtype=i32), 0))
    tile_expert = jnp.minimum(tile_expert, last_expert).astype(i32)
    tail_tiles = n_used[0] + jnp.arange(N_EXPERTS, dtype=i32)
    fill = jnp.concatenate([ends - tm, tail_tiles * tm,
                            (counts > 0).astype(i32), (tail_tiles < n_tiles).astype(i32)]).astype(i32)

    x_sorted = _dispatch(x1, pos, fill, n_tiles * tm, tt=tt, tm=tm)
    y_sorted = _ffn(x_sorted, tile_expert, n_used, w_gate, w_up, w_down, tm=tm, tf=tf,
                    name="ffn_experts")
    return _combine(pos, route, x1, y_sorted, ln_g, ln_b, tt=tt)


def kernel(x, w_in, conv_w, conv_b, w_a, b_a, w_x, b_x, lru_lambda, rel_bias, g_attn, g_lru, w_out, ln1_g, ln1_b, ln2_g, ln2_b, ffn_w_gate, ffn_w_up, ffn_w_down, router_w, moe_w_gate, moe_w_up, moe_w_down):
    batch, seq, _ = x.shape
    n = batch * seq
    h = x.reshape(n, D_MODEL).astype(f32)
    band_bias = jnp.stack([_band_bias(rel_bias, d) for _, d in DILATED_PATTERNS])
    for layer in range(DEPTH):
        qkv, ug = _in_proj(h, w_in[layer].astype(bf16))
        attn = _attention(qkv, band_bias, batch, seq)
        ylru = _lru_branch(ug, conv_w[layer], conv_b[layer], w_a[layer], b_a[layer], w_x[layer],
                           b_x[layer], lru_lambda[layer], g_lru[layer], batch, seq)
        j = layer // 2
        dense = layer % 2 == 0
        x1, route, counts = _mix_out(
            attn, ylru, h, w_out[layer].astype(bf16), g_attn[layer], ln1_g[layer],
            ln1_b[layer], None if dense else router_w[j])
        if dense:
            tm = 512
            h = _ffn(x1, jnp.zeros((n // tm,), jnp.int32), jnp.full((1,), n // tm, jnp.int32),
                     ffn_w_gate[j][None].astype(bf16), ffn_w_up[j][None].astype(bf16),
                     ffn_w_down[j][None].astype(bf16), (ln2_g[layer], ln2_b[layer]),
                     tm=tm, tf=ffn_w_gate.shape[2], name="ffn_dense")
        else:
            h = _moe(x1, route, counts, moe_w_gate[j].astype(bf16), moe_w_up[j].astype(bf16),
                     moe_w_down[j].astype(bf16), ln2_g[layer], ln2_b[layer],
                     tm=512, tf=1792, tt=512)
    return h.reshape(batch, seq, D_MODEL).astype(x.dtype)
```

```python
import functools

import numpy as np
import jax
import jax.numpy as jnp
from jax import lax
from jax.experimental import pallas as pl
from jax.experimental.pallas import tpu as pltpu

D_MODEL = 1024
N_HEADS = 8
HEAD_DIM = 64
D_ATTN = N_HEADS * HEAD_DIM
D_LRU = 512
N_LRU_BLOCKS = 8
LRU_BLOCK = D_LRU // N_LRU_BLOCKS
CONV_WIDTH = 4
LRU_C = 8.0
DILATED_PATTERNS = ((128, 1), (512, 4), (2048, 16))
ATTN_BLOCK = 128
STAGE_DILATION = 4
N_BUCKETS = 32
MAX_DISTANCE = 2048
D_IN = 3 * D_ATTN + 2 * D_LRU
N_EXPERTS = 8
TOP_K = 2
DEPTH = 2
DEEPNORM_ALPHA = (2.0 * DEPTH) ** 0.25
LN_EPS = 1e-5
RMS_EPS = 1e-6
NEG_INF = -1e30
LOG2_E = float(np.log2(np.e))

LANES = 128
SUBLANES = 8
VMEM_LIMIT_BYTES = 56 * 1024 * 1024
MXU_DEPTH = 256

IN_PROJ_ROWS = 1024
LRU_ROWS = 512
MIX_ROWS = 1024
ROUTER_ROWS = 512
DENSE_FFN_ROWS = 512
EXPERT_ROWS = 512
EXPERT_FF_CHUNK = 7 * MXU_DEPTH
ROUTE_ROWS = 512
DMA_UNROLL = 8
ATTN_UNROLL = 4

f32 = jnp.float32
bf16 = jnp.bfloat16


def _params(*semantics):
    return pltpu.CompilerParams(dimension_semantics=semantics,
                                vmem_limit_bytes=VMEM_LIMIT_BYTES)


def _in_proj_kernel(x_ref, w_ref, qkv_ref, ug_ref):
    xb = x_ref[...].astype(bf16)
    n_qkv = qkv_ref.shape[1]
    step = 2 * MXU_DEPTH
    for c in range(0, D_IN, step):
        p = jnp.dot(xb, w_ref[:, c:c + step], preferred_element_type=f32)
        if c < n_qkv:
            qkv_ref[:, c:c + step] = p
        else:
            ug_ref[:, c - n_qkv:c - n_qkv + step] = p


def _in_proj(x2d, w_in_bf16, tm=IN_PROJ_ROWS):
    n = x2d.shape[0]
    return pl.pallas_call(
        _in_proj_kernel,
        grid=(n // tm,),
        in_specs=[pl.BlockSpec((tm, D_MODEL), lambda i: (i, 0)),
                  pl.BlockSpec((D_MODEL, D_IN), lambda i: (0, 0))],
        out_specs=[pl.BlockSpec((tm, 3 * D_ATTN), lambda i: (i, 0)),
                   pl.BlockSpec((tm, 2 * D_LRU), lambda i: (i, 0))],
        out_shape=[jax.ShapeDtypeStruct((n, 3 * D_ATTN), f32),
                   jax.ShapeDtypeStruct((n, 2 * D_LRU), f32)],
        compiler_params=_params("parallel"),
        name="in_proj",
    )(x2d, w_in_bf16)


def _t5_bucket(dist):
    max_exact = N_BUCKETS // 2
    d = np.maximum(dist, 1).astype(np.float32)
    large = max_exact + (np.log(d / max_exact) / np.log(MAX_DISTANCE / max_exact)
                         * (N_BUCKETS - max_exact)).astype(np.int32)
    large = np.minimum(large, N_BUCKETS - 1)
    return np.where(dist < max_exact, dist, large).astype(np.int32)


def _band_bias(rel_bias, dilation):
    nk = ATTN_BLOCK
    qi = np.arange(nk)[:, None]
    kj = np.arange(2 * nk)[None, :]
    delta = qi + nk - kj
    band = (delta >= 0) & (delta <= nk)
    bucket = _t5_bucket(np.clip(delta, 0, nk) * dilation)
    onehot = np.eye(N_BUCKETS, dtype=np.float32)[bucket.reshape(-1)]
    bias = jnp.dot(jnp.asarray(onehot), rel_bias.astype(f32), precision=lax.Precision.HIGHEST)
    bias = jnp.transpose(bias.reshape(nk, 2 * nk, N_HEADS), (2, 0, 1))
    valid = np.stack([band, band & (kj >= nk)])[:, None]
    bias = jnp.where(jnp.asarray(valid), bias[None], NEG_INF)
    return (bias * LOG2_E).reshape(2, N_HEADS // 2, 2 * nk, 2 * nk)


def _attn_kernel(q_ref, k_ref, v_ref, bias_ref, o_ref, q4, k4, v4, qs, ks, vs, s_buf, m_buf,
                 m_s, l_s, acc_s, *, seq):
    nk = ATTN_BLOCK
    n_blocks = seq // nk
    chunk = 2 * nk
    lane = lax.broadcasted_iota(jnp.int32, (1, LANES), 1)
    head0 = lane < HEAD_DIM

    ks[0:nk, :] = jnp.zeros((nk, LANES), bf16)
    vs[0:nk, 0:LANES] = jnp.zeros((nk, LANES), bf16)
    vs[:, LANES:2 * LANES] = jnp.ones((seq + nk, LANES), bf16)

    def stage(t, carry):
        chunks_per_residue = seq // STAGE_DILATION // chunk
        src = pl.ds(t // chunks_per_residue + STAGE_DILATION * chunk * (t % chunks_per_residue),
                    chunk, stride=STAGE_DILATION)
        dst = pl.ds(pl.multiple_of(t * chunk, chunk), chunk)
        q4[dst, :] = q_ref[src, :]
        k4[dst, :] = k_ref[src, :]
        v4[dst, :] = v_ref[src, :]
        return carry

    lax.fori_loop(0, seq // chunk, stage, 0)

    order = sorted(range(len(DILATED_PATTERNS)), key=lambda p: -DILATED_PATTERNS[p][1])
    assert DILATED_PATTERNS[order[-1]][1] == 1
    for p in order:
        d = DILATED_PATTERNS[p][1]
        is_first, is_last = p == order[0], p == order[-1]
        length = seq // d
        nb = length // nk
        chunks_per_residue = length // chunk

        def gather(t, carry):
            r = t // chunks_per_residue
            c = t % chunks_per_residue
            dst = pl.multiple_of(t * chunk, chunk)
            if d % STAGE_DILATION == 0:
                sub = d // STAGE_DILATION
                start = ((r % STAGE_DILATION) * (seq // STAGE_DILATION) + r // STAGE_DILATION
                         + sub * chunk * c)
                src = (pl.ds(start, chunk, stride=sub) if sub > 1
                       else pl.ds(pl.multiple_of(start, chunk), chunk))
                q, k, v = q4[src, :], k4[src, :], v4[src, :]
            else:
                src = pl.ds(r + d * chunk * c, chunk, stride=d) if d > 1 else pl.ds(dst, chunk)
                q, k, v = q_ref[src, :], k_ref[src, :], v_ref[src, :]
            q = q * (HEAD_DIM ** -0.5 * LOG2_E)
            q0 = jnp.where(head0, q, 0.0).astype(bf16)
            q1 = jnp.where(head0, 0.0, q).astype(bf16)
            for half in range(2):
                base = pl.multiple_of(2 * dst + half * chunk, chunk)
                qs[pl.ds(base, nk), :] = q0[half * nk:(half + 1) * nk]
                qs[pl.ds(base + nk, nk), :] = q1[half * nk:(half + 1) * nk]
            ks[pl.ds(nk + dst, chunk), :] = k.astype(bf16)
            vs[pl.ds(nk + dst, chunk), 0:LANES] = v.astype(bf16)
            return carry

        lax.fori_loop(0, seq // chunk, gather, 0)

        def scores(b, carry):
            first = jnp.asarray(b % nb == 0, jnp.int32)
            rows = pl.ds(pl.multiple_of(b * chunk, chunk), chunk)
            k = ks[pl.ds(pl.multiple_of(b * nk, nk), chunk), :]
            s_buf[rows, :] = lax.dot_general(qs[rows, :], k, (((1,), (1,)), ((), ())),
                                             preferred_element_type=f32) + bias_ref[p, first]
            return carry

        def rowmax(b, carry):
            rows = pl.ds(pl.multiple_of(b * chunk, chunk), chunk)
            m_buf[rows, :] = jnp.broadcast_to(jnp.max(s_buf[rows, :], axis=-1, keepdims=True),
                                              (chunk, LANES))
            return carry

        def block(b, carry):
            r = b // nb
            i = b % nb
            rows = pl.ds(pl.multiple_of(b * chunk, chunk), chunk)
            v = vs[pl.ds(pl.multiple_of(b * nk, nk), chunk), :]
            m = m_buf[rows, :]
            e = jnp.exp2(s_buf[rows, :] - jnp.concatenate([m, m], axis=1))
            pv = jnp.dot(e.astype(bf16), v, preferred_element_type=f32)
            m_blk = jnp.where(head0, m[0:nk], m[nk:chunk])
            l_blk = jnp.where(head0, pv[0:nk, LANES:], pv[nk:chunk, LANES:])
            pv_blk = jnp.where(head0, pv[0:nk, 0:LANES], pv[nk:chunk, 0:LANES])
            if d == 1:
                tok = pl.ds(pl.multiple_of(b * nk, nk), nk)
            else:
                tok = pl.ds(r + d * nk * i, nk, stride=d)
            if is_first:
                m_s[tok, :] = m_blk
                l_s[tok, :] = l_blk
                acc_s[tok, :] = pv_blk
                return carry
            m_old = m_s[tok, :]
            m_new = jnp.maximum(m_old, m_blk)
            w_old = jnp.exp2(m_old - m_new)
            w_blk = jnp.exp2(m_blk - m_new)
            l_new = w_old * l_s[tok, :] + w_blk * l_blk
            acc_new = w_old * acc_s[tok, :] + w_blk * pv_blk
            if is_last:
                o_ref[tok, :] = (acc_new / l_new).astype(o_ref.dtype)
            else:
                l_s[tok, :] = l_new
                acc_s[tok, :] = acc_new
                m_s[tok, :] = m_new
            return carry

        n_groups = n_blocks // ATTN_UNROLL
        assert n_groups >= 3

        def group(fn, g):
            for j in range(ATTN_UNROLL):
                fn(g * ATTN_UNROLL + j, 0)

        def pipelined(g, carry):
            group(block, g)
            group(rowmax, g + 1)
            group(scores, g + 2)
            return carry

        group(scores, 0)
        group(scores, 1)
        group(rowmax, 0)
        lax.fori_loop(0, n_groups - 2, pipelined, 0)
        group(block, n_groups - 2)
        group(rowmax, n_groups - 1)
        group(block, n_groups - 1)


def _attention(qkv, band_bias, batch, seq):
    view = qkv.reshape(batch, seq, 3 * D_ATTN)
    pairs = D_ATTN // LANES

    def spec(offset):
        return pl.BlockSpec((None, seq, LANES), lambda b, hp: (b, 0, offset * pairs + hp))

    n_pat = len(DILATED_PATTERNS)
    o = pl.pallas_call(
        functools.partial(_attn_kernel, seq=seq),
        grid=(batch, pairs),
        in_specs=[spec(0), spec(1), spec(2),
                  pl.BlockSpec((n_pat, 2, None, 2 * ATTN_BLOCK, 2 * ATTN_BLOCK),
                               lambda b, hp: (0, 0, hp, 0, 0))],
        out_specs=pl.BlockSpec((None, seq, LANES), lambda b, hp: (b, 0, hp)),
        out_shape=jax.ShapeDtypeStruct((batch, seq, D_ATTN), bf16),
        scratch_shapes=[pltpu.VMEM((seq, LANES), f32), pltpu.VMEM((seq, LANES), f32),
                        pltpu.VMEM((seq, LANES), f32),
                        pltpu.VMEM((2 * seq, LANES), bf16),
                        pltpu.VMEM((seq + ATTN_BLOCK, LANES), bf16),
                        pltpu.VMEM((seq + ATTN_BLOCK, 2 * LANES), bf16),
                        pltpu.VMEM((2 * seq, 2 * ATTN_BLOCK), f32),
                        pltpu.VMEM((2 * seq, LANES), f32),
                        pltpu.VMEM((seq, LANES), f32), pltpu.VMEM((seq, LANES), f32),
                        pltpu.VMEM((seq, LANES), f32)],
        compiler_params=_params("parallel", "parallel"),
        name="attention",
    )(view, view, view, band_bias)
    return o.reshape(batch * seq, D_ATTN)


def _gelu_tanh(x):
    return 0.5 * x * (1.0 + jnp.tanh(np.sqrt(2.0 / np.pi) * (x + 0.044715 * x * x * x)))


def _lru_kernel(ug_ref, convw_ref, convb_ref, wg_ref, ba_ref, bx_ref, lam_ref, g_ref,
                y_ref, ubuf, hcarry, *, ts):
    pad = SUBLANES
    t = pl.program_id(1)

    @pl.when(t == 0)
    def _():
        ubuf[0:pad, :] = jnp.zeros((pad, D_LRU), f32)
        hcarry[...] = jnp.zeros_like(hcarry)

    @pl.when(t > 0)
    def _():
        ubuf[0:pad, :] = ubuf[ts:ts + pad, :]

    ubuf[pad:pad + ts, :] = ug_ref[:, 0:D_LRU]
    gate = ug_ref[:, D_LRU:2 * D_LRU]

    u = convb_ref[...] + convw_ref[CONV_WIDTH - 1:CONV_WIDTH, :] * ubuf[pad:pad + ts, :]
    for w in range(CONV_WIDTH - 1):
        back = CONV_WIDTH - 1 - w
        u = u + convw_ref[w:w + 1, :] * ubuf[pad - back:pad - back + ts, :]

    gates = jnp.dot(u.astype(bf16), wg_ref[...], preferred_element_type=f32)
    r = jax.nn.sigmoid(gates[:, 0:D_LRU] + ba_ref[...])
    i = jax.nn.sigmoid(gates[:, D_LRU:2 * D_LRU] + bx_ref[...])
    neg_lam = -lam_ref[...]
    softplus = jnp.maximum(neg_lam, 0.0) + jnp.log1p(jnp.exp(-jnp.abs(neg_lam)))
    log_a = (-LRU_C) * r * softplus
    a = jnp.exp(log_a)
    b = jnp.sqrt(-jnp.tanh(log_a) * (1.0 + a * a)) * (i * u)

    groups = ts // SUBLANES
    a = a.reshape(groups, SUBLANES, D_LRU)
    b = b.reshape(groups, SUBLANES, D_LRU)
    sub = lax.broadcasted_iota(jnp.int32, (1, SUBLANES, 1), 1)
    shift = 1
    while shift < SUBLANES:
        live = sub >= shift
        a_prev = jnp.where(live, pltpu.roll(a, shift, 1), 1.0)
        b_prev = jnp.where(live, pltpu.roll(b, shift, 1), 0.0)
        b = a * b_prev + b
        a = a * a_prev
        shift *= 2
    h_prev = hcarry[...]
    h_groups = []
    for g in range(groups):
        h_g = a[g] * h_prev + b[g]
        h_prev = h_g[SUBLANES - 1:SUBLANES, :]
        h_groups.append(h_g)
    hcarry[...] = h_prev
    h = jnp.concatenate(h_groups, axis=0)

    y = _gelu_tanh(gate) * h
    y = y * lax.rsqrt(jnp.mean(y * y, axis=-1, keepdims=True) + RMS_EPS) * g_ref[...]
    y_ref[...] = y.astype(y_ref.dtype)


def _block_diag(w):
    g, i, j = w.shape
    eye = jnp.eye(g, dtype=w.dtype)
    return jnp.einsum('gij,gh->gihj', w, eye).reshape(g * i, g * j)


def _lru_branch(ug, conv_w, conv_b, w_a, b_a, w_x, b_x, lam, g_lru, batch, seq, ts=LRU_ROWS):
    wg = jnp.concatenate([_block_diag(w_a), _block_diag(w_x)], axis=1).astype(bf16)
    row = lambda v: v.reshape(1, D_LRU).astype(f32)
    const = lambda shape: pl.BlockSpec(shape, lambda b, t: (0, 0))
    view = ug.reshape(batch, seq, 2 * D_LRU)
    y = pl.pallas_call(
        functools.partial(_lru_kernel, ts=ts),
        grid=(batch, seq // ts),
        in_specs=[pl.BlockSpec((None, ts, 2 * D_LRU), lambda b, t: (b, t, 0)),
                  const((CONV_WIDTH, D_LRU)), const((1, D_LRU)),
                  const((D_LRU, 2 * D_LRU)), const((1, D_LRU)), const((1, D_LRU)),
                  const((1, D_LRU)), const((1, D_LRU))],
        out_specs=pl.BlockSpec((None, ts, D_LRU), lambda b, t: (b, t, 0)),
        out_shape=jax.ShapeDtypeStruct((batch, seq, D_LRU), bf16),
        scratch_shapes=[pltpu.VMEM((ts + 16, D_LRU), f32), pltpu.VMEM((1, D_LRU), f32)],
        compiler_params=_params("parallel", "arbitrary"),
        name="rglru",
    )(view, conv_w.reshape(CONV_WIDTH, D_LRU).astype(f32), row(conv_b), wg, row(b_a), row(b_x),
      row(lam), row(g_lru))
    return y.reshape(batch * seq, D_LRU)


ROW_TILE = D_MODEL // LANES


def _store_row_tiles(ref, rows):
    t = rows.shape[0]
    for s in range(ROW_TILE):
        ref[pl.ds(s, t, stride=ROW_TILE), :] = rows[:, s * LANES:(s + 1) * LANES]


def _load_row_tiles(ref, t):
    return jnp.concatenate([ref[pl.ds(s, t, stride=ROW_TILE), :] for s in range(ROW_TILE)],
                           axis=-1)


def _layer_norm(z, g, b):
    mu = jnp.mean(z, axis=-1, keepdims=True)
    zc = z - mu
    var = jnp.mean(zc * zc, axis=-1, keepdims=True)
    return zc * lax.rsqrt(var + LN_EPS) * g + b


def _mix_out_kernel(attn_ref, ylru, x_ref, w_ref, gattn, lng, lnb, *rest, with_router):
    if with_router:
        rw_ref, tri_ref, x1_ref, route_ref, count_ref = rest
    else:
        (x1_ref,) = rest
    attn = attn_ref[...].astype(f32)
    attn = attn * lax.rsqrt(jnp.mean(attn * attn, axis=-1, keepdims=True) + RMS_EPS) * gattn[...]
    y = jnp.dot(attn.astype(bf16), w_ref[0:D_ATTN, :], preferred_element_type=f32)
    y = y + jnp.dot(ylru[...], w_ref[D_ATTN:, :], preferred_element_type=f32)
    x1 = _layer_norm(DEEPNORM_ALPHA * x_ref[...] + y, lng[...], lnb[...])
    x1_ref[...] = x1
    if with_router:
        lane = lax.broadcasted_iota(jnp.int32, (1, LANES), 1).astype(f32)
        x_hi = x1.astype(bf16)
        x_lo = (x1 - x_hi.astype(f32)).astype(bf16)
        logits = (jnp.dot(x_hi, rw_ref[0], preferred_element_type=f32)
                  + jnp.dot(x_lo, rw_ref[0], preferred_element_type=f32)
                  + jnp.dot(x_hi, rw_ref[1], preferred_element_type=f32))
        logits = jnp.where(lane < N_EXPERTS, logits, -jnp.inf)
        v1 = jnp.max(logits, axis=-1, keepdims=True)
        i1 = jnp.min(jnp.where(logits == v1, lane, float(LANES)), axis=-1, keepdims=True)
        rest_logits = jnp.where(lane == i1, -jnp.inf, logits)
        v2 = jnp.max(rest_logits, axis=-1, keepdims=True)
        i2 = jnp.min(jnp.where(rest_logits == v2, lane, float(LANES)), axis=-1, keepdims=True)
        e2 = jnp.exp(v2 - v1)
        p1 = 1.0 / (1.0 + e2)
        p2 = e2 / (1.0 + e2)

        @pl.when(pl.program_id(0) == 0)
        def _():
            count_ref[...] = jnp.zeros_like(count_ref)

        tm = x1.shape[0]
        chosen = jnp.logical_or(lane == i1, lane == i2)
        rank = count_ref[...] + jnp.dot(tri_ref[...], chosen.astype(bf16),
                                        preferred_element_type=f32)
        count_ref[...] += jnp.sum(chosen.astype(f32), axis=0, keepdims=True)
        r1 = jnp.sum(jnp.where(lane == i1, rank, 0.0), axis=-1, keepdims=True)
        r2 = jnp.sum(jnp.where(lane == i2, rank, 0.0), axis=-1, keepdims=True)
        fields = (i1, i2, r1, r2, p1, p2)
        route = jnp.zeros((tm, LANES), f32)
        for k, val in enumerate(fields):
            route = jnp.where(lane == k, val, route)
        route_ref[...] = route


def _mix_out(attn, ylru, x2d, w_out_bf16, g_attn, ln_g, ln_b, router_w=None):
    n = x2d.shape[0]
    with_router = router_w is not None
    tm = ROUTER_ROWS if with_router else MIX_ROWS
    tile = lambda width: pl.BlockSpec((tm, width), lambda i: (i, 0))
    const = lambda shape: pl.BlockSpec(shape, lambda i: (0, 0))
    in_specs = [tile(D_ATTN), tile(D_LRU), tile(D_MODEL), const((D_MODEL, D_MODEL)),
                const((1, D_ATTN)), const((1, D_MODEL)), const((1, D_MODEL))]
    args = [attn, ylru, x2d, w_out_bf16, g_attn.reshape(1, D_ATTN).astype(f32),
            ln_g.reshape(1, D_MODEL).astype(f32), ln_b.reshape(1, D_MODEL).astype(f32)]
    out_specs = [tile(D_MODEL)]
    out_shape = [jax.ShapeDtypeStruct((n, D_MODEL), f32)]
    if with_router:
        rw = jnp.zeros((D_MODEL, LANES), f32).at[:, :N_EXPERTS].set(router_w.astype(f32))
        rw_hi = rw.astype(bf16)
        rw_lo = (rw - rw_hi.astype(f32)).astype(bf16)
        strictly_lower = jnp.asarray(np.tri(tm, k=-1), bf16)
        in_specs += [pl.BlockSpec((2, D_MODEL, LANES), lambda i: (0, 0, 0)), const((tm, tm))]
        args += [jnp.stack([rw_hi, rw_lo]), strictly_lower]
        out_specs += [tile(LANES), const((1, LANES))]
        out_shape += [jax.ShapeDtypeStruct((n, LANES), f32), jax.ShapeDtypeStruct((1, LANES), f32)]
    res = pl.pallas_call(
        functools.partial(_mix_out_kernel, with_router=with_router),
        grid=(n // tm,),
        in_specs=in_specs,
        out_specs=out_specs,
        out_shape=out_shape,
        compiler_params=_params("arbitrary" if with_router else "parallel"),
        name="mix_out_router" if with_router else "mix_out",
    )(*args)
    return res if with_router else (res[0], None, None)


def _ffn_kernel(tile_expert_ref, n_used_ref, x_ref, wg_ref, wu_ref, wd_ref, *rest,
                n_chunks, fuse_ln):
    if fuse_ln:
        lng, lnb, out_ref, acc_ref, xb_ref = rest
    else:
        out_ref, acc_ref, xb_ref = rest
    del tile_expert_ref
    j = pl.program_id(1)
    used = pl.program_id(0) < n_used_ref[0]

    @pl.when(j == 0)
    def _():
        acc_ref[...] = jnp.zeros_like(acc_ref)

    @pl.when(used & (j == 0))
    def _():
        if fuse_ln:
            xb_ref[...] = x_ref[...].astype(bf16)
        else:
            xb_ref[...] = _load_row_tiles(x_ref, xb_ref.shape[0]).astype(bf16)

    @pl.when(used)
    def _():
        xb = xb_ref[...]
        g = jnp.dot(xb, wg_ref[...], preferred_element_type=f32)
        u = jnp.dot(xb, wu_ref[...], preferred_element_type=f32)
        h = (g * jax.nn.sigmoid(g)) * u
        acc_ref[...] += jnp.dot(h.astype(bf16), wd_ref[...], preferred_element_type=f32)

    @pl.when(j == n_chunks - 1)
    def _():
        if fuse_ln:
            out_ref[...] = _layer_norm(DEEPNORM_ALPHA * x_ref[...] + acc_ref[...],
                                       lng[...], lnb[...])
        else:
            _store_row_tiles(out_ref, acc_ref[...])


def _ffn(x2d, tile_expert, n_used, w_gate, w_up, w_down, ln=None, *, tm, tf, name):
    fuse_ln = ln is not None
    rows = x2d.shape[0] if fuse_ln else x2d.shape[0] // ROW_TILE
    d_ff = w_gate.shape[2]
    n_chunks = d_ff // tf
    io_block = (tm, D_MODEL) if fuse_ln else (tm * ROW_TILE, LANES)

    def chunk(i, j, nu):
        return jnp.where(i < nu[0], j, n_chunks - 1)

    resident = w_gate.shape[0] == 1 and n_chunks == 1
    mode = dict(pipeline_mode=pl.Buffered(1)) if resident else {}
    in_specs = [pl.BlockSpec(io_block, lambda i, j, te, nu: (i, 0)),
                pl.BlockSpec((None, D_MODEL, tf),
                             lambda i, j, te, nu: (te[i], 0, chunk(i, j, nu)), **mode),
                pl.BlockSpec((None, D_MODEL, tf),
                             lambda i, j, te, nu: (te[i], 0, chunk(i, j, nu)), **mode),
                pl.BlockSpec((None, tf, D_MODEL),
                             lambda i, j, te, nu: (te[i], chunk(i, j, nu), 0), **mode)]
    args = [x2d, w_gate, w_up, w_down]
    if fuse_ln:
        in_specs += [pl.BlockSpec((1, D_MODEL), lambda i, j, te, nu: (0, 0))] * 2
        args += [v.reshape(1, D_MODEL).astype(f32) for v in ln]
    return pl.pallas_call(
        functools.partial(_ffn_kernel, n_chunks=n_chunks, fuse_ln=fuse_ln),
        grid_spec=pltpu.PrefetchScalarGridSpec(
            num_scalar_prefetch=2,
            grid=(rows // tm, n_chunks),
            in_specs=in_specs,
            out_specs=pl.BlockSpec(io_block, lambda i, j, te, nu: (i, 0)),
            scratch_shapes=[pltpu.VMEM((tm, D_MODEL), f32), pltpu.VMEM((tm, D_MODEL), bf16)]),
        out_shape=jax.ShapeDtypeStruct(x2d.shape, f32),
        compiler_params=_params("parallel", "arbitrary"),
        name=name,
    )(tile_expert, n_used, *args)


def _dispatch_kernel(fill_ref, pos_ref, x_ref, xs_hbm, zero_buf, rows_buf, sem, *, tt, tm):
    step = pl.program_id(0)

    n_fill = fill_ref.shape[0] // 2

    @pl.when(step == 0)
    def _():
        zero_buf[...] = jnp.zeros_like(zero_buf)

        def fill_copy(f):
            start = pl.multiple_of(fill_ref[f] * ROW_TILE, ROW_TILE)
            return pltpu.make_async_copy(zero_buf, xs_hbm.at[pl.ds(start, tm * ROW_TILE)], sem)

        for f in range(n_fill):
            @pl.when(fill_ref[n_fill + f] > 0)
            def _():
                fill_copy(f).start()
        for f in range(n_fill):
            @pl.when(fill_ref[n_fill + f] > 0)
            def _():
                fill_copy(f).wait()

    _store_row_tiles(rows_buf, x_ref[...])

    def issue(t, carry):
        src = pl.multiple_of(t * ROW_TILE, ROW_TILE)
        for k in range(TOP_K):
            dst = pl.multiple_of(pos_ref[0, TOP_K * t + k] * ROW_TILE, ROW_TILE)
            pltpu.make_async_copy(rows_buf.at[pl.ds(src, ROW_TILE)],
                                  xs_hbm.at[pl.ds(dst, ROW_TILE)], sem).start(priority=k)
        return carry

    lax.fori_loop(0, tt, issue, 0, unroll=DMA_UNROLL)
    for _ in range(TOP_K):
        pltpu.make_async_copy(rows_buf, xs_hbm.at[pl.ds(0, tt * ROW_TILE)], sem).wait()


def _dispatch(x1, pos_blocks, fill, rows_sorted, *, tt, tm):
    n = x1.shape[0]
    return pl.pallas_call(
        functools.partial(_dispatch_kernel, tt=tt, tm=tm),
        grid_spec=pltpu.PrefetchScalarGridSpec(
            num_scalar_prefetch=1,
            grid=(n // tt,),
            in_specs=[pl.BlockSpec((None, 1, TOP_K * tt), lambda i, fill: (i, 0, 0),
                                   memory_space=pltpu.SMEM),
                      pl.BlockSpec((tt, D_MODEL), lambda i, fill: (i, 0))],
            out_specs=pl.BlockSpec(memory_space=pl.ANY),
            scratch_shapes=[pltpu.VMEM((tm * ROW_TILE, LANES), f32),
                            pltpu.VMEM((tt * ROW_TILE, LANES), f32),
                            pltpu.SemaphoreType.DMA(())]),
        out_shape=jax.ShapeDtypeStruct((rows_sorted * ROW_TILE, LANES), f32),
        compiler_params=_params("arbitrary"),
        name="moe_dispatch",
    )(fill, pos_blocks, x1)


def _combine_kernel(pos_ref, pos_next_ref, route_ref, x1_ref, ys_hbm, lng, lnb, out_ref, buf, sems,
                    *, tt, n_steps):
    step = pl.program_id(0)
    slot = step % 2

    def issue_tile(tile_pos_ref, into):
        def issue(t, carry):
            dst = pl.multiple_of(t * ROW_TILE, ROW_TILE)
            for k in range(TOP_K):
                src = pl.multiple_of(tile_pos_ref[0, TOP_K * t + k] * ROW_TILE, ROW_TILE)
                pltpu.make_async_copy(ys_hbm.at[pl.ds(src, ROW_TILE)],
                                      buf.at[into, k, pl.ds(dst, ROW_TILE)],
                                      sems.at[into]).start(priority=k)
            return carry

        lax.fori_loop(0, tt, issue, 0, unroll=DMA_UNROLL)

    @pl.when(step == 0)
    def _():
        issue_tile(pos_ref, 0)

    @pl.when(step + 1 < n_steps)
    def _():
        issue_tile(pos_next_ref, 1 - slot)

    for k in range(TOP_K):
        pltpu.make_async_copy(ys_hbm.at[pl.ds(0, tt * ROW_TILE)], buf.at[slot, k],
                              sems.at[slot]).wait()

    lane = lax.broadcasted_iota(jnp.int32, (1, LANES), 1)
    route = route_ref[...]
    p1 = jnp.sum(jnp.where(lane == 4, route, 0.0), axis=-1, keepdims=True)
    p2 = jnp.sum(jnp.where(lane == 5, route, 0.0), axis=-1, keepdims=True)
    y = p1 * _load_row_tiles(buf.at[slot, 0], tt) + p2 * _load_row_tiles(buf.at[slot, 1], tt)
    out_ref[...] = _layer_norm(DEEPNORM_ALPHA * x1_ref[...] + y, lng[...], lnb[...])


def _combine(pos_blocks, route, x1, y_sorted, ln_g, ln_b, *, tt):
    n = x1.shape[0]
    n_steps = n // tt
    pos_spec = lambda index: pl.BlockSpec((None, 1, TOP_K * tt), index, memory_space=pltpu.SMEM)
    return pl.pallas_call(
        functools.partial(_combine_kernel, tt=tt, n_steps=n_steps),
        grid=(n_steps,),
        in_specs=[pos_spec(lambda i: (i, 0, 0)),
                  pos_spec(lambda i: (jnp.minimum(i + 1, n_steps - 1), 0, 0)),
                  pl.BlockSpec((tt, LANES), lambda i: (i, 0)),
                  pl.BlockSpec((tt, D_MODEL), lambda i: (i, 0)),
                  pl.BlockSpec(memory_space=pl.ANY),
                  pl.BlockSpec((1, D_MODEL), lambda i: (0, 0)),
                  pl.BlockSpec((1, D_MODEL), lambda i: (0, 0))],
        out_specs=pl.BlockSpec((tt, D_MODEL), lambda i: (i, 0)),
        out_shape=jax.ShapeDtypeStruct((n, D_MODEL), f32),
        scratch_shapes=[pltpu.VMEM((2, TOP_K, tt * ROW_TILE, LANES), f32),
                        pltpu.SemaphoreType.DMA((2,))],
        compiler_params=_params("arbitrary"),
        name="moe_combine",
    )(pos_blocks, pos_blocks, route, x1, y_sorted, ln_g.reshape(1, D_MODEL).astype(f32),
      ln_b.reshape(1, D_MODEL).astype(f32))


def _moe(x1, route, counts, w_gate, w_up, w_down, ln_g, ln_b, *, tm, tf, tt):
    n = x1.shape[0]
    i32 = jnp.int32
    counts = counts[0, :N_EXPERTS].astype(i32)
    padded = (counts + tm - 1) // tm * tm
    ends = jnp.cumsum(padded)
    offsets = ends - padded
    experts = route[:, 0:TOP_K].astype(i32)
    ranks = route[:, TOP_K:2 * TOP_K].astype(i32)
    pos = (offsets[experts] + ranks).reshape(n // tt, 1, TOP_K * tt)
    n_tiles = TOP_K * n // tm + N_EXPERTS
    n_used = (ends[-1] // tm).astype(i32).reshape(1)
    tile_ids = jnp.arange(n_tiles, dtype=i32)
    tile_expert = jnp.sum((tile_ids[:, None] >= (ends // tm)[None, :]).astype(i32), axis=1)
    last_expert = jnp.max(jnp.where(counts > 0, jnp.arange(N_EXPERTS, dtype=i32), 0))
    tile_expert = jnp.minimum(tile_expert, last_expert).astype(i32)
    tail_tiles = n_used[0] + jnp.arange(N_EXPERTS, dtype=i32)
    fill = jnp.concatenate([ends - tm, tail_tiles * tm,
                            (counts > 0).astype(i32), (tail_tiles < n_tiles).astype(i32)]).astype(i32)

    x_sorted = _dispatch(x1, pos, fill, n_tiles * tm, tt=tt, tm=tm)
    y_sorted = _ffn(x_sorted, tile_expert, n_used, w_gate, w_up, w_down, tm=tm, tf=tf,
                    name="ffn_experts")
    return _combine(pos, route, x1, y_sorted, ln_g, ln_b, tt=tt)


def kernel(x, w_in, conv_w, conv_b, w_a, b_a, w_x, b_x, lru_lambda, rel_bias, g_attn, g_lru, w_out, ln1_g, ln1_b, ln2_g, ln2_b, ffn_w_gate, ffn_w_up, ffn_w_down, router_w, moe_w_gate, moe_w_up, moe_w_down):
    batch, seq, _ = x.shape
    n = batch * seq
    h = x.reshape(n, D_MODEL).astype(f32)
    band_bias = jnp.stack([_band_bias(rel_bias, d) for _, d in DILATED_PATTERNS])
    for layer in range(DEPTH):
        qkv, ug = _in_proj(h, w_in[layer].astype(bf16))
        attn = _attention(qkv, band_bias, batch, seq)
        ylru = _lru_branch(ug, conv_w[layer], conv_b[layer], w_a[layer], b_a[layer], w_x[layer],
                           b_x[layer], lru_lambda[layer], g_lru[layer], batch, seq)
        j = layer // 2
        dense = layer % 2 == 0
        x1, route, counts = _mix_out(
            attn, ylru, h, w_out[layer].astype(bf16), g_attn[layer], ln1_g[layer],
            ln1_b[layer], None if dense else router_w[j])
        if dense:
            tm = DENSE_FFN_ROWS
            h = _ffn(x1, jnp.zeros((n // tm,), jnp.int32), jnp.full((1,), n // tm, jnp.int32),
                     ffn_w_gate[j][None].astype(bf16), ffn_w_up[j][None].astype(bf16),
                     ffn_w_down[j][None].astype(bf16), (ln2_g[layer], ln2_b[layer]),
                     tm=tm, tf=ffn_w_gate.shape[2], name="ffn_dense")
        else:
            h = _moe(x1, route, counts, moe_w_gate[j].astype(bf16), moe_w_up[j].astype(bf16),
                     moe_w_down[j].astype(bf16), ln2_g[layer], ln2_b[layer],
                     tm=EXPERT_ROWS, tf=EXPERT_FF_CHUNK, tt=ROUTE_ROWS)
    return h.reshape(batch, seq, D_MODEL).astype(x.dtype)
```

```python
import functools

import numpy as np
import jax
import jax.numpy as jnp
from jax import lax
from jax.experimental import pallas as pl
from jax.experimental.pallas import tpu as pltpu

D_MODEL = 1024
N_HEADS = 8
HEAD_DIM = 64
D_ATTN = N_HEADS * HEAD_DIM
D_LRU = 512
N_LRU_BLOCKS = 8
LRU_BLOCK = D_LRU // N_LRU_BLOCKS
CONV_WIDTH = 4
LRU_C = 8.0
DILATED_PATTERNS = ((128, 1), (512, 4), (2048, 16))
ATTN_BLOCK = 128
STAGE_DILATION = 4
N_BUCKETS = 32
MAX_DISTANCE = 2048
D_IN = 3 * D_ATTN + 2 * D_LRU
N_EXPERTS = 8
TOP_K = 2
DEPTH = 2
DEEPNORM_ALPHA = (2.0 * DEPTH) ** 0.25
LN_EPS = 1e-5
RMS_EPS = 1e-6
NEG_INF = -1e30
LOG2_E = float(np.log2(np.e))

LANES = 128
SUBLANES = 8
VMEM_LIMIT_BYTES = 56 * 1024 * 1024
MXU_DEPTH = 256

IN_PROJ_ROWS = 1024
LRU_ROWS = 512
MIX_ROWS = 1024
ROUTER_ROWS = 512
DENSE_FFN_ROWS = 512
EXPERT_ROWS = 512
EXPERT_FF_CHUNK = 7 * MXU_DEPTH
ROUTE_ROWS = 512
DMA_UNROLL = 8
ATTN_UNROLL = 4

f32 = jnp.float32
bf16 = jnp.bfloat16


def _params(*semantics):
    return pltpu.CompilerParams(dimension_semantics=semantics,
                                vmem_limit_bytes=VMEM_LIMIT_BYTES)


def _in_proj_kernel(x_ref, w_ref, qkv_ref, ug_ref):
    xb = x_ref[...].astype(bf16)
    n_qkv = qkv_ref.shape[1]
    step = 2 * MXU_DEPTH
    for c in range(0, D_IN, step):
        p = jnp.dot(xb, w_ref[:, c:c + step], preferred_element_type=f32)
        if c < n_qkv:
            qkv_ref[:, c:c + step] = p
        else:
            ug_ref[:, c - n_qkv:c - n_qkv + step] = p


def _in_proj(x2d, w_in_bf16, tm=IN_PROJ_ROWS):
    n = x2d.shape[0]
    return pl.pallas_call(
        _in_proj_kernel,
        grid=(n // tm,),
        in_specs=[pl.BlockSpec((tm, D_MODEL), lambda i: (i, 0)),
                  pl.BlockSpec((D_MODEL, D_IN), lambda i: (0, 0))],
        out_specs=[pl.BlockSpec((tm, 3 * D_ATTN), lambda i: (i, 0)),
                   pl.BlockSpec((tm, 2 * D_LRU), lambda i: (i, 0))],
        out_shape=[jax.ShapeDtypeStruct((n, 3 * D_ATTN), f32),
                   jax.ShapeDtypeStruct((n, 2 * D_LRU), f32)],
        compiler_params=_params("parallel"),
        name="in_proj",
    )(x2d, w_in_bf16)


def _t5_bucket(dist):
    max_exact = N_BUCKETS // 2
    d = np.maximum(dist, 1).astype(np.float32)
    large = max_exact + (np.log(d / max_exact) / np.log(MAX_DISTANCE / max_exact)
                         * (N_BUCKETS - max_exact)).astype(np.int32)
    large = np.minimum(large, N_BUCKETS - 1)
    return np.where(dist < max_exact, dist, large).astype(np.int32)


def _band_bias(rel_bias, dilation):
    nk = ATTN_BLOCK
    qi = np.arange(nk)[:, None]
    kj = np.arange(2 * nk)[None, :]
    delta = qi + nk - kj
    band = (delta >= 0) & (delta <= nk)
    bucket = _t5_bucket(np.clip(delta, 0, nk) * dilation)
    onehot = np.eye(N_BUCKETS, dtype=np.float32)[bucket.reshape(-1)]
    bias = jnp.dot(jnp.asarray(onehot), rel_bias.astype(f32), precision=lax.Precision.HIGHEST)
    bias = jnp.transpose(bias.reshape(nk, 2 * nk, N_HEADS), (2, 0, 1))
    valid = np.stack([band, band & (kj >= nk)])[:, None]
    bias = jnp.where(jnp.asarray(valid), bias[None], NEG_INF)
    return (bias * LOG2_E).reshape(2, N_HEADS // 2, 2 * nk, 2 * nk)


def _attn_kernel(q_ref, k_ref, v_ref, bias_ref, o_ref, q4, k4, v4, qs, ks, vs, s_buf, m_buf,
                 m_s, l_s, acc_s, *, seq):
    nk = ATTN_BLOCK
    n_blocks = seq // nk
    chunk = 2 * nk
    lane = lax.broadcasted_iota(jnp.int32, (1, LANES), 1)
    head0 = lane < HEAD_DIM

    ks[0:nk, :] = jnp.zeros((nk, LANES), bf16)
    vs[0:nk, 0:LANES] = jnp.zeros((nk, LANES), bf16)
    vs[:, LANES:2 * LANES] = jnp.ones((seq + nk, LANES), bf16)

    def stage(t, carry):
        chunks_per_residue = seq // STAGE_DILATION // chunk
        src = pl.ds(t // chunks_per_residue + STAGE_DILATION * chunk * (t % chunks_per_residue),
                    chunk, stride=STAGE_DILATION)
        dst = pl.ds(pl.multiple_of(t * chunk, chunk), chunk)
        q4[dst, :] = q_ref[src, :]
        k4[dst, :] = k_ref[src, :]
        v4[dst, :] = v_ref[src, :]
        return carry

    lax.fori_loop(0, seq // chunk, stage, 0)

    order = sorted(range(len(DILATED_PATTERNS)), key=lambda p: -DILATED_PATTERNS[p][1])
    assert DILATED_PATTERNS[order[-1]][1] == 1
    for p in order:
        d = DILATED_PATTERNS[p][1]
        is_first, is_last = p == order[0], p == order[-1]
        length = seq // d
        nb = length // nk
        chunks_per_residue = length // chunk

        def gather(t, carry):
            r = t // chunks_per_residue
            c = t % chunks_per_residue
            dst = pl.multiple_of(t * chunk, chunk)
            if d % STAGE_DILATION == 0:
                sub = d // STAGE_DILATION
                start = ((r % STAGE_DILATION) * (seq // STAGE_DILATION) + r // STAGE_DILATION
                         + sub * chunk * c)
                src = (pl.ds(start, chunk, stride=sub) if sub > 1
                       else pl.ds(pl.multiple_of(start, chunk), chunk))
                q, k, v = q4[src, :], k4[src, :], v4[src, :]
            else:
                src = pl.ds(r + d * chunk * c, chunk, stride=d) if d > 1 else pl.ds(dst, chunk)
                q, k, v = q_ref[src, :], k_ref[src, :], v_ref[src, :]
            q = q * (HEAD_DIM ** -0.5 * LOG2_E)
            q0 = jnp.where(head0, q, 0.0).astype(bf16)
            q1 = jnp.where(head0, 0.0, q).astype(bf16)
            for half in range(2):
                base = pl.multiple_of(2 * dst + half * chunk, chunk)
                qs[pl.ds(base, nk), :] = q0[half * nk:(half + 1) * nk]
                qs[pl.ds(base + nk, nk), :] = q1[half * nk:(half + 1) * nk]
            ks[pl.ds(nk + dst, chunk), :] = k.astype(bf16)
            vs[pl.ds(nk + dst, chunk), 0:LANES] = v.astype(bf16)
            return carry

        lax.fori_loop(0, seq // chunk, gather, 0)

        def scores(b, carry):
            first = jnp.asarray(b % nb == 0, jnp.int32)
            rows = pl.ds(pl.multiple_of(b * chunk, chunk), chunk)
            k = ks[pl.ds(pl.multiple_of(b * nk, nk), chunk), :]
            s_buf[rows, :] = lax.dot_general(qs[rows, :], k, (((1,), (1,)), ((), ())),
                                             preferred_element_type=f32) + bias_ref[p, first]
            return carry

        def rowmax(b, carry):
            rows = pl.ds(pl.multiple_of(b * chunk, chunk), chunk)
            m_buf[rows, :] = jnp.broadcast_to(jnp.max(s_buf[rows, :], axis=-1, keepdims=True),
                                              (chunk, LANES))
            return carry

        def block(b, carry):
            r = b // nb
            i = b % nb
            rows = pl.ds(pl.multiple_of(b * chunk, chunk), chunk)
            v = vs[pl.ds(pl.multiple_of(b * nk, nk), chunk), :]
            m = m_buf[rows, :]
            e = jnp.exp2(s_buf[rows, :] - jnp.concatenate([m, m], axis=1))
            pv = jnp.dot(e.astype(bf16), v, preferred_element_type=f32)
            m_blk = jnp.where(head0, m[0:nk], m[nk:chunk])
            l_blk = jnp.where(head0, pv[0:nk, LANES:], pv[nk:chunk, LANES:])
            pv_blk = jnp.where(head0, pv[0:nk, 0:LANES], pv[nk:chunk, 0:LANES])
            if d == 1:
                tok = pl.ds(pl.multiple_of(b * nk, nk), nk)
            else:
                tok = pl.ds(r + d * nk * i, nk, stride=d)
            if is_first:
                m_s[tok, :] = m_blk
                l_s[tok, :] = l_blk
                acc_s[tok, :] = pv_blk
                return carry
            m_old = m_s[tok, :]
            m_new = jnp.maximum(m_old, m_blk)
            w_old = jnp.exp2(m_old - m_new)
            w_blk = jnp.exp2(m_blk - m_new)
            l_new = w_old * l_s[tok, :] + w_blk * l_blk
            acc_new = w_old * acc_s[tok, :] + w_blk * pv_blk
            if is_last:
                o_ref[tok, :] = (acc_new / l_new).astype(o_ref.dtype)
            else:
                l_s[tok, :] = l_new
                acc_s[tok, :] = acc_new
                m_s[tok, :] = m_new
            return carry

        n_groups = n_blocks // ATTN_UNROLL
        assert n_groups >= 3

        def group(fn, g):
            for j in range(ATTN_UNROLL):
                fn(g * ATTN_UNROLL + j, 0)

        def pipelined(g, carry):
            group(block, g)
            group(rowmax, g + 1)
            group(scores, g + 2)
            return carry

        group(scores, 0)
        group(scores, 1)
        group(rowmax, 0)
        lax.fori_loop(0, n_groups - 2, pipelined, 0)
        group(block, n_groups - 2)
        group(rowmax, n_groups - 1)
        group(block, n_groups - 1)


def _attention(qkv, band_bias, batch, seq):
    view = qkv.reshape(batch, seq, 3 * D_ATTN)
    pairs = D_ATTN // LANES

    def spec(offset):
        return pl.BlockSpec((None, seq, LANES), lambda b, hp: (b, 0, offset * pairs + hp))

    n_pat = len(DILATED_PATTERNS)
    o = pl.pallas_call(
        functools.partial(_attn_kernel, seq=seq),
        grid=(batch, pairs),
        in_specs=[spec(0), spec(1), spec(2),
                  pl.BlockSpec((n_pat, 2, None, 2 * ATTN_BLOCK, 2 * ATTN_BLOCK),
                               lambda b, hp: (0, 0, hp, 0, 0))],
        out_specs=pl.BlockSpec((None, seq, LANES), lambda b, hp: (b, 0, hp)),
        out_shape=jax.ShapeDtypeStruct((batch, seq, D_ATTN), bf16),
        scratch_shapes=[pltpu.VMEM((seq, LANES), f32), pltpu.VMEM((seq, LANES), f32),
                        pltpu.VMEM((seq, LANES), f32),
                        pltpu.VMEM((2 * seq, LANES), bf16),
                        pltpu.VMEM((seq + ATTN_BLOCK, LANES), bf16),
                        pltpu.VMEM((seq + ATTN_BLOCK, 2 * LANES), bf16),
                        pltpu.VMEM((2 * seq, 2 * ATTN_BLOCK), f32),
                        pltpu.VMEM((2 * seq, LANES), f32),
                        pltpu.VMEM((seq, LANES), f32), pltpu.VMEM((seq, LANES), f32),
                        pltpu.VMEM((seq, LANES), f32)],
        compiler_params=_params("parallel", "parallel"),
        name="attention",
    )(view, view, view, band_bias)
    return o.reshape(batch * seq, D_ATTN)


def _gelu_tanh(x):
    return 0.5 * x * (1.0 + jnp.tanh(np.sqrt(2.0 / np.pi) * (x + 0.044715 * x * x * x)))


def _lru_kernel(ug_ref, convw_ref, convb_ref, wg_ref, ba_ref, bx_ref, lam_ref, g_ref,
                y_ref, ubuf, hcarry, *, ts):
    pad = SUBLANES
    t = pl.program_id(1)

    @pl.when(t == 0)
    def _():
        ubuf[0:pad, :] = jnp.zeros((pad, D_LRU), f32)
        hcarry[...] = jnp.zeros_like(hcarry)

    @pl.when(t > 0)
    def _():
        ubuf[0:pad, :] = ubuf[ts:ts + pad, :]

    ubuf[pad:pad + ts, :] = ug_ref[:, 0:D_LRU]
    gate = ug_ref[:, D_LRU:2 * D_LRU]

    u = convb_ref[...] + convw_ref[CONV_WIDTH - 1:CONV_WIDTH, :] * ubuf[pad:pad + ts, :]
    for w in range(CONV_WIDTH - 1):
        back = CONV_WIDTH - 1 - w
        u = u + convw_ref[w:w + 1, :] * ubuf[pad - back:pad - back + ts, :]

    gates = jnp.dot(u.astype(bf16), wg_ref[...], preferred_element_type=f32)
    r = jax.nn.sigmoid(gates[:, 0:D_LRU] + ba_ref[...])
    i = jax.nn.sigmoid(gates[:, D_LRU:2 * D_LRU] + bx_ref[...])
    neg_lam = -lam_ref[...]
    softplus = jnp.maximum(neg_lam, 0.0) + jnp.log1p(jnp.exp(-jnp.abs(neg_lam)))
    log_a = (-LRU_C) * r * softplus
    a = jnp.exp(log_a)
    b = jnp.sqrt(-jnp.tanh(log_a) * (1.0 + a * a)) * (i * u)

    groups = ts // SUBLANES
    a = a.reshape(groups, SUBLANES, D_LRU)
    b = b.reshape(groups, SUBLANES, D_LRU)
    sub = lax.broadcasted_iota(jnp.int32, (1, SUBLANES, 1), 1)
    shift = 1
    while shift < SUBLANES:
        live = sub >= shift
        a_prev = jnp.where(live, pltpu.roll(a, shift, 1), 1.0)
        b_prev = jnp.where(live, pltpu.roll(b, shift, 1), 0.0)
        b = a * b_prev + b
        a = a * a_prev
        shift *= 2
    h_prev = hcarry[...]
    h_groups = []
    for g in range(groups):
        h_g = a[g] * h_prev + b[g]
        h_prev = h_g[SUBLANES - 1:SUBLANES, :]
        h_groups.append(h_g)
    hcarry[...] = h_prev
    h = jnp.concatenate(h_groups, axis=0)

    y = _gelu_tanh(gate) * h
    y = y * lax.rsqrt(jnp.mean(y * y, axis=-1, keepdims=True) + RMS_EPS) * g_ref[...]
    y_ref[...] = y.astype(y_ref.dtype)


def _block_diag(w):
    g, i, j = w.shape
    eye = jnp.eye(g, dtype=w.dtype)
    return jnp.einsum('gij,gh->gihj', w, eye).reshape(g * i, g * j)


def _lru_branch(ug, conv_w, conv_b, w_a, b_a, w_x, b_x, lam, g_lru, batch, seq, ts=LRU_ROWS):
    wg = jnp.concatenate([_block_diag(w_a), _block_diag(w_x)], axis=1).astype(bf16)
    row = lambda v: v.reshape(1, D_LRU).astype(f32)
    const = lambda shape: pl.BlockSpec(shape, lambda b, t: (0, 0))
    view = ug.reshape(batch, seq, 2 * D_LRU)
    y = pl.pallas_call(
        functools.partial(_lru_kernel, ts=ts),
        grid=(batch, seq // ts),
        in_specs=[pl.BlockSpec((None, ts, 2 * D_LRU), lambda b, t: (b, t, 0)),
                  const((CONV_WIDTH, D_LRU)), const((1, D_LRU)),
                  const((D_LRU, 2 * D_LRU)), const((1, D_LRU)), const((1, D_LRU)),
                  const((1, D_LRU)), const((1, D_LRU))],
        out_specs=pl.BlockSpec((None, ts, D_LRU), lambda b, t: (b, t, 0)),
        out_shape=jax.ShapeDtypeStruct((batch, seq, D_LRU), bf16),
        scratch_shapes=[pltpu.VMEM((ts + 16, D_LRU), f32), pltpu.VMEM((1, D_LRU), f32)],
        compiler_params=_params("parallel", "arbitrary"),
        name="rglru",
    )(view, conv_w.reshape(CONV_WIDTH, D_LRU).astype(f32), row(conv_b), wg, row(b_a), row(b_x),
      row(lam), row(g_lru))
    return y.reshape(batch * seq, D_LRU)


ROW_TILE = D_MODEL // LANES


def _store_row_tiles(ref, rows):
    t = rows.shape[0]
    for s in range(ROW_TILE):
        ref[pl.ds(s, t, stride=ROW_TILE), :] = rows[:, s * LANES:(s + 1) * LANES]


def _load_row_tiles(ref, t):
    return jnp.concatenate([ref[pl.ds(s, t, stride=ROW_TILE), :] for s in range(ROW_TILE)],
                           axis=-1)


def _layer_norm(z, g, b):
    mu = jnp.mean(z, axis=-1, keepdims=True)
    zc = z - mu
    var = jnp.mean(zc * zc, axis=-1, keepdims=True)
    return zc * lax.rsqrt(var + LN_EPS) * g + b


def _mix_out_kernel(attn_ref, ylru, x_ref, w_ref, gattn, lng, lnb, *rest, with_router):
    if with_router:
        rw_ref, tri_ref, x1_ref, route_ref, count_ref = rest
    else:
        (x1_ref,) = rest
    attn = attn_ref[...].astype(f32)
    attn = attn * lax.rsqrt(jnp.mean(attn * attn, axis=-1, keepdims=True) + RMS_EPS) * gattn[...]
    y = jnp.dot(attn.astype(bf16), w_ref[0:D_ATTN, :], preferred_element_type=f32)
    y = y + jnp.dot(ylru[...], w_ref[D_ATTN:, :], preferred_element_type=f32)
    x1 = _layer_norm(DEEPNORM_ALPHA * x_ref[...] + y, lng[...], lnb[...])
    x1_ref[...] = x1
    if with_router:
        lane = lax.broadcasted_iota(jnp.int32, (1, LANES), 1).astype(f32)
        x_hi = x1.astype(bf16)
        x_lo = (x1 - x_hi.astype(f32)).astype(bf16)
        logits = (jnp.dot(x_hi, rw_ref[0], preferred_element_type=f32)
                  + jnp.dot(x_lo, rw_ref[0], preferred_element_type=f32)
                  + jnp.dot(x_hi, rw_ref[1], preferred_element_type=f32))
        logits = jnp.where(lane < N_EXPERTS, logits, -jnp.inf)
        v1 = jnp.max(logits, axis=-1, keepdims=True)
        i1 = jnp.min(jnp.where(logits == v1, lane, float(LANES)), axis=-1, keepdims=True)
        rest_logits = jnp.where(lane == i1, -jnp.inf, logits)
        v2 = jnp.max(rest_logits, axis=-1, keepdims=True)
        i2 = jnp.min(jnp.where(rest_logits == v2, lane, float(LANES)), axis=-1, keepdims=True)
        e2 = jnp.exp(v2 - v1)
        p1 = 1.0 / (1.0 + e2)
        p2 = e2 / (1.0 + e2)

        @pl.when(pl.program_id(0) == 0)
        def _():
            count_ref[...] = jnp.zeros_like(count_ref)

        tm = x1.shape[0]
        chosen = jnp.logical_or(lane == i1, lane == i2)
        rank = count_ref[...] + jnp.dot(tri_ref[...], chosen.astype(bf16),
                                        preferred_element_type=f32)
        count_ref[...] += jnp.sum(chosen.astype(f32), axis=0, keepdims=True)
        r1 = jnp.sum(jnp.where(lane == i1, rank, 0.0), axis=-1, keepdims=True)
        r2 = jnp.sum(jnp.where(lane == i2, rank, 0.0), axis=-1, keepdims=True)
        fields = (i1, i2, r1, r2, p1, p2)
        route = jnp.zeros((tm, LANES), f32)
        for k, val in enumerate(fields):
            route = jnp.where(lane == k, val, route)
        route_ref[...] = route


def _mix_out(attn, ylru, x2d, w_out_bf16, g_attn, ln_g, ln_b, router_w=None):
    n = x2d.shape[0]
    with_router = router_w is not None
    tm = ROUTER_ROWS if with_router else MIX_ROWS
    tile = lambda width: pl.BlockSpec((tm, width), lambda i: (i, 0))
    const = lambda shape: pl.BlockSpec(shape, lambda i: (0, 0))
    in_specs = [tile(D_ATTN), tile(D_LRU), tile(D_MODEL), const((D_MODEL, D_MODEL)),
                const((1, D_ATTN)), const((1, D_MODEL)), const((1, D_MODEL))]
    args = [attn, ylru, x2d, w_out_bf16, g_attn.reshape(1, D_ATTN).astype(f32),
            ln_g.reshape(1, D_MODEL).astype(f32), ln_b.reshape(1, D_MODEL).astype(f32)]
    out_specs = [tile(D_MODEL)]
    out_shape = [jax.ShapeDtypeStruct((n, D_MODEL), f32)]
    if with_router:
        rw = jnp.zeros((D_MODEL, LANES), f32).at[:, :N_EXPERTS].set(router_w.astype(f32))
        rw_hi = rw.astype(bf16)
        rw_lo = (rw - rw_hi.astype(f32)).astype(bf16)
        strictly_lower = jnp.asarray(np.tri(tm, k=-1), bf16)
        in_specs += [pl.BlockSpec((2, D_MODEL, LANES), lambda i: (0, 0, 0)), const((tm, tm))]
        args += [jnp.stack([rw_hi, rw_lo]), strictly_lower]
        out_specs += [tile(LANES), const((1, LANES))]
        out_shape += [jax.ShapeDtypeStruct((n, LANES), f32), jax.ShapeDtypeStruct((1, LANES), f32)]
    res = pl.pallas_call(
        functools.partial(_mix_out_kernel, with_router=with_router),
        grid=(n // tm,),
        in_specs=in_specs,
        out_specs=out_specs,
        out_shape=out_shape,
        compiler_params=_params("arbitrary" if with_router else "parallel"),
        name="mix_out_router" if with_router else "mix_out",
    )(*args)
    return res if with_router else (res[0], None, None)


def _ffn_kernel(tile_expert_ref, n_used_ref, x_ref, wg_ref, wu_ref, wd_ref, *rest,
                n_chunks, fuse_ln):
    if fuse_ln:
        lng, lnb, out_ref, acc_ref, xb_ref = rest
    else:
        out_ref, acc_ref, xb_ref = rest
    del tile_expert_ref
    j = pl.program_id(1)
    used = pl.program_id(0) < n_used_ref[0]
    last = n_chunks - 1

    def load_x():
        if fuse_ln:
            return x_ref[...].astype(bf16)
        return _load_row_tiles(x_ref, xb_ref.shape[0]).astype(bf16)

    def chunk_out(xb):
        g = jnp.dot(xb, wg_ref[...], preferred_element_type=f32)
        u = jnp.dot(xb, wu_ref[...], preferred_element_type=f32)
        h = (g * jax.nn.sigmoid(g)) * u
        return jnp.dot(h.astype(bf16), wd_ref[...], preferred_element_type=f32)

    def finish(y):
        if fuse_ln:
            out_ref[...] = _layer_norm(DEEPNORM_ALPHA * x_ref[...] + y, lng[...], lnb[...])
        else:
            _store_row_tiles(out_ref, y)

    if n_chunks == 1:
        @pl.when(used)
        def _():
            finish(chunk_out(load_x()))
    else:
        @pl.when(used & (j == 0))
        def _():
            xb = load_x()
            xb_ref[...] = xb
            acc_ref[...] = chunk_out(xb)

        if n_chunks > 2:
            @pl.when(used & (j > 0) & (j < last))
            def _():
                acc_ref[...] += chunk_out(xb_ref[...])

        @pl.when(used & (j == last))
        def _():
            finish(acc_ref[...] + chunk_out(xb_ref[...]))

    @pl.when(jnp.logical_not(used) & (j == last))
    def _():
        finish(jnp.zeros(acc_ref.shape, f32))


def _ffn(x2d, tile_expert, n_used, w_gate, w_up, w_down, ln=None, *, tm, tf, name):
    fuse_ln = ln is not None
    rows = x2d.shape[0] if fuse_ln else x2d.shape[0] // ROW_TILE
    d_ff = w_gate.shape[2]
    n_chunks = d_ff // tf
    io_block = (tm, D_MODEL) if fuse_ln else (tm * ROW_TILE, LANES)

    def chunk(i, j, nu):
        return jnp.where(i < nu[0], j, n_chunks - 1)

    resident = w_gate.shape[0] == 1 and n_chunks == 1
    mode = dict(pipeline_mode=pl.Buffered(1)) if resident else {}
    in_specs = [pl.BlockSpec(io_block, lambda i, j, te, nu: (i, 0)),
                pl.BlockSpec((None, D_MODEL, tf),
                             lambda i, j, te, nu: (te[i], 0, chunk(i, j, nu)), **mode),
                pl.BlockSpec((None, D_MODEL, tf),
                             lambda i, j, te, nu: (te[i], 0, chunk(i, j, nu)), **mode),
                pl.BlockSpec((None, tf, D_MODEL),
                             lambda i, j, te, nu: (te[i], chunk(i, j, nu), 0), **mode)]
    args = [x2d, w_gate, w_up, w_down]
    if fuse_ln:
        in_specs += [pl.BlockSpec((1, D_MODEL), lambda i, j, te, nu: (0, 0))] * 2
        args += [v.reshape(1, D_MODEL).astype(f32) for v in ln]
    return pl.pallas_call(
        functools.partial(_ffn_kernel, n_chunks=n_chunks, fuse_ln=fuse_ln),
        grid_spec=pltpu.PrefetchScalarGridSpec(
            num_scalar_prefetch=2,
            grid=(rows // tm, n_chunks),
            in_specs=in_specs,
            out_specs=pl.BlockSpec(io_block, lambda i, j, te, nu: (i, 0)),
            scratch_shapes=[pltpu.VMEM((tm, D_MODEL), f32), pltpu.VMEM((tm, D_MODEL), bf16)]),
        out_shape=jax.ShapeDtypeStruct(x2d.shape, f32),
        compiler_params=_params("parallel", "arbitrary"),
        name=name,
    )(tile_expert, n_used, *args)


def _dispatch_kernel(fill_ref, pos_ref, x_ref, xs_hbm, zero_buf, rows_buf, sem, *, tt, tm):
    step = pl.program_id(0)

    n_fill = fill_ref.shape[0] // 2

    @pl.when(step == 0)
    def _():
        zero_buf[...] = jnp.zeros_like(zero_buf)

        def fill_copy(f):
            start = pl.multiple_of(fill_ref[f] * ROW_TILE, ROW_TILE)
            return pltpu.make_async_copy(zero_buf, xs_hbm.at[pl.ds(start, tm * ROW_TILE)], sem)

        for f in range(n_fill):
            @pl.when(fill_ref[n_fill + f] > 0)
            def _():
                fill_copy(f).start()
        for f in range(n_fill):
            @pl.when(fill_ref[n_fill + f] > 0)
            def _():
                fill_copy(f).wait()

    _store_row_tiles(rows_buf, x_ref[...])

    def issue(t, carry):
        src = pl.multiple_of(t * ROW_TILE, ROW_TILE)
        for k in range(TOP_K):
            dst = pl.multiple_of(pos_ref[0, TOP_K * t + k] * ROW_TILE, ROW_TILE)
            pltpu.make_async_copy(rows_buf.at[pl.ds(src, ROW_TILE)],
                                  xs_hbm.at[pl.ds(dst, ROW_TILE)], sem).start(priority=k)
        return carry

    lax.fori_loop(0, tt, issue, 0, unroll=DMA_UNROLL)
    for _ in range(TOP_K):
        pltpu.make_async_copy(rows_buf, xs_hbm.at[pl.ds(0, tt * ROW_TILE)], sem).wait()


def _dispatch(x1, pos_blocks, fill, rows_sorted, *, tt, tm):
    n = x1.shape[0]
    return pl.pallas_call(
        functools.partial(_dispatch_kernel, tt=tt, tm=tm),
        grid_spec=pltpu.PrefetchScalarGridSpec(
            num_scalar_prefetch=1,
            grid=(n // tt,),
            in_specs=[pl.BlockSpec((None, 1, TOP_K * tt), lambda i, fill: (i, 0, 0),
                                   memory_space=pltpu.SMEM),
                      pl.BlockSpec((tt, D_MODEL), lambda i, fill: (i, 0))],
            out_specs=pl.BlockSpec(memory_space=pl.ANY),
            scratch_shapes=[pltpu.VMEM((tm * ROW_TILE, LANES), f32),
                            pltpu.VMEM((tt * ROW_TILE, LANES), f32),
                            pltpu.SemaphoreType.DMA(())]),
        out_shape=jax.ShapeDtypeStruct((rows_sorted * ROW_TILE, LANES), f32),
        compiler_params=_params("arbitrary"),
        name="moe_dispatch",
    )(fill, pos_blocks, x1)


def _combine_kernel(pos_ref, pos_next_ref, route_ref, x1_ref, ys_hbm, lng, lnb, out_ref, buf, sems,
                    *, tt, n_steps):
    step = pl.program_id(0)
    slot = step % 2

    def issue_tile(tile_pos_ref, into):
        def issue(t, carry):
            dst = pl.multiple_of(t * ROW_TILE, ROW_TILE)
            for k in range(TOP_K):
                src = pl.multiple_of(tile_pos_ref[0, TOP_K * t + k] * ROW_TILE, ROW_TILE)
                pltpu.make_async_copy(ys_hbm.at[pl.ds(src, ROW_TILE)],
                                      buf.at[into, k, pl.ds(dst, ROW_TILE)],
                                      sems.at[into]).start(priority=k)
            return carry

        lax.fori_loop(0, tt, issue, 0, unroll=DMA_UNROLL)

    @pl.when(step == 0)
    def _():
        issue_tile(pos_ref, 0)

    @pl.when(step + 1 < n_steps)
    def _():
        issue_tile(pos_next_ref, 1 - slot)

    for k in range(TOP_K):
        pltpu.make_async_copy(ys_hbm.at[pl.ds(0, tt * ROW_TILE)], buf.at[slot, k],
                              sems.at[slot]).wait()

    lane = lax.broadcasted_iota(jnp.int32, (1, LANES), 1)
    route = route_ref[...]
    p1 = jnp.sum(jnp.where(lane == 4, route, 0.0), axis=-1, keepdims=True)
    p2 = jnp.sum(jnp.where(lane == 5, route, 0.0), axis=-1, keepdims=True)
    y = p1 * _load_row_tiles(buf.at[slot, 0], tt) + p2 * _load_row_tiles(buf.at[slot, 1], tt)
    out_ref[...] = _layer_norm(DEEPNORM_ALPHA * x1_ref[...] + y, lng[...], lnb[...])


def _combine(pos_blocks, route, x1, y_sorted, ln_g, ln_b, *, tt):
    n = x1.shape[0]
    n_steps = n // tt
    pos_spec = lambda index: pl.BlockSpec((None, 1, TOP_K * tt), index, memory_space=pltpu.SMEM)
    return pl.pallas_call(
        functools.partial(_combine_kernel, tt=tt, n_steps=n_steps),
        grid=(n_steps,),
        in_specs=[pos_spec(lambda i: (i, 0, 0)),
                  pos_spec(lambda i: (jnp.minimum(i + 1, n_steps - 1), 0, 0)),
                  pl.BlockSpec((tt, LANES), lambda i: (i, 0)),
                  pl.BlockSpec((tt, D_MODEL), lambda i: (i, 0)),
                  pl.BlockSpec(memory_space=pl.ANY),
                  pl.BlockSpec((1, D_MODEL), lambda i: (0, 0)),
                  pl.BlockSpec((1, D_MODEL), lambda i: (0, 0))],
        out_specs=pl.BlockSpec((tt, D_MODEL), lambda i: (i, 0)),
        out_shape=jax.ShapeDtypeStruct((n, D_MODEL), f32),
        scratch_shapes=[pltpu.VMEM((2, TOP_K, tt * ROW_TILE, LANES), f32),
                        pltpu.SemaphoreType.DMA((2,))],
        compiler_params=_params("arbitrary"),
        name="moe_combine",
    )(pos_blocks, pos_blocks, route, x1, y_sorted, ln_g.reshape(1, D_MODEL).astype(f32),
      ln_b.reshape(1, D_MODEL).astype(f32))


def _moe(x1, route, counts, w_gate, w_up, w_down, ln_g, ln_b, *, tm, tf, tt):
    n = x1.shape[0]
    i32 = jnp.int32
    counts = counts[0, :N_EXPERTS].astype(i32)
    padded = (counts + tm - 1) // tm * tm
    ends = jnp.cumsum(padded)
    offsets = ends - padded
    experts = route[:, 0:TOP_K].astype(i32)
    ranks = route[:, TOP_K:2 * TOP_K].astype(i32)
    pos = (offsets[experts] + ranks).reshape(n // tt, 1, TOP_K * tt)
    n_tiles = TOP_K * n // tm + N_EXPERTS
    n_used = (ends[-1] // tm).astype(i32).reshape(1)
    tile_ids = jnp.arange(n_tiles, dtype=i32)
    tile_expert = jnp.sum((tile_ids[:, None] >= (ends // tm)[None, :]).astype(i32), axis=1)
    last_expert = jnp.max(jnp.where(counts > 0, jnp.arange(N_EXPERTS, dtype=i32), 0))
    tile_expert = jnp.minimum(tile_expert, last_expert).astype(i32)
    tail_tiles = n_used[0] + jnp.arange(N_EXPERTS, dtype=i32)
    fill = jnp.concatenate([ends - tm, tail_tiles * tm,
                            (counts > 0).astype(i32), (tail_tiles < n_tiles).astype(i32)]).astype(i32)

    x_sorted = _dispatch(x1, pos, fill, n_tiles * tm, tt=tt, tm=tm)
    y_sorted = _ffn(x_sorted, tile_expert, n_used, w_gate, w_up, w_down, tm=tm, tf=tf,
                    name="ffn_experts")
    return _combine(pos, route, x1, y_sorted, ln_g, ln_b, tt=tt)


def kernel(x, w_in, conv_w, conv_b, w_a, b_a, w_x, b_x, lru_lambda, rel_bias, g_attn, g_lru, w_out, ln1_g, ln1_b, ln2_g, ln2_b, ffn_w_gate, ffn_w_up, ffn_w_down, router_w, moe_w_gate, moe_w_up, moe_w_down):
    batch, seq, _ = x.shape
    n = batch * seq
    h = x.reshape(n, D_MODEL).astype(f32)
    band_bias = jnp.stack([_band_bias(rel_bias, d) for _, d in DILATED_PATTERNS])
    for layer in range(DEPTH):
        qkv, ug = _in_proj(h, w_in[layer].astype(bf16))
        attn = _attention(qkv, band_bias, batch, seq)
        ylru = _lru_branch(ug, conv_w[layer], conv_b[layer], w_a[layer], b_a[layer], w_x[layer],
                           b_x[layer], lru_lambda[layer], g_lru[layer], batch, seq)
        j = layer // 2
        dense = layer % 2 == 0
        x1, route, counts = _mix_out(
            attn, ylru, h, w_out[layer].astype(bf16), g_attn[layer], ln1_g[layer],
            ln1_b[layer], None if dense else router_w[j])
        if dense:
            tm = DENSE_FFN_ROWS
            h = _ffn(x1, jnp.zeros((n // tm,), jnp.int32), jnp.full((1,), n // tm, jnp.int32),
                     ffn_w_gate[j][None].astype(bf16), ffn_w_up[j][None].astype(bf16),
                     ffn_w_down[j][None].astype(bf16), (ln2_g[layer], ln2_b[layer]),
                     tm=tm, tf=ffn_w_gate.shape[2], name="ffn_dense")
        else:
            h = _moe(x1, route, counts, moe_w_gate[j].astype(bf16), moe_w_up[j].astype(bf16),
                     moe_w_down[j].astype(bf16), ln2_g[layer], ln2_b[layer],
                     tm=EXPERT_ROWS, tf=EXPERT_FF_CHUNK, tt=ROUTE_ROWS)
    return h.reshape(batch, seq, D_MODEL).astype(x.dtype)
```

```python
import functools

import numpy as np
import jax
import jax.numpy as jnp
from jax import lax
from jax.experimental import pallas as pl
from jax.experimental.pallas import tpu as pltpu

D_MODEL = 1024
N_HEADS = 8
HEAD_DIM = 64
D_ATTN = N_HEADS * HEAD_DIM
D_LRU = 512
N_LRU_BLOCKS = 8
LRU_BLOCK = D_LRU // N_LRU_BLOCKS
CONV_WIDTH = 4
LRU_C = 8.0
DILATED_PATTERNS = ((128, 1), (512, 4), (2048, 16))
ATTN_BLOCK = 128
STAGE_DILATION = 4
N_BUCKETS = 32
MAX_DISTANCE = 2048
D_IN = 3 * D_ATTN + 2 * D_LRU
N_EXPERTS = 8
TOP_K = 2
DEPTH = 2
DEEPNORM_ALPHA = (2.0 * DEPTH) ** 0.25
LN_EPS = 1e-5
RMS_EPS = 1e-6
NEG_INF = -1e30
LOG2_E = float(np.log2(np.e))

LANES = 128
SUBLANES = 8
VMEM_LIMIT_BYTES = 56 * 1024 * 1024
MXU_DEPTH = 256

LRU_ROWS = 512
MIX_ROWS = 1024
ROUTER_ROWS = 512
DENSE_FFN_ROWS = 512
EXPERT_ROWS = 512
EXPERT_FF_CHUNK = 7 * MXU_DEPTH
ROUTE_ROWS = 512
DMA_UNROLL = 8
ATTN_UNROLL = 4

f32 = jnp.float32
bf16 = jnp.bfloat16


def _params(*semantics):
    return pltpu.CompilerParams(dimension_semantics=semantics,
                                vmem_limit_bytes=VMEM_LIMIT_BYTES)


def _t5_bucket(dist):
    max_exact = N_BUCKETS // 2
    d = np.maximum(dist, 1).astype(np.float32)
    large = max_exact + (np.log(d / max_exact) / np.log(MAX_DISTANCE / max_exact)
                         * (N_BUCKETS - max_exact)).astype(np.int32)
    large = np.minimum(large, N_BUCKETS - 1)
    return np.where(dist < max_exact, dist, large).astype(np.int32)


def _band_bias(rel_bias, dilation):
    nk = ATTN_BLOCK
    qi = np.arange(nk)[:, None]
    kj = np.arange(2 * nk)[None, :]
    delta = qi + nk - kj
    band = (delta >= 0) & (delta <= nk)
    bucket = _t5_bucket(np.clip(delta, 0, nk) * dilation)
    onehot = np.eye(N_BUCKETS, dtype=np.float32)[bucket.reshape(-1)]
    bias = jnp.dot(jnp.asarray(onehot), rel_bias.astype(f32), precision=lax.Precision.HIGHEST)
    bias = jnp.transpose(bias.reshape(nk, 2 * nk, N_HEADS), (2, 0, 1))
    valid = np.stack([band, band & (kj >= nk)])[:, None]
    bias = jnp.where(jnp.asarray(valid), bias[None], NEG_INF)
    return (bias * LOG2_E).reshape(2, N_HEADS // 2, 2 * nk, 2 * nk)


def _attn_kernel(q_ref, k_ref, v_ref, bias_ref, o_ref, q4, k4, v4, qs, ks, vs, s_buf, m_buf,
                 m_s, l_s, acc_s, *, seq):
    nk = ATTN_BLOCK
    n_blocks = seq // nk
    chunk = 2 * nk
    lane = lax.broadcasted_iota(jnp.int32, (1, LANES), 1)
    head0 = lane < HEAD_DIM

    ks[0:nk, :] = jnp.zeros((nk, LANES), bf16)
    vs[0:nk, 0:LANES] = jnp.zeros((nk, LANES), bf16)
    vs[:, LANES:2 * LANES] = jnp.ones((seq + nk, LANES), bf16)

    def stage(t, carry):
        chunks_per_residue = seq // STAGE_DILATION // chunk
        src = pl.ds(t // chunks_per_residue + STAGE_DILATION * chunk * (t % chunks_per_residue),
                    chunk, stride=STAGE_DILATION)
        dst = pl.ds(pl.multiple_of(t * chunk, chunk), chunk)
        q4[dst, :] = q_ref[src, :]
        k4[dst, :] = k_ref[src, :]
        v4[dst, :] = v_ref[src, :]
        return carry

    lax.fori_loop(0, seq // chunk, stage, 0)

    order = sorted(range(len(DILATED_PATTERNS)), key=lambda p: -DILATED_PATTERNS[p][1])
    assert DILATED_PATTERNS[order[-1]][1] == 1
    for p in order:
        d = DILATED_PATTERNS[p][1]
        is_first, is_last = p == order[0], p == order[-1]
        length = seq // d
        nb = length // nk
        chunks_per_residue = length // chunk

        def gather(t, carry):
            r = t // chunks_per_residue
            c = t % chunks_per_residue
            dst = pl.multiple_of(t * chunk, chunk)
            if d % STAGE_DILATION == 0:
                sub = d // STAGE_DILATION
                start = ((r % STAGE_DILATION) * (seq // STAGE_DILATION) + r // STAGE_DILATION
                         + sub * chunk * c)
                src = (pl.ds(start, chunk, stride=sub) if sub > 1
                       else pl.ds(pl.multiple_of(start, chunk), chunk))
                q, k, v = q4[src, :], k4[src, :], v4[src, :]
            else:
                src = pl.ds(r + d * chunk * c, chunk, stride=d) if d > 1 else pl.ds(dst, chunk)
                q, k, v = q_ref[src, :], k_ref[src, :], v_ref[src, :]
            q = q * (HEAD_DIM ** -0.5 * LOG2_E)
            q0 = jnp.where(head0, q, 0.0).astype(bf16)
            q1 = jnp.where(head0, 0.0, q).astype(bf16)
            for half in range(2):
                base = pl.multiple_of(2 * dst + half * chunk, chunk)
                qs[pl.ds(base, nk), :] = q0[half * nk:(half + 1) * nk]
                qs[pl.ds(base + nk, nk), :] = q1[half * nk:(half + 1) * nk]
            ks[pl.ds(nk + dst, chunk), :] = k.astype(bf16)
            vs[pl.ds(nk + dst, chunk), 0:LANES] = v.astype(bf16)
            return carry

        lax.fori_loop(0, seq // chunk, gather, 0)

        def scores(b, carry):
            first = jnp.asarray(b % nb == 0, jnp.int32)
            rows = pl.ds(pl.multiple_of(b * chunk, chunk), chunk)
            k = ks[pl.ds(pl.multiple_of(b * nk, nk), chunk), :]
            s_buf[rows, :] = lax.dot_general(qs[rows, :], k, (((1,), (1,)), ((), ())),
                                             preferred_element_type=f32) + bias_ref[p, first]
            return carry

        def rowmax(b, carry):
            rows = pl.ds(pl.multiple_of(b * chunk, chunk), chunk)
            m_buf[rows, :] = jnp.broadcast_to(jnp.max(s_buf[rows, :], axis=-1, keepdims=True),
                                              (chunk, LANES))
            return carry

        def block(b, carry):
            r = b // nb
            i = b % nb
            rows = pl.ds(pl.multiple_of(b * chunk, chunk), chunk)
            v = vs[pl.ds(pl.multiple_of(b * nk, nk), chunk), :]
            m = m_buf[rows, :]
            e = jnp.exp2(s_buf[rows, :] - jnp.concatenate([m, m], axis=1))
            pv = jnp.dot(e.astype(bf16), v, preferred_element_type=f32)
            m_blk = jnp.where(head0, m[0:nk], m[nk:chunk])
            l_blk = jnp.where(head0, pv[0:nk, LANES:], pv[nk:chunk, LANES:])
            pv_blk = jnp.where(head0, pv[0:nk, 0:LANES], pv[nk:chunk, 0:LANES])
            if d == 1:
                tok = pl.ds(pl.multiple_of(b * nk, nk), nk)
            else:
                tok = pl.ds(r + d * nk * i, nk, stride=d)
            if is_first:
                m_s[tok, :] = m_blk
                l_s[tok, :] = l_blk
                acc_s[tok, :] = pv_blk
                return carry
            m_old = m_s[tok, :]
            m_new = jnp.maximum(m_old, m_blk)
            w_old = jnp.exp2(m_old - m_new)
            w_blk = jnp.exp2(m_blk - m_new)
            l_new = w_old * l_s[tok, :] + w_blk * l_blk
            acc_new = w_old * acc_s[tok, :] + w_blk * pv_blk
            if is_last:
                o_ref[tok, :] = (acc_new / l_new).astype(o_ref.dtype)
            else:
                l_s[tok, :] = l_new
                acc_s[tok, :] = acc_new
                m_s[tok, :] = m_new
            return carry

        n_groups = n_blocks // ATTN_UNROLL
        assert n_groups >= 3

        def group(fn, g):
            for j in range(ATTN_UNROLL):
                fn(g * ATTN_UNROLL + j, 0)

        def pipelined(g, carry):
            group(block, g)
            group(rowmax, g + 1)
            group(scores, g + 2)
            return carry

        group(scores, 0)
        group(scores, 1)
        group(rowmax, 0)
        lax.fori_loop(0, n_groups - 2, pipelined, 0)
        group(block, n_groups - 2)
        group(rowmax, n_groups - 1)
        group(block, n_groups - 1)


def _attention(qkv, band_bias, batch, seq):
    view = qkv.reshape(batch, seq, 3 * D_ATTN)
    pairs = D_ATTN // LANES

    def spec(offset):
        return pl.BlockSpec((None, seq, LANES), lambda b, hp: (b, 0, offset * pairs + hp))

    n_pat = len(DILATED_PATTERNS)
    o = pl.pallas_call(
        functools.partial(_attn_kernel, seq=seq),
        grid=(batch, pairs),
        in_specs=[spec(0), spec(1), spec(2),
                  pl.BlockSpec((n_pat, 2, None, 2 * ATTN_BLOCK, 2 * ATTN_BLOCK),
                               lambda b, hp: (0, 0, hp, 0, 0))],
        out_specs=pl.BlockSpec((None, seq, LANES), lambda b, hp: (b, 0, hp)),
        out_shape=jax.ShapeDtypeStruct((batch, seq, D_ATTN), bf16),
        scratch_shapes=[pltpu.VMEM((seq, LANES), f32), pltpu.VMEM((seq, LANES), f32),
                        pltpu.VMEM((seq, LANES), f32),
                        pltpu.VMEM((2 * seq, LANES), bf16),
                        pltpu.VMEM((seq + ATTN_BLOCK, LANES), bf16),
                        pltpu.VMEM((seq + ATTN_BLOCK, 2 * LANES), bf16),
                        pltpu.VMEM((2 * seq, 2 * ATTN_BLOCK), f32),
                        pltpu.VMEM((2 * seq, LANES), f32),
                        pltpu.VMEM((seq, LANES), f32), pltpu.VMEM((seq, LANES), f32),
                        pltpu.VMEM((seq, LANES), f32)],
        compiler_params=_params("parallel", "parallel"),
        name="attention",
    )(view, view, view, band_bias)
    return o.reshape(batch * seq, D_ATTN)


def _gelu_tanh(x):
    return 0.5 * x * (1.0 + jnp.tanh(np.sqrt(2.0 / np.pi) * (x + 0.044715 * x * x * x)))


def _proj_lru_kernel(x_ref, w_ref, convw_ref, convb_ref, wg_ref, ba_ref, bx_ref, lam_ref, g_ref,
                     qkv_ref, y_ref, ubuf, hcarry, *, ts):
    pad = SUBLANES
    t = pl.program_id(1)
    n_qkv = qkv_ref.shape[1]
    xb = x_ref[...].astype(bf16)

    @pl.when(t == 0)
    def _():
        ubuf[0:pad, :] = jnp.zeros((pad, D_LRU), f32)
        hcarry[...] = jnp.zeros_like(hcarry)

    @pl.when(t > 0)
    def _():
        ubuf[0:pad, :] = ubuf[ts:ts + pad, :]

    ubuf[pad:pad + ts, :] = jnp.dot(xb, w_ref[:, n_qkv:n_qkv + D_LRU],
                                    preferred_element_type=f32)
    gate = jnp.dot(xb, w_ref[:, n_qkv + D_LRU:], preferred_element_type=f32)

    u = convb_ref[...] + convw_ref[CONV_WIDTH - 1:CONV_WIDTH, :] * ubuf[pad:pad + ts, :]
    for w in range(CONV_WIDTH - 1):
        back = CONV_WIDTH - 1 - w
        u = u + convw_ref[w:w + 1, :] * ubuf[pad - back:pad - back + ts, :]

    gates = jnp.dot(u.astype(bf16), wg_ref[...], preferred_element_type=f32)
    r = jax.nn.sigmoid(gates[:, 0:D_LRU] + ba_ref[...])
    i = jax.nn.sigmoid(gates[:, D_LRU:2 * D_LRU] + bx_ref[...])
    neg_lam = -lam_ref[...]
    softplus = jnp.maximum(neg_lam, 0.0) + jnp.log1p(jnp.exp(-jnp.abs(neg_lam)))
    log_a = (-LRU_C) * r * softplus
    a = jnp.exp(log_a)
    b = jnp.sqrt(-jnp.tanh(log_a) * (1.0 + a * a)) * (i * u)

    groups = ts // SUBLANES
    a = a.reshape(groups, SUBLANES, D_LRU)
    b = b.reshape(groups, SUBLANES, D_LRU)
    sub = lax.broadcasted_iota(jnp.int32, (1, SUBLANES, 1), 1)
    shift = 1
    while shift < SUBLANES:
        live = sub >= shift
        a_prev = jnp.where(live, pltpu.roll(a, shift, 1), 1.0)
        b_prev = jnp.where(live, pltpu.roll(b, shift, 1), 0.0)
        b = a * b_prev + b
        a = a * a_prev
        shift *= 2
    h_prev = hcarry[...]
    h_groups = []
    for g in range(groups):
        h_g = a[g] * h_prev + b[g]
        h_prev = h_g[SUBLANES - 1:SUBLANES, :]
        h_groups.append(h_g)
    hcarry[...] = h_prev
    h = jnp.concatenate(h_groups, axis=0)

    y = _gelu_tanh(gate) * h
    y = y * lax.rsqrt(jnp.mean(y * y, axis=-1, keepdims=True) + RMS_EPS) * g_ref[...]
    y_ref[...] = y.astype(y_ref.dtype)

    step = 2 * MXU_DEPTH
    for c in range(0, n_qkv, step):
        qkv_ref[:, c:c + step] = jnp.dot(xb, w_ref[:, c:c + step], preferred_element_type=f32)


def _block_diag(w):
    g, i, j = w.shape
    eye = jnp.eye(g, dtype=w.dtype)
    return jnp.einsum('gij,gh->gihj', w, eye).reshape(g * i, g * j)


def _proj_lru(x2d, w_in_bf16, conv_w, conv_b, w_a, b_a, w_x, b_x, lam, g_lru, batch, seq,
              ts=LRU_ROWS):
    wg = jnp.concatenate([_block_diag(w_a), _block_diag(w_x)], axis=1).astype(bf16)
    row = lambda v: v.reshape(1, D_LRU).astype(f32)
    const = lambda shape: pl.BlockSpec(shape, lambda b, t: (0, 0))
    tile = lambda width: pl.BlockSpec((None, ts, width), lambda b, t: (b, t, 0))
    qkv, y = pl.pallas_call(
        functools.partial(_proj_lru_kernel, ts=ts),
        grid=(batch, seq // ts),
        in_specs=[tile(D_MODEL), const((D_MODEL, D_IN)),
                  const((CONV_WIDTH, D_LRU)), const((1, D_LRU)),
                  const((D_LRU, 2 * D_LRU)), const((1, D_LRU)), const((1, D_LRU)),
                  const((1, D_LRU)), const((1, D_LRU))],
        out_specs=[tile(3 * D_ATTN), tile(D_LRU)],
        out_shape=[jax.ShapeDtypeStruct((batch, seq, 3 * D_ATTN), f32),
                   jax.ShapeDtypeStruct((batch, seq, D_LRU), bf16)],
        scratch_shapes=[pltpu.VMEM((ts + 2 * SUBLANES, D_LRU), f32), pltpu.VMEM((1, D_LRU), f32)],
        compiler_params=_params("parallel", "arbitrary"),
        name="proj_rglru",
    )(x2d.reshape(batch, seq, D_MODEL), w_in_bf16,
      conv_w.reshape(CONV_WIDTH, D_LRU).astype(f32), row(conv_b), wg, row(b_a), row(b_x),
      row(lam), row(g_lru))
    return qkv.reshape(batch * seq, 3 * D_ATTN), y.reshape(batch * seq, D_LRU)


ROW_TILE = D_MODEL // LANES


def _store_row_tiles(ref, rows):
    t = rows.shape[0]
    for s in range(ROW_TILE):
        ref[pl.ds(s, t, stride=ROW_TILE), :] = rows[:, s * LANES:(s + 1) * LANES]


def _load_row_tiles(ref, t):
    return jnp.concatenate([ref[pl.ds(s, t, stride=ROW_TILE), :] for s in range(ROW_TILE)],
                           axis=-1)


def _layer_norm(z, g, b):
    mu = jnp.mean(z, axis=-1, keepdims=True)
    zc = z - mu
    var = jnp.mean(zc * zc, axis=-1, keepdims=True)
    return zc * lax.rsqrt(var + LN_EPS) * g + b


def _mix_out_kernel(attn_ref, ylru, x_ref, w_ref, gattn, lng, lnb, *rest, with_router):
    if with_router:
        rw_ref, tri_ref, x1_ref, route_ref, count_ref = rest
    else:
        (x1_ref,) = rest
    attn = attn_ref[...].astype(f32)
    attn = attn * lax.rsqrt(jnp.mean(attn * attn, axis=-1, keepdims=True) + RMS_EPS) * gattn[...]
    y = jnp.dot(attn.astype(bf16), w_ref[0:D_ATTN, :], preferred_element_type=f32)
    y = y + jnp.dot(ylru[...], w_ref[D_ATTN:, :], preferred_element_type=f32)
    x1 = _layer_norm(DEEPNORM_ALPHA * x_ref[...] + y, lng[...], lnb[...])
    x1_ref[...] = x1
    if with_router:
        lane = lax.broadcasted_iota(jnp.int32, (1, LANES), 1).astype(f32)
        x_hi = x1.astype(bf16)
        x_lo = (x1 - x_hi.astype(f32)).astype(bf16)
        logits = (jnp.dot(x_hi, rw_ref[0], preferred_element_type=f32)
                  + jnp.dot(x_lo, rw_ref[0], preferred_element_type=f32)
                  + jnp.dot(x_hi, rw_ref[1], preferred_element_type=f32))
        logits = jnp.where(lane < N_EXPERTS, logits, -jnp.inf)
        v1 = jnp.max(logits, axis=-1, keepdims=True)
        i1 = jnp.min(jnp.where(logits == v1, lane, float(LANES)), axis=-1, keepdims=True)
        rest_logits = jnp.where(lane == i1, -jnp.inf, logits)
        v2 = jnp.max(rest_logits, axis=-1, keepdims=True)
        i2 = jnp.min(jnp.where(rest_logits == v2, lane, float(LANES)), axis=-1, keepdims=True)
        e2 = jnp.exp(v2 - v1)
        p1 = 1.0 / (1.0 + e2)
        p2 = e2 / (1.0 + e2)

        @pl.when(pl.program_id(0) == 0)
        def _():
            count_ref[...] = jnp.zeros_like(count_ref)

        tm = x1.shape[0]
        chosen = jnp.logical_or(lane == i1, lane == i2)
        rank = count_ref[...] + jnp.dot(tri_ref[...], chosen.astype(bf16),
                                        preferred_element_type=f32)
        count_ref[...] += jnp.sum(chosen.astype(f32), axis=0, keepdims=True)
        r1 = jnp.sum(jnp.where(lane == i1, rank, 0.0), axis=-1, keepdims=True)
        r2 = jnp.sum(jnp.where(lane == i2, rank, 0.0), axis=-1, keepdims=True)
        fields = (i1, i2, r1, r2, p1, p2)
        route = jnp.zeros((tm, LANES), f32)
        for k, val in enumerate(fields):
            route = jnp.where(lane == k, val, route)
        route_ref[...] = route


def _mix_out(attn, ylru, x2d, w_out_bf16, g_attn, ln_g, ln_b, router_w=None):
    n = x2d.shape[0]
    with_router = router_w is not None
    tm = ROUTER_ROWS if with_router else MIX_ROWS
    tile = lambda width: pl.BlockSpec((tm, width), lambda i: (i, 0))
    const = lambda shape: pl.BlockSpec(shape, lambda i: (0, 0))
    in_specs = [tile(D_ATTN), tile(D_LRU), tile(D_MODEL), const((D_MODEL, D_MODEL)),
                const((1, D_ATTN)), const((1, D_MODEL)), const((1, D_MODEL))]
    args = [attn, ylru, x2d, w_out_bf16, g_attn.reshape(1, D_ATTN).astype(f32),
            ln_g.reshape(1, D_MODEL).astype(f32), ln_b.reshape(1, D_MODEL).astype(f32)]
    out_specs = [tile(D_MODEL)]
    out_shape = [jax.ShapeDtypeStruct((n, D_MODEL), f32)]
    if with_router:
        rw = jnp.zeros((D_MODEL, LANES), f32).at[:, :N_EXPERTS].set(router_w.astype(f32))
        rw_hi = rw.astype(bf16)
        rw_lo = (rw - rw_hi.astype(f32)).astype(bf16)
        strictly_lower = jnp.asarray(np.tri(tm, k=-1), bf16)
        in_specs += [pl.BlockSpec((2, D_MODEL, LANES), lambda i: (0, 0, 0)), const((tm, tm))]
        args += [jnp.stack([rw_hi, rw_lo]), strictly_lower]
        out_specs += [tile(LANES), const((1, LANES))]
        out_shape += [jax.ShapeDtypeStruct((n, LANES), f32), jax.ShapeDtypeStruct((1, LANES), f32)]
    res = pl.pallas_call(
        functools.partial(_mix_out_kernel, with_router=with_router),
        grid=(n // tm,),
        in_specs=in_specs,
        out_specs=out_specs,
        out_shape=out_shape,
        compiler_params=_params("arbitrary" if with_router else "parallel"),
        name="mix_out_router" if with_router else "mix_out",
    )(*args)
    return res if with_router else (res[0], None, None)


def _ffn_kernel(tile_expert_ref, n_used_ref, x_ref, wg_ref, wu_ref, wd_ref, *rest,
                n_chunks, fuse_ln):
    if fuse_ln:
        lng, lnb, out_ref, acc_ref, xb_ref = rest
    else:
        out_ref, acc_ref, xb_ref = rest
    del tile_expert_ref
    j = pl.program_id(1)
    used = pl.program_id(0) < n_used_ref[0]
    last = n_chunks - 1

    def load_x():
        if fuse_ln:
            return x_ref[...].astype(bf16)
        return _load_row_tiles(x_ref, xb_ref.shape[0]).astype(bf16)

    def chunk_out(xb):
        g = jnp.dot(xb, wg_ref[...], preferred_element_type=f32)
        u = jnp.dot(xb, wu_ref[...], preferred_element_type=f32)
        h = (g * jax.nn.sigmoid(g)) * u
        return jnp.dot(h.astype(bf16), wd_ref[...], preferred_element_type=f32)

    def finish(y):
        if fuse_ln:
            out_ref[...] = _layer_norm(DEEPNORM_ALPHA * x_ref[...] + y, lng[...], lnb[...])
        else:
            _store_row_tiles(out_ref, y)

    if n_chunks == 1:
        @pl.when(used)
        def _():
            finish(chunk_out(load_x()))
    else:
        @pl.when(used & (j == 0))
        def _():
            xb = load_x()
            xb_ref[...] = xb
            acc_ref[...] = chunk_out(xb)

        if n_chunks > 2:
            @pl.when(used & (j > 0) & (j < last))
            def _():
                acc_ref[...] += chunk_out(xb_ref[...])

        @pl.when(used & (j == last))
        def _():
            finish(acc_ref[...] + chunk_out(xb_ref[...]))

    @pl.when(jnp.logical_not(used) & (j == last))
    def _():
        finish(jnp.zeros(acc_ref.shape, f32))


def _ffn(x2d, tile_expert, n_used, w_gate, w_up, w_down, ln=None, *, tm, tf, name):
    fuse_ln = ln is not None
    rows = x2d.shape[0] if fuse_ln else x2d.shape[0] // ROW_TILE
    d_ff = w_gate.shape[2]
    n_chunks = d_ff // tf
    io_block = (tm, D_MODEL) if fuse_ln else (tm * ROW_TILE, LANES)

    def chunk(i, j, nu):
        return jnp.where(i < nu[0], j, n_chunks - 1)

    resident = w_gate.shape[0] == 1 and n_chunks == 1
    mode = dict(pipeline_mode=pl.Buffered(1)) if resident else {}
    in_specs = [pl.BlockSpec(io_block, lambda i, j, te, nu: (i, 0)),
                pl.BlockSpec((None, D_MODEL, tf),
                             lambda i, j, te, nu: (te[i], 0, chunk(i, j, nu)), **mode),
                pl.BlockSpec((None, D_MODEL, tf),
                             lambda i, j, te, nu: (te[i], 0, chunk(i, j, nu)), **mode),
                pl.BlockSpec((None, tf, D_MODEL),
                             lambda i, j, te, nu: (te[i], chunk(i, j, nu), 0), **mode)]
    args = [x2d, w_gate, w_up, w_down]
    if fuse_ln:
        in_specs += [pl.BlockSpec((1, D_MODEL), lambda i, j, te, nu: (0, 0))] * 2
        args += [v.reshape(1, D_MODEL).astype(f32) for v in ln]
    return pl.pallas_call(
        functools.partial(_ffn_kernel, n_chunks=n_chunks, fuse_ln=fuse_ln),
        grid_spec=pltpu.PrefetchScalarGridSpec(
            num_scalar_prefetch=2,
            grid=(rows // tm, n_chunks),
            in_specs=in_specs,
            out_specs=pl.BlockSpec(io_block, lambda i, j, te, nu: (i, 0)),
            scratch_shapes=[pltpu.VMEM((tm, D_MODEL), f32), pltpu.VMEM((tm, D_MODEL), bf16)]),
        out_shape=jax.ShapeDtypeStruct(x2d.shape, f32),
        compiler_params=_params("parallel", "arbitrary"),
        name=name,
    )(tile_expert, n_used, *args)


def _dispatch_kernel(fill_ref, pos_ref, x_ref, xs_hbm, zero_buf, rows_buf, sem, *, tt, tm):
    step = pl.program_id(0)

    n_fill = fill_ref.shape[0] // 2

    @pl.when(step == 0)
    def _():
        zero_buf[...] = jnp.zeros_like(zero_buf)

        def fill_copy(f):
            start = pl.multiple_of(fill_ref[f] * ROW_TILE, ROW_TILE)
            return pltpu.make_async_copy(zero_buf, xs_hbm.at[pl.ds(start, tm * ROW_TILE)], sem)

        for f in range(n_fill):
            @pl.when(fill_ref[n_fill + f] > 0)
            def _():
                fill_copy(f).start()
        for f in range(n_fill):
            @pl.when(fill_ref[n_fill + f] > 0)
            def _():
                fill_copy(f).wait()

    _store_row_tiles(rows_buf, x_ref[...])

    def issue(t, carry):
        src = pl.multiple_of(t * ROW_TILE, ROW_TILE)
        for k in range(TOP_K):
            dst = pl.multiple_of(pos_ref[0, TOP_K * t + k] * ROW_TILE, ROW_TILE)
            pltpu.make_async_copy(rows_buf.at[pl.ds(src, ROW_TILE)],
                                  xs_hbm.at[pl.ds(dst, ROW_TILE)], sem).start(priority=k)
        return carry

    lax.fori_loop(0, tt, issue, 0, unroll=DMA_UNROLL)
    for _ in range(TOP_K):
        pltpu.make_async_copy(rows_buf, xs_hbm.at[pl.ds(0, tt * ROW_TILE)], sem).wait()


def _dispatch(x1, pos_blocks, fill, rows_sorted, *, tt, tm):
    n = x1.shape[0]
    return pl.pallas_call(
        functools.partial(_dispatch_kernel, tt=tt, tm=tm),
        grid_spec=pltpu.PrefetchScalarGridSpec(
            num_scalar_prefetch=1,
            grid=(n // tt,),
            in_specs=[pl.BlockSpec((None, 1, TOP_K * tt), lambda i, fill: (i, 0, 0),
                                   memory_space=pltpu.SMEM),
                      pl.BlockSpec((tt, D_MODEL), lambda i, fill: (i, 0))],
            out_specs=pl.BlockSpec(memory_space=pl.ANY),
            scratch_shapes=[pltpu.VMEM((tm * ROW_TILE, LANES), f32),
                            pltpu.VMEM((tt * ROW_TILE, LANES), f32),
                            pltpu.SemaphoreType.DMA(())]),
        out_shape=jax.ShapeDtypeStruct((rows_sorted * ROW_TILE, LANES), f32),
        compiler_params=_params("arbitrary"),
        name="moe_dispatch",
    )(fill, pos_blocks, x1)


def _combine_kernel(pos_ref, pos_next_ref, route_ref, x1_ref, ys_hbm, lng, lnb, out_ref, buf, sems,
                    *, tt, n_steps):
    step = pl.program_id(0)
    slot = step % 2

    def issue_tile(tile_pos_ref, into):
        def issue(t, carry):
            dst = pl.multiple_of(t * ROW_TILE, ROW_TILE)
            for k in range(TOP_K):
                src = pl.multiple_of(tile_pos_ref[0, TOP_K * t + k] * ROW_TILE, ROW_TILE)
                pltpu.make_async_copy(ys_hbm.at[pl.ds(src, ROW_TILE)],
                                      buf.at[into, k, pl.ds(dst, ROW_TILE)],
                                      sems.at[into]).start(priority=k)
            return carry

        lax.fori_loop(0, tt, issue, 0, unroll=DMA_UNROLL)

    @pl.when(step == 0)
    def _():
        issue_tile(pos_ref, 0)

    @pl.when(step + 1 < n_steps)
    def _():
        issue_tile(pos_next_ref, 1 - slot)

    for k in range(TOP_K):
        pltpu.make_async_copy(ys_hbm.at[pl.ds(0, tt * ROW_TILE)], buf.at[slot, k],
                              sems.at[slot]).wait()

    lane = lax.broadcasted_iota(jnp.int32, (1, LANES), 1)
    route = route_ref[...]
    p1 = jnp.sum(jnp.where(lane == 4, route, 0.0), axis=-1, keepdims=True)
    p2 = jnp.sum(jnp.where(lane == 5, route, 0.0), axis=-1, keepdims=True)
    y = p1 * _load_row_tiles(buf.at[slot, 0], tt) + p2 * _load_row_tiles(buf.at[slot, 1], tt)
    out_ref[...] = _layer_norm(DEEPNORM_ALPHA * x1_ref[...] + y, lng[...], lnb[...])


def _combine(pos_blocks, route, x1, y_sorted, ln_g, ln_b, *, tt):
    n = x1.shape[0]
    n_steps = n // tt
    pos_spec = lambda index: pl.BlockSpec((None, 1, TOP_K * tt), index, memory_space=pltpu.SMEM)
    return pl.pallas_call(
        functools.partial(_combine_kernel, tt=tt, n_steps=n_steps),
        grid=(n_steps,),
        in_specs=[pos_spec(lambda i: (i, 0, 0)),
                  pos_spec(lambda i: (jnp.minimum(i + 1, n_steps - 1), 0, 0)),
                  pl.BlockSpec((tt, LANES), lambda i: (i, 0)),
                  pl.BlockSpec((tt, D_MODEL), lambda i: (i, 0)),
                  pl.BlockSpec(memory_space=pl.ANY),
                  pl.BlockSpec((1, D_MODEL), lambda i: (0, 0)),
                  pl.BlockSpec((1, D_MODEL), lambda i: (0, 0))],
        out_specs=pl.BlockSpec((tt, D_MODEL), lambda i: (i, 0)),
        out_shape=jax.ShapeDtypeStruct((n, D_MODEL), f32),
        scratch_shapes=[pltpu.VMEM((2, TOP_K, tt * ROW_TILE, LANES), f32),
                        pltpu.SemaphoreType.DMA((2,))],
        compiler_params=_params("arbitrary"),
        name="moe_combine",
    )(pos_blocks, pos_blocks, route, x1, y_sorted, ln_g.reshape(1, D_MODEL).astype(f32),
      ln_b.reshape(1, D_MODEL).astype(f32))


def _moe(x1, route, counts, w_gate, w_up, w_down, ln_g, ln_b, *, tm, tf, tt):
    n = x1.shape[0]
    i32 = jnp.int32
    counts = counts[0, :N_EXPERTS].astype(i32)
    padded = (counts + tm - 1) // tm * tm
    ends = jnp.cumsum(padded)
    offsets = ends - padded
    experts = route[:, 0:TOP_K].astype(i32)
    ranks = route[:, TOP_K:2 * TOP_K].astype(i32)
    pos = (offsets[experts] + ranks).reshape(n // tt, 1, TOP_K * tt)
    n_tiles = TOP_K * n // tm + N_EXPERTS
    n_used = (ends[-1] // tm).astype(i32).reshape(1)
    tile_ids = jnp.arange(n_tiles, dtype=i32)
    tile_expert = jnp.sum((tile_ids[:, None] >= (ends // tm)[None, :]).astype(i32), axis=1)
    last_expert = jnp.max(jnp.where(counts > 0, jnp.arange(N_EXPERTS, dtype=i32), 0))
    tile_expert = jnp.minimum(tile_expert, last_expert).astype(i32)
    tail_tiles = n_used[0] + jnp.arange(N_EXPERTS, dtype=i32)
    fill = jnp.concatenate([ends - tm, tail_tiles * tm,
                            (counts > 0).astype(i32), (tail_tiles < n_tiles).astype(i32)]).astype(i32)

    x_sorted = _dispatch(x1, pos, fill, n_tiles * tm, tt=tt, tm=tm)
    y_sorted = _ffn(x_sorted, tile_expert, n_used, w_gate, w_up, w_down, tm=tm, tf=tf,
                    name="ffn_experts")
    return _combine(pos, route, x1, y_sorted, ln_g, ln_b, tt=tt)


def kernel(x, w_in, conv_w, conv_b, w_a, b_a, w_x, b_x, lru_lambda, rel_bias, g_attn, g_lru, w_out, ln1_g, ln1_b, ln2_g, ln2_b, ffn_w_gate, ffn_w_up, ffn_w_down, router_w, moe_w_gate, moe_w_up, moe_w_down):
    batch, seq, _ = x.shape
    n = batch * seq
    h = x.reshape(n, D_MODEL).astype(f32)
    band_bias = jnp.stack([_band_bias(rel_bias, d) for _, d in DILATED_PATTERNS])
    for layer in range(DEPTH):
        qkv, ylru = _proj_lru(h, w_in[layer].astype(bf16), conv_w[layer], conv_b[layer],
                              w_a[layer], b_a[layer], w_x[layer], b_x[layer], lru_lambda[layer],
                              g_lru[layer], batch, seq)
        attn = _attention(qkv, band_bias, batch, seq)
        j = layer // 2
        dense = layer % 2 == 0
        x1, route, counts = _mix_out(
            attn, ylru, h, w_out[layer].astype(bf16), g_attn[layer], ln1_g[layer],
            ln1_b[layer], None if dense else router_w[j])
        if dense:
            tm = DENSE_FFN_ROWS
            h = _ffn(x1, jnp.zeros((n // tm,), jnp.int32), jnp.full((1,), n // tm, jnp.int32),
                     ffn_w_gate[j][None].astype(bf16), ffn_w_up[j][None].astype(bf16),
                     ffn_w_down[j][None].astype(bf16), (ln2_g[layer], ln2_b[layer]),
                     tm=tm, tf=ffn_w_gate.shape[2], name="ffn_dense")
        else:
            h = _moe(x1, route, counts, moe_w_gate[j].astype(bf16), moe_w_up[j].astype(bf16),
                     moe_w_down[j].astype(bf16), ln2_g[layer], ln2_b[layer],
                     tm=EXPERT_ROWS, tf=EXPERT_FF_CHUNK, tt=ROUTE_ROWS)
    return h.reshape(batch, seq, D_MODEL).astype(x.dtype)
```

```python
import functools

import numpy as np
import jax
import jax.numpy as jnp
from jax import lax
from jax.experimental import pallas as pl
from jax.experimental.pallas import tpu as pltpu

D_MODEL = 1024
N_HEADS = 8
HEAD_DIM = 64
D_ATTN = N_HEADS * HEAD_DIM
D_LRU = 512
N_LRU_BLOCKS = 8
LRU_BLOCK = D_LRU // N_LRU_BLOCKS
CONV_WIDTH = 4
LRU_C = 8.0
DILATED_PATTERNS = ((128, 1), (512, 4), (2048, 16))
ATTN_BLOCK = 128
STAGE_DILATION = 4
N_BUCKETS = 32
MAX_DISTANCE = 2048
D_IN = 3 * D_ATTN + 2 * D_LRU
N_EXPERTS = 8
TOP_K = 2
DEPTH = 2
DEEPNORM_ALPHA = (2.0 * DEPTH) ** 0.25
LN_EPS = 1e-5
RMS_EPS = 1e-6
NEG_INF = -1e30
LOG2_E = float(np.log2(np.e))

LANES = 128
SUBLANES = 8
VMEM_LIMIT_BYTES = 56 * 1024 * 1024
MXU_DEPTH = 256

LRU_ROWS = 512
MIX_ROWS = 1024
ROUTER_ROWS = 512
DENSE_FFN_ROWS = 512
EXPERT_ROWS = 512
EXPERT_FF_CHUNK = 7 * MXU_DEPTH
ROUTE_ROWS = 512
DMA_UNROLL = 8
ATTN_UNROLL = 4

f32 = jnp.float32
bf16 = jnp.bfloat16


def _params(*semantics):
    return pltpu.CompilerParams(dimension_semantics=semantics,
                                vmem_limit_bytes=VMEM_LIMIT_BYTES)


def _t5_bucket(dist):
    max_exact = N_BUCKETS // 2
    d = np.maximum(dist, 1).astype(np.float32)
    large = max_exact + (np.log(d / max_exact) / np.log(MAX_DISTANCE / max_exact)
                         * (N_BUCKETS - max_exact)).astype(np.int32)
    large = np.minimum(large, N_BUCKETS - 1)
    return np.where(dist < max_exact, dist, large).astype(np.int32)


def _band_bias(rel_bias, dilation):
    nk = ATTN_BLOCK
    qi = np.arange(nk)[:, None]
    kj = np.arange(2 * nk)[None, :]
    delta = qi + nk - kj
    band = (delta >= 0) & (delta <= nk)
    bucket = _t5_bucket(np.clip(delta, 0, nk) * dilation)
    onehot = np.eye(N_BUCKETS, dtype=np.float32)[bucket.reshape(-1)]
    bias = jnp.dot(jnp.asarray(onehot), rel_bias.astype(f32), precision=lax.Precision.HIGHEST)
    bias = jnp.transpose(bias.reshape(nk, 2 * nk, N_HEADS), (2, 0, 1))
    valid = np.stack([band, band & (kj >= nk)])[:, None]
    bias = jnp.where(jnp.asarray(valid), bias[None], NEG_INF)
    return (bias * LOG2_E).reshape(2, N_HEADS // 2, 2 * nk, 2 * nk)


def _attn_kernel(q_ref, k_ref, v_ref, bias_ref, o_ref, q4, k4, v4, qs, ks, vs, s_buf, m_buf,
                 m_s, l_s, acc_s, *, seq):
    nk = ATTN_BLOCK
    n_blocks = seq // nk
    chunk = 2 * nk
    lane = lax.broadcasted_iota(jnp.int32, (1, LANES), 1)
    head0 = lane < HEAD_DIM

    ks[0:nk, :] = jnp.zeros((nk, LANES), bf16)
    vs[0:nk, 0:LANES] = jnp.zeros((nk, LANES), bf16)
    vs[:, LANES:2 * LANES] = jnp.ones((seq + nk, LANES), bf16)

    def stage(t, carry):
        chunks_per_residue = seq // STAGE_DILATION // chunk
        src = pl.ds(t // chunks_per_residue + STAGE_DILATION * chunk * (t % chunks_per_residue),
                    chunk, stride=STAGE_DILATION)
        dst = pl.ds(pl.multiple_of(t * chunk, chunk), chunk)
        q4[dst, :] = q_ref[src, :]
        k4[dst, :] = k_ref[src, :]
        v4[dst, :] = v_ref[src, :]
        return carry

    lax.fori_loop(0, seq // chunk, stage, 0)

    order = sorted(range(len(DILATED_PATTERNS)), key=lambda p: -DILATED_PATTERNS[p][1])
    assert DILATED_PATTERNS[order[-1]][1] == 1
    for p in order:
        d = DILATED_PATTERNS[p][1]
        is_first, is_last = p == order[0], p == order[-1]
        length = seq // d
        nb = length // nk
        chunks_per_residue = length // chunk

        def gather(t, carry):
            r = t // chunks_per_residue
            c = t % chunks_per_residue
            dst = pl.multiple_of(t * chunk, chunk)
            if d % STAGE_DILATION == 0:
                sub = d // STAGE_DILATION
                start = ((r % STAGE_DILATION) * (seq // STAGE_DILATION) + r // STAGE_DILATION
                         + sub * chunk * c)
                src = (pl.ds(start, chunk, stride=sub) if sub > 1
                       else pl.ds(pl.multiple_of(start, chunk), chunk))
                q, k, v = q4[src, :], k4[src, :], v4[src, :]
            else:
                src = pl.ds(r + d * chunk * c, chunk, stride=d) if d > 1 else pl.ds(dst, chunk)
                q, k, v = q_ref[src, :], k_ref[src, :], v_ref[src, :]
            q = q * (HEAD_DIM ** -0.5 * LOG2_E)
            q0 = jnp.where(head0, q, 0.0).astype(bf16)
            q1 = jnp.where(head0, 0.0, q).astype(bf16)
            for half in range(2):
                base = pl.multiple_of(2 * dst + half * chunk, chunk)
                qs[pl.ds(base, nk), :] = q0[half * nk:(half + 1) * nk]
                qs[pl.ds(base + nk, nk), :] = q1[half * nk:(half + 1) * nk]
            ks[pl.ds(nk + dst, chunk), :] = k.astype(bf16)
            vs[pl.ds(nk + dst, chunk), 0:LANES] = v.astype(bf16)
            return carry

        lax.fori_loop(0, seq // chunk, gather, 0)

        def scores(b, carry):
            first = jnp.asarray(b % nb == 0, jnp.int32)
            rows = pl.ds(pl.multiple_of(b * chunk, chunk), chunk)
            k = ks[pl.ds(pl.multiple_of(b * nk, nk), chunk), :]
            s_buf[rows, :] = lax.dot_general(qs[rows, :], k, (((1,), (1,)), ((), ())),
                                             preferred_element_type=f32) + bias_ref[p, first]
            return carry

        def rowmax(b, carry):
            rows = pl.ds(pl.multiple_of(b * chunk, chunk), chunk)
            m_buf[rows, :] = jnp.broadcast_to(jnp.max(s_buf[rows, :], axis=-1, keepdims=True),
                                              (chunk, LANES))
            return carry

        def block(b, carry):
            r = b // nb
            i = b % nb
            rows = pl.ds(pl.multiple_of(b * chunk, chunk), chunk)
            v = vs[pl.ds(pl.multiple_of(b * nk, nk), chunk), :]
            m = m_buf[rows, :]
            e = jnp.exp2(s_buf[rows, :] - jnp.concatenate([m, m], axis=1))
            pv = jnp.dot(e.astype(bf16), v, preferred_element_type=f32)
            m_blk = jnp.where(head0, m[0:nk], m[nk:chunk])
            l_blk = jnp.where(head0, pv[0:nk, LANES:], pv[nk:chunk, LANES:])
            pv_blk = jnp.where(head0, pv[0:nk, 0:LANES], pv[nk:chunk, 0:LANES])
            if d == 1:
                tok = pl.ds(pl.multiple_of(b * nk, nk), nk)
            else:
                tok = pl.ds(r + d * nk * i, nk, stride=d)
            if is_first:
                m_s[tok, :] = m_blk
                l_s[tok, :] = l_blk
                acc_s[tok, :] = pv_blk
                return carry
            m_old = m_s[tok, :]
            m_new = jnp.maximum(m_old, m_blk)
            w_old = jnp.exp2(m_old - m_new)
            w_blk = jnp.exp2(m_blk - m_new)
            l_new = w_old * l_s[tok, :] + w_blk * l_blk
            acc_new = w_old * acc_s[tok, :] + w_blk * pv_blk
            if is_last:
                o_ref[tok, :] = (acc_new / l_new).astype(o_ref.dtype)
            else:
                l_s[tok, :] = l_new
                acc_s[tok, :] = acc_new
                m_s[tok, :] = m_new
            return carry

        n_groups = n_blocks // ATTN_UNROLL
        assert n_groups >= 3

        def group(fn, g):
            for j in range(ATTN_UNROLL):
                fn(g * ATTN_UNROLL + j, 0)

        def pipelined(g, carry):
            group(block, g)
            group(rowmax, g + 1)
            group(scores, g + 2)
            return carry

        group(scores, 0)
        group(scores, 1)
        group(rowmax, 0)
        lax.fori_loop(0, n_groups - 2, pipelined, 0)
        group(block, n_groups - 2)
        group(rowmax, n_groups - 1)
        group(block, n_groups - 1)


def _attention(qkv, band_bias, batch, seq):
    view = qkv.reshape(batch, seq, 3 * D_ATTN)
    pairs = D_ATTN // LANES

    def spec(offset):
        return pl.BlockSpec((None, seq, LANES), lambda b, hp: (b, 0, offset * pairs + hp))

    n_pat = len(DILATED_PATTERNS)
    o = pl.pallas_call(
        functools.partial(_attn_kernel, seq=seq),
        grid=(batch, pairs),
        in_specs=[spec(0), spec(1), spec(2),
                  pl.BlockSpec((n_pat, 2, None, 2 * ATTN_BLOCK, 2 * ATTN_BLOCK),
                               lambda b, hp: (0, 0, hp, 0, 0))],
        out_specs=pl.BlockSpec((None, seq, LANES), lambda b, hp: (b, 0, hp)),
        out_shape=jax.ShapeDtypeStruct((batch, seq, D_ATTN), bf16),
        scratch_shapes=[pltpu.VMEM((seq, LANES), f32), pltpu.VMEM((seq, LANES), f32),
                        pltpu.VMEM((seq, LANES), f32),
                        pltpu.VMEM((2 * seq, LANES), bf16),
                        pltpu.VMEM((seq + ATTN_BLOCK, LANES), bf16),
                        pltpu.VMEM((seq + ATTN_BLOCK, 2 * LANES), bf16),
                        pltpu.VMEM((2 * seq, 2 * ATTN_BLOCK), f32),
                        pltpu.VMEM((2 * seq, LANES), f32),
                        pltpu.VMEM((seq, LANES), f32), pltpu.VMEM((seq, LANES), f32),
                        pltpu.VMEM((seq, LANES), f32)],
        compiler_params=_params("parallel", "parallel"),
        name="attention",
    )(view, view, view, band_bias)
    return o.reshape(batch * seq, D_ATTN)


def _gelu_tanh(x):
    return 0.5 * x * (1.0 + jnp.tanh(np.sqrt(2.0 / np.pi) * (x + 0.044715 * x * x * x)))


def _proj_lru_kernel(x_ref, w_ref, convw_ref, convb_ref, wg_ref, ba_ref, bx_ref, lam_ref, g_ref,
                     qkv_ref, y_ref, ubuf, hcarry, *, ts):
    pad = SUBLANES
    t = pl.program_id(1)
    n_qkv = qkv_ref.shape[1]
    xb = x_ref[...].astype(bf16)

    @pl.when(t == 0)
    def _():
        ubuf[0:pad, :] = jnp.zeros((pad, D_LRU), f32)
        hcarry[...] = jnp.zeros_like(hcarry)

    @pl.when(t > 0)
    def _():
        ubuf[0:pad, :] = ubuf[ts:ts + pad, :]

    ubuf[pad:pad + ts, :] = jnp.dot(xb, w_ref[:, n_qkv:n_qkv + D_LRU],
                                    preferred_element_type=f32)
    gate = jnp.dot(xb, w_ref[:, n_qkv + D_LRU:], preferred_element_type=f32)

    qkv_step = n_qkv // 3
    assert qkv_step == D_LRU

    def qkv_chunk(c):
        cols = slice(c * qkv_step, (c + 1) * qkv_step)
        p = jnp.dot(xb, w_ref[:, cols], preferred_element_type=f32)
        qkv_ref[:, cols] = p
        bits = lax.bitcast_convert_type(p, jnp.uint32).reshape(ts // SUBLANES, SUBLANES, qkv_step)
        folded = functools.reduce(jnp.bitwise_or, [bits[g] for g in range(ts // SUBLANES)])
        return ((folded >> 16) >> 16).astype(f32)

    u = convb_ref[...] + convw_ref[CONV_WIDTH - 1:CONV_WIDTH, :] * ubuf[pad:pad + ts, :]
    for w in range(CONV_WIDTH - 1):
        back = CONV_WIDTH - 1 - w
        u = u + convw_ref[w:w + 1, :] * ubuf[pad - back:pad - back + ts, :]

    gates = jnp.dot(u.astype(bf16), wg_ref[...], preferred_element_type=f32)
    r = jax.nn.sigmoid(gates[:, 0:D_LRU] + ba_ref[...])
    i = jax.nn.sigmoid(gates[:, D_LRU:2 * D_LRU] + bx_ref[...])
    neg_lam = -lam_ref[...]
    softplus = jnp.maximum(neg_lam, 0.0) + jnp.log1p(jnp.exp(-jnp.abs(neg_lam)))
    log_a = (-LRU_C) * r * softplus
    a = jnp.exp(log_a)
    b = jnp.sqrt(-jnp.tanh(log_a) * (1.0 + a * a)) * (i * u)
    zero0 = qkv_chunk(0)

    groups = ts // SUBLANES
    a = a.reshape(groups, SUBLANES, D_LRU)
    b = b.reshape(groups, SUBLANES, D_LRU) + zero0[None]
    sub = lax.broadcasted_iota(jnp.int32, (1, SUBLANES, 1), 1)
    shift = 1
    while shift < SUBLANES:
        live = sub >= shift
        a_prev = jnp.where(live, pltpu.roll(a, shift, 1), 1.0)
        b_prev = jnp.where(live, pltpu.roll(b, shift, 1), 0.0)
        b = a * b_prev + b
        a = a * a_prev
        shift *= 2
    h_prev = hcarry[...] + qkv_chunk(1)[0:1, :]
    h_groups = []
    for g in range(groups):
        h_g = a[g] * h_prev + b[g]
        h_prev = h_g[SUBLANES - 1:SUBLANES, :]
        h_groups.append(h_g)
    hcarry[...] = h_prev
    h = jnp.concatenate(h_groups, axis=0)
    zero2 = qkv_chunk(2)[0:1, 0:1]

    y = _gelu_tanh(gate) * h
    y = y * lax.rsqrt(jnp.mean(y * y, axis=-1, keepdims=True) + (RMS_EPS + zero2)) * g_ref[...]
    y_ref[...] = y.astype(y_ref.dtype)


def _block_diag(w):
    g, i, j = w.shape
    eye = jnp.eye(g, dtype=w.dtype)
    return jnp.einsum('gij,gh->gihj', w, eye).reshape(g * i, g * j)


def _proj_lru(x2d, w_in_bf16, conv_w, conv_b, w_a, b_a, w_x, b_x, lam, g_lru, batch, seq,
              ts=LRU_ROWS):
    wg = jnp.concatenate([_block_diag(w_a), _block_diag(w_x)], axis=1).astype(bf16)
    row = lambda v: v.reshape(1, D_LRU).astype(f32)
    const = lambda shape: pl.BlockSpec(shape, lambda b, t: (0, 0))
    tile = lambda width: pl.BlockSpec((None, ts, width), lambda b, t: (b, t, 0))
    qkv, y = pl.pallas_call(
        functools.partial(_proj_lru_kernel, ts=ts),
        grid=(batch, seq // ts),
        in_specs=[tile(D_MODEL), const((D_MODEL, D_IN)),
                  const((CONV_WIDTH, D_LRU)), const((1, D_LRU)),
                  const((D_LRU, 2 * D_LRU)), const((1, D_LRU)), const((1, D_LRU)),
                  const((1, D_LRU)), const((1, D_LRU))],
        out_specs=[tile(3 * D_ATTN), tile(D_LRU)],
        out_shape=[jax.ShapeDtypeStruct((batch, seq, 3 * D_ATTN), f32),
                   jax.ShapeDtypeStruct((batch, seq, D_LRU), bf16)],
        scratch_shapes=[pltpu.VMEM((ts + 2 * SUBLANES, D_LRU), f32), pltpu.VMEM((1, D_LRU), f32)],
        compiler_params=_params("parallel", "arbitrary"),
        name="proj_rglru",
    )(x2d.reshape(batch, seq, D_MODEL), w_in_bf16,
      conv_w.reshape(CONV_WIDTH, D_LRU).astype(f32), row(conv_b), wg, row(b_a), row(b_x),
      row(lam), row(g_lru))
    return qkv.reshape(batch * seq, 3 * D_ATTN), y.reshape(batch * seq, D_LRU)


ROW_TILE = D_MODEL // LANES


def _store_row_tiles(ref, rows):
    t = rows.shape[0]
    for s in range(ROW_TILE):
        ref[pl.ds(s, t, stride=ROW_TILE), :] = rows[:, s * LANES:(s + 1) * LANES]


def _load_row_tiles(ref, t):
    return jnp.concatenate([ref[pl.ds(s, t, stride=ROW_TILE), :] for s in range(ROW_TILE)],
                           axis=-1)


def _layer_norm(z, g, b):
    mu = jnp.mean(z, axis=-1, keepdims=True)
    zc = z - mu
    var = jnp.mean(zc * zc, axis=-1, keepdims=True)
    return zc * lax.rsqrt(var + LN_EPS) * g + b


def _mix_out_kernel(attn_ref, ylru, x_ref, w_ref, gattn, lng, lnb, *rest, with_router):
    if with_router:
        rw_ref, tri_ref, x1_ref, route_ref, count_ref = rest
    else:
        (x1_ref,) = rest
    attn = attn_ref[...].astype(f32)
    attn = attn * lax.rsqrt(jnp.mean(attn * attn, axis=-1, keepdims=True) + RMS_EPS) * gattn[...]
    y = jnp.dot(attn.astype(bf16), w_ref[0:D_ATTN, :], preferred_element_type=f32)
    y = y + jnp.dot(ylru[...], w_ref[D_ATTN:, :], preferred_element_type=f32)
    x1 = _layer_norm(DEEPNORM_ALPHA * x_ref[...] + y, lng[...], lnb[...])
    x1_ref[...] = x1
    if with_router:
        lane = lax.broadcasted_iota(jnp.int32, (1, LANES), 1).astype(f32)
        x_hi = x1.astype(bf16)
        x_lo = (x1 - x_hi.astype(f32)).astype(bf16)
        logits = (jnp.dot(x_hi, rw_ref[0], preferred_element_type=f32)
                  + jnp.dot(x_lo, rw_ref[0], preferred_element_type=f32)
                  + jnp.dot(x_hi, rw_ref[1], preferred_element_type=f32))
        logits = jnp.where(lane < N_EXPERTS, logits, -jnp.inf)
        v1 = jnp.max(logits, axis=-1, keepdims=True)
        i1 = jnp.min(jnp.where(logits == v1, lane, float(LANES)), axis=-1, keepdims=True)
        rest_logits = jnp.where(lane == i1, -jnp.inf, logits)
        v2 = jnp.max(rest_logits, axis=-1, keepdims=True)
        i2 = jnp.min(jnp.where(rest_logits == v2, lane, float(LANES)), axis=-1, keepdims=True)
        e2 = jnp.exp(v2 - v1)
        p1 = 1.0 / (1.0 + e2)
        p2 = e2 / (1.0 + e2)

        @pl.when(pl.program_id(0) == 0)
        def _():
            count_ref[...] = jnp.zeros_like(count_ref)

        tm = x1.shape[0]
        chosen = jnp.logical_or(lane == i1, lane == i2)
        rank = count_ref[...] + jnp.dot(tri_ref[...], chosen.astype(bf16),
                                        preferred_element_type=f32)
        count_ref[...] += jnp.sum(chosen.astype(f32), axis=0, keepdims=True)
        r1 = jnp.sum(jnp.where(lane == i1, rank, 0.0), axis=-1, keepdims=True)
        r2 = jnp.sum(jnp.where(lane == i2, rank, 0.0), axis=-1, keepdims=True)
        fields = (i1, i2, r1, r2, p1, p2)
        route = jnp.zeros((tm, LANES), f32)
        for k, val in enumerate(fields):
            route = jnp.where(lane == k, val, route)
        route_ref[...] = route


def _mix_out(attn, ylru, x2d, w_out_bf16, g_attn, ln_g, ln_b, router_w=None):
    n = x2d.shape[0]
    with_router = router_w is not None
    tm = ROUTER_ROWS if with_router else MIX_ROWS
    tile = lambda width: pl.BlockSpec((tm, width), lambda i: (i, 0))
    const = lambda shape: pl.BlockSpec(shape, lambda i: (0, 0))
    in_specs = [tile(D_ATTN), tile(D_LRU), tile(D_MODEL), const((D_MODEL, D_MODEL)),
                const((1, D_ATTN)), const((1, D_MODEL)), const((1, D_MODEL))]
    args = [attn, ylru, x2d, w_out_bf16, g_attn.reshape(1, D_ATTN).astype(f32),
            ln_g.reshape(1, D_MODEL).astype(f32), ln_b.reshape(1, D_MODEL).astype(f32)]
    out_specs = [tile(D_MODEL)]
    out_shape = [jax.ShapeDtypeStruct((n, D_MODEL), f32)]
    if with_router:
        rw = jnp.zeros((D_MODEL, LANES), f32).at[:, :N_EXPERTS].set(router_w.astype(f32))
        rw_hi = rw.astype(bf16)
        rw_lo = (rw - rw_hi.astype(f32)).astype(bf16)
        strictly_lower = jnp.asarray(np.tri(tm, k=-1), bf16)
        in_specs += [pl.BlockSpec((2, D_MODEL, LANES), lambda i: (0, 0, 0)), const((tm, tm))]
        args += [jnp.stack([rw_hi, rw_lo]), strictly_lower]
        out_specs += [tile(LANES), const((1, LANES))]
        out_shape += [jax.ShapeDtypeStruct((n, LANES), f32), jax.ShapeDtypeStruct((1, LANES), f32)]
    res = pl.pallas_call(
        functools.partial(_mix_out_kernel, with_router=with_router),
        grid=(n // tm,),
        in_specs=in_specs,
        out_specs=out_specs,
        out_shape=out_shape,
        compiler_params=_params("arbitrary" if with_router else "parallel"),
        name="mix_out_router" if with_router else "mix_out",
    )(*args)
    return res if with_router else (res[0], None, None)


def _ffn_kernel(tile_expert_ref, n_used_ref, x_ref, wg_ref, wu_ref, wd_ref, *rest,
                n_chunks, fuse_ln):
    if fuse_ln:
        lng, lnb, out_ref, acc_ref, xb_ref = rest
    else:
        out_ref, acc_ref, xb_ref = rest
    del tile_expert_ref
    j = pl.program_id(1)
    used = pl.program_id(0) < n_used_ref[0]
    last = n_chunks - 1

    def load_x():
        if fuse_ln:
            return x_ref[...].astype(bf16)
        return _load_row_tiles(x_ref, xb_ref.shape[0]).astype(bf16)

    def chunk_out(xb):
        g = jnp.dot(xb, wg_ref[...], preferred_element_type=f32)
        u = jnp.dot(xb, wu_ref[...], preferred_element_type=f32)
        h = (g * jax.nn.sigmoid(g)) * u
        return jnp.dot(h.astype(bf16), wd_ref[...], preferred_element_type=f32)

    def finish(y):
        if fuse_ln:
            out_ref[...] = _layer_norm(DEEPNORM_ALPHA * x_ref[...] + y, lng[...], lnb[...])
        else:
            _store_row_tiles(out_ref, y)

    if n_chunks == 1:
        @pl.when(used)
        def _():
            finish(chunk_out(load_x()))
    else:
        @pl.when(used & (j == 0))
        def _():
            xb = load_x()
            xb_ref[...] = xb
            acc_ref[...] = chunk_out(xb)

        if n_chunks > 2:
            @pl.when(used & (j > 0) & (j < last))
            def _():
                acc_ref[...] += chunk_out(xb_ref[...])

        @pl.when(used & (j == last))
        def _():
            finish(acc_ref[...] + chunk_out(xb_ref[...]))

    @pl.when(jnp.logical_not(used) & (j == last))
    def _():
        finish(jnp.zeros(acc_ref.shape, f32))


def _ffn(x2d, tile_expert, n_used, w_gate, w_up, w_down, ln=None, *, tm, tf, name):
    fuse_ln = ln is not None
    rows = x2d.shape[0] if fuse_ln else x2d.shape[0] // ROW_TILE
    d_ff = w_gate.shape[2]
    n_chunks = d_ff // tf
    io_block = (tm, D_MODEL) if fuse_ln else (tm * ROW_TILE, LANES)

    def chunk(i, j, nu):
        return jnp.where(i < nu[0], j, n_chunks - 1)

    resident = w_gate.shape[0] == 1 and n_chunks == 1
    mode = dict(pipeline_mode=pl.Buffered(1)) if resident else {}
    in_specs = [pl.BlockSpec(io_block, lambda i, j, te, nu: (i, 0)),
                pl.BlockSpec((None, D_MODEL, tf),
                             lambda i, j, te, nu: (te[i], 0, chunk(i, j, nu)), **mode),
                pl.BlockSpec((None, D_MODEL, tf),
                             lambda i, j, te, nu: (te[i], 0, chunk(i, j, nu)), **mode),
                pl.BlockSpec((None, tf, D_MODEL),
                             lambda i, j, te, nu: (te[i], chunk(i, j, nu), 0), **mode)]
    args = [x2d, w_gate, w_up, w_down]
    if fuse_ln:
        in_specs += [pl.BlockSpec((1, D_MODEL), lambda i, j, te, nu: (0, 0))] * 2
        args += [v.reshape(1, D_MODEL).astype(f32) for v in ln]
    return pl.pallas_call(
        functools.partial(_ffn_kernel, n_chunks=n_chunks, fuse_ln=fuse_ln),
        grid_spec=pltpu.PrefetchScalarGridSpec(
            num_scalar_prefetch=2,
            grid=(rows // tm, n_chunks),
            in_specs=in_specs,
            out_specs=pl.BlockSpec(io_block, lambda i, j, te, nu: (i, 0)),
            scratch_shapes=[pltpu.VMEM((tm, D_MODEL), f32), pltpu.VMEM((tm, D_MODEL), bf16)]),
        out_shape=jax.ShapeDtypeStruct(x2d.shape, f32),
        compiler_params=_params("parallel", "arbitrary"),
        name=name,
    )(tile_expert, n_used, *args)


def _dispatch_kernel(fill_ref, pos_ref, x_ref, xs_hbm, zero_buf, rows_buf, sem, *, tt, tm):
    step = pl.program_id(0)

    n_fill = fill_ref.shape[0] // 2

    @pl.when(step == 0)
    def _():
        zero_buf[...] = jnp.zeros_like(zero_buf)

        def fill_copy(f):
            start = pl.multiple_of(fill_ref[f] * ROW_TILE, ROW_TILE)
            return pltpu.make_async_copy(zero_buf, xs_hbm.at[pl.ds(start, tm * ROW_TILE)], sem)

        for f in range(n_fill):
            @pl.when(fill_ref[n_fill + f] > 0)
            def _():
                fill_copy(f).start()
        for f in range(n_fill):
            @pl.when(fill_ref[n_fill + f] > 0)
            def _():
                fill_copy(f).wait()

    _store_row_tiles(rows_buf, x_ref[...])

    def issue(t, carry):
        src = pl.multiple_of(t * ROW_TILE, ROW_TILE)
        for k in range(TOP_K):
            dst = pl.multiple_of(pos_ref[0, TOP_K * t + k] * ROW_TILE, ROW_TILE)
            pltpu.make_async_copy(rows_buf.at[pl.ds(src, ROW_TILE)],
                                  xs_hbm.at[pl.ds(dst, ROW_TILE)], sem).start(priority=k)
        return carry

    lax.fori_loop(0, tt, issue, 0, unroll=DMA_UNROLL)
    for _ in range(TOP_K):
        pltpu.make_async_copy(rows_buf, xs_hbm.at[pl.ds(0, tt * ROW_TILE)], sem).wait()


def _dispatch(x1, pos_blocks, fill, rows_sorted, *, tt, tm):
    n = x1.shape[0]
    return pl.pallas_call(
        functools.partial(_dispatch_kernel, tt=tt, tm=tm),
        grid_spec=pltpu.PrefetchScalarGridSpec(
            num_scalar_prefetch=1,
            grid=(n // tt,),
            in_specs=[pl.BlockSpec((None, 1, TOP_K * tt), lambda i, fill: (i, 0, 0),
                                   memory_space=pltpu.SMEM),
                      pl.BlockSpec((tt, D_MODEL), lambda i, fill: (i, 0))],
            out_specs=pl.BlockSpec(memory_space=pl.ANY),
            scratch_shapes=[pltpu.VMEM((tm * ROW_TILE, LANES), f32),
                            pltpu.VMEM((tt * ROW_TILE, LANES), f32),
                            pltpu.SemaphoreType.DMA(())]),
        out_shape=jax.ShapeDtypeStruct((rows_sorted * ROW_TILE, LANES), f32),
        compiler_params=_params("arbitrary"),
        name="moe_dispatch",
    )(fill, pos_blocks, x1)


def _combine_kernel(pos_ref, pos_next_ref, route_ref, x1_ref, ys_hbm, lng, lnb, out_ref, buf, sems,
                    *, tt, n_steps):
    step = pl.program_id(0)
    slot = step % 2

    def issue_tile(tile_pos_ref, into):
        def issue(t, carry):
            dst = pl.multiple_of(t * ROW_TILE, ROW_TILE)
            for k in range(TOP_K):
                src = pl.multiple_of(tile_pos_ref[0, TOP_K * t + k] * ROW_TILE, ROW_TILE)
                pltpu.make_async_copy(ys_hbm.at[pl.ds(src, ROW_TILE)],
                                      buf.at[into, k, pl.ds(dst, ROW_TILE)],
                                      sems.at[into]).start(priority=k)
            return carry

        lax.fori_loop(0, tt, issue, 0, unroll=DMA_UNROLL)

    @pl.when(step == 0)
    def _():
        issue_tile(pos_ref, 0)

    @pl.when(step + 1 < n_steps)
    def _():
        issue_tile(pos_next_ref, 1 - slot)

    for k in range(TOP_K):
        pltpu.make_async_copy(ys_hbm.at[pl.ds(0, tt * ROW_TILE)], buf.at[slot, k],
                              sems.at[slot]).wait()

    lane = lax.broadcasted_iota(jnp.int32, (1, LANES), 1)
    route = route_ref[...]
    p1 = jnp.sum(jnp.where(lane == 4, route, 0.0), axis=-1, keepdims=True)
    p2 = jnp.sum(jnp.where(lane == 5, route, 0.0), axis=-1, keepdims=True)
    y = p1 * _load_row_tiles(buf.at[slot, 0], tt) + p2 * _load_row_tiles(buf.at[slot, 1], tt)
    out_ref[...] = _layer_norm(DEEPNORM_ALPHA * x1_ref[...] + y, lng[...], lnb[...])


def _combine(pos_blocks, route, x1, y_sorted, ln_g, ln_b, *, tt):
    n = x1.shape[0]
    n_steps = n // tt
    pos_spec = lambda index: pl.BlockSpec((None, 1, TOP_K * tt), index, memory_space=pltpu.SMEM)
    return pl.pallas_call(
        functools.partial(_combine_kernel, tt=tt, n_steps=n_steps),
        grid=(n_steps,),
        in_specs=[pos_spec(lambda i: (i, 0, 0)),
                  pos_spec(lambda i: (jnp.minimum(i + 1, n_steps - 1), 0, 0)),
                  pl.BlockSpec((tt, LANES), lambda i: (i, 0)),
                  pl.BlockSpec((tt, D_MODEL), lambda i: (i, 0)),
                  pl.BlockSpec(memory_space=pl.ANY),
                  pl.BlockSpec((1, D_MODEL), lambda i: (0, 0)),
                  pl.BlockSpec((1, D_MODEL), lambda i: (0, 0))],
        out_specs=pl.BlockSpec((tt, D_MODEL), lambda i: (i, 0)),
        out_shape=jax.ShapeDtypeStruct((n, D_MODEL), f32),
        scratch_shapes=[pltpu.VMEM((2, TOP_K, tt * ROW_TILE, LANES), f32),
                        pltpu.SemaphoreType.DMA((2,))],
        compiler_params=_params("arbitrary"),
        name="moe_combine",
    )(pos_blocks, pos_blocks, route, x1, y_sorted, ln_g.reshape(1, D_MODEL).astype(f32),
      ln_b.reshape(1, D_MODEL).astype(f32))


def _moe(x1, route, counts, w_gate, w_up, w_down, ln_g, ln_b, *, tm, tf, tt):
    n = x1.shape[0]
    i32 = jnp.int32
    counts = counts[0, :N_EXPERTS].astype(i32)
    padded = (counts + tm - 1) // tm * tm
    ends = jnp.cumsum(padded)
    offsets = ends - padded
    experts = route[:, 0:TOP_K].astype(i32)
    ranks = route[:, TOP_K:2 * TOP_K].astype(i32)
    pos = (offsets[experts] + ranks).reshape(n // tt, 1, TOP_K * tt)
    n_tiles = TOP_K * n // tm + N_EXPERTS
    n_used = (ends[-1] // tm).astype(i32).reshape(1)
    tile_ids = jnp.arange(n_tiles, dtype=i32)
    tile_expert = jnp.sum((tile_ids[:, None] >= (ends // tm)[None, :]).astype(i32), axis=1)
    last_expert = jnp.max(jnp.where(counts > 0, jnp.arange(N_EXPERTS, dtype=i32), 0))
    tile_expert = jnp.minimum(tile_expert, last_expert).astype(i32)
    tail_tiles = n_used[0] + jnp.arange(N_EXPERTS, dtype=i32)
    fill = jnp.concatenate([ends - tm, tail_tiles * tm,
                            (counts > 0).astype(i32), (tail_tiles < n_tiles).astype(i32)]).astype(i32)

    x_sorted = _dispatch(x1, pos, fill, n_tiles * tm, tt=tt, tm=tm)
    y_sorted = _ffn(x_sorted, tile_expert, n_used, w_gate, w_up, w_down, tm=tm, tf=tf,
                    name="ffn_experts")
    return _combine(pos, route, x1, y_sorted, ln_g, ln_b, tt=tt)


def kernel(x, w_in, conv_w, conv_b, w_a, b_a, w_x, b_x, lru_lambda, rel_bias, g_attn, g_lru, w_out, ln1_g, ln1_b, ln2_g, ln2_b, ffn_w_gate, ffn_w_up, ffn_w_down, router_w, moe_w_gate, moe_w_up, moe_w_down):
    batch, seq, _ = x.shape
    n = batch * seq
    h = x.reshape(n, D_MODEL).astype(f32)
    band_bias = jnp.stack([_band_bias(rel_bias, d) for _, d in DILATED_PATTERNS])
    for layer in range(DEPTH):
        qkv, ylru = _proj_lru(h, w_in[layer].astype(bf16), conv_w[layer], conv_b[layer],
                              w_a[layer], b_a[layer], w_x[layer], b_x[layer], lru_lambda[layer],
                              g_lru[layer], batch, seq)
        attn = _attention(qkv, band_bias, batch, seq)
        j = layer // 2
        dense = layer % 2 == 0
        x1, route, counts = _mix_out(
            attn, ylru, h, w_out[layer].astype(bf16), g_attn[layer], ln1_g[layer],
            ln1_b[layer], None if dense else router_w[j])
        if dense:
            tm = DENSE_FFN_ROWS
            h = _ffn(x1, jnp.zeros((n // tm,), jnp.int32), jnp.full((1,), n // tm, jnp.int32),
                     ffn_w_gate[j][None].astype(bf16), ffn_w_up[j][None].astype(bf16),
                     ffn_w_down[j][None].astype(bf16), (ln2_g[layer], ln2_b[layer]),
                     tm=tm, tf=ffn_w_gate.shape[2], name="ffn_dense")
        else:
            h = _moe(x1, route, counts, moe_w_gate[j].astype(bf16), moe_w_up[j].astype(bf16),
                     moe_w_down[j].astype(bf16), ln2_g[layer], ln2_b[layer],
                     tm=EXPERT_ROWS, tf=EXPERT_FF_CHUNK, tt=ROUTE_ROWS)
    return h.reshape(batch, seq, D_MODEL).astype(x.dtype)
```

```python
import functools

import numpy as np
import jax
import jax.numpy as jnp
from jax import lax
from jax.experimental import pallas as pl
from jax.experimental.pallas import tpu as pltpu

D_MODEL = 1024
N_HEADS = 8
HEAD_DIM = 64
D_ATTN = N_HEADS * HEAD_DIM
D_LRU = 512
N_LRU_BLOCKS = 8
LRU_BLOCK = D_LRU // N_LRU_BLOCKS
CONV_WIDTH = 4
LRU_C = 8.0
DILATED_PATTERNS = ((128, 1), (512, 4), (2048, 16))
ATTN_BLOCK = 128
STAGE_DILATION = 4
N_BUCKETS = 32
MAX_DISTANCE = 2048
D_IN = 3 * D_ATTN + 2 * D_LRU
N_EXPERTS = 8
TOP_K = 2
DEPTH = 2
DEEPNORM_ALPHA = (2.0 * DEPTH) ** 0.25
LN_EPS = 1e-5
RMS_EPS = 1e-6
NEG_INF = -1e30
LOG2_E = float(np.log2(np.e))

LANES = 128
SUBLANES = 8
VMEM_LIMIT_BYTES = 56 * 1024 * 1024
MXU_DEPTH = 256

LRU_ROWS = 1024
MIX_ROWS = 1024
ROUTER_ROWS = 512
DENSE_FFN_ROWS = 512
EXPERT_ROWS = 512
EXPERT_FF_CHUNK = 7 * MXU_DEPTH
ROUTE_ROWS = 1024
DMA_UNROLL = 8
ATTN_UNROLL = 4

f32 = jnp.float32
bf16 = jnp.bfloat16


def _params(*semantics):
    return pltpu.CompilerParams(dimension_semantics=semantics,
                                vmem_limit_bytes=VMEM_LIMIT_BYTES)


def _t5_bucket(dist):
    max_exact = N_BUCKETS // 2
    d = np.maximum(dist, 1).astype(np.float32)
    large = max_exact + (np.log(d / max_exact) / np.log(MAX_DISTANCE / max_exact)
                         * (N_BUCKETS - max_exact)).astype(np.int32)
    large = np.minimum(large, N_BUCKETS - 1)
    return np.where(dist < max_exact, dist, large).astype(np.int32)


def _band_bias(rel_bias, dilation):
    nk = ATTN_BLOCK
    qi = np.arange(nk)[:, None]
    kj = np.arange(2 * nk)[None, :]
    delta = qi + nk - kj
    band = (delta >= 0) & (delta <= nk)
    bucket = _t5_bucket(np.clip(delta, 0, nk) * dilation)
    onehot = np.eye(N_BUCKETS, dtype=np.float32)[bucket.reshape(-1)]
    bias = jnp.dot(jnp.asarray(onehot), rel_bias.astype(f32), precision=lax.Precision.HIGHEST)
    bias = jnp.transpose(bias.reshape(nk, 2 * nk, N_HEADS), (2, 0, 1))
    valid = np.stack([band, band & (kj >= nk)])[:, None]
    bias = jnp.where(jnp.asarray(valid), bias[None], NEG_INF)
    return (bias * LOG2_E).reshape(2, N_HEADS // 2, 2 * nk, 2 * nk)


def _attn_kernel(q_ref, k_ref, v_ref, bias_ref, o_ref, q4, k4, v4, qs, ks, vs, s_buf, m_buf,
                 m_s, l_s, acc_s, *, seq):
    nk = ATTN_BLOCK
    n_blocks = seq // nk
    chunk = 2 * nk
    lane = lax.broadcasted_iota(jnp.int32, (1, LANES), 1)
    head0 = lane < HEAD_DIM

    ks[0:nk, :] = jnp.zeros((nk, LANES), bf16)
    vs[0:nk, 0:LANES] = jnp.zeros((nk, LANES), bf16)
    vs[:, LANES:2 * LANES] = jnp.ones((seq + nk, LANES), bf16)

    def stage(t, carry):
        chunks_per_residue = seq // STAGE_DILATION // chunk
        src = pl.ds(t // chunks_per_residue + STAGE_DILATION * chunk * (t % chunks_per_residue),
                    chunk, stride=STAGE_DILATION)
        dst = pl.ds(pl.multiple_of(t * chunk, chunk), chunk)
        q4[dst, :] = q_ref[src, :]
        k4[dst, :] = k_ref[src, :]
        v4[dst, :] = v_ref[src, :]
        return carry

    lax.fori_loop(0, seq // chunk, stage, 0)

    order = sorted(range(len(DILATED_PATTERNS)), key=lambda p: -DILATED_PATTERNS[p][1])
    assert DILATED_PATTERNS[order[-1]][1] == 1
    for p in order:
        d = DILATED_PATTERNS[p][1]
        is_first, is_last = p == order[0], p == order[-1]
        length = seq // d
        nb = length // nk
        chunks_per_residue = length // chunk

        def gather(t, carry):
            r = t // chunks_per_residue
            c = t % chunks_per_residue
            dst = pl.multiple_of(t * chunk, chunk)
            if d % STAGE_DILATION == 0:
                sub = d // STAGE_DILATION
                start = ((r % STAGE_DILATION) * (seq // STAGE_DILATION) + r // STAGE_DILATION
                         + sub * chunk * c)
                src = (pl.ds(start, chunk, stride=sub) if sub > 1
                       else pl.ds(pl.multiple_of(start, chunk), chunk))
                q, k, v = q4[src, :], k4[src, :], v4[src, :]
            else:
                src = pl.ds(r + d * chunk * c, chunk, stride=d) if d > 1 else pl.ds(dst, chunk)
                q, k, v = q_ref[src, :], k_ref[src, :], v_ref[src, :]
            q = q * (HEAD_DIM ** -0.5 * LOG2_E)
            q0 = jnp.where(head0, q, 0.0).astype(bf16)
            q1 = jnp.where(head0, 0.0, q).astype(bf16)
            for half in range(2):
                base = pl.multiple_of(2 * dst + half * chunk, chunk)
                qs[pl.ds(base, nk), :] = q0[half * nk:(half + 1) * nk]
                qs[pl.ds(base + nk, nk), :] = q1[half * nk:(half + 1) * nk]
            ks[pl.ds(nk + dst, chunk), :] = k.astype(bf16)
            vs[pl.ds(nk + dst, chunk), 0:LANES] = v.astype(bf16)
            return carry

        lax.fori_loop(0, seq // chunk, gather, 0)

        def scores(b, carry):
            first = jnp.asarray(b % nb == 0, jnp.int32)
            rows = pl.ds(pl.multiple_of(b * chunk, chunk), chunk)
            k = ks[pl.ds(pl.multiple_of(b * nk, nk), chunk), :]
            s_buf[rows, :] = lax.dot_general(qs[rows, :], k, (((1,), (1,)), ((), ())),
                                             preferred_element_type=f32) + bias_ref[p, first]
            return carry

        def rowmax(b, carry):
            rows = pl.ds(pl.multiple_of(b * chunk, chunk), chunk)
            m_buf[rows, :] = jnp.broadcast_to(jnp.max(s_buf[rows, :], axis=-1, keepdims=True),
                                              (chunk, LANES))
            return carry

        def block(b, carry):
            r = b // nb
            i = b % nb
            rows = pl.ds(pl.multiple_of(b * chunk, chunk), chunk)
            v = vs[pl.ds(pl.multiple_of(b * nk, nk), chunk), :]
            m = m_buf[rows, :]
            e = jnp.exp2(s_buf[rows, :] - jnp.concatenate([m, m], axis=1))
            pv = jnp.dot(e.astype(bf16), v, preferred_element_type=f32)
            m_blk = jnp.where(head0, m[0:nk], m[nk:chunk])
            l_blk = jnp.where(head0, pv[0:nk, LANES:], pv[nk:chunk, LANES:])
            pv_blk = jnp.where(head0, pv[0:nk, 0:LANES], pv[nk:chunk, 0:LANES])
            if d == 1:
                tok = pl.ds(pl.multiple_of(b * nk, nk), nk)
            else:
                tok = pl.ds(r + d * nk * i, nk, stride=d)
            if is_first:
                m_s[tok, :] = m_blk
                l_s[tok, :] = l_blk
                acc_s[tok, :] = pv_blk
                return carry
            m_old = m_s[tok, :]
            m_new = jnp.maximum(m_old, m_blk)
            w_old = jnp.exp2(m_old - m_new)
            w_blk = jnp.exp2(m_blk - m_new)
            l_new = w_old * l_s[tok, :] + w_blk * l_blk
            acc_new = w_old * acc_s[tok, :] + w_blk * pv_blk
            if is_last:
                o_ref[tok, :] = (acc_new / l_new).astype(o_ref.dtype)
            else:
                l_s[tok, :] = l_new
                acc_s[tok, :] = acc_new
                m_s[tok, :] = m_new
            return carry

        n_groups = n_blocks // ATTN_UNROLL
        assert n_groups >= 3

        def group(fn, g):
            for j in range(ATTN_UNROLL):
                fn(g * ATTN_UNROLL + j, 0)

        def pipelined(g, carry):
            group(block, g)
            group(rowmax, g + 1)
            group(scores, g + 2)
            return carry

        group(scores, 0)
        group(scores, 1)
        group(rowmax, 0)
        lax.fori_loop(0, n_groups - 2, pipelined, 0)
        group(block, n_groups - 2)
        group(rowmax, n_groups - 1)
        group(block, n_groups - 1)


def _attention(qkv, band_bias, batch, seq):
    view = qkv.reshape(batch, seq, 3 * D_ATTN)
    pairs = D_ATTN // LANES

    def spec(offset):
        return pl.BlockSpec((None, seq, LANES), lambda b, hp: (b, 0, offset * pairs + hp))

    n_pat = len(DILATED_PATTERNS)
    o = pl.pallas_call(
        functools.partial(_attn_kernel, seq=seq),
        grid=(batch, pairs),
        in_specs=[spec(0), spec(1), spec(2),
                  pl.BlockSpec((n_pat, 2, None, 2 * ATTN_BLOCK, 2 * ATTN_BLOCK),
                               lambda b, hp: (0, 0, hp, 0, 0))],
        out_specs=pl.BlockSpec((None, seq, LANES), lambda b, hp: (b, 0, hp)),
        out_shape=jax.ShapeDtypeStruct((batch, seq, D_ATTN), bf16),
        scratch_shapes=[pltpu.VMEM((seq, LANES), f32), pltpu.VMEM((seq, LANES), f32),
                        pltpu.VMEM((seq, LANES), f32),
                        pltpu.VMEM((2 * seq, LANES), bf16),
                        pltpu.VMEM((seq + ATTN_BLOCK, LANES), bf16),
                        pltpu.VMEM((seq + ATTN_BLOCK, 2 * LANES), bf16),
                        pltpu.VMEM((2 * seq, 2 * ATTN_BLOCK), f32),
                        pltpu.VMEM((2 * seq, LANES), f32),
                        pltpu.VMEM((seq, LANES), f32), pltpu.VMEM((seq, LANES), f32),
                        pltpu.VMEM((seq, LANES), f32)],
        compiler_params=_params("parallel", "parallel"),
        name="attention",
    )(view, view, view, band_bias)
    return o.reshape(batch * seq, D_ATTN)


def _gelu_tanh(x):
    return 0.5 * x * (1.0 + jnp.tanh(np.sqrt(2.0 / np.pi) * (x + 0.044715 * x * x * x)))


def _proj_lru_kernel(x_ref, w_ref, convw_ref, convb_ref, wg_ref, ba_ref, bx_ref, lam_ref, g_ref,
                     qkv_ref, y_ref, ubuf, hcarry, *, ts):
    pad = SUBLANES
    t = pl.program_id(1)
    n_qkv = qkv_ref.shape[1]
    xb = x_ref[...].astype(bf16)

    @pl.when(t == 0)
    def _():
        ubuf[0:pad, :] = jnp.zeros((pad, D_LRU), f32)
        hcarry[...] = jnp.zeros_like(hcarry)

    @pl.when(t > 0)
    def _():
        ubuf[0:pad, :] = ubuf[ts:ts + pad, :]

    ubuf[pad:pad + ts, :] = jnp.dot(xb, w_ref[:, n_qkv:n_qkv + D_LRU],
                                    preferred_element_type=f32)
    gate = jnp.dot(xb, w_ref[:, n_qkv + D_LRU:], preferred_element_type=f32)

    qkv_step = n_qkv // 3
    assert qkv_step == D_LRU

    def qkv_chunk(c):
        cols = slice(c * qkv_step, (c + 1) * qkv_step)
        p = jnp.dot(xb, w_ref[:, cols], preferred_element_type=f32)
        qkv_ref[:, cols] = p
        bits = lax.bitcast_convert_type(p, jnp.uint32).reshape(ts // SUBLANES, SUBLANES, qkv_step)
        folded = functools.reduce(jnp.bitwise_or, [bits[g] for g in range(ts // SUBLANES)])
        return ((folded >> 16) >> 16).astype(f32)

    u = convb_ref[...] + convw_ref[CONV_WIDTH - 1:CONV_WIDTH, :] * ubuf[pad:pad + ts, :]
    for w in range(CONV_WIDTH - 1):
        back = CONV_WIDTH - 1 - w
        u = u + convw_ref[w:w + 1, :] * ubuf[pad - back:pad - back + ts, :]

    gates = jnp.dot(u.astype(bf16), wg_ref[...], preferred_element_type=f32)
    r = jax.nn.sigmoid(gates[:, 0:D_LRU] + ba_ref[...])
    i = jax.nn.sigmoid(gates[:, D_LRU:2 * D_LRU] + bx_ref[...])
    neg_lam = -lam_ref[...]
    softplus = jnp.maximum(neg_lam, 0.0) + jnp.log1p(jnp.exp(-jnp.abs(neg_lam)))
    log_a = (-LRU_C) * r * softplus
    a = jnp.exp(log_a)
    b = jnp.sqrt(-jnp.tanh(log_a) * (1.0 + a * a)) * (i * u)
    zero0 = qkv_chunk(0)

    groups = ts // SUBLANES
    a = a.reshape(groups, SUBLANES, D_LRU)
    b = b.reshape(groups, SUBLANES, D_LRU) + zero0[None]
    sub = lax.broadcasted_iota(jnp.int32, (1, SUBLANES, 1), 1)
    shift = 1
    while shift < SUBLANES:
        live = sub >= shift
        a_prev = jnp.where(live, pltpu.roll(a, shift, 1), 1.0)
        b_prev = jnp.where(live, pltpu.roll(b, shift, 1), 0.0)
        b = a * b_prev + b
        a = a * a_prev
        shift *= 2
    h_prev = hcarry[...] + qkv_chunk(1)[0:1, :]
    h_groups = []
    for g in range(groups):
        h_g = a[g] * h_prev + b[g]
        h_prev = h_g[SUBLANES - 1:SUBLANES, :]
        h_groups.append(h_g)
    hcarry[...] = h_prev
    h = jnp.concatenate(h_groups, axis=0)
    zero2 = qkv_chunk(2)[0:1, 0:1]

    y = _gelu_tanh(gate) * h
    y = y * lax.rsqrt(jnp.mean(y * y, axis=-1, keepdims=True) + (RMS_EPS + zero2)) * g_ref[...]
    y_ref[...] = y.astype(y_ref.dtype)


def _block_diag(w):
    g, i, j = w.shape
    eye = jnp.eye(g, dtype=w.dtype)
    return jnp.einsum('gij,gh->gihj', w, eye).reshape(g * i, g * j)


def _proj_lru(x2d, w_in_bf16, conv_w, conv_b, w_a, b_a, w_x, b_x, lam, g_lru, batch, seq,
              ts=LRU_ROWS):
    wg = jnp.concatenate([_block_diag(w_a), _block_diag(w_x)], axis=1).astype(bf16)
    row = lambda v: v.reshape(1, D_LRU).astype(f32)
    const = lambda shape: pl.BlockSpec(shape, lambda b, t: (0, 0))
    tile = lambda width: pl.BlockSpec((None, ts, width), lambda b, t: (b, t, 0))
    qkv, y = pl.pallas_call(
        functools.partial(_proj_lru_kernel, ts=ts),
        grid=(batch, seq // ts),
        in_specs=[tile(D_MODEL), const((D_MODEL, D_IN)),
                  const((CONV_WIDTH, D_LRU)), const((1, D_LRU)),
                  const((D_LRU, 2 * D_LRU)), const((1, D_LRU)), const((1, D_LRU)),
                  const((1, D_LRU)), const((1, D_LRU))],
        out_specs=[tile(3 * D_ATTN), tile(D_LRU)],
        out_shape=[jax.ShapeDtypeStruct((batch, seq, 3 * D_ATTN), f32),
                   jax.ShapeDtypeStruct((batch, seq, D_LRU), bf16)],
        scratch_shapes=[pltpu.VMEM((ts + 2 * SUBLANES, D_LRU), f32), pltpu.VMEM((1, D_LRU), f32)],
        compiler_params=_params("parallel", "arbitrary"),
        name="proj_rglru",
    )(x2d.reshape(batch, seq, D_MODEL), w_in_bf16,
      conv_w.reshape(CONV_WIDTH, D_LRU).astype(f32), row(conv_b), wg, row(b_a), row(b_x),
      row(lam), row(g_lru))
    return qkv.reshape(batch * seq, 3 * D_ATTN), y.reshape(batch * seq, D_LRU)


ROW_TILE = D_MODEL // LANES


def _store_row_tiles(ref, rows):
    t = rows.shape[0]
    for s in range(ROW_TILE):
        ref[pl.ds(s, t, stride=ROW_TILE), :] = rows[:, s * LANES:(s + 1) * LANES]


def _load_row_tiles(ref, t):
    return jnp.concatenate([ref[pl.ds(s, t, stride=ROW_TILE), :] for s in range(ROW_TILE)],
                           axis=-1)


def _layer_norm(z, g, b):
    mu = jnp.mean(z, axis=-1, keepdims=True)
    zc = z - mu
    var = jnp.mean(zc * zc, axis=-1, keepdims=True)
    return zc * lax.rsqrt(var + LN_EPS) * g + b


def _mix_out_kernel(attn_ref, ylru, x_ref, w_ref, gattn, lng, lnb, *rest, with_router):
    if with_router:
        rw_ref, tri_ref, x1_ref, route_ref, count_ref = rest
    else:
        (x1_ref,) = rest
    attn = attn_ref[...].astype(f32)
    attn = attn * lax.rsqrt(jnp.mean(attn * attn, axis=-1, keepdims=True) + RMS_EPS) * gattn[...]
    y = jnp.dot(attn.astype(bf16), w_ref[0:D_ATTN, :], preferred_element_type=f32)
    y = y + jnp.dot(ylru[...], w_ref[D_ATTN:, :], preferred_element_type=f32)
    x1 = _layer_norm(DEEPNORM_ALPHA * x_ref[...] + y, lng[...], lnb[...])
    x1_ref[...] = x1
    if with_router:
        lane = lax.broadcasted_iota(jnp.int32, (1, LANES), 1).astype(f32)
        x_hi = x1.astype(bf16)
        x_lo = (x1 - x_hi.astype(f32)).astype(bf16)
        logits = (jnp.dot(x_hi, rw_ref[0], preferred_element_type=f32)
                  + jnp.dot(x_lo, rw_ref[0], preferred_element_type=f32)
                  + jnp.dot(x_hi, rw_ref[1], preferred_element_type=f32))
        logits = jnp.where(lane < N_EXPERTS, logits, -jnp.inf)
        v1 = jnp.max(logits, axis=-1, keepdims=True)
        i1 = jnp.min(jnp.where(logits == v1, lane, float(LANES)), axis=-1, keepdims=True)
        rest_logits = jnp.where(lane == i1, -jnp.inf, logits)
        v2 = jnp.max(rest_logits, axis=-1, keepdims=True)
        i2 = jnp.min(jnp.where(rest_logits == v2, lane, float(LANES)), axis=-1, keepdims=True)
        e2 = jnp.exp(v2 - v1)
        p1 = 1.0 / (1.0 + e2)
        p2 = e2 / (1.0 + e2)

        @pl.when(pl.program_id(0) == 0)
        def _():
            count_ref[...] = jnp.zeros_like(count_ref)

        tm = x1.shape[0]
        chosen = jnp.logical_or(lane == i1, lane == i2)
        rank = count_ref[...] + jnp.dot(tri_ref[...], chosen.astype(bf16),
                                        preferred_element_type=f32)
        count_ref[...] += jnp.sum(chosen.astype(f32), axis=0, keepdims=True)
        r1 = jnp.sum(jnp.where(lane == i1, rank, 0.0), axis=-1, keepdims=True)
        r2 = jnp.sum(jnp.where(lane == i2, rank, 0.0), axis=-1, keepdims=True)
        fields = (i1, i2, r1, r2, p1, p2)
        route = jnp.zeros((tm, LANES), f32)
        for k, val in enumerate(fields):
            route = jnp.where(lane == k, val, route)
        route_ref[...] = route


def _mix_out(attn, ylru, x2d, w_out_bf16, g_attn, ln_g, ln_b, router_w=None):
    n = x2d.shape[0]
    with_router = router_w is not None
    tm = ROUTER_ROWS if with_router else MIX_ROWS
    tile = lambda width: pl.BlockSpec((tm, width), lambda i: (i, 0))
    const = lambda shape: pl.BlockSpec(shape, lambda i: (0, 0))
    in_specs = [tile(D_ATTN), tile(D_LRU), tile(D_MODEL), const((D_MODEL, D_MODEL)),
                const((1, D_ATTN)), const((1, D_MODEL)), const((1, D_MODEL))]
    args = [attn, ylru, x2d, w_out_bf16, g_attn.reshape(1, D_ATTN).astype(f32),
            ln_g.reshape(1, D_MODEL).astype(f32), ln_b.reshape(1, D_MODEL).astype(f32)]
    out_specs = [tile(D_MODEL)]
    out_shape = [jax.ShapeDtypeStruct((n, D_MODEL), f32)]
    if with_router:
        rw = jnp.zeros((D_MODEL, LANES), f32).at[:, :N_EXPERTS].set(router_w.astype(f32))
        rw_hi = rw.astype(bf16)
        rw_lo = (rw - rw_hi.astype(f32)).astype(bf16)
        strictly_lower = jnp.asarray(np.tri(tm, k=-1), bf16)
        in_specs += [pl.BlockSpec((2, D_MODEL, LANES), lambda i: (0, 0, 0)), const((tm, tm))]
        args += [jnp.stack([rw_hi, rw_lo]), strictly_lower]
        out_specs += [tile(LANES), const((1, LANES))]
        out_shape += [jax.ShapeDtypeStruct((n, LANES), f32), jax.ShapeDtypeStruct((1, LANES), f32)]
    res = pl.pallas_call(
        functools.partial(_mix_out_kernel, with_router=with_router),
        grid=(n // tm,),
        in_specs=in_specs,
        out_specs=out_specs,
        out_shape=out_shape,
        compiler_params=_params("arbitrary" if with_router else "parallel"),
        name="mix_out_router" if with_router else "mix_out",
    )(*args)
    return res if with_router else (res[0], None, None)


def _ffn_kernel(tile_expert_ref, n_used_ref, x_ref, wg_ref, wu_ref, wd_ref, *rest,
                n_chunks, fuse_ln):
    if fuse_ln:
        lng, lnb, out_ref, acc_ref, xb_ref = rest
    else:
        out_ref, acc_ref, xb_ref = rest
    del tile_expert_ref
    j = pl.program_id(1)
    used = pl.program_id(0) < n_used_ref[0]
    last = n_chunks - 1

    def load_x():
        if fuse_ln:
            return x_ref[...].astype(bf16)
        return _load_row_tiles(x_ref, xb_ref.shape[0]).astype(bf16)

    def chunk_out(xb):
        g = jnp.dot(xb, wg_ref[...], preferred_element_type=f32)
        u = jnp.dot(xb, wu_ref[...], preferred_element_type=f32)
        h = (g * jax.nn.sigmoid(g)) * u
        return jnp.dot(h.astype(bf16), wd_ref[...], preferred_element_type=f32)

    def finish(y):
        if fuse_ln:
            out_ref[...] = _layer_norm(DEEPNORM_ALPHA * x_ref[...] + y, lng[...], lnb[...])
        else:
            _store_row_tiles(out_ref, y)

    if n_chunks == 1:
        @pl.when(used)
        def _():
            finish(chunk_out(load_x()))
    else:
        @pl.when(used & (j == 0))
        def _():
            xb = load_x()
            xb_ref[...] = xb
            acc_ref[...] = chunk_out(xb)

        if n_chunks > 2:
            @pl.when(used & (j > 0) & (j < last))
            def _():
                acc_ref[...] += chunk_out(xb_ref[...])

        @pl.when(used & (j == last))
        def _():
            finish(acc_ref[...] + chunk_out(xb_ref[...]))

    @pl.when(jnp.logical_not(used) & (j == last))
    def _():
        finish(jnp.zeros(acc_ref.shape, f32))


def _ffn(x2d, tile_expert, n_used, w_gate, w_up, w_down, ln=None, *, tm, tf, name):
    fuse_ln = ln is not None
    rows = x2d.shape[0] if fuse_ln else x2d.shape[0] // ROW_TILE
    d_ff = w_gate.shape[2]
    n_chunks = d_ff // tf
    io_block = (tm, D_MODEL) if fuse_ln else (tm * ROW_TILE, LANES)

    def chunk(i, j, nu):
        return jnp.where(i < nu[0], j, n_chunks - 1)

    resident = w_gate.shape[0] == 1 and n_chunks == 1
    mode = dict(pipeline_mode=pl.Buffered(1)) if resident else {}
    in_specs = [pl.BlockSpec(io_block, lambda i, j, te, nu: (i, 0)),
                pl.BlockSpec((None, D_MODEL, tf),
                             lambda i, j, te, nu: (te[i], 0, chunk(i, j, nu)), **mode),
                pl.BlockSpec((None, D_MODEL, tf),
                             lambda i, j, te, nu: (te[i], 0, chunk(i, j, nu)), **mode),
                pl.BlockSpec((None, tf, D_MODEL),
                             lambda i, j, te, nu: (te[i], chunk(i, j, nu), 0), **mode)]
    args = [x2d, w_gate, w_up, w_down]
    if fuse_ln:
        in_specs += [pl.BlockSpec((1, D_MODEL), lambda i, j, te, nu: (0, 0))] * 2
        args += [v.reshape(1, D_MODEL).astype(f32) for v in ln]
    return pl.pallas_call(
        functools.partial(_ffn_kernel, n_chunks=n_chunks, fuse_ln=fuse_ln),
        grid_spec=pltpu.PrefetchScalarGridSpec(
            num_scalar_prefetch=2,
            grid=(rows // tm, n_chunks),
            in_specs=in_specs,
            out_specs=pl.BlockSpec(io_block, lambda i, j, te, nu: (i, 0)),
            scratch_shapes=[pltpu.VMEM((tm, D_MODEL), f32), pltpu.VMEM((tm, D_MODEL), bf16)]),
        out_shape=jax.ShapeDtypeStruct(x2d.shape, f32),
        compiler_params=_params("parallel", "arbitrary"),
        name=name,
    )(tile_expert, n_used, *args)


def _dispatch_kernel(fill_ref, pos_ref, x_ref, xs_hbm, zero_buf, rows_buf, sem, *, tt, tm):
    step = pl.program_id(0)

    n_fill = fill_ref.shape[0] // 2

    @pl.when(step == 0)
    def _():
        zero_buf[...] = jnp.zeros_like(zero_buf)

        def fill_copy(f):
            start = pl.multiple_of(fill_ref[f] * ROW_TILE, ROW_TILE)
            return pltpu.make_async_copy(zero_buf, xs_hbm.at[pl.ds(start, tm * ROW_TILE)], sem)

        for f in range(n_fill):
            @pl.when(fill_ref[n_fill + f] > 0)
            def _():
                fill_copy(f).start()
        for f in range(n_fill):
            @pl.when(fill_ref[n_fill + f] > 0)
            def _():
                fill_copy(f).wait()

    _store_row_tiles(rows_buf, x_ref[...])

    def issue(t, carry):
        src = pl.multiple_of(t * ROW_TILE, ROW_TILE)
        for k in range(TOP_K):
            dst = pl.multiple_of(pos_ref[0, TOP_K * t + k] * ROW_TILE, ROW_TILE)
            pltpu.make_async_copy(rows_buf.at[pl.ds(src, ROW_TILE)],
                                  xs_hbm.at[pl.ds(dst, ROW_TILE)], sem).start(priority=k)
        return carry

    lax.fori_loop(0, tt, issue, 0, unroll=DMA_UNROLL)
    for _ in range(TOP_K):
        pltpu.make_async_copy(rows_buf, xs_hbm.at[pl.ds(0, tt * ROW_TILE)], sem).wait()


def _dispatch(x1, pos_blocks, fill, rows_sorted, *, tt, tm):
    n = x1.shape[0]
    return pl.pallas_call(
        functools.partial(_dispatch_kernel, tt=tt, tm=tm),
        grid_spec=pltpu.PrefetchScalarGridSpec(
            num_scalar_prefetch=1,
            grid=(n // tt,),
            in_specs=[pl.BlockSpec((None, 1, TOP_K * tt), lambda i, fill: (i, 0, 0),
                                   memory_space=pltpu.SMEM),
                      pl.BlockSpec((tt, D_MODEL), lambda i, fill: (i, 0))],
            out_specs=pl.BlockSpec(memory_space=pl.ANY),
            scratch_shapes=[pltpu.VMEM((tm * ROW_TILE, LANES), f32),
                            pltpu.VMEM((tt * ROW_TILE, LANES), f32),
                            pltpu.SemaphoreType.DMA(())]),
        out_shape=jax.ShapeDtypeStruct((rows_sorted * ROW_TILE, LANES), f32),
        compiler_params=_params("arbitrary"),
        name="moe_dispatch",
    )(fill, pos_blocks, x1)


def _combine_kernel(pos_ref, pos_next_ref, route_ref, x1_ref, ys_hbm, lng, lnb, out_ref, buf, sems,
                    *, tt, n_steps):
    step = pl.program_id(0)
    slot = step % 2

    def issue_tile(tile_pos_ref, into):
        def issue(t, carry):
            dst = pl.multiple_of(t * ROW_TILE, ROW_TILE)
            for k in range(TOP_K):
                src = pl.multiple_of(tile_pos_ref[0, TOP_K * t + k] * ROW_TILE, ROW_TILE)
                pltpu.make_async_copy(ys_hbm.at[pl.ds(src, ROW_TILE)],
                                      buf.at[into, k, pl.ds(dst, ROW_TILE)],
                                      sems.at[into]).start(priority=k)
            return carry

        lax.fori_loop(0, tt, issue, 0, unroll=DMA_UNROLL)

    @pl.when(step == 0)
    def _():
        issue_tile(pos_ref, 0)

    @pl.when(step + 1 < n_steps)
    def _():
        issue_tile(pos_next_ref, 1 - slot)

    for k in range(TOP_K):
        pltpu.make_async_copy(ys_hbm.at[pl.ds(0, tt * ROW_TILE)], buf.at[slot, k],
                              sems.at[slot]).wait()

    lane = lax.broadcasted_iota(jnp.int32, (1, LANES), 1)
    route = route_ref[...]
    p1 = jnp.sum(jnp.where(lane == 4, route, 0.0), axis=-1, keepdims=True)
    p2 = jnp.sum(jnp.where(lane == 5, route, 0.0), axis=-1, keepdims=True)
    y = p1 * _load_row_tiles(buf.at[slot, 0], tt) + p2 * _load_row_tiles(buf.at[slot, 1], tt)
    out_ref[...] = _layer_norm(DEEPNORM_ALPHA * x1_ref[...] + y, lng[...], lnb[...])


def _combine(pos_blocks, route, x1, y_sorted, ln_g, ln_b, *, tt):
    n = x1.shape[0]
    n_steps = n // tt
    pos_spec = lambda index: pl.BlockSpec((None, 1, TOP_K * tt), index, memory_space=pltpu.SMEM)
    return pl.pallas_call(
        functools.partial(_combine_kernel, tt=tt, n_steps=n_steps),
        grid=(n_steps,),
        in_specs=[pos_spec(lambda i: (i, 0, 0)),
                  pos_spec(lambda i: (jnp.minimum(i + 1, n_steps - 1), 0, 0)),
                  pl.BlockSpec((tt, LANES), lambda i: (i, 0)),
                  pl.BlockSpec((tt, D_MODEL), lambda i: (i, 0)),
                  pl.BlockSpec(memory_space=pl.ANY),
                  pl.BlockSpec((1, D_MODEL), lambda i: (0, 0)),
                  pl.BlockSpec((1, D_MODEL), lambda i: (0, 0))],
        out_specs=pl.BlockSpec((tt, D_MODEL), lambda i: (i, 0)),
        out_shape=jax.ShapeDtypeStruct((n, D_MODEL), f32),
        scratch_shapes=[pltpu.VMEM((2, TOP_K, tt * ROW_TILE, LANES), f32),
                        pltpu.SemaphoreType.DMA((2,))],
        compiler_params=_params("arbitrary"),
        name="moe_combine",
    )(pos_blocks, pos_blocks, route, x1, y_sorted, ln_g.reshape(1, D_MODEL).astype(f32),
      ln_b.reshape(1, D_MODEL).astype(f32))


def _moe(x1, route, counts, w_gate, w_up, w_down, ln_g, ln_b, *, tm, tf, tt):
    n = x1.shape[0]
    i32 = jnp.int32
    counts = counts[0, :N_EXPERTS].astype(i32)
    padded = (counts + tm - 1) // tm * tm
    ends = jnp.cumsum(padded)
    offsets = ends - padded
    experts = route[:, 0:TOP_K].astype(i32)
    ranks = route[:, TOP_K:2 * TOP_K].astype(i32)
    pos = (offsets[experts] + ranks).reshape(n // tt, 1, TOP_K * tt)
    n_tiles = TOP_K * n // tm + N_EXPERTS
    n_used = (ends[-1] // tm).astype(i32).reshape(1)
    tile_ids = jnp.arange(n_tiles, dtype=i32)
    tile_expert = jnp.sum((tile_ids[:, None] >= (ends // tm)[None, :]).astype(i32), axis=1)
    last_expert = jnp.max(jnp.where(counts > 0, jnp.arange(N_EXPERTS, dtype=i32), 0))
    tile_expert = jnp.minimum(tile_expert, last_expert).astype(i32)
    tail_tiles = n_used[0] + jnp.arange(N_EXPERTS, dtype=i32)
    fill = jnp.concatenate([ends - tm, tail_tiles * tm,
                            (counts > 0).astype(i32), (tail_tiles < n_tiles).astype(i32)]).astype(i32)

    x_sorted = _dispatch(x1, pos, fill, n_tiles * tm, tt=tt, tm=tm)
    y_sorted = _ffn(x_sorted, tile_expert, n_used, w_gate, w_up, w_down, tm=tm, tf=tf,
                    name="ffn_experts")
    return _combine(pos, route, x1, y_sorted, ln_g, ln_b, tt=tt)


def kernel(x, w_in, conv_w, conv_b, w_a, b_a, w_x, b_x, lru_lambda, rel_bias, g_attn, g_lru, w_out, ln1_g, ln1_b, ln2_g, ln2_b, ffn_w_gate, ffn_w_up, ffn_w_down, router_w, moe_w_gate, moe_w_up, moe_w_down):
    batch, seq, _ = x.shape
    n = batch * seq
    h = x.reshape(n, D_MODEL).astype(f32)
    band_bias = jnp.stack([_band_bias(rel_bias, d) for _, d in DILATED_PATTERNS])
    for layer in range(DEPTH):
        qkv, ylru = _proj_lru(h, w_in[layer].astype(bf16), conv_w[layer], conv_b[layer],
                              w_a[layer], b_a[layer], w_x[layer], b_x[layer], lru_lambda[layer],
                              g_lru[layer], batch, seq)
        attn = _attention(qkv, band_bias, batch, seq)
        j = layer // 2
        dense = layer % 2 == 0
        x1, route, counts = _mix_out(
            attn, ylru, h, w_out[layer].astype(bf16), g_attn[layer], ln1_g[layer],
            ln1_b[layer], None if dense else router_w[j])
        if dense:
            tm = DENSE_FFN_ROWS
            h = _ffn(x1, jnp.zeros((n // tm,), jnp.int32), jnp.full((1,), n // tm, jnp.int32),
                     ffn_w_gate[j][None].astype(bf16), ffn_w_up[j][None].astype(bf16),
                     ffn_w_down[j][None].astype(bf16), (ln2_g[layer], ln2_b[layer]),
                     tm=tm, tf=ffn_w_gate.shape[2], name="ffn_dense")
        else:
            h = _moe(x1, route, counts, moe_w_gate[j].astype(bf16), moe_w_up[j].astype(bf16),
                     moe_w_down[j].astype(bf16), ln2_g[layer], ln2_b[layer],
                     tm=EXPERT_ROWS, tf=EXPERT_FF_CHUNK, tt=ROUTE_ROWS)
    return h.reshape(batch, seq, D_MODEL).astype(x.dtype)
```

```python
import functools

import numpy as np
import jax
import jax.numpy as jnp
from jax import lax
from jax.experimental import pallas as pl
from jax.experimental.pallas import tpu as pltpu

D_MODEL = 1024
N_HEADS = 8
HEAD_DIM = 64
D_ATTN = N_HEADS * HEAD_DIM
D_LRU = 512
N_LRU_BLOCKS = 8
LRU_BLOCK = D_LRU // N_LRU_BLOCKS
CONV_WIDTH = 4
LRU_C = 8.0
DILATED_PATTERNS = ((128, 1), (512, 4), (2048, 16))
ATTN_BLOCK = 128
STAGE_DILATION = 4
N_BUCKETS = 32
MAX_DISTANCE = 2048
D_IN = 3 * D_ATTN + 2 * D_LRU
N_EXPERTS = 8
TOP_K = 2
DEPTH = 2
DEEPNORM_ALPHA = (2.0 * DEPTH) ** 0.25
LN_EPS = 1e-5
RMS_EPS = 1e-6
NEG_INF = -1e30
LOG2_E = float(np.log2(np.e))

LANES = 128
SUBLANES = 8
VMEM_LIMIT_BYTES = 56 * 1024 * 1024
MXU_DEPTH = 256

LRU_ROWS = 1024
MIX_ROWS = 1024
ROUTER_ROWS = 512
DENSE_FFN_ROWS = 512
EXPERT_ROWS = 512
EXPERT_FF_CHUNK = 7 * MXU_DEPTH
DISPATCH_ROWS = 1024
COMBINE_ROWS = 512
DMA_UNROLL = 8
ATTN_UNROLL = 4

f32 = jnp.float32
bf16 = jnp.bfloat16


def _params(*semantics):
    return pltpu.CompilerParams(dimension_semantics=semantics,
                                vmem_limit_bytes=VMEM_LIMIT_BYTES)


def _t5_bucket(dist):
    max_exact = N_BUCKETS // 2
    d = np.maximum(dist, 1).astype(np.float32)
    large = max_exact + (np.log(d / max_exact) / np.log(MAX_DISTANCE / max_exact)
                         * (N_BUCKETS - max_exact)).astype(np.int32)
    large = np.minimum(large, N_BUCKETS - 1)
    return np.where(dist < max_exact, dist, large).astype(np.int32)


def _band_bias(rel_bias, dilation):
    nk = ATTN_BLOCK
    qi = np.arange(nk)[:, None]
    kj = np.arange(2 * nk)[None, :]
    delta = qi + nk - kj
    band = (delta >= 0) & (delta <= nk)
    bucket = _t5_bucket(np.clip(delta, 0, nk) * dilation)
    onehot = np.eye(N_BUCKETS, dtype=np.float32)[bucket.reshape(-1)]
    bias = jnp.dot(jnp.asarray(onehot), rel_bias.astype(f32), precision=lax.Precision.HIGHEST)
    bias = jnp.transpose(bias.reshape(nk, 2 * nk, N_HEADS), (2, 0, 1))
    valid = np.stack([band, band & (kj >= nk)])[:, None]
    bias = jnp.where(jnp.asarray(valid), bias[None], NEG_INF)
    return (bias * LOG2_E).reshape(2, N_HEADS // 2, 2 * nk, 2 * nk)


def _attn_kernel(q_ref, k_ref, v_ref, bias_ref, o_ref, q4, k4, v4, qs, ks, vs, s_buf, m_buf,
                 m_s, l_s, acc_s, *, seq):
    nk = ATTN_BLOCK
    n_blocks = seq // nk
    chunk = 2 * nk
    lane = lax.broadcasted_iota(jnp.int32, (1, LANES), 1)
    head0 = lane < HEAD_DIM

    ks[0:nk, :] = jnp.zeros((nk, LANES), bf16)
    vs[0:nk, 0:LANES] = jnp.zeros((nk, LANES), bf16)
    vs[:, LANES:2 * LANES] = jnp.ones((seq + nk, LANES), bf16)

    def stage(t, carry):
        chunks_per_residue = seq // STAGE_DILATION // chunk
        src = pl.ds(t // chunks_per_residue + STAGE_DILATION * chunk * (t % chunks_per_residue),
                    chunk, stride=STAGE_DILATION)
        dst = pl.ds(pl.multiple_of(t * chunk, chunk), chunk)
        q4[dst, :] = q_ref[src, :]
        k4[dst, :] = k_ref[src, :]
        v4[dst, :] = v_ref[src, :]
        return carry

    lax.fori_loop(0, seq // chunk, stage, 0)

    order = sorted(range(len(DILATED_PATTERNS)), key=lambda p: -DILATED_PATTERNS[p][1])
    assert DILATED_PATTERNS[order[-1]][1] == 1
    for p in order:
        d = DILATED_PATTERNS[p][1]
        is_first, is_last = p == order[0], p == order[-1]
        length = seq // d
        nb = length // nk
        chunks_per_residue = length // chunk

        def gather(t, carry):
            r = t // chunks_per_residue
            c = t % chunks_per_residue
            dst = pl.multiple_of(t * chunk, chunk)
            if d % STAGE_DILATION == 0:
                sub = d // STAGE_DILATION
                start = ((r % STAGE_DILATION) * (seq // STAGE_DILATION) + r // STAGE_DILATION
                         + sub * chunk * c)
                src = (pl.ds(start, chunk, stride=sub) if sub > 1
                       else pl.ds(pl.multiple_of(start, chunk), chunk))
                q, k, v = q4[src, :], k4[src, :], v4[src, :]
            else:
                src = pl.ds(r + d * chunk * c, chunk, stride=d) if d > 1 else pl.ds(dst, chunk)
                q, k, v = q_ref[src, :], k_ref[src, :], v_ref[src, :]
            q = q * (HEAD_DIM ** -0.5 * LOG2_E)
            q0 = jnp.where(head0, q, 0.0).astype(bf16)
            q1 = jnp.where(head0, 0.0, q).astype(bf16)
            for half in range(2):
                base = pl.multiple_of(2 * dst + half * chunk, chunk)
                qs[pl.ds(base, nk), :] = q0[half * nk:(half + 1) * nk]
                qs[pl.ds(base + nk, nk), :] = q1[half * nk:(half + 1) * nk]
            ks[pl.ds(nk + dst, chunk), :] = k.astype(bf16)
            vs[pl.ds(nk + dst, chunk), 0:LANES] = v.astype(bf16)
            return carry

        lax.fori_loop(0, seq // chunk, gather, 0)

        def scores(b, carry):
            first = jnp.asarray(b % nb == 0, jnp.int32)
            rows = pl.ds(pl.multiple_of(b * chunk, chunk), chunk)
            k = ks[pl.ds(pl.multiple_of(b * nk, nk), chunk), :]
            s_buf[rows, :] = lax.dot_general(qs[rows, :], k, (((1,), (1,)), ((), ())),
                                             preferred_element_type=f32) + bias_ref[p, first]
            return carry

        def rowmax(b, carry):
            rows = pl.ds(pl.multiple_of(b * chunk, chunk), chunk)
            m_buf[rows, :] = jnp.broadcast_to(jnp.max(s_buf[rows, :], axis=-1, keepdims=True),
                                              (chunk, LANES))
            return carry

        def block(b, carry):
            r = b // nb
            i = b % nb
            rows = pl.ds(pl.multiple_of(b * chunk, chunk), chunk)
            v = vs[pl.ds(pl.multiple_of(b * nk, nk), chunk), :]
            m = m_buf[rows, :]
            e = jnp.exp2(s_buf[rows, :] - jnp.concatenate([m, m], axis=1))
            pv = jnp.dot(e.astype(bf16), v, preferred_element_type=f32)
            m_blk = jnp.where(head0, m[0:nk], m[nk:chunk])
            l_blk = jnp.where(head0, pv[0:nk, LANES:], pv[nk:chunk, LANES:])
            pv_blk = jnp.where(head0, pv[0:nk, 0:LANES], pv[nk:chunk, 0:LANES])
            if d == 1:
                tok = pl.ds(pl.multiple_of(b * nk, nk), nk)
            else:
                tok = pl.ds(r + d * nk * i, nk, stride=d)
            if is_first:
                m_s[tok, :] = m_blk
                l_s[tok, :] = l_blk
                acc_s[tok, :] = pv_blk
                return carry
            m_old = m_s[tok, :]
            m_new = jnp.maximum(m_old, m_blk)
            w_old = jnp.exp2(m_old - m_new)
            w_blk = jnp.exp2(m_blk - m_new)
            l_new = w_old * l_s[tok, :] + w_blk * l_blk
            acc_new = w_old * acc_s[tok, :] + w_blk * pv_blk
            if is_last:
                o_ref[tok, :] = (acc_new / l_new).astype(o_ref.dtype)
            else:
                l_s[tok, :] = l_new
                acc_s[tok, :] = acc_new
                m_s[tok, :] = m_new
            return carry

        n_groups = n_blocks // ATTN_UNROLL
        assert n_groups >= 3

        def group(fn, g):
            for j in range(ATTN_UNROLL):
                fn(g * ATTN_UNROLL + j, 0)

        def pipelined(g, carry):
            group(block, g)
            group(rowmax, g + 1)
            group(scores, g + 2)
            return carry

        group(scores, 0)
        group(scores, 1)
        group(rowmax, 0)
        lax.fori_loop(0, n_groups - 2, pipelined, 0)
        group(block, n_groups - 2)
        group(rowmax, n_groups - 1)
        group(block, n_groups - 1)


def _attention(qkv, band_bias, batch, seq):
    view = qkv.reshape(batch, seq, 3 * D_ATTN)
    pairs = D_ATTN // LANES

    def spec(offset):
        return pl.BlockSpec((None, seq, LANES), lambda b, hp: (b, 0, offset * pairs + hp))

    n_pat = len(DILATED_PATTERNS)
    o = pl.pallas_call(
        functools.partial(_attn_kernel, seq=seq),
        grid=(batch, pairs),
        in_specs=[spec(0), spec(1), spec(2),
                  pl.BlockSpec((n_pat, 2, None, 2 * ATTN_BLOCK, 2 * ATTN_BLOCK),
                               lambda b, hp: (0, 0, hp, 0, 0))],
        out_specs=pl.BlockSpec((None, seq, LANES), lambda b, hp: (b, 0, hp)),
        out_shape=jax.ShapeDtypeStruct((batch, seq, D_ATTN), bf16),
        scratch_shapes=[pltpu.VMEM((seq, LANES), f32), pltpu.VMEM((seq, LANES), f32),
                        pltpu.VMEM((seq, LANES), f32),
                        pltpu.VMEM((2 * seq, LANES), bf16),
                        pltpu.VMEM((seq + ATTN_BLOCK, LANES), bf16),
                        pltpu.VMEM((seq + ATTN_BLOCK, 2 * LANES), bf16),
                        pltpu.VMEM((2 * seq, 2 * ATTN_BLOCK), f32),
                        pltpu.VMEM((2 * seq, LANES), f32),
                        pltpu.VMEM((seq, LANES), f32), pltpu.VMEM((seq, LANES), f32),
                        pltpu.VMEM((seq, LANES), f32)],
        compiler_params=_params("parallel", "parallel"),
        name="attention",
    )(view, view, view, band_bias)
    return o.reshape(batch * seq, D_ATTN)


def _gelu_tanh(x):
    return 0.5 * x * (1.0 + jnp.tanh(np.sqrt(2.0 / np.pi) * (x + 0.044715 * x * x * x)))


def _proj_lru_kernel(x_ref, w_ref, convw_ref, convb_ref, wg_ref, ba_ref, bx_ref, lam_ref, g_ref,
                     qkv_ref, y_ref, ubuf, hcarry, *, ts):
    pad = SUBLANES
    t = pl.program_id(1)
    n_qkv = qkv_ref.shape[1]
    xb = x_ref[...].astype(bf16)

    @pl.when(t == 0)
    def _():
        ubuf[0:pad, :] = jnp.zeros((pad, D_LRU), f32)
        hcarry[...] = jnp.zeros_like(hcarry)

    @pl.when(t > 0)
    def _():
        ubuf[0:pad, :] = ubuf[ts:ts + pad, :]

    ubuf[pad:pad + ts, :] = jnp.dot(xb, w_ref[:, n_qkv:n_qkv + D_LRU],
                                    preferred_element_type=f32)
    gate = jnp.dot(xb, w_ref[:, n_qkv + D_LRU:], preferred_element_type=f32)

    qkv_step = n_qkv // 3
    assert qkv_step == D_LRU

    def qkv_chunk(c):
        cols = slice(c * qkv_step, (c + 1) * qkv_step)
        p = jnp.dot(xb, w_ref[:, cols], preferred_element_type=f32)
        qkv_ref[:, cols] = p
        bits = lax.bitcast_convert_type(p, jnp.uint32).reshape(ts // SUBLANES, SUBLANES, qkv_step)
        folded = functools.reduce(jnp.bitwise_or, [bits[g] for g in range(ts // SUBLANES)])
        return ((folded >> 16) >> 16).astype(f32)

    u = convb_ref[...] + convw_ref[CONV_WIDTH - 1:CONV_WIDTH, :] * ubuf[pad:pad + ts, :]
    for w in range(CONV_WIDTH - 1):
        back = CONV_WIDTH - 1 - w
        u = u + convw_ref[w:w + 1, :] * ubuf[pad - back:pad - back + ts, :]

    gates = jnp.dot(u.astype(bf16), wg_ref[...], preferred_element_type=f32)
    r = jax.nn.sigmoid(gates[:, 0:D_LRU] + ba_ref[...])
    i = jax.nn.sigmoid(gates[:, D_LRU:2 * D_LRU] + bx_ref[...])
    neg_lam = -lam_ref[...]
    softplus = jnp.maximum(neg_lam, 0.0) + jnp.log1p(jnp.exp(-jnp.abs(neg_lam)))
    log_a = (-LRU_C) * r * softplus
    a = jnp.exp(log_a)
    b = jnp.sqrt(-jnp.tanh(log_a) * (1.0 + a * a)) * (i * u)
    zero0 = qkv_chunk(0)

    groups = ts // SUBLANES
    a = a.reshape(groups, SUBLANES, D_LRU)
    b = b.reshape(groups, SUBLANES, D_LRU) + zero0[None]
    sub = lax.broadcasted_iota(jnp.int32, (1, SUBLANES, 1), 1)
    shift = 1
    while shift < SUBLANES:
        live = sub >= shift
        a_prev = jnp.where(live, pltpu.roll(a, shift, 1), 1.0)
        b_prev = jnp.where(live, pltpu.roll(b, shift, 1), 0.0)
        b = a * b_prev + b
        a = a * a_prev
        shift *= 2
    h_prev = hcarry[...] + qkv_chunk(1)[0:1, :]
    h_groups = []
    for g in range(groups):
        h_g = a[g] * h_prev + b[g]
        h_prev = h_g[SUBLANES - 1:SUBLANES, :]
        h_groups.append(h_g)
    hcarry[...] = h_prev
    h = jnp.concatenate(h_groups, axis=0)
    zero2 = qkv_chunk(2)[0:1, 0:1]

    y = _gelu_tanh(gate) * h
    y = y * lax.rsqrt(jnp.mean(y * y, axis=-1, keepdims=True) + (RMS_EPS + zero2)) * g_ref[...]
    y_ref[...] = y.astype(y_ref.dtype)


def _block_diag(w):
    g, i, j = w.shape
    eye = jnp.eye(g, dtype=w.dtype)
    return jnp.einsum('gij,gh->gihj', w, eye).reshape(g * i, g * j)


def _proj_lru(x2d, w_in_bf16, conv_w, conv_b, w_a, b_a, w_x, b_x, lam, g_lru, batch, seq,
              ts=LRU_ROWS):
    wg = jnp.concatenate([_block_diag(w_a), _block_diag(w_x)], axis=1).astype(bf16)
    row = lambda v: v.reshape(1, D_LRU).astype(f32)
    const = lambda shape: pl.BlockSpec(shape, lambda b, t: (0, 0))
    tile = lambda width: pl.BlockSpec((None, ts, width), lambda b, t: (b, t, 0))
    qkv, y = pl.pallas_call(
        functools.partial(_proj_lru_kernel, ts=ts),
        grid=(batch, seq // ts),
        in_specs=[tile(D_MODEL), const((D_MODEL, D_IN)),
                  const((CONV_WIDTH, D_LRU)), const((1, D_LRU)),
                  const((D_LRU, 2 * D_LRU)), const((1, D_LRU)), const((1, D_LRU)),
                  const((1, D_LRU)), const((1, D_LRU))],
        out_specs=[tile(3 * D_ATTN), tile(D_LRU)],
        out_shape=[jax.ShapeDtypeStruct((batch, seq, 3 * D_ATTN), f32),
                   jax.ShapeDtypeStruct((batch, seq, D_LRU), bf16)],
        scratch_shapes=[pltpu.VMEM((ts + 2 * SUBLANES, D_LRU), f32), pltpu.VMEM((1, D_LRU), f32)],
        compiler_params=_params("parallel", "arbitrary"),
        name="proj_rglru",
    )(x2d.reshape(batch, seq, D_MODEL), w_in_bf16,
      conv_w.reshape(CONV_WIDTH, D_LRU).astype(f32), row(conv_b), wg, row(b_a), row(b_x),
      row(lam), row(g_lru))
    return qkv.reshape(batch * seq, 3 * D_ATTN), y.reshape(batch * seq, D_LRU)


ROW_TILE = D_MODEL // LANES


def _store_row_tiles(ref, rows):
    t = rows.shape[0]
    for s in range(ROW_TILE):
        ref[pl.ds(s, t, stride=ROW_TILE), :] = rows[:, s * LANES:(s + 1) * LANES]


def _load_row_tiles(ref, t):
    return jnp.concatenate([ref[pl.ds(s, t, stride=ROW_TILE), :] for s in range(ROW_TILE)],
                           axis=-1)


def _layer_norm(z, g, b):
    mu = jnp.mean(z, axis=-1, keepdims=True)
    zc = z - mu
    var = jnp.mean(zc * zc, axis=-1, keepdims=True)
    return zc * lax.rsqrt(var + LN_EPS) * g + b


def _mix_out_kernel(attn_ref, ylru, x_ref, w_ref, gattn, lng, lnb, *rest, with_router):
    if with_router:
        rw_ref, tri_ref, x1_ref, route_ref, count_ref = rest
    else:
        (x1_ref,) = rest
    attn = attn_ref[...].astype(f32)
    attn = attn * lax.rsqrt(jnp.mean(attn * attn, axis=-1, keepdims=True) + RMS_EPS) * gattn[...]
    y = jnp.dot(attn.astype(bf16), w_ref[0:D_ATTN, :], preferred_element_type=f32)
    y = y + jnp.dot(ylru[...], w_ref[D_ATTN:, :], preferred_element_type=f32)
    x1 = _layer_norm(DEEPNORM_ALPHA * x_ref[...] + y, lng[...], lnb[...])
    x1_ref[...] = x1
    if with_router:
        lane = lax.broadcasted_iota(jnp.int32, (1, LANES), 1).astype(f32)
        x_hi = x1.astype(bf16)
        x_lo = (x1 - x_hi.astype(f32)).astype(bf16)
        logits = (jnp.dot(x_hi, rw_ref[0], preferred_element_type=f32)
                  + jnp.dot(x_lo, rw_ref[0], preferred_element_type=f32)
                  + jnp.dot(x_hi, rw_ref[1], preferred_element_type=f32))
        logits = jnp.where(lane < N_EXPERTS, logits, -jnp.inf)
        v1 = jnp.max(logits, axis=-1, keepdims=True)
        i1 = jnp.min(jnp.where(logits == v1, lane, float(LANES)), axis=-1, keepdims=True)
        rest_logits = jnp.where(lane == i1, -jnp.inf, logits)
        v2 = jnp.max(rest_logits, axis=-1, keepdims=True)
        i2 = jnp.min(jnp.where(rest_logits == v2, lane, float(LANES)), axis=-1, keepdims=True)
        e2 = jnp.exp(v2 - v1)
        p1 = 1.0 / (1.0 + e2)
        p2 = e2 / (1.0 + e2)

        @pl.when(pl.program_id(0) == 0)
        def _():
            count_ref[...] = jnp.zeros_like(count_ref)

        tm = x1.shape[0]
        chosen = jnp.logical_or(lane == i1, lane == i2)
        rank = count_ref[...] + jnp.dot(tri_ref[...], chosen.astype(bf16),
                                        preferred_element_type=f32)
        count_ref[...] += jnp.sum(chosen.astype(f32), axis=0, keepdims=True)
        r1 = jnp.sum(jnp.where(lane == i1, rank, 0.0), axis=-1, keepdims=True)
        r2 = jnp.sum(jnp.where(lane == i2, rank, 0.0), axis=-1, keepdims=True)
        fields = (i1, i2, r1, r2, p1, p2)
        route = jnp.zeros((tm, LANES), f32)
        for k, val in enumerate(fields):
            route = jnp.where(lane == k, val, route)
        route_ref[...] = route


def _mix_out(attn, ylru, x2d, w_out_bf16, g_attn, ln_g, ln_b, router_w=None):
    n = x2d.shape[0]
    with_router = router_w is not None
    tm = ROUTER_ROWS if with_router else MIX_ROWS
    tile = lambda width: pl.BlockSpec((tm, width), lambda i: (i, 0))
    const = lambda shape: pl.BlockSpec(shape, lambda i: (0, 0))
    in_specs = [tile(D_ATTN), tile(D_LRU), tile(D_MODEL), const((D_MODEL, D_MODEL)),
                const((1, D_ATTN)), const((1, D_MODEL)), const((1, D_MODEL))]
    args = [attn, ylru, x2d, w_out_bf16, g_attn.reshape(1, D_ATTN).astype(f32),
            ln_g.reshape(1, D_MODEL).astype(f32), ln_b.reshape(1, D_MODEL).astype(f32)]
    out_specs = [tile(D_MODEL)]
    out_shape = [jax.ShapeDtypeStruct((n, D_MODEL), f32)]
    if with_router:
        rw = jnp.zeros((D_MODEL, LANES), f32).at[:, :N_EXPERTS].set(router_w.astype(f32))
        rw_hi = rw.astype(bf16)
        rw_lo = (rw - rw_hi.astype(f32)).astype(bf16)
        strictly_lower = jnp.asarray(np.tri(tm, k=-1), bf16)
        in_specs += [pl.BlockSpec((2, D_MODEL, LANES), lambda i: (0, 0, 0)), const((tm, tm))]
        args += [jnp.stack([rw_hi, rw_lo]), strictly_lower]
        out_specs += [tile(LANES), const((1, LANES))]
        out_shape += [jax.ShapeDtypeStruct((n, LANES), f32), jax.ShapeDtypeStruct((1, LANES), f32)]
    res = pl.pallas_call(
        functools.partial(_mix_out_kernel, with_router=with_router),
        grid=(n // tm,),
        in_specs=in_specs,
        out_specs=out_specs,
        out_shape=out_shape,
        compiler_params=_params("arbitrary" if with_router else "parallel"),
        name="mix_out_router" if with_router else "mix_out",
    )(*args)
    return res if with_router else (res[0], None, None)


def _ffn_kernel(tile_expert_ref, n_used_ref, x_ref, wg_ref, wu_ref, wd_ref, *rest,
                n_chunks, fuse_ln):
    if fuse_ln:
        lng, lnb, out_ref, acc_ref, xb_ref = rest
    else:
        out_ref, acc_ref, xb_ref = rest
    del tile_expert_ref
    j = pl.program_id(1)
    used = pl.program_id(0) < n_used_ref[0]
    last = n_chunks - 1

    def load_x():
        if fuse_ln:
            return x_ref[...].astype(bf16)
        return _load_row_tiles(x_ref, xb_ref.shape[0]).astype(bf16)

    def chunk_out(xb):
        g = jnp.dot(xb, wg_ref[...], preferred_element_type=f32)
        u = jnp.dot(xb, wu_ref[...], preferred_element_type=f32)
        h = (g * jax.nn.sigmoid(g)) * u
        return jnp.dot(h.astype(bf16), wd_ref[...], preferred_element_type=f32)

    def finish(y):
        if fuse_ln:
            out_ref[...] = _layer_norm(DEEPNORM_ALPHA * x_ref[...] + y, lng[...], lnb[...])
        else:
            _store_row_tiles(out_ref, y)

    if n_chunks == 1:
        @pl.when(used)
        def _():
            finish(chunk_out(load_x()))
    else:
        @pl.when(used & (j == 0))
        def _():
            xb = load_x()
            xb_ref[...] = xb
            acc_ref[...] = chunk_out(xb)

        if n_chunks > 2:
            @pl.when(used & (j > 0) & (j < last))
            def _():
                acc_ref[...] += chunk_out(xb_ref[...])

        @pl.when(used & (j == last))
        def _():
            finish(acc_ref[...] + chunk_out(xb_ref[...]))

    @pl.when(jnp.logical_not(used) & (j == last))
    def _():
        finish(jnp.zeros(acc_ref.shape, f32))


def _ffn(x2d, tile_expert, n_used, w_gate, w_up, w_down, ln=None, *, tm, tf, name):
    fuse_ln = ln is not None
    rows = x2d.shape[0] if fuse_ln else x2d.shape[0] // ROW_TILE
    d_ff = w_gate.shape[2]
    n_chunks = d_ff // tf
    io_block = (tm, D_MODEL) if fuse_ln else (tm * ROW_TILE, LANES)

    def chunk(i, j, nu):
        return jnp.where(i < nu[0], j, n_chunks - 1)

    resident = w_gate.shape[0] == 1 and n_chunks == 1
    mode = dict(pipeline_mode=pl.Buffered(1)) if resident else {}
    in_specs = [pl.BlockSpec(io_block, lambda i, j, te, nu: (i, 0)),
                pl.BlockSpec((None, D_MODEL, tf),
                             lambda i, j, te, nu: (te[i], 0, chunk(i, j, nu)), **mode),
                pl.BlockSpec((None, D_MODEL, tf),
                             lambda i, j, te, nu: (te[i], 0, chunk(i, j, nu)), **mode),
                pl.BlockSpec((None, tf, D_MODEL),
                             lambda i, j, te, nu: (te[i], chunk(i, j, nu), 0), **mode)]
    args = [x2d, w_gate, w_up, w_down]
    if fuse_ln:
        in_specs += [pl.BlockSpec((1, D_MODEL), lambda i, j, te, nu: (0, 0))] * 2
        args += [v.reshape(1, D_MODEL).astype(f32) for v in ln]
    return pl.pallas_call(
        functools.partial(_ffn_kernel, n_chunks=n_chunks, fuse_ln=fuse_ln),
        grid_spec=pltpu.PrefetchScalarGridSpec(
            num_scalar_prefetch=2,
            grid=(rows // tm, n_chunks),
            in_specs=in_specs,
            out_specs=pl.BlockSpec(io_block, lambda i, j, te, nu: (i, 0)),
            scratch_shapes=[pltpu.VMEM((tm, D_MODEL), f32), pltpu.VMEM((tm, D_MODEL), bf16)]),
        out_shape=jax.ShapeDtypeStruct(x2d.shape, f32),
        compiler_params=_params("parallel", "arbitrary"),
        name=name,
    )(tile_expert, n_used, *args)


def _dispatch_kernel(fill_ref, pos_ref, x_ref, xs_hbm, zero_buf, rows_buf, sem, *, tt, tm):
    step = pl.program_id(0)

    n_fill = fill_ref.shape[0] // 2

    @pl.when(step == 0)
    def _():
        zero_buf[...] = jnp.zeros_like(zero_buf)

        def fill_copy(f):
            start = pl.multiple_of(fill_ref[f] * ROW_TILE, ROW_TILE)
            return pltpu.make_async_copy(zero_buf, xs_hbm.at[pl.ds(start, tm * ROW_TILE)], sem)

        for f in range(n_fill):
            @pl.when(fill_ref[n_fill + f] > 0)
            def _():
                fill_copy(f).start()
        for f in range(n_fill):
            @pl.when(fill_ref[n_fill + f] > 0)
            def _():
                fill_copy(f).wait()

    _store_row_tiles(rows_buf, x_ref[...])

    def issue(t, carry):
        src = pl.multiple_of(t * ROW_TILE, ROW_TILE)
        for k in range(TOP_K):
            dst = pl.multiple_of(pos_ref[0, TOP_K * t + k] * ROW_TILE, ROW_TILE)
            pltpu.make_async_copy(rows_buf.at[pl.ds(src, ROW_TILE)],
                                  xs_hbm.at[pl.ds(dst, ROW_TILE)], sem).start(priority=k)
        return carry

    lax.fori_loop(0, tt, issue, 0, unroll=DMA_UNROLL)
    for _ in range(TOP_K):
        pltpu.make_async_copy(rows_buf, xs_hbm.at[pl.ds(0, tt * ROW_TILE)], sem).wait()


def _dispatch(x1, pos_blocks, fill, rows_sorted, *, tt, tm):
    n = x1.shape[0]
    return pl.pallas_call(
        functools.partial(_dispatch_kernel, tt=tt, tm=tm),
        grid_spec=pltpu.PrefetchScalarGridSpec(
            num_scalar_prefetch=1,
            grid=(n // tt,),
            in_specs=[pl.BlockSpec((None, 1, TOP_K * tt), lambda i, fill: (i, 0, 0),
                                   memory_space=pltpu.SMEM),
                      pl.BlockSpec((tt, D_MODEL), lambda i, fill: (i, 0))],
            out_specs=pl.BlockSpec(memory_space=pl.ANY),
            scratch_shapes=[pltpu.VMEM((tm * ROW_TILE, LANES), f32),
                            pltpu.VMEM((tt * ROW_TILE, LANES), f32),
                            pltpu.SemaphoreType.DMA(())]),
        out_shape=jax.ShapeDtypeStruct((rows_sorted * ROW_TILE, LANES), f32),
        compiler_params=_params("arbitrary"),
        name="moe_dispatch",
    )(fill, pos_blocks, x1)


def _combine_kernel(pos_ref, pos_next_ref, route_ref, x1_ref, ys_hbm, lng, lnb, out_ref, buf, sems,
                    *, tt, n_steps):
    step = pl.program_id(0)
    slot = step % 2

    def issue_tile(tile_pos_ref, into):
        def issue(t, carry):
            dst = pl.multiple_of(t * ROW_TILE, ROW_TILE)
            for k in range(TOP_K):
                src = pl.multiple_of(tile_pos_ref[0, TOP_K * t + k] * ROW_TILE, ROW_TILE)
                pltpu.make_async_copy(ys_hbm.at[pl.ds(src, ROW_TILE)],
                                      buf.at[into, k, pl.ds(dst, ROW_TILE)],
                                      sems.at[into]).start(priority=k)
            return carry

        lax.fori_loop(0, tt, issue, 0, unroll=DMA_UNROLL)

    @pl.when(step == 0)
    def _():
        issue_tile(pos_ref, 0)

    @pl.when(step + 1 < n_steps)
    def _():
        issue_tile(pos_next_ref, 1 - slot)

    for k in range(TOP_K):
        pltpu.make_async_copy(ys_hbm.at[pl.ds(0, tt * ROW_TILE)], buf.at[slot, k],
                              sems.at[slot]).wait()

    lane = lax.broadcasted_iota(jnp.int32, (1, LANES), 1)
    route = route_ref[...]
    p1 = jnp.sum(jnp.where(lane == 4, route, 0.0), axis=-1, keepdims=True)
    p2 = jnp.sum(jnp.where(lane == 5, route, 0.0), axis=-1, keepdims=True)
    y = p1 * _load_row_tiles(buf.at[slot, 0], tt) + p2 * _load_row_tiles(buf.at[slot, 1], tt)
    out_ref[...] = _layer_norm(DEEPNORM_ALPHA * x1_ref[...] + y, lng[...], lnb[...])


def _combine(pos_blocks, route, x1, y_sorted, ln_g, ln_b, *, tt):
    n = x1.shape[0]
    n_steps = n // tt
    pos_spec = lambda index: pl.BlockSpec((None, 1, TOP_K * tt), index, memory_space=pltpu.SMEM)
    return pl.pallas_call(
        functools.partial(_combine_kernel, tt=tt, n_steps=n_steps),
        grid=(n_steps,),
        in_specs=[pos_spec(lambda i: (i, 0, 0)),
                  pos_spec(lambda i: (jnp.minimum(i + 1, n_steps - 1), 0, 0)),
                  pl.BlockSpec((tt, LANES), lambda i: (i, 0)),
                  pl.BlockSpec((tt, D_MODEL), lambda i: (i, 0)),
                  pl.BlockSpec(memory_space=pl.ANY),
                  pl.BlockSpec((1, D_MODEL), lambda i: (0, 0)),
                  pl.BlockSpec((1, D_MODEL), lambda i: (0, 0))],
        out_specs=pl.BlockSpec((tt, D_MODEL), lambda i: (i, 0)),
        out_shape=jax.ShapeDtypeStruct((n, D_MODEL), f32),
        scratch_shapes=[pltpu.VMEM((2, TOP_K, tt * ROW_TILE, LANES), f32),
                        pltpu.SemaphoreType.DMA((2,))],
        compiler_params=_params("arbitrary"),
        name="moe_combine",
    )(pos_blocks, pos_blocks, route, x1, y_sorted, ln_g.reshape(1, D_MODEL).astype(f32),
      ln_b.reshape(1, D_MODEL).astype(f32))


def _moe(x1, route, counts, w_gate, w_up, w_down, ln_g, ln_b, *, tm, tf):
    n = x1.shape[0]
    i32 = jnp.int32
    counts = counts[0, :N_EXPERTS].astype(i32)
    padded = (counts + tm - 1) // tm * tm
    ends = jnp.cumsum(padded)
    offsets = ends - padded
    experts = route[:, 0:TOP_K].astype(i32)
    ranks = route[:, TOP_K:2 * TOP_K].astype(i32)
    pos = offsets[experts] + ranks
    pos_blocks = lambda tt: pos.reshape(n // tt, 1, TOP_K * tt)
    n_tiles = TOP_K * n // tm + N_EXPERTS
    n_used = (ends[-1] // tm).astype(i32).reshape(1)
    tile_ids = jnp.arange(n_tiles, dtype=i32)
    tile_expert = jnp.sum((tile_ids[:, None] >= (ends // tm)[None, :]).astype(i32), axis=1)
    last_expert = jnp.max(jnp.where(counts > 0, jnp.arange(N_EXPERTS, dtype=i32), 0))
    tile_expert = jnp.minimum(tile_expert, last_expert).astype(i32)
    tail_tiles = n_used[0] + jnp.arange(N_EXPERTS, dtype=i32)
    fill = jnp.concatenate([ends - tm, tail_tiles * tm,
                            (counts > 0).astype(i32), (tail_tiles < n_tiles).astype(i32)]).astype(i32)

    x_sorted = _dispatch(x1, pos_blocks(DISPATCH_ROWS), fill, n_tiles * tm, tt=DISPATCH_ROWS,
                         tm=tm)
    y_sorted = _ffn(x_sorted, tile_expert, n_used, w_gate, w_up, w_down, tm=tm, tf=tf,
                    name="ffn_experts")
    return _combine(pos_blocks(COMBINE_ROWS), route, x1, y_sorted, ln_g, ln_b, tt=COMBINE_ROWS)


def kernel(x, w_in, conv_w, conv_b, w_a, b_a, w_x, b_x, lru_lambda, rel_bias, g_attn, g_lru, w_out, ln1_g, ln1_b, ln2_g, ln2_b, ffn_w_gate, ffn_w_up, ffn_w_down, router_w, moe_w_gate, moe_w_up, moe_w_down):
    batch, seq, _ = x.shape
    n = batch * seq
    h = x.reshape(n, D_MODEL).astype(f32)
    band_bias = jnp.stack([_band_bias(rel_bias, d) for _, d in DILATED_PATTERNS])
    for layer in range(DEPTH):
        qkv, ylru = _proj_lru(h, w_in[layer].astype(bf16), conv_w[layer], conv_b[layer],
                              w_a[layer], b_a[layer], w_x[layer], b_x[layer], lru_lambda[layer],
                              g_lru[layer], batch, seq)
        attn = _attention(qkv, band_bias, batch, seq)
        j = layer // 2
        dense = layer % 2 == 0
        x1, route, counts = _mix_out(
            attn, ylru, h, w_out[layer].astype(bf16), g_attn[layer], ln1_g[layer],
            ln1_b[layer], None if dense else router_w[j])
        if dense:
            tm = DENSE_FFN_ROWS
            h = _ffn(x1, jnp.zeros((n // tm,), jnp.int32), jnp.full((1,), n // tm, jnp.int32),
                     ffn_w_gate[j][None].astype(bf16), ffn_w_up[j][None].astype(bf16),
                     ffn_w_down[j][None].astype(bf16), (ln2_g[layer], ln2_b[layer]),
                     tm=tm, tf=ffn_w_gate.shape[2], name="ffn_dense")
        else:
            h = _moe(x1, route, counts, moe_w_gate[j].astype(bf16), moe_w_up[j].astype(bf16),
                     moe_w_down[j].astype(bf16), ln2_g[layer], ln2_b[layer],
                     tm=EXPERT_ROWS, tf=EXPERT_FF_CHUNK)
    return h.reshape(batch, seq, D_MODEL).astype(x.dtype)
```

```python
import functools

import numpy as np
import jax
import jax.numpy as jnp
from jax import lax
from jax.experimental import pallas as pl
from jax.experimental.pallas import tpu as pltpu

D_MODEL = 1024
N_HEADS = 8
HEAD_DIM = 64
D_ATTN = N_HEADS * HEAD_DIM
D_LRU = 512
N_LRU_BLOCKS = 8
LRU_BLOCK = D_LRU // N_LRU_BLOCKS
CONV_WIDTH = 4
LRU_C = 8.0
DILATED_PATTERNS = ((128, 1), (512, 4), (2048, 16))
ATTN_BLOCK = 128
STAGE_DILATION = 4
N_BUCKETS = 32
MAX_DISTANCE = 2048
D_IN = 3 * D_ATTN + 2 * D_LRU
N_EXPERTS = 8
TOP_K = 2
DEPTH = 2
DEEPNORM_ALPHA = (2.0 * DEPTH) ** 0.25
LN_EPS = 1e-5
RMS_EPS = 1e-6
NEG_INF = -1e30
LOG2_E = float(np.log2(np.e))

LANES = 128
SUBLANES = 8
VMEM_LIMIT_BYTES = 56 * 1024 * 1024
MXU_DEPTH = 256

LRU_ROWS = 1024
MIX_ROWS = 1024
ROUTER_ROWS = 512
DENSE_FFN_ROWS = 512
EXPERT_ROWS = 512
EXPERT_FF_CHUNK = 7 * MXU_DEPTH
DISPATCH_ROWS = 1024
COMBINE_ROWS = 512
DMA_UNROLL = 8
ATTN_UNROLL = 4

f32 = jnp.float32
bf16 = jnp.bfloat16


def _params(*semantics):
    return pltpu.CompilerParams(dimension_semantics=semantics,
                                vmem_limit_bytes=VMEM_LIMIT_BYTES)


def _t5_bucket(dist):
    max_exact = N_BUCKETS // 2
    d = np.maximum(dist, 1).astype(np.float32)
    large = max_exact + (np.log(d / max_exact) / np.log(MAX_DISTANCE / max_exact)
                         * (N_BUCKETS - max_exact)).astype(np.int32)
    large = np.minimum(large, N_BUCKETS - 1)
    return np.where(dist < max_exact, dist, large).astype(np.int32)


def _band_bias(rel_bias, dilation):
    nk = ATTN_BLOCK
    qi = np.arange(nk)[:, None]
    kj = np.arange(2 * nk)[None, :]
    delta = qi + nk - kj
    band = (delta >= 0) & (delta <= nk)
    bucket = _t5_bucket(np.clip(delta, 0, nk) * dilation)
    onehot = np.eye(N_BUCKETS, dtype=np.float32)[bucket.reshape(-1)]
    bias = jnp.dot(jnp.asarray(onehot), rel_bias.astype(f32), precision=lax.Precision.HIGHEST)
    bias = jnp.transpose(bias.reshape(nk, 2 * nk, N_HEADS), (2, 0, 1))
    valid = np.stack([band, band & (kj >= nk)])[:, None]
    bias = jnp.where(jnp.asarray(valid), bias[None], NEG_INF)
    return (bias * LOG2_E).reshape(2, N_HEADS // 2, 2 * nk, 2 * nk)


def _attn_kernel(q_ref, k_ref, v_ref, bias_ref, o_ref, q4, k4, v4, qs, ks, vs, s_buf, m_buf,
                 m_s, l_s, acc_s, *, seq):
    nk = ATTN_BLOCK
    n_blocks = seq // nk
    chunk = 2 * nk
    lane = lax.broadcasted_iota(jnp.int32, (1, LANES), 1)
    head0 = lane < HEAD_DIM

    ks[0:nk, :] = jnp.zeros((nk, LANES), bf16)
    vs[0:nk, 0:LANES] = jnp.zeros((nk, LANES), bf16)
    vs[:, LANES:2 * LANES] = jnp.ones((seq + nk, LANES), bf16)

    def stage(t, carry):
        chunks_per_residue = seq // STAGE_DILATION // chunk
        src = pl.ds(t // chunks_per_residue + STAGE_DILATION * chunk * (t % chunks_per_residue),
                    chunk, stride=STAGE_DILATION)
        dst = pl.ds(pl.multiple_of(t * chunk, chunk), chunk)
        q4[dst, :] = q_ref[src, :]
        k4[dst, :] = k_ref[src, :]
        v4[dst, :] = v_ref[src, :]
        return carry

    lax.fori_loop(0, seq // chunk, stage, 0)

    order = sorted(range(len(DILATED_PATTERNS)), key=lambda p: -DILATED_PATTERNS[p][1])
    assert DILATED_PATTERNS[order[-1]][1] == 1
    for p in order:
        d = DILATED_PATTERNS[p][1]
        is_first, is_last = p == order[0], p == order[-1]
        length = seq // d
        nb = length // nk
        chunks_per_residue = length // chunk

        def gather(t, carry):
            r = t // chunks_per_residue
            c = t % chunks_per_residue
            dst = pl.multiple_of(t * chunk, chunk)
            if d % STAGE_DILATION == 0:
                sub = d // STAGE_DILATION
                start = ((r % STAGE_DILATION) * (seq // STAGE_DILATION) + r // STAGE_DILATION
                         + sub * chunk * c)
                src = (pl.ds(start, chunk, stride=sub) if sub > 1
                       else pl.ds(pl.multiple_of(start, chunk), chunk))
                q, k, v = q4[src, :], k4[src, :], v4[src, :]
            else:
                src = pl.ds(r + d * chunk * c, chunk, stride=d) if d > 1 else pl.ds(dst, chunk)
                q, k, v = q_ref[src, :], k_ref[src, :], v_ref[src, :]
            q = q * (HEAD_DIM ** -0.5 * LOG2_E)
            q0 = jnp.where(head0, q, 0.0).astype(bf16)
            q1 = jnp.where(head0, 0.0, q).astype(bf16)
            for half in range(2):
                base = pl.multiple_of(2 * dst + half * chunk, chunk)
                qs[pl.ds(base, nk), :] = q0[half * nk:(half + 1) * nk]
                qs[pl.ds(base + nk, nk), :] = q1[half * nk:(half + 1) * nk]
            ks[pl.ds(nk + dst, chunk), :] = k.astype(bf16)
            vs[pl.ds(nk + dst, chunk), 0:LANES] = v.astype(bf16)
            return carry

        lax.fori_loop(0, seq // chunk, gather, 0)

        def scores(b, carry):
            first = jnp.asarray(b % nb == 0, jnp.int32)
            rows = pl.ds(pl.multiple_of(b * chunk, chunk), chunk)
            k = ks[pl.ds(pl.multiple_of(b * nk, nk), chunk), :]
            s_buf[rows, :] = lax.dot_general(qs[rows, :], k, (((1,), (1,)), ((), ())),
                                             preferred_element_type=f32) + bias_ref[p, first]
            return carry

        def rowmax(b, carry):
            rows = pl.ds(pl.multiple_of(b * chunk, chunk), chunk)
            m_buf[rows, :] = jnp.broadcast_to(jnp.max(s_buf[rows, :], axis=-1, keepdims=True),
                                              (chunk, LANES))
            return carry

        def block(b, carry):
            r = b // nb
            i = b % nb
            rows = pl.ds(pl.multiple_of(b * chunk, chunk), chunk)
            v = vs[pl.ds(pl.multiple_of(b * nk, nk), chunk), :]
            m = m_buf[rows, :]
            e = jnp.exp2(s_buf[rows, :] - jnp.concatenate([m, m], axis=1))
            pv = jnp.dot(e.astype(bf16), v, preferred_element_type=f32)
            m_blk = jnp.where(head0, m[0:nk], m[nk:chunk])
            l_blk = jnp.where(head0, pv[0:nk, LANES:], pv[nk:chunk, LANES:])
            pv_blk = jnp.where(head0, pv[0:nk, 0:LANES], pv[nk:chunk, 0:LANES])
            if d == 1:
                tok = pl.ds(pl.multiple_of(b * nk, nk), nk)
            else:
                tok = pl.ds(r + d * nk * i, nk, stride=d)
            if is_first:
                m_s[tok, :] = m_blk
                l_s[tok, :] = l_blk
                acc_s[tok, :] = pv_blk
                return carry
            m_old = m_s[tok, :]
            m_new = jnp.maximum(m_old, m_blk)
            w_old = jnp.exp2(m_old - m_new)
            w_blk = jnp.exp2(m_blk - m_new)
            l_new = w_old * l_s[tok, :] + w_blk * l_blk
            acc_new = w_old * acc_s[tok, :] + w_blk * pv_blk
            if is_last:
                o_ref[tok, :] = (acc_new / l_new).astype(o_ref.dtype)
            else:
                l_s[tok, :] = l_new
                acc_s[tok, :] = acc_new
                m_s[tok, :] = m_new
            return carry

        n_groups = n_blocks // ATTN_UNROLL
        assert n_groups >= 3

        def group(fn, g):
            for j in range(ATTN_UNROLL):
                fn(g * ATTN_UNROLL + j, 0)

        def pipelined(g, carry):
            group(block, g)
            group(rowmax, g + 1)
            group(scores, g + 2)
            return carry

        group(scores, 0)
        group(scores, 1)
        group(rowmax, 0)
        lax.fori_loop(0, n_groups - 2, pipelined, 0)
        group(block, n_groups - 2)
        group(rowmax, n_groups - 1)
        group(block, n_groups - 1)


def _attention(qkv, band_bias, batch, seq):
    view = qkv.reshape(batch, seq, 3 * D_ATTN)
    pairs = D_ATTN // LANES

    def spec(offset):
        return pl.BlockSpec((None, seq, LANES), lambda b, hp: (b, 0, offset * pairs + hp))

    n_pat = len(DILATED_PATTERNS)
    o = pl.pallas_call(
        functools.partial(_attn_kernel, seq=seq),
        grid=(batch, pairs),
        in_specs=[spec(0), spec(1), spec(2),
                  pl.BlockSpec((n_pat, 2, None, 2 * ATTN_BLOCK, 2 * ATTN_BLOCK),
                               lambda b, hp: (0, 0, hp, 0, 0))],
        out_specs=pl.BlockSpec((None, seq, LANES), lambda b, hp: (b, 0, hp)),
        out_shape=jax.ShapeDtypeStruct((batch, seq, D_ATTN), bf16),
        scratch_shapes=[pltpu.VMEM((seq, LANES), f32), pltpu.VMEM((seq, LANES), f32),
                        pltpu.VMEM((seq, LANES), f32),
                        pltpu.VMEM((2 * seq, LANES), bf16),
                        pltpu.VMEM((seq + ATTN_BLOCK, LANES), bf16),
                        pltpu.VMEM((seq + ATTN_BLOCK, 2 * LANES), bf16),
                        pltpu.VMEM((2 * seq, 2 * ATTN_BLOCK), f32),
                        pltpu.VMEM((2 * seq, LANES), f32),
                        pltpu.VMEM((seq, LANES), f32), pltpu.VMEM((seq, LANES), f32),
                        pltpu.VMEM((seq, LANES), f32)],
        compiler_params=_params("parallel", "parallel"),
        name="attention",
    )(view, view, view, band_bias)
    return o.reshape(batch * seq, D_ATTN)


def _gelu_tanh(x):
    return 0.5 * x * (1.0 + jnp.tanh(np.sqrt(2.0 / np.pi) * (x + 0.044715 * x * x * x)))


def _proj_lru_kernel(x_ref, w_ref, convw_ref, convb_ref, wg_ref, ba_ref, bx_ref, lam_ref, g_ref,
                     qkv_ref, y_ref, ubuf, hcarry, *, ts):
    pad = SUBLANES
    t = pl.program_id(1)
    n_qkv = qkv_ref.shape[1]
    xb = x_ref[...].astype(bf16)

    @pl.when(t == 0)
    def _():
        ubuf[0:pad, :] = jnp.zeros((pad, D_LRU), f32)
        hcarry[...] = jnp.zeros_like(hcarry)

    @pl.when(t > 0)
    def _():
        ubuf[0:pad, :] = ubuf[ts:ts + pad, :]

    ubuf[pad:pad + ts, :] = jnp.dot(xb, w_ref[:, n_qkv:n_qkv + D_LRU],
                                    preferred_element_type=f32)
    gate = jnp.dot(xb, w_ref[:, n_qkv + D_LRU:], preferred_element_type=f32)

    qkv_step = n_qkv // 3
    assert qkv_step == D_LRU

    def qkv_chunk(c):
        cols = slice(c * qkv_step, (c + 1) * qkv_step)
        p = jnp.dot(xb, w_ref[:, cols], preferred_element_type=f32)
        qkv_ref[:, cols] = p
        bits = lax.bitcast_convert_type(p, jnp.uint32).reshape(ts // SUBLANES, SUBLANES, qkv_step)
        folded = functools.reduce(jnp.bitwise_or, [bits[g] for g in range(ts // SUBLANES)])
        return ((folded >> 16) >> 16).astype(f32)

    u = convb_ref[...] + convw_ref[CONV_WIDTH - 1:CONV_WIDTH, :] * ubuf[pad:pad + ts, :]
    for w in range(CONV_WIDTH - 1):
        back = CONV_WIDTH - 1 - w
        u = u + convw_ref[w:w + 1, :] * ubuf[pad - back:pad - back + ts, :]

    gates = jnp.dot(u.astype(bf16), wg_ref[...], preferred_element_type=f32)
    r = jax.nn.sigmoid(gates[:, 0:D_LRU] + ba_ref[...])
    i = jax.nn.sigmoid(gates[:, D_LRU:2 * D_LRU] + bx_ref[...])
    neg_lam = -lam_ref[...]
    softplus = jnp.maximum(neg_lam, 0.0) + jnp.log1p(jnp.exp(-jnp.abs(neg_lam)))
    log_a = (-LRU_C) * r * softplus
    a = jnp.exp(log_a)
    b = jnp.sqrt(-jnp.tanh(log_a) * (1.0 + a * a)) * (i * u)
    zero0 = qkv_chunk(0)

    groups = ts // SUBLANES
    a = a.reshape(groups, SUBLANES, D_LRU)
    b = b.reshape(groups, SUBLANES, D_LRU) + zero0[None]
    sub = lax.broadcasted_iota(jnp.int32, (1, SUBLANES, 1), 1)
    shift = 1
    while shift < SUBLANES:
        live = sub >= shift
        a_prev = jnp.where(live, pltpu.roll(a, shift, 1), 1.0)
        b_prev = jnp.where(live, pltpu.roll(b, shift, 1), 0.0)
        b = a * b_prev + b
        a = a * a_prev
        shift *= 2
    h_prev = hcarry[...] + qkv_chunk(1)[0:1, :]
    h_groups = []
    for g in range(groups):
        h_g = a[g] * h_prev + b[g]
        h_prev = h_g[SUBLANES - 1:SUBLANES, :]
        h_groups.append(h_g)
    hcarry[...] = h_prev
    h = jnp.concatenate(h_groups, axis=0)
    zero2 = qkv_chunk(2)[0:1, 0:1]

    y = _gelu_tanh(gate) * h
    y = y * lax.rsqrt(jnp.mean(y * y, axis=-1, keepdims=True) + (RMS_EPS + zero2)) * g_ref[...]
    y_ref[...] = y.astype(y_ref.dtype)


def _block_diag(w):
    g, i, j = w.shape
    eye = jnp.eye(g, dtype=w.dtype)
    return jnp.einsum('gij,gh->gihj', w, eye).reshape(g * i, g * j)


def _proj_lru(x2d, w_in_bf16, conv_w, conv_b, w_a, b_a, w_x, b_x, lam, g_lru, batch, seq,
              ts=LRU_ROWS):
    wg = jnp.concatenate([_block_diag(w_a), _block_diag(w_x)], axis=1).astype(bf16)
    row = lambda v: v.reshape(1, D_LRU).astype(f32)
    const = lambda shape: pl.BlockSpec(shape, lambda b, t: (0, 0))
    tile = lambda width: pl.BlockSpec((None, ts, width), lambda b, t: (b, t, 0))
    qkv, y = pl.pallas_call(
        functools.partial(_proj_lru_kernel, ts=ts),
        grid=(batch, seq // ts),
        in_specs=[tile(D_MODEL), const((D_MODEL, D_IN)),
                  const((CONV_WIDTH, D_LRU)), const((1, D_LRU)),
                  const((D_LRU, 2 * D_LRU)), const((1, D_LRU)), const((1, D_LRU)),
                  const((1, D_LRU)), const((1, D_LRU))],
        out_specs=[tile(3 * D_ATTN), tile(D_LRU)],
        out_shape=[jax.ShapeDtypeStruct((batch, seq, 3 * D_ATTN), f32),
                   jax.ShapeDtypeStruct((batch, seq, D_LRU), bf16)],
        scratch_shapes=[pltpu.VMEM((ts + 2 * SUBLANES, D_LRU), f32), pltpu.VMEM((1, D_LRU), f32)],
        compiler_params=_params("parallel", "arbitrary"),
        name="proj_rglru",
    )(x2d.reshape(batch, seq, D_MODEL), w_in_bf16,
      conv_w.reshape(CONV_WIDTH, D_LRU).astype(f32), row(conv_b), wg, row(b_a), row(b_x),
      row(lam), row(g_lru))
    return qkv.reshape(batch * seq, 3 * D_ATTN), y.reshape(batch * seq, D_LRU)


ROW_TILE = D_MODEL // LANES


def _store_row_tiles(ref, rows):
    t = rows.shape[0]
    for s in range(ROW_TILE):
        ref[pl.ds(s, t, stride=ROW_TILE), :] = rows[:, s * LANES:(s + 1) * LANES]


def _load_row_tiles(ref, t):
    return jnp.concatenate([ref[pl.ds(s, t, stride=ROW_TILE), :] for s in range(ROW_TILE)],
                           axis=-1)


def _layer_norm(z, g, b):
    mu = jnp.mean(z, axis=-1, keepdims=True)
    zc = z - mu
    var = jnp.mean(zc * zc, axis=-1, keepdims=True)
    return zc * lax.rsqrt(var + LN_EPS) * g + b


def _mix_out_kernel(attn_ref, ylru, x_ref, w_ref, gattn, lng, lnb, *rest, with_router):
    if with_router:
        rw_ref, tri_ref, x1_ref, route_ref, route_t_ref, count_ref = rest
    else:
        (x1_ref,) = rest
    attn = attn_ref[...].astype(f32)
    attn = attn * lax.rsqrt(jnp.mean(attn * attn, axis=-1, keepdims=True) + RMS_EPS) * gattn[...]
    y = jnp.dot(attn.astype(bf16), w_ref[0:D_ATTN, :], preferred_element_type=f32)
    y = y + jnp.dot(ylru[...], w_ref[D_ATTN:, :], preferred_element_type=f32)
    x1 = _layer_norm(DEEPNORM_ALPHA * x_ref[...] + y, lng[...], lnb[...])
    x1_ref[...] = x1
    if with_router:
        lane = lax.broadcasted_iota(jnp.int32, (1, LANES), 1).astype(f32)
        x_hi = x1.astype(bf16)
        x_lo = (x1 - x_hi.astype(f32)).astype(bf16)
        logits = (jnp.dot(x_hi, rw_ref[0], preferred_element_type=f32)
                  + jnp.dot(x_lo, rw_ref[0], preferred_element_type=f32)
                  + jnp.dot(x_hi, rw_ref[1], preferred_element_type=f32))
        logits = jnp.where(lane < N_EXPERTS, logits, -jnp.inf)
        v1 = jnp.max(logits, axis=-1, keepdims=True)
        i1 = jnp.min(jnp.where(logits == v1, lane, float(LANES)), axis=-1, keepdims=True)
        rest_logits = jnp.where(lane == i1, -jnp.inf, logits)
        v2 = jnp.max(rest_logits, axis=-1, keepdims=True)
        i2 = jnp.min(jnp.where(rest_logits == v2, lane, float(LANES)), axis=-1, keepdims=True)
        e2 = jnp.exp(v2 - v1)
        p1 = 1.0 / (1.0 + e2)
        p2 = e2 / (1.0 + e2)

        @pl.when(pl.program_id(0) == 0)
        def _():
            count_ref[...] = jnp.zeros_like(count_ref)

        tm = x1.shape[0]
        chosen = jnp.logical_or(lane == i1, lane == i2)
        rank = count_ref[...] + jnp.dot(tri_ref[...], chosen.astype(bf16),
                                        preferred_element_type=f32)
        count_ref[...] += jnp.sum(chosen.astype(f32), axis=0, keepdims=True)
        r1 = jnp.sum(jnp.where(lane == i1, rank, 0.0), axis=-1, keepdims=True)
        r2 = jnp.sum(jnp.where(lane == i2, rank, 0.0), axis=-1, keepdims=True)
        fields = (i1, i2, r1, r2, p1, p2)
        route = jnp.zeros((tm, LANES), f32)
        for k, val in enumerate(fields):
            route = jnp.where(lane == k, val, route)
        route_ref[...] = route
        route_t_ref[...] = route.T[0:SUBLANES, :]


def _mix_out(attn, ylru, x2d, w_out_bf16, g_attn, ln_g, ln_b, router_w=None):
    n = x2d.shape[0]
    with_router = router_w is not None
    tm = ROUTER_ROWS if with_router else MIX_ROWS
    tile = lambda width: pl.BlockSpec((tm, width), lambda i: (i, 0))
    const = lambda shape: pl.BlockSpec(shape, lambda i: (0, 0))
    in_specs = [tile(D_ATTN), tile(D_LRU), tile(D_MODEL), const((D_MODEL, D_MODEL)),
                const((1, D_ATTN)), const((1, D_MODEL)), const((1, D_MODEL))]
    args = [attn, ylru, x2d, w_out_bf16, g_attn.reshape(1, D_ATTN).astype(f32),
            ln_g.reshape(1, D_MODEL).astype(f32), ln_b.reshape(1, D_MODEL).astype(f32)]
    out_specs = [tile(D_MODEL)]
    out_shape = [jax.ShapeDtypeStruct((n, D_MODEL), f32)]
    if with_router:
        rw = jnp.zeros((D_MODEL, LANES), f32).at[:, :N_EXPERTS].set(router_w.astype(f32))
        rw_hi = rw.astype(bf16)
        rw_lo = (rw - rw_hi.astype(f32)).astype(bf16)
        strictly_lower = jnp.asarray(np.tri(tm, k=-1), bf16)
        in_specs += [pl.BlockSpec((2, D_MODEL, LANES), lambda i: (0, 0, 0)), const((tm, tm))]
        args += [jnp.stack([rw_hi, rw_lo]), strictly_lower]
        out_specs += [tile(LANES), pl.BlockSpec((SUBLANES, tm), lambda i: (0, i)), const((1, LANES))]
        out_shape += [jax.ShapeDtypeStruct((n, LANES), f32), jax.ShapeDtypeStruct((SUBLANES, n), f32),
                      jax.ShapeDtypeStruct((1, LANES), f32)]
    res = pl.pallas_call(
        functools.partial(_mix_out_kernel, with_router=with_router),
        grid=(n // tm,),
        in_specs=in_specs,
        out_specs=out_specs,
        out_shape=out_shape,
        compiler_params=_params("arbitrary" if with_router else "parallel"),
        name="mix_out_router" if with_router else "mix_out",
    )(*args)
    return res if with_router else (res[0], None, None, None)


def _ffn_kernel(tile_expert_ref, n_used_ref, x_ref, wg_ref, wu_ref, wd_ref, *rest,
                n_chunks, fuse_ln):
    if fuse_ln:
        lng, lnb, out_ref, acc_ref, xb_ref = rest
    else:
        out_ref, acc_ref, xb_ref = rest
    del tile_expert_ref
    j = pl.program_id(1)
    used = pl.program_id(0) < n_used_ref[0]
    last = n_chunks - 1

    def load_x():
        if fuse_ln:
            return x_ref[...].astype(bf16)
        return _load_row_tiles(x_ref, xb_ref.shape[0]).astype(bf16)

    def chunk_out(xb):
        g = jnp.dot(xb, wg_ref[...], preferred_element_type=f32)
        u = jnp.dot(xb, wu_ref[...], preferred_element_type=f32)
        h = (g * jax.nn.sigmoid(g)) * u
        return jnp.dot(h.astype(bf16), wd_ref[...], preferred_element_type=f32)

    def finish(y):
        if fuse_ln:
            out_ref[...] = _layer_norm(DEEPNORM_ALPHA * x_ref[...] + y, lng[...], lnb[...])
        else:
            _store_row_tiles(out_ref, y)

    if n_chunks == 1:
        @pl.when(used)
        def _():
            finish(chunk_out(load_x()))
    else:
        @pl.when(used & (j == 0))
        def _():
            xb = load_x()
            xb_ref[...] = xb
            acc_ref[...] = chunk_out(xb)

        if n_chunks > 2:
            @pl.when(used & (j > 0) & (j < last))
            def _():
                acc_ref[...] += chunk_out(xb_ref[...])

        @pl.when(used & (j == last))
        def _():
            finish(acc_ref[...] + chunk_out(xb_ref[...]))

    @pl.when(jnp.logical_not(used) & (j == last))
    def _():
        finish(jnp.zeros(acc_ref.shape, f32))


def _ffn(x2d, tile_expert, n_used, w_gate, w_up, w_down, ln=None, *, tm, tf, name):
    fuse_ln = ln is not None
    rows = x2d.shape[0] if fuse_ln else x2d.shape[0] // ROW_TILE
    d_ff = w_gate.shape[2]
    n_chunks = d_ff // tf
    io_block = (tm, D_MODEL) if fuse_ln else (tm * ROW_TILE, LANES)

    def chunk(i, j, nu):
        return jnp.where(i < nu[0], j, n_chunks - 1)

    resident = w_gate.shape[0] == 1 and n_chunks == 1
    mode = dict(pipeline_mode=pl.Buffered(1)) if resident else {}
    in_specs = [pl.BlockSpec(io_block, lambda i, j, te, nu: (i, 0)),
                pl.BlockSpec((None, D_MODEL, tf),
                             lambda i, j, te, nu: (te[i], 0, chunk(i, j, nu)), **mode),
                pl.BlockSpec((None, D_MODEL, tf),
                             lambda i, j, te, nu: (te[i], 0, chunk(i, j, nu)), **mode),
                pl.BlockSpec((None, tf, D_MODEL),
                             lambda i, j, te, nu: (te[i], chunk(i, j, nu), 0), **mode)]
    args = [x2d, w_gate, w_up, w_down]
    if fuse_ln:
        in_specs += [pl.BlockSpec((1, D_MODEL), lambda i, j, te, nu: (0, 0))] * 2
        args += [v.reshape(1, D_MODEL).astype(f32) for v in ln]
    return pl.pallas_call(
        functools.partial(_ffn_kernel, n_chunks=n_chunks, fuse_ln=fuse_ln),
        grid_spec=pltpu.PrefetchScalarGridSpec(
            num_scalar_prefetch=2,
            grid=(rows // tm, n_chunks),
            in_specs=in_specs,
            out_specs=pl.BlockSpec(io_block, lambda i, j, te, nu: (i, 0)),
            scratch_shapes=[pltpu.VMEM((tm, D_MODEL), f32), pltpu.VMEM((tm, D_MODEL), bf16)]),
        out_shape=jax.ShapeDtypeStruct(x2d.shape, f32),
        compiler_params=_params("parallel", "arbitrary"),
        name=name,
    )(tile_expert, n_used, *args)


def _dispatch_kernel(fill_ref, pos_ref, x_ref, xs_hbm, zero_buf, rows_buf, sem, *, tt, tm):
    step = pl.program_id(0)

    n_fill = fill_ref.shape[0] // 2

    @pl.when(step == 0)
    def _():
        zero_buf[...] = jnp.zeros_like(zero_buf)

        def fill_copy(f):
            start = pl.multiple_of(fill_ref[f] * ROW_TILE, ROW_TILE)
            return pltpu.make_async_copy(zero_buf, xs_hbm.at[pl.ds(start, tm * ROW_TILE)], sem)

        for f in range(n_fill):
            @pl.when(fill_ref[n_fill + f] > 0)
            def _():
                fill_copy(f).start()
        for f in range(n_fill):
            @pl.when(fill_ref[n_fill + f] > 0)
            def _():
                fill_copy(f).wait()

    _store_row_tiles(rows_buf, x_ref[...])

    def issue(t, carry):
        src = pl.multiple_of(t * ROW_TILE, ROW_TILE)
        for k in range(TOP_K):
            dst = pl.multiple_of(pos_ref[k, t] * ROW_TILE, ROW_TILE)
            pltpu.make_async_copy(rows_buf.at[pl.ds(src, ROW_TILE)],
                                  xs_hbm.at[pl.ds(dst, ROW_TILE)], sem).start(priority=k)
        return carry

    lax.fori_loop(0, tt, issue, 0, unroll=DMA_UNROLL)
    for _ in range(TOP_K):
        pltpu.make_async_copy(rows_buf, xs_hbm.at[pl.ds(0, tt * ROW_TILE)], sem).wait()


def _dispatch(x1, pos_blocks, fill, rows_sorted, *, tt, tm):
    n = x1.shape[0]
    return pl.pallas_call(
        functools.partial(_dispatch_kernel, tt=tt, tm=tm),
        grid_spec=pltpu.PrefetchScalarGridSpec(
            num_scalar_prefetch=1,
            grid=(n // tt,),
            in_specs=[pl.BlockSpec((TOP_K, tt), lambda i, fill: (0, i), memory_space=pltpu.SMEM),
                      pl.BlockSpec((tt, D_MODEL), lambda i, fill: (i, 0))],
            out_specs=pl.BlockSpec(memory_space=pl.ANY),
            scratch_shapes=[pltpu.VMEM((tm * ROW_TILE, LANES), f32),
                            pltpu.VMEM((tt * ROW_TILE, LANES), f32),
                            pltpu.SemaphoreType.DMA(())]),
        out_shape=jax.ShapeDtypeStruct((rows_sorted * ROW_TILE, LANES), f32),
        compiler_params=_params("arbitrary"),
        name="moe_dispatch",
    )(fill, pos_blocks, x1)


def _combine_kernel(pos_ref, pos_next_ref, route_ref, x1_ref, ys_hbm, lng, lnb, out_ref, buf, sems,
                    *, tt, n_steps):
    step = pl.program_id(0)
    slot = step % 2

    def issue_tile(tile_pos_ref, into):
        def issue(t, carry):
            dst = pl.multiple_of(t * ROW_TILE, ROW_TILE)
            for k in range(TOP_K):
                src = pl.multiple_of(tile_pos_ref[k, t] * ROW_TILE, ROW_TILE)
                pltpu.make_async_copy(ys_hbm.at[pl.ds(src, ROW_TILE)],
                                      buf.at[into, k, pl.ds(dst, ROW_TILE)],
                                      sems.at[into]).start(priority=k)
            return carry

        lax.fori_loop(0, tt, issue, 0, unroll=DMA_UNROLL)

    @pl.when(step == 0)
    def _():
        issue_tile(pos_ref, 0)

    @pl.when(step + 1 < n_steps)
    def _():
        issue_tile(pos_next_ref, 1 - slot)

    for k in range(TOP_K):
        pltpu.make_async_copy(ys_hbm.at[pl.ds(0, tt * ROW_TILE)], buf.at[slot, k],
                              sems.at[slot]).wait()

    lane = lax.broadcasted_iota(jnp.int32, (1, LANES), 1)
    route = route_ref[...]
    p1 = jnp.sum(jnp.where(lane == 4, route, 0.0), axis=-1, keepdims=True)
    p2 = jnp.sum(jnp.where(lane == 5, route, 0.0), axis=-1, keepdims=True)
    y = p1 * _load_row_tiles(buf.at[slot, 0], tt) + p2 * _load_row_tiles(buf.at[slot, 1], tt)
    out_ref[...] = _layer_norm(DEEPNORM_ALPHA * x1_ref[...] + y, lng[...], lnb[...])


def _combine(pos_blocks, route, x1, y_sorted, ln_g, ln_b, *, tt):
    n = x1.shape[0]
    n_steps = n // tt
    pos_spec = lambda index: pl.BlockSpec((TOP_K, tt), index, memory_space=pltpu.SMEM)
    return pl.pallas_call(
        functools.partial(_combine_kernel, tt=tt, n_steps=n_steps),
        grid=(n_steps,),
        in_specs=[pos_spec(lambda i: (0, i)),
                  pos_spec(lambda i: (0, jnp.minimum(i + 1, n_steps - 1))),
                  pl.BlockSpec((tt, LANES), lambda i: (i, 0)),
                  pl.BlockSpec((tt, D_MODEL), lambda i: (i, 0)),
                  pl.BlockSpec(memory_space=pl.ANY),
                  pl.BlockSpec((1, D_MODEL), lambda i: (0, 0)),
                  pl.BlockSpec((1, D_MODEL), lambda i: (0, 0))],
        out_specs=pl.BlockSpec((tt, D_MODEL), lambda i: (i, 0)),
        out_shape=jax.ShapeDtypeStruct((n, D_MODEL), f32),
        scratch_shapes=[pltpu.VMEM((2, TOP_K, tt * ROW_TILE, LANES), f32),
                        pltpu.SemaphoreType.DMA((2,))],
        compiler_params=_params("arbitrary"),
        name="moe_combine",
    )(pos_blocks, pos_blocks, route, x1, y_sorted, ln_g.reshape(1, D_MODEL).astype(f32),
      ln_b.reshape(1, D_MODEL).astype(f32))


def _moe(x1, route, route_t, counts, w_gate, w_up, w_down, ln_g, ln_b, *, tm, tf):
    n = x1.shape[0]
    i32 = jnp.int32
    counts = counts[0, :N_EXPERTS].astype(i32)
    padded = (counts + tm - 1) // tm * tm
    ends = jnp.cumsum(padded)
    offsets = ends - padded
    experts = route_t[0:TOP_K].astype(i32)
    ranks = route_t[TOP_K:2 * TOP_K].astype(i32)
    pos = offsets[experts] + ranks
    n_tiles = TOP_K * n // tm + N_EXPERTS
    n_used = (ends[-1] // tm).astype(i32).reshape(1)
    tile_ids = jnp.arange(n_tiles, dtype=i32)
    tile_expert = jnp.sum((tile_ids[:, None] >= (ends // tm)[None, :]).astype(i32), axis=1)
    last_expert = jnp.max(jnp.where(counts > 0, jnp.arange(N_EXPERTS, dtype=i32), 0))
    tile_expert = jnp.minimum(tile_expert, last_expert).astype(i32)
    tail_tiles = n_used[0] + jnp.arange(N_EXPERTS, dtype=i32)
    fill = jnp.concatenate([ends - tm, tail_tiles * tm,
                            (counts > 0).astype(i32), (tail_tiles < n_tiles).astype(i32)]).astype(i32)

    x_sorted = _dispatch(x1, pos, fill, n_tiles * tm, tt=DISPATCH_ROWS, tm=tm)
    y_sorted = _ffn(x_sorted, tile_expert, n_used, w_gate, w_up, w_down, tm=tm, tf=tf,
                    name="ffn_experts")
    return _combine(pos, route, x1, y_sorted, ln_g, ln_b, tt=COMBINE_ROWS)


def kernel(x, w_in, conv_w, conv_b, w_a, b_a, w_x, b_x, lru_lambda, rel_bias, g_attn, g_lru, w_out, ln1_g, ln1_b, ln2_g, ln2_b, ffn_w_gate, ffn_w_up, ffn_w_down, router_w, moe_w_gate, moe_w_up, moe_w_down):
    batch, seq, _ = x.shape
    n = batch * seq
    h = x.reshape(n, D_MODEL).astype(f32)
    band_bias = jnp.stack([_band_bias(rel_bias, d) for _, d in DILATED_PATTERNS])
    for layer in range(DEPTH):
        qkv, ylru = _proj_lru(h, w_in[layer].astype(bf16), conv_w[layer], conv_b[layer],
                              w_a[layer], b_a[layer], w_x[layer], b_x[layer], lru_lambda[layer],
                              g_lru[layer], batch, seq)
        attn = _attention(qkv, band_bias, batch, seq)
        j = layer // 2
        dense = layer % 2 == 0
        x1, route, route_t, counts = _mix_out(
            attn, ylru, h, w_out[layer].astype(bf16), g_attn[layer], ln1_g[layer],
            ln1_b[layer], None if dense else router_w[j])
        if dense:
            tm = DENSE_FFN_ROWS
            h = _ffn(x1, jnp.zeros((n // tm,), jnp.int32), jnp.full((1,), n // tm, jnp.int32),
                     ffn_w_gate[j][None].astype(bf16), ffn_w_up[j][None].astype(bf16),
                     ffn_w_down[j][None].astype(bf16), (ln2_g[layer], ln2_b[layer]),
                     tm=tm, tf=ffn_w_gate.shape[2], name="ffn_dense")
        else:
            h = _moe(x1, route, route_t, counts, moe_w_gate[j].astype(bf16),
                     moe_w_up[j].astype(bf16), moe_w_down[j].astype(bf16), ln2_g[layer],
                     ln2_b[layer], tm=EXPERT_ROWS, tf=EXPERT_FF_CHUNK)
    return h.reshape(batch, seq, D_MODEL).astype(x.dtype)
```

```python
import functools

import numpy as np
import jax
import jax.numpy as jnp
from jax import lax
from jax.experimental import pallas as pl
from jax.experimental.pallas import tpu as pltpu

D_MODEL = 1024
N_HEADS = 8
HEAD_DIM = 64
D_ATTN = N_HEADS * HEAD_DIM
D_LRU = 512
N_LRU_BLOCKS = 8
LRU_BLOCK = D_LRU // N_LRU_BLOCKS
CONV_WIDTH = 4
LRU_C = 8.0
DILATED_PATTERNS = ((128, 1), (512, 4), (2048, 16))
ATTN_BLOCK = 128
STAGE_DILATION = 4
N_BUCKETS = 32
MAX_DISTANCE = 2048
D_IN = 3 * D_ATTN + 2 * D_LRU
N_EXPERTS = 8
TOP_K = 2
DEPTH = 2
DEEPNORM_ALPHA = (2.0 * DEPTH) ** 0.25
LN_EPS = 1e-5
RMS_EPS = 1e-6
NEG_INF = -1e30
LOG2_E = float(np.log2(np.e))

LANES = 128
SUBLANES = 8
VMEM_LIMIT_BYTES = 56 * 1024 * 1024
MXU_DEPTH = 256

LRU_ROWS = 1024
ROUTER_ROWS = 512
DENSE_FFN_ROWS = 512
EXPERT_ROWS = 512
EXPERT_FF_CHUNK = 7 * MXU_DEPTH
DISPATCH_ROWS = 1024
COMBINE_ROWS = 512
DMA_UNROLL = 8
ATTN_UNROLL = 4

f32 = jnp.float32
bf16 = jnp.bfloat16


def _params(*semantics):
    return pltpu.CompilerParams(dimension_semantics=semantics,
                                vmem_limit_bytes=VMEM_LIMIT_BYTES)


def _t5_bucket(dist):
    max_exact = N_BUCKETS // 2
    d = np.maximum(dist, 1).astype(np.float32)
    large = max_exact + (np.log(d / max_exact) / np.log(MAX_DISTANCE / max_exact)
                         * (N_BUCKETS - max_exact)).astype(np.int32)
    large = np.minimum(large, N_BUCKETS - 1)
    return np.where(dist < max_exact, dist, large).astype(np.int32)


def _band_bias(rel_bias, dilation):
    nk = ATTN_BLOCK
    qi = np.arange(nk)[:, None]
    kj = np.arange(2 * nk)[None, :]
    delta = qi + nk - kj
    band = (delta >= 0) & (delta <= nk)
    bucket = _t5_bucket(np.clip(delta, 0, nk) * dilation)
    onehot = np.eye(N_BUCKETS, dtype=np.float32)[bucket.reshape(-1)]
    bias = jnp.dot(jnp.asarray(onehot), rel_bias.astype(f32), precision=lax.Precision.HIGHEST)
    bias = jnp.transpose(bias.reshape(nk, 2 * nk, N_HEADS), (2, 0, 1))
    valid = np.stack([band, band & (kj >= nk)])[:, None]
    bias = jnp.where(jnp.asarray(valid), bias[None], NEG_INF)
    return (bias * LOG2_E).reshape(2, N_HEADS // 2, 2 * nk, 2 * nk)


def _attn_kernel(q_ref, k_ref, v_ref, bias_ref, o_ref, q4, k4, v4, qs, ks, vs, s_buf, m_buf,
                 m_s, l_s, acc_s, *, seq):
    nk = ATTN_BLOCK
    n_blocks = seq // nk
    chunk = 2 * nk
    lane = lax.broadcasted_iota(jnp.int32, (1, LANES), 1)
    head0 = lane < HEAD_DIM

    ks[0:nk, :] = jnp.zeros((nk, LANES), bf16)
    vs[0:nk, 0:LANES] = jnp.zeros((nk, LANES), bf16)
    vs[:, LANES:2 * LANES] = jnp.ones((seq + nk, LANES), bf16)

    def stage(t, carry):
        chunks_per_residue = seq // STAGE_DILATION // chunk
        src = pl.ds(t // chunks_per_residue + STAGE_DILATION * chunk * (t % chunks_per_residue),
                    chunk, stride=STAGE_DILATION)
        dst = pl.ds(pl.multiple_of(t * chunk, chunk), chunk)
        q4[dst, :] = q_ref[src, :]
        k4[dst, :] = k_ref[src, :]
        v4[dst, :] = v_ref[src, :]
        return carry

    lax.fori_loop(0, seq // chunk, stage, 0)

    order = sorted(range(len(DILATED_PATTERNS)), key=lambda p: -DILATED_PATTERNS[p][1])
    assert DILATED_PATTERNS[order[-1]][1] == 1
    for p in order:
        d = DILATED_PATTERNS[p][1]
        is_first, is_last = p == order[0], p == order[-1]
        length = seq // d
        nb = length // nk
        chunks_per_residue = length // chunk

        def gather(t, carry):
            r = t // chunks_per_residue
            c = t % chunks_per_residue
            dst = pl.multiple_of(t * chunk, chunk)
            if d % STAGE_DILATION == 0:
                sub = d // STAGE_DILATION
                start = ((r % STAGE_DILATION) * (seq // STAGE_DILATION) + r // STAGE_DILATION
                         + sub * chunk * c)
                src = (pl.ds(start, chunk, stride=sub) if sub > 1
                       else pl.ds(pl.multiple_of(start, chunk), chunk))
                q, k, v = q4[src, :], k4[src, :], v4[src, :]
            else:
                src = pl.ds(r + d * chunk * c, chunk, stride=d) if d > 1 else pl.ds(dst, chunk)
                q, k, v = q_ref[src, :], k_ref[src, :], v_ref[src, :]
            q = q * (HEAD_DIM ** -0.5 * LOG2_E)
            q0 = jnp.where(head0, q, 0.0).astype(bf16)
            q1 = jnp.where(head0, 0.0, q).astype(bf16)
            for half in range(2):
                base = pl.multiple_of(2 * dst + half * chunk, chunk)
                qs[pl.ds(base, nk), :] = q0[half * nk:(half + 1) * nk]
                qs[pl.ds(base + nk, nk), :] = q1[half * nk:(half + 1) * nk]
            ks[pl.ds(nk + dst, chunk), :] = k.astype(bf16)
            vs[pl.ds(nk + dst, chunk), 0:LANES] = v.astype(bf16)
            return carry

        lax.fori_loop(0, seq // chunk, gather, 0)

        def scores(b, carry):
            first = jnp.asarray(b % nb == 0, jnp.int32)
            rows = pl.ds(pl.multiple_of(b * chunk, chunk), chunk)
            k = ks[pl.ds(pl.multiple_of(b * nk, nk), chunk), :]
            s_buf[rows, :] = lax.dot_general(qs[rows, :], k, (((1,), (1,)), ((), ())),
                                             preferred_element_type=f32) + bias_ref[p, first]
            return carry

        def rowmax(b, carry):
            rows = pl.ds(pl.multiple_of(b * chunk, chunk), chunk)
            m_buf[rows, :] = jnp.broadcast_to(jnp.max(s_buf[rows, :], axis=-1, keepdims=True),
                                              (chunk, LANES))
            return carry

        def block(b, carry):
            r = b // nb
            i = b % nb
            rows = pl.ds(pl.multiple_of(b * chunk, chunk), chunk)
            v = vs[pl.ds(pl.multiple_of(b * nk, nk), chunk), :]
            m = m_buf[rows, :]
            e = jnp.exp2(s_buf[rows, :] - jnp.concatenate([m, m], axis=1))
            pv = jnp.dot(e.astype(bf16), v, preferred_element_type=f32)
            m_blk = jnp.where(head0, m[0:nk], m[nk:chunk])
            l_blk = jnp.where(head0, pv[0:nk, LANES:], pv[nk:chunk, LANES:])
            pv_blk = jnp.where(head0, pv[0:nk, 0:LANES], pv[nk:chunk, 0:LANES])
            if d == 1:
                tok = pl.ds(pl.multiple_of(b * nk, nk), nk)
            else:
                tok = pl.ds(r + d * nk * i, nk, stride=d)
            if is_first:
                m_s[tok, :] = m_blk
                l_s[tok, :] = l_blk
                acc_s[tok, :] = pv_blk
                return carry
            m_old = m_s[tok, :]
            m_new = jnp.maximum(m_old, m_blk)
            w_old = jnp.exp2(m_old - m_new)
            w_blk = jnp.exp2(m_blk - m_new)
            l_new = w_old * l_s[tok, :] + w_blk * l_blk
            acc_new = w_old * acc_s[tok, :] + w_blk * pv_blk
            if is_last:
                o_ref[tok, :] = (acc_new / l_new).astype(o_ref.dtype)
            else:
                l_s[tok, :] = l_new
                acc_s[tok, :] = acc_new
                m_s[tok, :] = m_new
            return carry

        n_groups = n_blocks // ATTN_UNROLL
        assert n_groups >= 3

        def group(fn, g):
            for j in range(ATTN_UNROLL):
                fn(g * ATTN_UNROLL + j, 0)

        def pipelined(g, carry):
            group(block, g)
            group(rowmax, g + 1)
            group(scores, g + 2)
            return carry

        group(scores, 0)
        group(scores, 1)
        group(rowmax, 0)
        lax.fori_loop(0, n_groups - 2, pipelined, 0)
        group(block, n_groups - 2)
        group(rowmax, n_groups - 1)
        group(block, n_groups - 1)


def _attention(qkv, band_bias, batch, seq):
    view = qkv.reshape(batch, seq, 3 * D_ATTN)
    pairs = D_ATTN // LANES

    def spec(offset):
        return pl.BlockSpec((None, seq, LANES), lambda b, hp: (b, 0, offset * pairs + hp))

    n_pat = len(DILATED_PATTERNS)
    o = pl.pallas_call(
        functools.partial(_attn_kernel, seq=seq),
        grid=(batch, pairs),
        in_specs=[spec(0), spec(1), spec(2),
                  pl.BlockSpec((n_pat, 2, None, 2 * ATTN_BLOCK, 2 * ATTN_BLOCK),
                               lambda b, hp: (0, 0, hp, 0, 0))],
        out_specs=pl.BlockSpec((None, seq, LANES), lambda b, hp: (b, 0, hp)),
        out_shape=jax.ShapeDtypeStruct((batch, seq, D_ATTN), bf16),
        scratch_shapes=[pltpu.VMEM((seq, LANES), f32), pltpu.VMEM((seq, LANES), f32),
                        pltpu.VMEM((seq, LANES), f32),
                        pltpu.VMEM((2 * seq, LANES), bf16),
                        pltpu.VMEM((seq + ATTN_BLOCK, LANES), bf16),
                        pltpu.VMEM((seq + ATTN_BLOCK, 2 * LANES), bf16),
                        pltpu.VMEM((2 * seq, 2 * ATTN_BLOCK), f32),
                        pltpu.VMEM((2 * seq, LANES), f32),
                        pltpu.VMEM((seq, LANES), f32), pltpu.VMEM((seq, LANES), f32),
                        pltpu.VMEM((seq, LANES), f32)],
        compiler_params=_params("parallel", "parallel"),
        name="attention",
    )(view, view, view, band_bias)
    return o.reshape(batch * seq, D_ATTN)


def _gelu_tanh(x):
    return 0.5 * x * (1.0 + jnp.tanh(np.sqrt(2.0 / np.pi) * (x + 0.044715 * x * x * x)))


def _proj_lru_kernel(x_ref, w_ref, convw_ref, convb_ref, wg_ref, ba_ref, bx_ref, lam_ref, g_ref,
                     qkv_ref, y_ref, ubuf, hcarry, *, ts):
    pad = SUBLANES
    t = pl.program_id(1)
    n_qkv = qkv_ref.shape[1]
    xb = x_ref[...].astype(bf16)

    @pl.when(t == 0)
    def _():
        ubuf[0:pad, :] = jnp.zeros((pad, D_LRU), f32)
        hcarry[...] = jnp.zeros_like(hcarry)

    @pl.when(t > 0)
    def _():
        ubuf[0:pad, :] = ubuf[ts:ts + pad, :]

    ubuf[pad:pad + ts, :] = jnp.dot(xb, w_ref[:, n_qkv:n_qkv + D_LRU],
                                    preferred_element_type=f32)
    gate = jnp.dot(xb, w_ref[:, n_qkv + D_LRU:], preferred_element_type=f32)

    qkv_step = n_qkv // 3
    assert qkv_step == D_LRU

    def qkv_chunk(c):
        cols = slice(c * qkv_step, (c + 1) * qkv_step)
        p = jnp.dot(xb, w_ref[:, cols], preferred_element_type=f32)
        qkv_ref[:, cols] = p
        bits = lax.bitcast_convert_type(p, jnp.uint32).reshape(ts // SUBLANES, SUBLANES, qkv_step)
        folded = functools.reduce(jnp.bitwise_or, [bits[g] for g in range(ts // SUBLANES)])
        return ((folded >> 16) >> 16).astype(f32)

    u = convb_ref[...] + convw_ref[CONV_WIDTH - 1:CONV_WIDTH, :] * ubuf[pad:pad + ts, :]
    for w in range(CONV_WIDTH - 1):
        back = CONV_WIDTH - 1 - w
        u = u + convw_ref[w:w + 1, :] * ubuf[pad - back:pad - back + ts, :]

    gates = jnp.dot(u.astype(bf16), wg_ref[...], preferred_element_type=f32)
    r = jax.nn.sigmoid(gates[:, 0:D_LRU] + ba_ref[...])
    i = jax.nn.sigmoid(gates[:, D_LRU:2 * D_LRU] + bx_ref[...])
    neg_lam = -lam_ref[...]
    softplus = jnp.maximum(neg_lam, 0.0) + jnp.log1p(jnp.exp(-jnp.abs(neg_lam)))
    log_a = (-LRU_C) * r * softplus
    a = jnp.exp(log_a)
    b = jnp.sqrt(-jnp.tanh(log_a) * (1.0 + a * a)) * (i * u)
    zero0 = qkv_chunk(0)

    groups = ts // SUBLANES
    a = a.reshape(groups, SUBLANES, D_LRU)
    b = b.reshape(groups, SUBLANES, D_LRU) + zero0[None]
    sub = lax.broadcasted_iota(jnp.int32, (1, SUBLANES, 1), 1)
    shift = 1
    while shift < SUBLANES:
        live = sub >= shift
        a_prev = jnp.where(live, pltpu.roll(a, shift, 1), 1.0)
        b_prev = jnp.where(live, pltpu.roll(b, shift, 1), 0.0)
        b = a * b_prev + b
        a = a * a_prev
        shift *= 2
    h_prev = hcarry[...] + qkv_chunk(1)[0:1, :]
    h_groups = []
    for g in range(groups):
        h_g = a[g] * h_prev + b[g]
        h_prev = h_g[SUBLANES - 1:SUBLANES, :]
        h_groups.append(h_g)
    hcarry[...] = h_prev
    h = jnp.concatenate(h_groups, axis=0)
    zero2 = qkv_chunk(2)[0:1, 0:1]

    y = _gelu_tanh(gate) * h
    y = y * lax.rsqrt(jnp.mean(y * y, axis=-1, keepdims=True) + (RMS_EPS + zero2)) * g_ref[...]
    y_ref[...] = y.astype(y_ref.dtype)


def _block_diag(w):
    g, i, j = w.shape
    eye = jnp.eye(g, dtype=w.dtype)
    return jnp.einsum('gij,gh->gihj', w, eye).reshape(g * i, g * j)


def _proj_lru(x2d, w_in_bf16, conv_w, conv_b, w_a, b_a, w_x, b_x, lam, g_lru, batch, seq,
              ts=LRU_ROWS):
    wg = jnp.concatenate([_block_diag(w_a), _block_diag(w_x)], axis=1).astype(bf16)
    row = lambda v: v.reshape(1, D_LRU).astype(f32)
    const = lambda shape: pl.BlockSpec(shape, lambda b, t: (0, 0))
    tile = lambda width: pl.BlockSpec((None, ts, width), lambda b, t: (b, t, 0))
    qkv, y = pl.pallas_call(
        functools.partial(_proj_lru_kernel, ts=ts),
        grid=(batch, seq // ts),
        in_specs=[tile(D_MODEL), const((D_MODEL, D_IN)),
                  const((CONV_WIDTH, D_LRU)), const((1, D_LRU)),
                  const((D_LRU, 2 * D_LRU)), const((1, D_LRU)), const((1, D_LRU)),
                  const((1, D_LRU)), const((1, D_LRU))],
        out_specs=[tile(3 * D_ATTN), tile(D_LRU)],
        out_shape=[jax.ShapeDtypeStruct((batch, seq, 3 * D_ATTN), f32),
                   jax.ShapeDtypeStruct((batch, seq, D_LRU), bf16)],
        scratch_shapes=[pltpu.VMEM((ts + 2 * SUBLANES, D_LRU), f32), pltpu.VMEM((1, D_LRU), f32)],
        compiler_params=_params("parallel", "arbitrary"),
        name="proj_rglru",
    )(x2d.reshape(batch, seq, D_MODEL), w_in_bf16,
      conv_w.reshape(CONV_WIDTH, D_LRU).astype(f32), row(conv_b), wg, row(b_a), row(b_x),
      row(lam), row(g_lru))
    return qkv.reshape(batch * seq, 3 * D_ATTN), y.reshape(batch * seq, D_LRU)


ROW_TILE = D_MODEL // LANES


def _store_row_tiles(ref, rows):
    t = rows.shape[0]
    for s in range(ROW_TILE):
        ref[pl.ds(s, t, stride=ROW_TILE), :] = rows[:, s * LANES:(s + 1) * LANES]


def _load_row_tiles(ref, t):
    return jnp.concatenate([ref[pl.ds(s, t, stride=ROW_TILE), :] for s in range(ROW_TILE)],
                           axis=-1)


def _layer_norm(z, g, b):
    mu = jnp.mean(z, axis=-1, keepdims=True)
    zc = z - mu
    var = jnp.mean(zc * zc, axis=-1, keepdims=True)
    return zc * lax.rsqrt(var + LN_EPS) * g + b


def _mixer_residual(attn_ref, ylru_ref, x_ref, w_ref, gattn, lng, lnb):
    attn = attn_ref[...].astype(f32)
    attn = attn * lax.rsqrt(jnp.mean(attn * attn, axis=-1, keepdims=True) + RMS_EPS) * gattn[...]
    y = jnp.dot(attn.astype(bf16), w_ref[0:D_ATTN, :], preferred_element_type=f32)
    y = y + jnp.dot(ylru_ref[...], w_ref[D_ATTN:, :], preferred_element_type=f32)
    return _layer_norm(DEEPNORM_ALPHA * x_ref[...] + y, lng[...], lnb[...])


def _swiglu(xb, wg, wu, wd):
    g = jnp.dot(xb, wg, preferred_element_type=f32)
    u = jnp.dot(xb, wu, preferred_element_type=f32)
    h = (g * jax.nn.sigmoid(g)) * u
    return jnp.dot(h.astype(bf16), wd, preferred_element_type=f32)


def _mix_ffn_kernel(attn_ref, ylru_ref, x_ref, wo_ref, gattn, ln1g, ln1b, wg_ref, wu_ref, wd_ref,
                    ln2g, ln2b, out_ref):
    x1 = _mixer_residual(attn_ref, ylru_ref, x_ref, wo_ref, gattn, ln1g, ln1b)
    f = _swiglu(x1.astype(bf16), wg_ref[...], wu_ref[...], wd_ref[...])
    out_ref[...] = _layer_norm(DEEPNORM_ALPHA * x1 + f, ln2g[...], ln2b[...])


def _mix_ffn_dense(attn, ylru, x2d, w_out_bf16, g_attn, ln1, w_gate, w_up, w_down, ln2,
                   tm=DENSE_FFN_ROWS):
    n = x2d.shape[0]
    d_ff = w_gate.shape[1]
    tile = lambda width: pl.BlockSpec((tm, width), lambda i: (i, 0))
    const = lambda shape: pl.BlockSpec(shape, lambda i: (0, 0), pipeline_mode=pl.Buffered(1))
    row = lambda v: v.reshape(1, -1).astype(f32)
    return pl.pallas_call(
        _mix_ffn_kernel,
        grid=(n // tm,),
        in_specs=[tile(D_ATTN), tile(D_LRU), tile(D_MODEL), const((D_MODEL, D_MODEL)),
                  const((1, D_ATTN)), const((1, D_MODEL)), const((1, D_MODEL)),
                  const((D_MODEL, d_ff)), const((D_MODEL, d_ff)), const((d_ff, D_MODEL)),
                  const((1, D_MODEL)), const((1, D_MODEL))],
        out_specs=tile(D_MODEL),
        out_shape=jax.ShapeDtypeStruct((n, D_MODEL), f32),
        compiler_params=_params("parallel"),
        name="mix_ffn_dense",
    )(attn, ylru, x2d, w_out_bf16, row(g_attn), row(ln1[0]), row(ln1[1]), w_gate, w_up, w_down,
      row(ln2[0]), row(ln2[1]))


def _mix_router_kernel(attn_ref, ylru, x_ref, w_ref, gattn, lng, lnb, rw_ref, tri_ref,
                       x1_ref, route_ref, count_ref):
    x1 = _mixer_residual(attn_ref, ylru, x_ref, w_ref, gattn, lng, lnb)
    x1_ref[...] = x1
    lane = lax.broadcasted_iota(jnp.int32, (1, LANES), 1).astype(f32)
    x_hi = x1.astype(bf16)
    x_lo = (x1 - x_hi.astype(f32)).astype(bf16)
    logits = (jnp.dot(x_hi, rw_ref[0], preferred_element_type=f32)
              + jnp.dot(x_lo, rw_ref[0], preferred_element_type=f32)
              + jnp.dot(x_hi, rw_ref[1], preferred_element_type=f32))
    logits = jnp.where(lane < N_EXPERTS, logits, -jnp.inf)
    v1 = jnp.max(logits, axis=-1, keepdims=True)
    i1 = jnp.min(jnp.where(logits == v1, lane, float(LANES)), axis=-1, keepdims=True)
    rest_logits = jnp.where(lane == i1, -jnp.inf, logits)
    v2 = jnp.max(rest_logits, axis=-1, keepdims=True)
    i2 = jnp.min(jnp.where(rest_logits == v2, lane, float(LANES)), axis=-1, keepdims=True)
    e2 = jnp.exp(v2 - v1)
    p1 = 1.0 / (1.0 + e2)
    p2 = e2 / (1.0 + e2)

    @pl.when(pl.program_id(0) == 0)
    def _():
        count_ref[...] = jnp.zeros_like(count_ref)

    tm = x1.shape[0]
    chosen = jnp.logical_or(lane == i1, lane == i2)
    rank = count_ref[...] + jnp.dot(tri_ref[...], chosen.astype(bf16),
                                    preferred_element_type=f32)
    count_ref[...] += jnp.sum(chosen.astype(f32), axis=0, keepdims=True)
    r1 = jnp.sum(jnp.where(lane == i1, rank, 0.0), axis=-1, keepdims=True)
    r2 = jnp.sum(jnp.where(lane == i2, rank, 0.0), axis=-1, keepdims=True)
    fields = (i1, i2, r1, r2, p1, p2)
    route = jnp.zeros((tm, LANES), f32)
    for k, val in enumerate(fields):
        route = jnp.where(lane == k, val, route)
    route_ref[...] = route


def _mix_router(attn, ylru, x2d, w_out_bf16, g_attn, ln_g, ln_b, router_w, tm=ROUTER_ROWS):
    n = x2d.shape[0]
    tile = lambda width: pl.BlockSpec((tm, width), lambda i: (i, 0))
    const = lambda shape: pl.BlockSpec(shape, lambda i: (0,) * len(shape))
    row = lambda v: v.reshape(1, -1).astype(f32)
    rw = jnp.zeros((D_MODEL, LANES), f32).at[:, :N_EXPERTS].set(router_w.astype(f32))
    rw_hi = rw.astype(bf16)
    rw_lo = (rw - rw_hi.astype(f32)).astype(bf16)
    strictly_lower = jnp.asarray(np.tri(tm, k=-1), bf16)
    return pl.pallas_call(
        _mix_router_kernel,
        grid=(n // tm,),
        in_specs=[tile(D_ATTN), tile(D_LRU), tile(D_MODEL), const((D_MODEL, D_MODEL)),
                  const((1, D_ATTN)), const((1, D_MODEL)), const((1, D_MODEL)),
                  const((2, D_MODEL, LANES)), const((tm, tm))],
        out_specs=[tile(D_MODEL), tile(LANES), const((1, LANES))],
        out_shape=[jax.ShapeDtypeStruct((n, D_MODEL), f32), jax.ShapeDtypeStruct((n, LANES), f32),
                   jax.ShapeDtypeStruct((1, LANES), f32)],
        compiler_params=_params("arbitrary"),
        name="mix_out_router",
    )(attn, ylru, x2d, w_out_bf16, row(g_attn), row(ln_g), row(ln_b), jnp.stack([rw_hi, rw_lo]),
      strictly_lower)


def _expert_ffn_kernel(tile_expert_ref, n_used_ref, x_ref, wg_ref, wu_ref, wd_ref, out_ref,
                       acc_ref, xb_ref, *, n_chunks):
    del tile_expert_ref
    j = pl.program_id(1)
    used = pl.program_id(0) < n_used_ref[0]
    last = n_chunks - 1
    assert n_chunks >= 2

    def chunk_out(xb):
        return _swiglu(xb, wg_ref[...], wu_ref[...], wd_ref[...])

    @pl.when(used & (j == 0))
    def _():
        xb = _load_row_tiles(x_ref, xb_ref.shape[0]).astype(bf16)
        xb_ref[...] = xb
        acc_ref[...] = chunk_out(xb)

    if n_chunks > 2:
        @pl.when(used & (j > 0) & (j < last))
        def _():
            acc_ref[...] += chunk_out(xb_ref[...])

    @pl.when(used & (j == last))
    def _():
        _store_row_tiles(out_ref, acc_ref[...] + chunk_out(xb_ref[...]))

    @pl.when(jnp.logical_not(used) & (j == last))
    def _():
        _store_row_tiles(out_ref, jnp.zeros(acc_ref.shape, f32))


def _expert_ffn(x_rows, tile_expert, n_used, w_gate, w_up, w_down, *, tm, tf):
    rows = x_rows.shape[0] // ROW_TILE
    n_chunks = w_gate.shape[2] // tf
    io_block = (tm * ROW_TILE, LANES)

    def chunk(i, j, nu):
        return jnp.where(i < nu[0], j, n_chunks - 1)

    return pl.pallas_call(
        functools.partial(_expert_ffn_kernel, n_chunks=n_chunks),
        grid_spec=pltpu.PrefetchScalarGridSpec(
            num_scalar_prefetch=2,
            grid=(rows // tm, n_chunks),
            in_specs=[pl.BlockSpec(io_block, lambda i, j, te, nu: (i, 0)),
                      pl.BlockSpec((None, D_MODEL, tf),
                                   lambda i, j, te, nu: (te[i], 0, chunk(i, j, nu))),
                      pl.BlockSpec((None, D_MODEL, tf),
                                   lambda i, j, te, nu: (te[i], 0, chunk(i, j, nu))),
                      pl.BlockSpec((None, tf, D_MODEL),
                                   lambda i, j, te, nu: (te[i], chunk(i, j, nu), 0))],
            out_specs=pl.BlockSpec(io_block, lambda i, j, te, nu: (i, 0)),
            scratch_shapes=[pltpu.VMEM((tm, D_MODEL), f32), pltpu.VMEM((tm, D_MODEL), bf16)]),
        out_shape=jax.ShapeDtypeStruct(x_rows.shape, f32),
        compiler_params=_params("parallel", "arbitrary"),
        name="ffn_experts",
    )(tile_expert, n_used, x_rows, w_gate, w_up, w_down)


def _dispatch_kernel(fill_ref, pos_ref, x_ref, xs_hbm, zero_buf, rows_buf, sem, *, tt, tm):
    step = pl.program_id(0)

    n_fill = fill_ref.shape[0] // 2

    @pl.when(step == 0)
    def _():
        zero_buf[...] = jnp.zeros_like(zero_buf)

        def fill_copy(f):
            start = pl.multiple_of(fill_ref[f] * ROW_TILE, ROW_TILE)
            return pltpu.make_async_copy(zero_buf, xs_hbm.at[pl.ds(start, tm * ROW_TILE)], sem)

        for f in range(n_fill):
            @pl.when(fill_ref[n_fill + f] > 0)
            def _():
                fill_copy(f).start()
        for f in range(n_fill):
            @pl.when(fill_ref[n_fill + f] > 0)
            def _():
                fill_copy(f).wait()

    _store_row_tiles(rows_buf, x_ref[...])

    def issue(t, carry):
        src = pl.multiple_of(t * ROW_TILE, ROW_TILE)
        for k in range(TOP_K):
            dst = pl.multiple_of(pos_ref[0, TOP_K * t + k] * ROW_TILE, ROW_TILE)
            pltpu.make_async_copy(rows_buf.at[pl.ds(src, ROW_TILE)],
                                  xs_hbm.at[pl.ds(dst, ROW_TILE)], sem).start(priority=k)
        return carry

    lax.fori_loop(0, tt, issue, 0, unroll=DMA_UNROLL)
    for _ in range(TOP_K):
        pltpu.make_async_copy(rows_buf, xs_hbm.at[pl.ds(0, tt * ROW_TILE)], sem).wait()


def _dispatch(x1, pos_blocks, fill, rows_sorted, *, tt, tm):
    n = x1.shape[0]
    return pl.pallas_call(
        functools.partial(_dispatch_kernel, tt=tt, tm=tm),
        grid_spec=pltpu.PrefetchScalarGridSpec(
            num_scalar_prefetch=1,
            grid=(n // tt,),
            in_specs=[pl.BlockSpec((None, 1, TOP_K * tt), lambda i, fill: (i, 0, 0),
                                   memory_space=pltpu.SMEM),
                      pl.BlockSpec((tt, D_MODEL), lambda i, fill: (i, 0))],
            out_specs=pl.BlockSpec(memory_space=pl.ANY),
            scratch_shapes=[pltpu.VMEM((tm * ROW_TILE, LANES), f32),
                            pltpu.VMEM((tt * ROW_TILE, LANES), f32),
                            pltpu.SemaphoreType.DMA(())]),
        out_shape=jax.ShapeDtypeStruct((rows_sorted * ROW_TILE, LANES), f32),
        compiler_params=_params("arbitrary"),
        name="moe_dispatch",
    )(fill, pos_blocks, x1)


def _combine_kernel(pos_ref, pos_next_ref, route_ref, x1_ref, ys_hbm, lng, lnb, out_ref, buf, sems,
                    *, tt, n_steps):
    step = pl.program_id(0)
    slot = step % 2

    def issue_tile(tile_pos_ref, into):
        def issue(t, carry):
            dst = pl.multiple_of(t * ROW_TILE, ROW_TILE)
            for k in range(TOP_K):
                src = pl.multiple_of(tile_pos_ref[0, TOP_K * t + k] * ROW_TILE, ROW_TILE)
                pltpu.make_async_copy(ys_hbm.at[pl.ds(src, ROW_TILE)],
                                      buf.at[into, k, pl.ds(dst, ROW_TILE)],
                                      sems.at[into]).start(priority=k)
            return carry

        lax.fori_loop(0, tt, issue, 0, unroll=DMA_UNROLL)

    @pl.when(step == 0)
    def _():
        issue_tile(pos_ref, 0)

    @pl.when(step + 1 < n_steps)
    def _():
        issue_tile(pos_next_ref, 1 - slot)

    for k in range(TOP_K):
        pltpu.make_async_copy(ys_hbm.at[pl.ds(0, tt * ROW_TILE)], buf.at[slot, k],
                              sems.at[slot]).wait()

    lane = lax.broadcasted_iota(jnp.int32, (1, LANES), 1)
    route = route_ref[...]
    p1 = jnp.sum(jnp.where(lane == 4, route, 0.0), axis=-1, keepdims=True)
    p2 = jnp.sum(jnp.where(lane == 5, route, 0.0), axis=-1, keepdims=True)
    y = p1 * _load_row_tiles(buf.at[slot, 0], tt) + p2 * _load_row_tiles(buf.at[slot, 1], tt)
    out_ref[...] = _layer_norm(DEEPNORM_ALPHA * x1_ref[...] + y, lng[...], lnb[...])


def _combine(pos_blocks, route, x1, y_sorted, ln_g, ln_b, *, tt):
    n = x1.shape[0]
    n_steps = n // tt
    pos_spec = lambda index: pl.BlockSpec((None, 1, TOP_K * tt), index, memory_space=pltpu.SMEM)
    return pl.pallas_call(
        functools.partial(_combine_kernel, tt=tt, n_steps=n_steps),
        grid=(n_steps,),
        in_specs=[pos_spec(lambda i: (i, 0, 0)),
                  pos_spec(lambda i: (jnp.minimum(i + 1, n_steps - 1), 0, 0)),
                  pl.BlockSpec((tt, LANES), lambda i: (i, 0)),
                  pl.BlockSpec((tt, D_MODEL), lambda i: (i, 0)),
                  pl.BlockSpec(memory_space=pl.ANY),
                  pl.BlockSpec((1, D_MODEL), lambda i: (0, 0)),
                  pl.BlockSpec((1, D_MODEL), lambda i: (0, 0))],
        out_specs=pl.BlockSpec((tt, D_MODEL), lambda i: (i, 0)),
        out_shape=jax.ShapeDtypeStruct((n, D_MODEL), f32),
        scratch_shapes=[pltpu.VMEM((2, TOP_K, tt * ROW_TILE, LANES), f32),
                        pltpu.SemaphoreType.DMA((2,))],
        compiler_params=_params("arbitrary"),
        name="moe_combine",
    )(pos_blocks, pos_blocks, route, x1, y_sorted, ln_g.reshape(1, D_MODEL).astype(f32),
      ln_b.reshape(1, D_MODEL).astype(f32))


def _moe(x1, route, counts, w_gate, w_up, w_down, ln_g, ln_b, *, tm, tf):
    n = x1.shape[0]
    i32 = jnp.int32
    counts = counts[0, :N_EXPERTS].astype(i32)
    padded = (counts + tm - 1) // tm * tm
    ends = jnp.cumsum(padded)
    offsets = ends - padded
    experts = route[:, 0:TOP_K].astype(i32)
    ranks = route[:, TOP_K:2 * TOP_K].astype(i32)
    pos = offsets[experts] + ranks
    pos_blocks = lambda tt: pos.reshape(n // tt, 1, TOP_K * tt)
    n_tiles = TOP_K * n // tm + N_EXPERTS
    n_used = (ends[-1] // tm).astype(i32).reshape(1)
    tile_ids = jnp.arange(n_tiles, dtype=i32)
    tile_expert = jnp.sum((tile_ids[:, None] >= (ends // tm)[None, :]).astype(i32), axis=1)
    last_expert = jnp.max(jnp.where(counts > 0, jnp.arange(N_EXPERTS, dtype=i32), 0))
    tile_expert = jnp.minimum(tile_expert, last_expert).astype(i32)
    tail_tiles = n_used[0] + jnp.arange(N_EXPERTS, dtype=i32)
    fill = jnp.concatenate([ends - tm, tail_tiles * tm,
                            (counts > 0).astype(i32), (tail_tiles < n_tiles).astype(i32)]).astype(i32)

    x_sorted = _dispatch(x1, pos_blocks(DISPATCH_ROWS), fill, n_tiles * tm, tt=DISPATCH_ROWS,
                         tm=tm)
    y_sorted = _expert_ffn(x_sorted, tile_expert, n_used, w_gate, w_up, w_down, tm=tm, tf=tf)
    return _combine(pos_blocks(COMBINE_ROWS), route, x1, y_sorted, ln_g, ln_b, tt=COMBINE_ROWS)


def kernel(x, w_in, conv_w, conv_b, w_a, b_a, w_x, b_x, lru_lambda, rel_bias, g_attn, g_lru, w_out, ln1_g, ln1_b, ln2_g, ln2_b, ffn_w_gate, ffn_w_up, ffn_w_down, router_w, moe_w_gate, moe_w_up, moe_w_down):
    batch, seq, _ = x.shape
    n = batch * seq
    h = x.reshape(n, D_MODEL).astype(f32)
    band_bias = jnp.stack([_band_bias(rel_bias, d) for _, d in DILATED_PATTERNS])
    for layer in range(DEPTH):
        qkv, ylru = _proj_lru(h, w_in[layer].astype(bf16), conv_w[layer], conv_b[layer],
                              w_a[layer], b_a[layer], w_x[layer], b_x[layer], lru_lambda[layer],
                              g_lru[layer], batch, seq)
        attn = _attention(qkv, band_bias, batch, seq)
        j = layer // 2
        w_o = w_out[layer].astype(bf16)
        if layer % 2 == 0:
            h = _mix_ffn_dense(attn, ylru, h, w_o, g_attn[layer], (ln1_g[layer], ln1_b[layer]),
                               ffn_w_gate[j].astype(bf16), ffn_w_up[j].astype(bf16),
                               ffn_w_down[j].astype(bf16), (ln2_g[layer], ln2_b[layer]))
        else:
            x1, route, counts = _mix_router(attn, ylru, h, w_o, g_attn[layer], ln1_g[layer],
                                            ln1_b[layer], router_w[j])
            h = _moe(x1, route, counts, moe_w_gate[j].astype(bf16), moe_w_up[j].astype(bf16),
                     moe_w_down[j].astype(bf16), ln2_g[layer], ln2_b[layer],
                     tm=EXPERT_ROWS, tf=EXPERT_FF_CHUNK)
    return h.reshape(batch, seq, D_MODEL).astype(x.dtype)
```

```python
import functools

import numpy as np
import jax
import jax.numpy as jnp
from jax import lax
from jax.experimental import pallas as pl
from jax.experimental.pallas import tpu as pltpu

D_MODEL = 1024
N_HEADS = 8
HEAD_DIM = 64
D_ATTN = N_HEADS * HEAD_DIM
D_LRU = 512
N_LRU_BLOCKS = 8
LRU_BLOCK = D_LRU // N_LRU_BLOCKS
CONV_WIDTH = 4
LRU_C = 8.0
DILATED_PATTERNS = ((128, 1), (512, 4), (2048, 16))
ATTN_BLOCK = 128
STAGE_DILATION = 4
N_BUCKETS = 32
MAX_DISTANCE = 2048
D_IN = 3 * D_ATTN + 2 * D_LRU
N_EXPERTS = 8
TOP_K = 2
DEPTH = 2
DEEPNORM_ALPHA = (2.0 * DEPTH) ** 0.25
LN_EPS = 1e-5
RMS_EPS = 1e-6
NEG_INF = -1e30
LOG2_E = float(np.log2(np.e))

LANES = 128
SUBLANES = 8
VMEM_LIMIT_BYTES = 56 * 1024 * 1024
MXU_DEPTH = 256

LRU_ROWS = 1024
ROUTER_ROWS = 1024
DENSE_FFN_ROWS = 1024
EXPERT_ROWS = 512
EXPERT_FF_CHUNK = 7 * MXU_DEPTH
DISPATCH_ROWS = 1024
COMBINE_ROWS = 512
DMA_UNROLL = 8
ATTN_UNROLL = 4

f32 = jnp.float32
bf16 = jnp.bfloat16


def _params(*semantics):
    return pltpu.CompilerParams(dimension_semantics=semantics,
                                vmem_limit_bytes=VMEM_LIMIT_BYTES)


def _t5_bucket(dist):
    max_exact = N_BUCKETS // 2
    d = np.maximum(dist, 1).astype(np.float32)
    large = max_exact + (np.log(d / max_exact) / np.log(MAX_DISTANCE / max_exact)
                         * (N_BUCKETS - max_exact)).astype(np.int32)
    large = np.minimum(large, N_BUCKETS - 1)
    return np.where(dist < max_exact, dist, large).astype(np.int32)


def _band_bias(rel_bias, dilation):
    nk = ATTN_BLOCK
    qi = np.arange(nk)[:, None]
    kj = np.arange(2 * nk)[None, :]
    delta = qi + nk - kj
    band = (delta >= 0) & (delta <= nk)
    bucket = _t5_bucket(np.clip(delta, 0, nk) * dilation)
    onehot = np.eye(N_BUCKETS, dtype=np.float32)[bucket.reshape(-1)]
    bias = jnp.dot(jnp.asarray(onehot), rel_bias.astype(f32), precision=lax.Precision.HIGHEST)
    bias = jnp.transpose(bias.reshape(nk, 2 * nk, N_HEADS), (2, 0, 1))
    valid = np.stack([band, band & (kj >= nk)])[:, None]
    bias = jnp.where(jnp.asarray(valid), bias[None], NEG_INF)
    return (bias * LOG2_E).reshape(2, N_HEADS // 2, 2 * nk, 2 * nk)


def _attn_kernel(q_ref, k_ref, v_ref, bias_ref, o_ref, q4, k4, v4, qs, ks, vs, s_buf, m_buf,
                 m_s, l_s, acc_s, *, seq):
    nk = ATTN_BLOCK
    n_blocks = seq // nk
    chunk = 2 * nk
    lane = lax.broadcasted_iota(jnp.int32, (1, LANES), 1)
    head0 = lane < HEAD_DIM

    ks[0:nk, :] = jnp.zeros((nk, LANES), bf16)
    vs[0:nk, 0:LANES] = jnp.zeros((nk, LANES), bf16)
    vs[:, LANES:2 * LANES] = jnp.ones((seq + nk, LANES), bf16)

    def stage(t, carry):
        chunks_per_residue = seq // STAGE_DILATION // chunk
        src = pl.ds(t // chunks_per_residue + STAGE_DILATION * chunk * (t % chunks_per_residue),
                    chunk, stride=STAGE_DILATION)
        dst = pl.ds(pl.multiple_of(t * chunk, chunk), chunk)
        q4[dst, :] = q_ref[src, :]
        k4[dst, :] = k_ref[src, :]
        v4[dst, :] = v_ref[src, :]
        return carry

    lax.fori_loop(0, seq // chunk, stage, 0)

    order = sorted(range(len(DILATED_PATTERNS)), key=lambda p: -DILATED_PATTERNS[p][1])
    assert DILATED_PATTERNS[order[-1]][1] == 1
    for p in order:
        d = DILATED_PATTERNS[p][1]
        is_first, is_last = p == order[0], p == order[-1]
        length = seq // d
        nb = length // nk
        chunks_per_residue = length // chunk

        def gather(t, carry):
            r = t // chunks_per_residue
            c = t % chunks_per_residue
            dst = pl.multiple_of(t * chunk, chunk)
            if d % STAGE_DILATION == 0:
                sub = d // STAGE_DILATION
                start = ((r % STAGE_DILATION) * (seq // STAGE_DILATION) + r // STAGE_DILATION
                         + sub * chunk * c)
                src = (pl.ds(start, chunk, stride=sub) if sub > 1
                       else pl.ds(pl.multiple_of(start, chunk), chunk))
                q, k, v = q4[src, :], k4[src, :], v4[src, :]
            else:
                src = pl.ds(r + d * chunk * c, chunk, stride=d) if d > 1 else pl.ds(dst, chunk)
                q, k, v = q_ref[src, :], k_ref[src, :], v_ref[src, :]
            q = q * (HEAD_DIM ** -0.5 * LOG2_E)
            q0 = jnp.where(head0, q, 0.0).astype(bf16)
            q1 = jnp.where(head0, 0.0, q).astype(bf16)
            for half in range(2):
                base = pl.multiple_of(2 * dst + half * chunk, chunk)
                qs[pl.ds(base, nk), :] = q0[half * nk:(half + 1) * nk]
                qs[pl.ds(base + nk, nk), :] = q1[half * nk:(half + 1) * nk]
            ks[pl.ds(nk + dst, chunk), :] = k.astype(bf16)
            vs[pl.ds(nk + dst, chunk), 0:LANES] = v.astype(bf16)
            return carry

        lax.fori_loop(0, seq // chunk, gather, 0)

        def scores(b, carry):
            first = jnp.asarray(b % nb == 0, jnp.int32)
            rows = pl.ds(pl.multiple_of(b * chunk, chunk), chunk)
            k = ks[pl.ds(pl.multiple_of(b * nk, nk), chunk), :]
            s_buf[rows, :] = lax.dot_general(qs[rows, :], k, (((1,), (1,)), ((), ())),
                                             preferred_element_type=f32) + bias_ref[p, first]
            return carry

        def rowmax(b, carry):
            rows = pl.ds(pl.multiple_of(b * chunk, chunk), chunk)
            m_buf[rows, :] = jnp.broadcast_to(jnp.max(s_buf[rows, :], axis=-1, keepdims=True),
                                              (chunk, LANES))
            return carry

        def block(b, carry):
            r = b // nb
            i = b % nb
            rows = pl.ds(pl.multiple_of(b * chunk, chunk), chunk)
            v = vs[pl.ds(pl.multiple_of(b * nk, nk), chunk), :]
            m = m_buf[rows, :]
            e = jnp.exp2(s_buf[rows, :] - jnp.concatenate([m, m], axis=1))
            pv = jnp.dot(e.astype(bf16), v, preferred_element_type=f32)
            m_blk = jnp.where(head0, m[0:nk], m[nk:chunk])
            l_blk = jnp.where(head0, pv[0:nk, LANES:], pv[nk:chunk, LANES:])
            pv_blk = jnp.where(head0, pv[0:nk, 0:LANES], pv[nk:chunk, 0:LANES])
            if d == 1:
                tok = pl.ds(pl.multiple_of(b * nk, nk), nk)
            else:
                tok = pl.ds(r + d * nk * i, nk, stride=d)
            if is_first:
                m_s[tok, :] = m_blk
                l_s[tok, :] = l_blk
                acc_s[tok, :] = pv_blk
                return carry
            m_old = m_s[tok, :]
            m_new = jnp.maximum(m_old, m_blk)
            w_old = jnp.exp2(m_old - m_new)
            w_blk = jnp.exp2(m_blk - m_new)
            l_new = w_old * l_s[tok, :] + w_blk * l_blk
            acc_new = w_old * acc_s[tok, :] + w_blk * pv_blk
            if is_last:
                o_ref[tok, :] = (acc_new / l_new).astype(o_ref.dtype)
            else:
                l_s[tok, :] = l_new
                acc_s[tok, :] = acc_new
                m_s[tok, :] = m_new
            return carry

        n_groups = n_blocks // ATTN_UNROLL
        assert n_groups >= 3

        def group(fn, g):
            for j in range(ATTN_UNROLL):
                fn(g * ATTN_UNROLL + j, 0)

        def pipelined(g, carry):
            group(block, g)
            group(rowmax, g + 1)
            group(scores, g + 2)
            return carry

        group(scores, 0)
        group(scores, 1)
        group(rowmax, 0)
        lax.fori_loop(0, n_groups - 2, pipelined, 0)
        group(block, n_groups - 2)
        group(rowmax, n_groups - 1)
        group(block, n_groups - 1)


def _attention(qkv, band_bias, batch, seq):
    view = qkv.reshape(batch, seq, 3 * D_ATTN)
    pairs = D_ATTN // LANES

    def spec(offset):
        return pl.BlockSpec((None, seq, LANES), lambda b, hp: (b, 0, offset * pairs + hp))

    n_pat = len(DILATED_PATTERNS)
    o = pl.pallas_call(
        functools.partial(_attn_kernel, seq=seq),
        grid=(batch, pairs),
        in_specs=[spec(0), spec(1), spec(2),
                  pl.BlockSpec((n_pat, 2, None, 2 * ATTN_BLOCK, 2 * ATTN_BLOCK),
                               lambda b, hp: (0, 0, hp, 0, 0))],
        out_specs=pl.BlockSpec((None, seq, LANES), lambda b, hp: (b, 0, hp)),
        out_shape=jax.ShapeDtypeStruct((batch, seq, D_ATTN), bf16),
        scratch_shapes=[pltpu.VMEM((seq, LANES), f32), pltpu.VMEM((seq, LANES), f32),
                        pltpu.VMEM((seq, LANES), f32),
                        pltpu.VMEM((2 * seq, LANES), bf16),
                        pltpu.VMEM((seq + ATTN_BLOCK, LANES), bf16),
                        pltpu.VMEM((seq + ATTN_BLOCK, 2 * LANES), bf16),
                        pltpu.VMEM((2 * seq, 2 * ATTN_BLOCK), f32),
                        pltpu.VMEM((2 * seq, LANES), f32),
                        pltpu.VMEM((seq, LANES), f32), pltpu.VMEM((seq, LANES), f32),
                        pltpu.VMEM((seq, LANES), f32)],
        compiler_params=_params("parallel", "parallel"),
        name="attention",
    )(view, view, view, band_bias)
    return o.reshape(batch * seq, D_ATTN)


def _gelu_tanh(x):
    return 0.5 * x * (1.0 + jnp.tanh(np.sqrt(2.0 / np.pi) * (x + 0.044715 * x * x * x)))


def _proj_lru_kernel(x_ref, w_ref, convw_ref, convb_ref, wg_ref, ba_ref, bx_ref, lam_ref, g_ref,
                     qkv_ref, y_ref, ubuf, hcarry, *, ts):
    pad = SUBLANES
    t = pl.program_id(1)
    n_qkv = qkv_ref.shape[1]
    xb = x_ref[...].astype(bf16)

    @pl.when(t == 0)
    def _():
        ubuf[0:pad, :] = jnp.zeros((pad, D_LRU), f32)
        hcarry[...] = jnp.zeros_like(hcarry)

    @pl.when(t > 0)
    def _():
        ubuf[0:pad, :] = ubuf[ts:ts + pad, :]

    ubuf[pad:pad + ts, :] = jnp.dot(xb, w_ref[:, n_qkv:n_qkv + D_LRU],
                                    preferred_element_type=f32)
    gate = jnp.dot(xb, w_ref[:, n_qkv + D_LRU:], preferred_element_type=f32)

    qkv_step = n_qkv // 3
    assert qkv_step == D_LRU

    def qkv_chunk(c):
        cols = slice(c * qkv_step, (c + 1) * qkv_step)
        p = jnp.dot(xb, w_ref[:, cols], preferred_element_type=f32)
        qkv_ref[:, cols] = p
        bits = lax.bitcast_convert_type(p, jnp.uint32).reshape(ts // SUBLANES, SUBLANES, qkv_step)
        folded = functools.reduce(jnp.bitwise_or, [bits[g] for g in range(ts // SUBLANES)])
        return ((folded >> 16) >> 16).astype(f32)

    u = convb_ref[...] + convw_ref[CONV_WIDTH - 1:CONV_WIDTH, :] * ubuf[pad:pad + ts, :]
    for w in range(CONV_WIDTH - 1):
        back = CONV_WIDTH - 1 - w
        u = u + convw_ref[w:w + 1, :] * ubuf[pad - back:pad - back + ts, :]

    gates = jnp.dot(u.astype(bf16), wg_ref[...], preferred_element_type=f32)
    r = jax.nn.sigmoid(gates[:, 0:D_LRU] + ba_ref[...])
    i = jax.nn.sigmoid(gates[:, D_LRU:2 * D_LRU] + bx_ref[...])
    neg_lam = -lam_ref[...]
    softplus = jnp.maximum(neg_lam, 0.0) + jnp.log1p(jnp.exp(-jnp.abs(neg_lam)))
    log_a = (-LRU_C) * r * softplus
    a = jnp.exp(log_a)
    b = jnp.sqrt(-jnp.tanh(log_a) * (1.0 + a * a)) * (i * u)
    zero0 = qkv_chunk(0)

    groups = ts // SUBLANES
    a = a.reshape(groups, SUBLANES, D_LRU)
    b = b.reshape(groups, SUBLANES, D_LRU) + zero0[None]
    sub = lax.broadcasted_iota(jnp.int32, (1, SUBLANES, 1), 1)
    shift = 1
    while shift < SUBLANES:
        live = sub >= shift
        a_prev = jnp.where(live, pltpu.roll(a, shift, 1), 1.0)
        b_prev = jnp.where(live, pltpu.roll(b, shift, 1), 0.0)
        b = a * b_prev + b
        a = a * a_prev
        shift *= 2
    h_prev = hcarry[...] + qkv_chunk(1)[0:1, :]
    h_groups = []
    for g in range(groups):
        h_g = a[g] * h_prev + b[g]
        h_prev = h_g[SUBLANES - 1:SUBLANES, :]
        h_groups.append(h_g)
    hcarry[...] = h_prev
    h = jnp.concatenate(h_groups, axis=0)
    zero2 = qkv_chunk(2)[0:1, 0:1]

    y = _gelu_tanh(gate) * h
    y = y * lax.rsqrt(jnp.mean(y * y, axis=-1, keepdims=True) + (RMS_EPS + zero2)) * g_ref[...]
    y_ref[...] = y.astype(y_ref.dtype)


def _block_diag(w):
    g, i, j = w.shape
    eye = jnp.eye(g, dtype=w.dtype)
    return jnp.einsum('gij,gh->gihj', w, eye).reshape(g * i, g * j)


def _proj_lru(x2d, w_in_bf16, conv_w, conv_b, w_a, b_a, w_x, b_x, lam, g_lru, batch, seq,
              ts=LRU_ROWS):
    wg = jnp.concatenate([_block_diag(w_a), _block_diag(w_x)], axis=1).astype(bf16)
    row = lambda v: v.reshape(1, D_LRU).astype(f32)
    const = lambda shape: pl.BlockSpec(shape, lambda b, t: (0, 0))
    tile = lambda width: pl.BlockSpec((None, ts, width), lambda b, t: (b, t, 0))
    qkv, y = pl.pallas_call(
        functools.partial(_proj_lru_kernel, ts=ts),
        grid=(batch, seq // ts),
        in_specs=[tile(D_MODEL), const((D_MODEL, D_IN)),
                  const((CONV_WIDTH, D_LRU)), const((1, D_LRU)),
                  const((D_LRU, 2 * D_LRU)), const((1, D_LRU)), const((1, D_LRU)),
                  const((1, D_LRU)), const((1, D_LRU))],
        out_specs=[tile(3 * D_ATTN), tile(D_LRU)],
        out_shape=[jax.ShapeDtypeStruct((batch, seq, 3 * D_ATTN), f32),
                   jax.ShapeDtypeStruct((batch, seq, D_LRU), bf16)],
        scratch_shapes=[pltpu.VMEM((ts + 2 * SUBLANES, D_LRU), f32), pltpu.VMEM((1, D_LRU), f32)],
        compiler_params=_params("parallel", "arbitrary"),
        name="proj_rglru",
    )(x2d.reshape(batch, seq, D_MODEL), w_in_bf16,
      conv_w.reshape(CONV_WIDTH, D_LRU).astype(f32), row(conv_b), wg, row(b_a), row(b_x),
      row(lam), row(g_lru))
    return qkv.reshape(batch * seq, 3 * D_ATTN), y.reshape(batch * seq, D_LRU)


ROW_TILE = D_MODEL // LANES


def _store_row_tiles(ref, rows):
    t = rows.shape[0]
    for s in range(ROW_TILE):
        ref[pl.ds(s, t, stride=ROW_TILE), :] = rows[:, s * LANES:(s + 1) * LANES]


def _load_row_tiles(ref, t):
    return jnp.concatenate([ref[pl.ds(s, t, stride=ROW_TILE), :] for s in range(ROW_TILE)],
                           axis=-1)


def _layer_norm(z, g, b):
    mu = jnp.mean(z, axis=-1, keepdims=True)
    zc = z - mu
    var = jnp.mean(zc * zc, axis=-1, keepdims=True)
    return zc * lax.rsqrt(var + LN_EPS) * g + b


def _mixer_residual(attn_ref, ylru_ref, x_ref, w_ref, gattn, lng, lnb):
    attn = attn_ref[...].astype(f32)
    attn = attn * lax.rsqrt(jnp.mean(attn * attn, axis=-1, keepdims=True) + RMS_EPS) * gattn[...]
    y = jnp.dot(attn.astype(bf16), w_ref[0:D_ATTN, :], preferred_element_type=f32)
    y = y + jnp.dot(ylru_ref[...], w_ref[D_ATTN:, :], preferred_element_type=f32)
    return _layer_norm(DEEPNORM_ALPHA * x_ref[...] + y, lng[...], lnb[...])


def _swiglu(xb, wg, wu, wd):
    g = jnp.dot(xb, wg, preferred_element_type=f32)
    u = jnp.dot(xb, wu, preferred_element_type=f32)
    h = (g * jax.nn.sigmoid(g)) * u
    return jnp.dot(h.astype(bf16), wd, preferred_element_type=f32)


def _mix_ffn_kernel(attn_ref, ylru_ref, x_ref, wo_ref, gattn, ln1g, ln1b, wg_ref, wu_ref, wd_ref,
                    ln2g, ln2b, out_ref):
    x1 = _mixer_residual(attn_ref, ylru_ref, x_ref, wo_ref, gattn, ln1g, ln1b)
    f = _swiglu(x1.astype(bf16), wg_ref[...], wu_ref[...], wd_ref[...])
    out_ref[...] = _layer_norm(DEEPNORM_ALPHA * x1 + f, ln2g[...], ln2b[...])


def _mix_ffn_dense(attn, ylru, x2d, w_out_bf16, g_attn, ln1, w_gate, w_up, w_down, ln2,
                   tm=DENSE_FFN_ROWS):
    n = x2d.shape[0]
    d_ff = w_gate.shape[1]
    tile = lambda width: pl.BlockSpec((tm, width), lambda i: (i, 0))
    const = lambda shape: pl.BlockSpec(shape, lambda i: (0, 0), pipeline_mode=pl.Buffered(1))
    row = lambda v: v.reshape(1, -1).astype(f32)
    return pl.pallas_call(
        _mix_ffn_kernel,
        grid=(n // tm,),
        in_specs=[tile(D_ATTN), tile(D_LRU), tile(D_MODEL), const((D_MODEL, D_MODEL)),
                  const((1, D_ATTN)), const((1, D_MODEL)), const((1, D_MODEL)),
                  const((D_MODEL, d_ff)), const((D_MODEL, d_ff)), const((d_ff, D_MODEL)),
                  const((1, D_MODEL)), const((1, D_MODEL))],
        out_specs=tile(D_MODEL),
        out_shape=jax.ShapeDtypeStruct((n, D_MODEL), f32),
        compiler_params=_params("parallel"),
        name="mix_ffn_dense",
    )(attn, ylru, x2d, w_out_bf16, row(g_attn), row(ln1[0]), row(ln1[1]), w_gate, w_up, w_down,
      row(ln2[0]), row(ln2[1]))


def _mix_router_kernel(attn_ref, ylru, x_ref, w_ref, gattn, lng, lnb, rw_ref, tri_ref,
                       x1_ref, route_ref, count_ref):
    x1 = _mixer_residual(attn_ref, ylru, x_ref, w_ref, gattn, lng, lnb)
    x1_ref[...] = x1
    lane = lax.broadcasted_iota(jnp.int32, (1, LANES), 1).astype(f32)
    x_hi = x1.astype(bf16)
    x_lo = (x1 - x_hi.astype(f32)).astype(bf16)
    logits = (jnp.dot(x_hi, rw_ref[0], preferred_element_type=f32)
              + jnp.dot(x_lo, rw_ref[0], preferred_element_type=f32)
              + jnp.dot(x_hi, rw_ref[1], preferred_element_type=f32))
    logits = jnp.where(lane < N_EXPERTS, logits, -jnp.inf)
    v1 = jnp.max(logits, axis=-1, keepdims=True)
    i1 = jnp.min(jnp.where(logits == v1, lane, float(LANES)), axis=-1, keepdims=True)
    rest_logits = jnp.where(lane == i1, -jnp.inf, logits)
    v2 = jnp.max(rest_logits, axis=-1, keepdims=True)
    i2 = jnp.min(jnp.where(rest_logits == v2, lane, float(LANES)), axis=-1, keepdims=True)
    e2 = jnp.exp(v2 - v1)
    p1 = 1.0 / (1.0 + e2)
    p2 = e2 / (1.0 + e2)

    @pl.when(pl.program_id(0) == 0)
    def _():
        count_ref[...] = jnp.zeros_like(count_ref)

    tm = x1.shape[0]
    chosen = jnp.logical_or(lane == i1, lane == i2)
    rank = count_ref[...] + jnp.dot(tri_ref[...], chosen.astype(bf16),
                                    preferred_element_type=f32)
    count_ref[...] += jnp.sum(chosen.astype(f32), axis=0, keepdims=True)
    r1 = jnp.sum(jnp.where(lane == i1, rank, 0.0), axis=-1, keepdims=True)
    r2 = jnp.sum(jnp.where(lane == i2, rank, 0.0), axis=-1, keepdims=True)
    fields = (i1, i2, r1, r2, p1, p2)
    route = jnp.zeros((tm, LANES), f32)
    for k, val in enumerate(fields):
        route = jnp.where(lane == k, val, route)
    route_ref[...] = route


def _mix_router(attn, ylru, x2d, w_out_bf16, g_attn, ln_g, ln_b, router_w, tm=ROUTER_ROWS):
    n = x2d.shape[0]
    tile = lambda width: pl.BlockSpec((tm, width), lambda i: (i, 0))
    const = lambda shape: pl.BlockSpec(shape, lambda i: (0,) * len(shape))
    row = lambda v: v.reshape(1, -1).astype(f32)
    rw = jnp.zeros((D_MODEL, LANES), f32).at[:, :N_EXPERTS].set(router_w.astype(f32))
    rw_hi = rw.astype(bf16)
    rw_lo = (rw - rw_hi.astype(f32)).astype(bf16)
    strictly_lower = jnp.asarray(np.tri(tm, k=-1), bf16)
    return pl.pallas_call(
        _mix_router_kernel,
        grid=(n // tm,),
        in_specs=[tile(D_ATTN), tile(D_LRU), tile(D_MODEL), const((D_MODEL, D_MODEL)),
                  const((1, D_ATTN)), const((1, D_MODEL)), const((1, D_MODEL)),
                  const((2, D_MODEL, LANES)), const((tm, tm))],
        out_specs=[tile(D_MODEL), tile(LANES), const((1, LANES))],
        out_shape=[jax.ShapeDtypeStruct((n, D_MODEL), f32), jax.ShapeDtypeStruct((n, LANES), f32),
                   jax.ShapeDtypeStruct((1, LANES), f32)],
        compiler_params=_params("arbitrary"),
        name="mix_out_router",
    )(attn, ylru, x2d, w_out_bf16, row(g_attn), row(ln_g), row(ln_b), jnp.stack([rw_hi, rw_lo]),
      strictly_lower)


def _expert_ffn_kernel(tile_expert_ref, n_used_ref, x_ref, wg_ref, wu_ref, wd_ref, out_ref,
                       acc_ref, xb_ref, *, n_chunks):
    del tile_expert_ref
    j = pl.program_id(1)
    used = pl.program_id(0) < n_used_ref[0]
    last = n_chunks - 1
    assert n_chunks >= 2

    def chunk_out(xb):
        return _swiglu(xb, wg_ref[...], wu_ref[...], wd_ref[...])

    @pl.when(used & (j == 0))
    def _():
        xb = _load_row_tiles(x_ref, xb_ref.shape[0]).astype(bf16)
        xb_ref[...] = xb
        acc_ref[...] = chunk_out(xb)

    if n_chunks > 2:
        @pl.when(used & (j > 0) & (j < last))
        def _():
            acc_ref[...] += chunk_out(xb_ref[...])

    @pl.when(used & (j == last))
    def _():
        _store_row_tiles(out_ref, acc_ref[...] + chunk_out(xb_ref[...]))

    @pl.when(jnp.logical_not(used) & (j == last))
    def _():
        _store_row_tiles(out_ref, jnp.zeros(acc_ref.shape, f32))


def _expert_ffn(x_rows, tile_expert, n_used, w_gate, w_up, w_down, *, tm, tf):
    rows = x_rows.shape[0] // ROW_TILE
    n_chunks = w_gate.shape[2] // tf
    io_block = (tm * ROW_TILE, LANES)

    def chunk(i, j, nu):
        return jnp.where(i < nu[0], j, n_chunks - 1)

    return pl.pallas_call(
        functools.partial(_expert_ffn_kernel, n_chunks=n_chunks),
        grid_spec=pltpu.PrefetchScalarGridSpec(
            num_scalar_prefetch=2,
            grid=(rows // tm, n_chunks),
            in_specs=[pl.BlockSpec(io_block, lambda i, j, te, nu: (i, 0)),
                      pl.BlockSpec((None, D_MODEL, tf),
                                   lambda i, j, te, nu: (te[i], 0, chunk(i, j, nu))),
                      pl.BlockSpec((None, D_MODEL, tf),
                                   lambda i, j, te, nu: (te[i], 0, chunk(i, j, nu))),
                      pl.BlockSpec((None, tf, D_MODEL),
                                   lambda i, j, te, nu: (te[i], chunk(i, j, nu), 0))],
            out_specs=pl.BlockSpec(io_block, lambda i, j, te, nu: (i, 0)),
            scratch_shapes=[pltpu.VMEM((tm, D_MODEL), f32), pltpu.VMEM((tm, D_MODEL), bf16)]),
        out_shape=jax.ShapeDtypeStruct(x_rows.shape, f32),
        compiler_params=_params("parallel", "arbitrary"),
        name="ffn_experts",
    )(tile_expert, n_used, x_rows, w_gate, w_up, w_down)


def _dispatch_kernel(fill_ref, pos_ref, x_ref, xs_hbm, zero_buf, rows_buf, sem, *, tt, tm):
    step = pl.program_id(0)

    n_fill = fill_ref.shape[0] // 2

    @pl.when(step == 0)
    def _():
        zero_buf[...] = jnp.zeros_like(zero_buf)

        def fill_copy(f):
            start = pl.multiple_of(fill_ref[f] * ROW_TILE, ROW_TILE)
            return pltpu.make_async_copy(zero_buf, xs_hbm.at[pl.ds(start, tm * ROW_TILE)], sem)

        for f in range(n_fill):
            @pl.when(fill_ref[n_fill + f] > 0)
            def _():
                fill_copy(f).start()
        for f in range(n_fill):
            @pl.when(fill_ref[n_fill + f] > 0)
            def _():
                fill_copy(f).wait()

    _store_row_tiles(rows_buf, x_ref[...])

    def issue(t, carry):
        src = pl.multiple_of(t * ROW_TILE, ROW_TILE)
        for k in range(TOP_K):
            dst = pl.multiple_of(pos_ref[0, TOP_K * t + k] * ROW_TILE, ROW_TILE)
            pltpu.make_async_copy(rows_buf.at[pl.ds(src, ROW_TILE)],
                                  xs_hbm.at[pl.ds(dst, ROW_TILE)], sem).start(priority=k)
        return carry

    lax.fori_loop(0, tt, issue, 0, unroll=DMA_UNROLL)
    for _ in range(TOP_K):
        pltpu.make_async_copy(rows_buf, xs_hbm.at[pl.ds(0, tt * ROW_TILE)], sem).wait()


def _dispatch(x1, pos_blocks, fill, rows_sorted, *, tt, tm):
    n = x1.shape[0]
    return pl.pallas_call(
        functools.partial(_dispatch_kernel, tt=tt, tm=tm),
        grid_spec=pltpu.PrefetchScalarGridSpec(
            num_scalar_prefetch=1,
            grid=(n // tt,),
            in_specs=[pl.BlockSpec((None, 1, TOP_K * tt), lambda i, fill: (i, 0, 0),
                                   memory_space=pltpu.SMEM),
                      pl.BlockSpec((tt, D_MODEL), lambda i, fill: (i, 0))],
            out_specs=pl.BlockSpec(memory_space=pl.ANY),
            scratch_shapes=[pltpu.VMEM((tm * ROW_TILE, LANES), f32),
                            pltpu.VMEM((tt * ROW_TILE, LANES), f32),
                            pltpu.SemaphoreType.DMA(())]),
        out_shape=jax.ShapeDtypeStruct((rows_sorted * ROW_TILE, LANES), f32),
        compiler_params=_params("arbitrary"),
        name="moe_dispatch",
    )(fill, pos_blocks, x1)


def _combine_kernel(pos_ref, pos_next_ref, route_ref, x1_ref, ys_hbm, lng, lnb, out_ref, buf, sems,
                    *, tt, n_steps):
    step = pl.program_id(0)
    slot = step % 2

    def issue_tile(tile_pos_ref, into):
        def issue(t, carry):
            dst = pl.multiple_of(t * ROW_TILE, ROW_TILE)
            for k in range(TOP_K):
                src = pl.multiple_of(tile_pos_ref[0, TOP_K * t + k] * ROW_TILE, ROW_TILE)
                pltpu.make_async_copy(ys_hbm.at[pl.ds(src, ROW_TILE)],
                                      buf.at[into, k, pl.ds(dst, ROW_TILE)],
                                      sems.at[into]).start(priority=k)
            return carry

        lax.fori_loop(0, tt, issue, 0, unroll=DMA_UNROLL)

    @pl.when(step == 0)
    def _():
        issue_tile(pos_ref, 0)

    @pl.when(step + 1 < n_steps)
    def _():
        issue_tile(pos_next_ref, 1 - slot)

    for k in range(TOP_K):
        pltpu.make_async_copy(ys_hbm.at[pl.ds(0, tt * ROW_TILE)], buf.at[slot, k],
                              sems.at[slot]).wait()

    lane = lax.broadcasted_iota(jnp.int32, (1, LANES), 1)
    route = route_ref[...]
    p1 = jnp.sum(jnp.where(lane == 4, route, 0.0), axis=-1, keepdims=True)
    p2 = jnp.sum(jnp.where(lane == 5, route, 0.0), axis=-1, keepdims=True)
    y = p1 * _load_row_tiles(buf.at[slot, 0], tt) + p2 * _load_row_tiles(buf.at[slot, 1], tt)
    out_ref[...] = _layer_norm(DEEPNORM_ALPHA * x1_ref[...] + y, lng[...], lnb[...])


def _combine(pos_blocks, route, x1, y_sorted, ln_g, ln_b, *, tt):
    n = x1.shape[0]
    n_steps = n // tt
    pos_spec = lambda index: pl.BlockSpec((None, 1, TOP_K * tt), index, memory_space=pltpu.SMEM)
    return pl.pallas_call(
        functools.partial(_combine_kernel, tt=tt, n_steps=n_steps),
        grid=(n_steps,),
        in_specs=[pos_spec(lambda i: (i, 0, 0)),
                  pos_spec(lambda i: (jnp.minimum(i + 1, n_steps - 1), 0, 0)),
                  pl.BlockSpec((tt, LANES), lambda i: (i, 0)),
                  pl.BlockSpec((tt, D_MODEL), lambda i: (i, 0)),
                  pl.BlockSpec(memory_space=pl.ANY),
                  pl.BlockSpec((1, D_MODEL), lambda i: (0, 0)),
                  pl.BlockSpec((1, D_MODEL), lambda i: (0, 0))],
        out_specs=pl.BlockSpec((tt, D_MODEL), lambda i: (i, 0)),
        out_shape=jax.ShapeDtypeStruct((n, D_MODEL), f32),
        scratch_shapes=[pltpu.VMEM((2, TOP_K, tt * ROW_TILE, LANES), f32),
                        pltpu.SemaphoreType.DMA((2,))],
        compiler_params=_params("arbitrary"),
        name="moe_combine",
    )(pos_blocks, pos_blocks, route, x1, y_sorted, ln_g.reshape(1, D_MODEL).astype(f32),
      ln_b.reshape(1, D_MODEL).astype(f32))


def _moe(x1, route, counts, w_gate, w_up, w_down, ln_g, ln_b, *, tm, tf):
    n = x1.shape[0]
    i32 = jnp.int32
    counts = counts[0, :N_EXPERTS].astype(i32)
    padded = (counts + tm - 1) // tm * tm
    ends = jnp.cumsum(padded)
    offsets = ends - padded
    experts = route[:, 0:TOP_K].astype(i32)
    ranks = route[:, TOP_K:2 * TOP_K].astype(i32)
    pos = offsets[experts] + ranks
    pos_blocks = lambda tt: pos.reshape(n // tt, 1, TOP_K * tt)
    n_tiles = TOP_K * n // tm + N_EXPERTS
    n_used = (ends[-1] // tm).astype(i32).reshape(1)
    tile_ids = jnp.arange(n_tiles, dtype=i32)
    tile_expert = jnp.sum((tile_ids[:, None] >= (ends // tm)[None, :]).astype(i32), axis=1)
    last_expert = jnp.max(jnp.where(counts > 0, jnp.arange(N_EXPERTS, dtype=i32), 0))
    tile_expert = jnp.minimum(tile_expert, last_expert).astype(i32)
    tail_tiles = n_used[0] + jnp.arange(N_EXPERTS, dtype=i32)
    fill = jnp.concatenate([ends - tm, tail_tiles * tm,
                            (counts > 0).astype(i32), (tail_tiles < n_tiles).astype(i32)]).astype(i32)

    x_sorted = _dispatch(x1, pos_blocks(DISPATCH_ROWS), fill, n_tiles * tm, tt=DISPATCH_ROWS,
                         tm=tm)
    y_sorted = _expert_ffn(x_sorted, tile_expert, n_used, w_gate, w_up, w_down, tm=tm, tf=tf)
    return _combine(pos_blocks(COMBINE_ROWS), route, x1, y_sorted, ln_g, ln_b, tt=COMBINE_ROWS)


def kernel(x, w_in, conv_w, conv_b, w_a, b_a, w_x, b_x, lru_lambda, rel_bias, g_attn, g_lru, w_out, ln1_g, ln1_b, ln2_g, ln2_b, ffn_w_gate, ffn_w_up, ffn_w_down, router_w, moe_w_gate, moe_w_up, moe_w_down):
    batch, seq, _ = x.shape
    n = batch * seq
    h = x.reshape(n, D_MODEL).astype(f32)
    band_bias = jnp.stack([_band_bias(rel_bias, d) for _, d in DILATED_PATTERNS])
    for layer in range(DEPTH):
        qkv, ylru = _proj_lru(h, w_in[layer].astype(bf16), conv_w[layer], conv_b[layer],
                              w_a[layer], b_a[layer], w_x[layer], b_x[layer], lru_lambda[layer],
                              g_lru[layer], batch, seq)
        attn = _attention(qkv, band_bias, batch, seq)
        j = layer // 2
        w_o = w_out[layer].astype(bf16)
        if layer % 2 == 0:
            h = _mix_ffn_dense(attn, ylru, h, w_o, g_attn[layer], (ln1_g[layer], ln1_b[layer]),
                               ffn_w_gate[j].astype(bf16), ffn_w_up[j].astype(bf16),
                               ffn_w_down[j].astype(bf16), (ln2_g[layer], ln2_b[layer]))
        else:
            x1, route, counts = _mix_router(attn, ylru, h, w_o, g_attn[layer], ln1_g[layer],
                                            ln1_b[layer], router_w[j])
            h = _moe(x1, route, counts, moe_w_gate[j].astype(bf16), moe_w_up[j].astype(bf16),
                     moe_w_down[j].astype(bf16), ln2_g[layer], ln2_b[layer],
                     tm=EXPERT_ROWS, tf=EXPERT_FF_CHUNK)
    return h.reshape(batch, seq, D_MODEL).astype(x.dtype)
```

```python
import functools

import numpy as np
import jax
import jax.numpy as jnp
from jax import lax
from jax.experimental import pallas as pl
from jax.experimental.pallas import tpu as pltpu

D_MODEL = 1024
N_HEADS = 8
HEAD_DIM = 64
D_ATTN = N_HEADS * HEAD_DIM
D_LRU = 512
N_LRU_BLOCKS = 8
LRU_BLOCK = D_LRU // N_LRU_BLOCKS
CONV_WIDTH = 4
LRU_C = 8.0
DILATED_PATTERNS = ((128, 1), (512, 4), (2048, 16))
ATTN_BLOCK = 128
STAGE_DILATION = 4
N_BUCKETS = 32
MAX_DISTANCE = 2048
D_IN = 3 * D_ATTN + 2 * D_LRU
N_EXPERTS = 8
TOP_K = 2
DEPTH = 2
DEEPNORM_ALPHA = (2.0 * DEPTH) ** 0.25
LN_EPS = 1e-5
RMS_EPS = 1e-6
NEG_INF = -1e30
LOG2_E = float(np.log2(np.e))

LANES = 128
SUBLANES = 8
VMEM_LIMIT_BYTES = 56 * 1024 * 1024
MXU_DEPTH = 256

LRU_ROWS = 1024
ROUTER_ROWS = 1024
DENSE_FFN_ROWS = 1024
EXPERT_ROWS = 512
EXPERT_FF_CHUNK = 7 * MXU_DEPTH
DISPATCH_ROWS = 2048
COMBINE_ROWS = 256
DMA_UNROLL = 8
ATTN_UNROLL = 4

f32 = jnp.float32
bf16 = jnp.bfloat16


def _params(*semantics):
    return pltpu.CompilerParams(dimension_semantics=semantics,
                                vmem_limit_bytes=VMEM_LIMIT_BYTES)


def _t5_bucket(dist):
    max_exact = N_BUCKETS // 2
    d = np.maximum(dist, 1).astype(np.float32)
    large = max_exact + (np.log(d / max_exact) / np.log(MAX_DISTANCE / max_exact)
                         * (N_BUCKETS - max_exact)).astype(np.int32)
    large = np.minimum(large, N_BUCKETS - 1)
    return np.where(dist < max_exact, dist, large).astype(np.int32)


def _band_bias(rel_bias, dilation):
    nk = ATTN_BLOCK
    qi = np.arange(nk)[:, None]
    kj = np.arange(2 * nk)[None, :]
    delta = qi + nk - kj
    band = (delta >= 0) & (delta <= nk)
    bucket = _t5_bucket(np.clip(delta, 0, nk) * dilation)
    onehot = np.eye(N_BUCKETS, dtype=np.float32)[bucket.reshape(-1)]
    bias = jnp.dot(jnp.asarray(onehot), rel_bias.astype(f32), precision=lax.Precision.HIGHEST)
    bias = jnp.transpose(bias.reshape(nk, 2 * nk, N_HEADS), (2, 0, 1))
    valid = np.stack([band, band & (kj >= nk)])[:, None]
    bias = jnp.where(jnp.asarray(valid), bias[None], NEG_INF)
    return (bias * LOG2_E).reshape(2, N_HEADS // 2, 2 * nk, 2 * nk)


def _attn_kernel(q_ref, k_ref, v_ref, bias_ref, o_ref, q4, k4, v4, qs, ks, vs, s_buf, m_buf,
                 m_s, l_s, acc_s, *, seq):
    nk = ATTN_BLOCK
    n_blocks = seq // nk
    chunk = 2 * nk
    lane = lax.broadcasted_iota(jnp.int32, (1, LANES), 1)
    head0 = lane < HEAD_DIM

    ks[0:nk, :] = jnp.zeros((nk, LANES), bf16)
    vs[0:nk, 0:LANES] = jnp.zeros((nk, LANES), bf16)
    vs[:, LANES:2 * LANES] = jnp.ones((seq + nk, LANES), bf16)

    def stage(t, carry):
        chunks_per_residue = seq // STAGE_DILATION // chunk
        src = pl.ds(t // chunks_per_residue + STAGE_DILATION * chunk * (t % chunks_per_residue),
                    chunk, stride=STAGE_DILATION)
        dst = pl.ds(pl.multiple_of(t * chunk, chunk), chunk)
        q4[dst, :] = q_ref[src, :]
        k4[dst, :] = k_ref[src, :]
        v4[dst, :] = v_ref[src, :]
        return carry

    lax.fori_loop(0, seq // chunk, stage, 0)

    order = sorted(range(len(DILATED_PATTERNS)), key=lambda p: -DILATED_PATTERNS[p][1])
    assert DILATED_PATTERNS[order[-1]][1] == 1
    for p in order:
        d = DILATED_PATTERNS[p][1]
        is_first, is_last = p == order[0], p == order[-1]
        length = seq // d
        nb = length // nk
        chunks_per_residue = length // chunk

        def gather(t, carry):
            r = t // chunks_per_residue
            c = t % chunks_per_residue
            dst = pl.multiple_of(t * chunk, chunk)
            if d % STAGE_DILATION == 0:
                sub = d // STAGE_DILATION
                start = ((r % STAGE_DILATION) * (seq // STAGE_DILATION) + r // STAGE_DILATION
                         + sub * chunk * c)
                src = (pl.ds(start, chunk, stride=sub) if sub > 1
                       else pl.ds(pl.multiple_of(start, chunk), chunk))
                q, k, v = q4[src, :], k4[src, :], v4[src, :]
            else:
                src = pl.ds(r + d * chunk * c, chunk, stride=d) if d > 1 else pl.ds(dst, chunk)
                q, k, v = q_ref[src, :], k_ref[src, :], v_ref[src, :]
            q = q * (HEAD_DIM ** -0.5 * LOG2_E)
            q0 = jnp.where(head0, q, 0.0).astype(bf16)
            q1 = jnp.where(head0, 0.0, q).astype(bf16)
            for half in range(2):
                base = pl.multiple_of(2 * dst + half * chunk, chunk)
                qs[pl.ds(base, nk), :] = q0[half * nk:(half + 1) * nk]
                qs[pl.ds(base + nk, nk), :] = q1[half * nk:(half + 1) * nk]
            ks[pl.ds(nk + dst, chunk), :] = k.astype(bf16)
            vs[pl.ds(nk + dst, chunk), 0:LANES] = v.astype(bf16)
            return carry

        lax.fori_loop(0, seq // chunk, gather, 0)

        def scores(b, carry):
            first = jnp.asarray(b % nb == 0, jnp.int32)
            rows = pl.ds(pl.multiple_of(b * chunk, chunk), chunk)
            k = ks[pl.ds(pl.multiple_of(b * nk, nk), chunk), :]
            s_buf[rows, :] = lax.dot_general(qs[rows, :], k, (((1,), (1,)), ((), ())),
                                             preferred_element_type=f32) + bias_ref[p, first]
            return carry

        def rowmax(b, carry):
            rows = pl.ds(pl.multiple_of(b * chunk, chunk), chunk)
            m_buf[rows, :] = jnp.broadcast_to(jnp.max(s_buf[rows, :], axis=-1, keepdims=True),
                                              (chunk, LANES))
            return carry

        def block(b, carry):
            r = b // nb
            i = b % nb
            rows = pl.ds(pl.multiple_of(b * chunk, chunk), chunk)
            v = vs[pl.ds(pl.multiple_of(b * nk, nk), chunk), :]
            m = m_buf[rows, :]
            e = jnp.exp2(s_buf[rows, :] - jnp.concatenate([m, m], axis=1))
            pv = jnp.dot(e.astype(bf16), v, preferred_element_type=f32)
            m_blk = jnp.where(head0, m[0:nk], m[nk:chunk])
            l_blk = jnp.where(head0, pv[0:nk, LANES:], pv[nk:chunk, LANES:])
            pv_blk = jnp.where(head0, pv[0:nk, 0:LANES], pv[nk:chunk, 0:LANES])
            if d == 1:
                tok = pl.ds(pl.multiple_of(b * nk, nk), nk)
            else:
                tok = pl.ds(r + d * nk * i, nk, stride=d)
            if is_first:
                m_s[tok, :] = m_blk
                l_s[tok, :] = l_blk
                acc_s[tok, :] = pv_blk
                return carry
            m_old = m_s[tok, :]
            m_new = jnp.maximum(m_old, m_blk)
            w_old = jnp.exp2(m_old - m_new)
            w_blk = jnp.exp2(m_blk - m_new)
            l_new = w_old * l_s[tok, :] + w_blk * l_blk
            acc_new = w_old * acc_s[tok, :] + w_blk * pv_blk
            if is_last:
                o_ref[tok, :] = (acc_new / l_new).astype(o_ref.dtype)
            else:
                l_s[tok, :] = l_new
                acc_s[tok, :] = acc_new
                m_s[tok, :] = m_new
            return carry

        n_groups = n_blocks // ATTN_UNROLL
        assert n_groups >= 3

        def group(fn, g):
            for j in range(ATTN_UNROLL):
                fn(g * ATTN_UNROLL + j, 0)

        def pipelined(g, carry):
            group(block, g)
            group(rowmax, g + 1)
            group(scores, g + 2)
            return carry

        group(scores, 0)
        group(scores, 1)
        group(rowmax, 0)
        lax.fori_loop(0, n_groups - 2, pipelined, 0)
        group(block, n_groups - 2)
        group(rowmax, n_groups - 1)
        group(block, n_groups - 1)


def _attention(qkv, band_bias, batch, seq):
    view = qkv.reshape(batch, seq, 3 * D_ATTN)
    pairs = D_ATTN // LANES

    def spec(offset):
        return pl.BlockSpec((None, seq, LANES), lambda b, hp: (b, 0, offset * pairs + hp))

    n_pat = len(DILATED_PATTERNS)
    o = pl.pallas_call(
        functools.partial(_attn_kernel, seq=seq),
        grid=(batch, pairs),
        in_specs=[spec(0), spec(1), spec(2),
                  pl.BlockSpec((n_pat, 2, None, 2 * ATTN_BLOCK, 2 * ATTN_BLOCK),
                               lambda b, hp: (0, 0, hp, 0, 0))],
        out_specs=pl.BlockSpec((None, seq, LANES), lambda b, hp: (b, 0, hp)),
        out_shape=jax.ShapeDtypeStruct((batch, seq, D_ATTN), bf16),
        scratch_shapes=[pltpu.VMEM((seq, LANES), f32), pltpu.VMEM((seq, LANES), f32),
                        pltpu.VMEM((seq, LANES), f32),
                        pltpu.VMEM((2 * seq, LANES), bf16),
                        pltpu.VMEM((seq + ATTN_BLOCK, LANES), bf16),
                        pltpu.VMEM((seq + ATTN_BLOCK, 2 * LANES), bf16),
                        pltpu.VMEM((2 * seq, 2 * ATTN_BLOCK), f32),
                        pltpu.VMEM((2 * seq, LANES), f32),
                        pltpu.VMEM((seq, LANES), f32), pltpu.VMEM((seq, LANES), f32),
                        pltpu.VMEM((seq, LANES), f32)],
        compiler_params=_params("parallel", "parallel"),
        name="attention",
    )(view, view, view, band_bias)
    return o.reshape(batch * seq, D_ATTN)


def _gelu_tanh(x):
    return 0.5 * x * (1.0 + jnp.tanh(np.sqrt(2.0 / np.pi) * (x + 0.044715 * x * x * x)))


def _proj_lru_kernel(x_ref, w_ref, convw_ref, convb_ref, wg_ref, ba_ref, bx_ref, lam_ref, g_ref,
                     qkv_ref, y_ref, ubuf, hcarry, *, ts):
    pad = SUBLANES
    t = pl.program_id(1)
    n_qkv = qkv_ref.shape[1]
    xb = x_ref[...].astype(bf16)

    @pl.when(t == 0)
    def _():
        ubuf[0:pad, :] = jnp.zeros((pad, D_LRU), f32)
        hcarry[...] = jnp.zeros_like(hcarry)

    @pl.when(t > 0)
    def _():
        ubuf[0:pad, :] = ubuf[ts:ts + pad, :]

    ubuf[pad:pad + ts, :] = jnp.dot(xb, w_ref[:, n_qkv:n_qkv + D_LRU],
                                    preferred_element_type=f32)
    gate = jnp.dot(xb, w_ref[:, n_qkv + D_LRU:], preferred_element_type=f32)

    qkv_step = n_qkv // 3
    assert qkv_step == D_LRU

    def qkv_chunk(c):
        cols = slice(c * qkv_step, (c + 1) * qkv_step)
        p = jnp.dot(xb, w_ref[:, cols], preferred_element_type=f32)
        qkv_ref[:, cols] = p
        bits = lax.bitcast_convert_type(p, jnp.uint32).reshape(ts // SUBLANES, SUBLANES, qkv_step)
        folded = functools.reduce(jnp.bitwise_or, [bits[g] for g in range(ts // SUBLANES)])
        return ((folded >> 16) >> 16).astype(f32)

    u = convb_ref[...] + convw_ref[CONV_WIDTH - 1:CONV_WIDTH, :] * ubuf[pad:pad + ts, :]
    for w in range(CONV_WIDTH - 1):
        back = CONV_WIDTH - 1 - w
        u = u + convw_ref[w:w + 1, :] * ubuf[pad - back:pad - back + ts, :]

    gates = jnp.dot(u.astype(bf16), wg_ref[...], preferred_element_type=f32)
    r = jax.nn.sigmoid(gates[:, 0:D_LRU] + ba_ref[...])
    i = jax.nn.sigmoid(gates[:, D_LRU:2 * D_LRU] + bx_ref[...])
    neg_lam = -lam_ref[...]
    softplus = jnp.maximum(neg_lam, 0.0) + jnp.log1p(jnp.exp(-jnp.abs(neg_lam)))
    log_a = (-LRU_C) * r * softplus
    a = jnp.exp(log_a)
    b = jnp.sqrt(-jnp.tanh(log_a) * (1.0 + a * a)) * (i * u)
    zero0 = qkv_chunk(0)

    groups = ts // SUBLANES
    a = a.reshape(groups, SUBLANES, D_LRU)
    b = b.reshape(groups, SUBLANES, D_LRU) + zero0[None]
    sub = lax.broadcasted_iota(jnp.int32, (1, SUBLANES, 1), 1)
    shift = 1
    while shift < SUBLANES:
        live = sub >= shift
        a_prev = jnp.where(live, pltpu.roll(a, shift, 1), 1.0)
        b_prev = jnp.where(live, pltpu.roll(b, shift, 1), 0.0)
        b = a * b_prev + b
        a = a * a_prev
        shift *= 2
    h_prev = hcarry[...] + qkv_chunk(1)[0:1, :]
    h_groups = []
    for g in range(groups):
        h_g = a[g] * h_prev + b[g]
        h_prev = h_g[SUBLANES - 1:SUBLANES, :]
        h_groups.append(h_g)
    hcarry[...] = h_prev
    h = jnp.concatenate(h_groups, axis=0)
    zero2 = qkv_chunk(2)[0:1, 0:1]

    y = _gelu_tanh(gate) * h
    y = y * lax.rsqrt(jnp.mean(y * y, axis=-1, keepdims=True) + (RMS_EPS + zero2)) * g_ref[...]
    y_ref[...] = y.astype(y_ref.dtype)


def _block_diag(w):
    g, i, j = w.shape
    eye = jnp.eye(g, dtype=w.dtype)
    return jnp.einsum('gij,gh->gihj', w, eye).reshape(g * i, g * j)


def _proj_lru(x2d, w_in_bf16, conv_w, conv_b, w_a, b_a, w_x, b_x, lam, g_lru, batch, seq,
              ts=LRU_ROWS):
    wg = jnp.concatenate([_block_diag(w_a), _block_diag(w_x)], axis=1).astype(bf16)
    row = lambda v: v.reshape(1, D_LRU).astype(f32)
    const = lambda shape: pl.BlockSpec(shape, lambda b, t: (0, 0))
    tile = lambda width: pl.BlockSpec((None, ts, width), lambda b, t: (b, t, 0))
    qkv, y = pl.pallas_call(
        functools.partial(_proj_lru_kernel, ts=ts),
        grid=(batch, seq // ts),
        in_specs=[tile(D_MODEL), const((D_MODEL, D_IN)),
                  const((CONV_WIDTH, D_LRU)), const((1, D_LRU)),
                  const((D_LRU, 2 * D_LRU)), const((1, D_LRU)), const((1, D_LRU)),
                  const((1, D_LRU)), const((1, D_LRU))],
        out_specs=[tile(3 * D_ATTN), tile(D_LRU)],
        out_shape=[jax.ShapeDtypeStruct((batch, seq, 3 * D_ATTN), f32),
                   jax.ShapeDtypeStruct((batch, seq, D_LRU), bf16)],
        scratch_shapes=[pltpu.VMEM((ts + 2 * SUBLANES, D_LRU), f32), pltpu.VMEM((1, D_LRU), f32)],
        compiler_params=_params("parallel", "arbitrary"),
        name="proj_rglru",
    )(x2d.reshape(batch, seq, D_MODEL), w_in_bf16,
      conv_w.reshape(CONV_WIDTH, D_LRU).astype(f32), row(conv_b), wg, row(b_a), row(b_x),
      row(lam), row(g_lru))
    return qkv.reshape(batch * seq, 3 * D_ATTN), y.reshape(batch * seq, D_LRU)


ROW_TILE = D_MODEL // LANES


def _store_row_tiles(ref, rows):
    t = rows.shape[0]
    for s in range(ROW_TILE):
        ref[pl.ds(s, t, stride=ROW_TILE), :] = rows[:, s * LANES:(s + 1) * LANES]


def _load_row_tiles(ref, t):
    return jnp.concatenate([ref[pl.ds(s, t, stride=ROW_TILE), :] for s in range(ROW_TILE)],
                           axis=-1)


def _layer_norm(z, g, b):
    mu = jnp.mean(z, axis=-1, keepdims=True)
    zc = z - mu
    var = jnp.mean(zc * zc, axis=-1, keepdims=True)
    return zc * lax.rsqrt(var + LN_EPS) * g + b


def _mixer_residual(attn_ref, ylru_ref, x_ref, w_ref, gattn, lng, lnb):
    attn = attn_ref[...].astype(f32)
    attn = attn * lax.rsqrt(jnp.mean(attn * attn, axis=-1, keepdims=True) + RMS_EPS) * gattn[...]
    y = jnp.dot(attn.astype(bf16), w_ref[0:D_ATTN, :], preferred_element_type=f32)
    y = y + jnp.dot(ylru_ref[...], w_ref[D_ATTN:, :], preferred_element_type=f32)
    return _layer_norm(DEEPNORM_ALPHA * x_ref[...] + y, lng[...], lnb[...])


def _swiglu(xb, wg, wu, wd):
    g = jnp.dot(xb, wg, preferred_element_type=f32)
    u = jnp.dot(xb, wu, preferred_element_type=f32)
    h = (g * jax.nn.sigmoid(g)) * u
    return jnp.dot(h.astype(bf16), wd, preferred_element_type=f32)


def _mix_ffn_kernel(attn_ref, ylru_ref, x_ref, wo_ref, gattn, ln1g, ln1b, wg_ref, wu_ref, wd_ref,
                    ln2g, ln2b, out_ref):
    x1 = _mixer_residual(attn_ref, ylru_ref, x_ref, wo_ref, gattn, ln1g, ln1b)
    f = _swiglu(x1.astype(bf16), wg_ref[...], wu_ref[...], wd_ref[...])
    out_ref[...] = _layer_norm(DEEPNORM_ALPHA * x1 + f, ln2g[...], ln2b[...])


def _mix_ffn_dense(attn, ylru, x2d, w_out_bf16, g_attn, ln1, w_gate, w_up, w_down, ln2,
                   tm=DENSE_FFN_ROWS):
    n = x2d.shape[0]
    d_ff = w_gate.shape[1]
    tile = lambda width: pl.BlockSpec((tm, width), lambda i: (i, 0))
    const = lambda shape: pl.BlockSpec(shape, lambda i: (0, 0), pipeline_mode=pl.Buffered(1))
    row = lambda v: v.reshape(1, -1).astype(f32)
    return pl.pallas_call(
        _mix_ffn_kernel,
        grid=(n // tm,),
        in_specs=[tile(D_ATTN), tile(D_LRU), tile(D_MODEL), const((D_MODEL, D_MODEL)),
                  const((1, D_ATTN)), const((1, D_MODEL)), const((1, D_MODEL)),
                  const((D_MODEL, d_ff)), const((D_MODEL, d_ff)), const((d_ff, D_MODEL)),
                  const((1, D_MODEL)), const((1, D_MODEL))],
        out_specs=tile(D_MODEL),
        out_shape=jax.ShapeDtypeStruct((n, D_MODEL), f32),
        compiler_params=_params("parallel"),
        name="mix_ffn_dense",
    )(attn, ylru, x2d, w_out_bf16, row(g_attn), row(ln1[0]), row(ln1[1]), w_gate, w_up, w_down,
      row(ln2[0]), row(ln2[1]))


def _mix_router_kernel(attn_ref, ylru, x_ref, w_ref, gattn, lng, lnb, rw_ref, tri_ref,
                       x1_ref, route_ref, count_ref):
    x1 = _mixer_residual(attn_ref, ylru, x_ref, w_ref, gattn, lng, lnb)
    x1_ref[...] = x1
    lane = lax.broadcasted_iota(jnp.int32, (1, LANES), 1).astype(f32)
    x_hi = x1.astype(bf16)
    x_lo = (x1 - x_hi.astype(f32)).astype(bf16)
    logits = (jnp.dot(x_hi, rw_ref[0], preferred_element_type=f32)
              + jnp.dot(x_lo, rw_ref[0], preferred_element_type=f32)
              + jnp.dot(x_hi, rw_ref[1], preferred_element_type=f32))
    logits = jnp.where(lane < N_EXPERTS, logits, -jnp.inf)
    v1 = jnp.max(logits, axis=-1, keepdims=True)
    i1 = jnp.min(jnp.where(logits == v1, lane, float(LANES)), axis=-1, keepdims=True)
    rest_logits = jnp.where(lane == i1, -jnp.inf, logits)
    v2 = jnp.max(rest_logits, axis=-1, keepdims=True)
    i2 = jnp.min(jnp.where(rest_logits == v2, lane, float(LANES)), axis=-1, keepdims=True)
    e2 = jnp.exp(v2 - v1)
    p1 = 1.0 / (1.0 + e2)
    p2 = e2 / (1.0 + e2)

    @pl.when(pl.program_id(0) == 0)
    def _():
        count_ref[...] = jnp.zeros_like(count_ref)

    tm = x1.shape[0]
    chosen = jnp.logical_or(lane == i1, lane == i2)
    rank = count_ref[...] + jnp.dot(tri_ref[...], chosen.astype(bf16),
                                    preferred_element_type=f32)
    count_ref[...] += jnp.sum(chosen.astype(f32), axis=0, keepdims=True)
    r1 = jnp.sum(jnp.where(lane == i1, rank, 0.0), axis=-1, keepdims=True)
    r2 = jnp.sum(jnp.where(lane == i2, rank, 0.0), axis=-1, keepdims=True)
    fields = (i1, i2, r1, r2, p1, p2)
    route = jnp.zeros((tm, LANES), f32)
    for k, val in enumerate(fields):
        route = jnp.where(lane == k, val, route)
    route_ref[...] = route


def _mix_router(attn, ylru, x2d, w_out_bf16, g_attn, ln_g, ln_b, router_w, tm=ROUTER_ROWS):
    n = x2d.shape[0]
    tile = lambda width: pl.BlockSpec((tm, width), lambda i: (i, 0))
    const = lambda shape: pl.BlockSpec(shape, lambda i: (0,) * len(shape))
    row = lambda v: v.reshape(1, -1).astype(f32)
    rw = jnp.zeros((D_MODEL, LANES), f32).at[:, :N_EXPERTS].set(router_w.astype(f32))
    rw_hi = rw.astype(bf16)
    rw_lo = (rw - rw_hi.astype(f32)).astype(bf16)
    strictly_lower = jnp.asarray(np.tri(tm, k=-1), bf16)
    return pl.pallas_call(
        _mix_router_kernel,
        grid=(n // tm,),
        in_specs=[tile(D_ATTN), tile(D_LRU), tile(D_MODEL), const((D_MODEL, D_MODEL)),
                  const((1, D_ATTN)), const((1, D_MODEL)), const((1, D_MODEL)),
                  const((2, D_MODEL, LANES)), const((tm, tm))],
        out_specs=[tile(D_MODEL), tile(LANES), const((1, LANES))],
        out_shape=[jax.ShapeDtypeStruct((n, D_MODEL), f32), jax.ShapeDtypeStruct((n, LANES), f32),
                   jax.ShapeDtypeStruct((1, LANES), f32)],
        compiler_params=_params("arbitrary"),
        name="mix_out_router",
    )(attn, ylru, x2d, w_out_bf16, row(g_attn), row(ln_g), row(ln_b), jnp.stack([rw_hi, rw_lo]),
      strictly_lower)


def _expert_ffn_kernel(tile_expert_ref, n_used_ref, x_ref, wg_ref, wu_ref, wd_ref, out_ref,
                       acc_ref, xb_ref, *, n_chunks):
    del tile_expert_ref
    j = pl.program_id(1)
    used = pl.program_id(0) < n_used_ref[0]
    last = n_chunks - 1
    assert n_chunks >= 2

    def chunk_out(xb):
        return _swiglu(xb, wg_ref[...], wu_ref[...], wd_ref[...])

    @pl.when(used & (j == 0))
    def _():
        xb = _load_row_tiles(x_ref, xb_ref.shape[0]).astype(bf16)
        xb_ref[...] = xb
        acc_ref[...] = chunk_out(xb)

    if n_chunks > 2:
        @pl.when(used & (j > 0) & (j < last))
        def _():
            acc_ref[...] += chunk_out(xb_ref[...])

    @pl.when(used & (j == last))
    def _():
        _store_row_tiles(out_ref, acc_ref[...] + chunk_out(xb_ref[...]))

    @pl.when(jnp.logical_not(used) & (j == last))
    def _():
        _store_row_tiles(out_ref, jnp.zeros(acc_ref.shape, f32))


def _expert_ffn(x_rows, tile_expert, n_used, w_gate, w_up, w_down, *, tm, tf):
    rows = x_rows.shape[0] // ROW_TILE
    n_chunks = w_gate.shape[2] // tf
    io_block = (tm * ROW_TILE, LANES)

    def chunk(i, j, nu):
        return jnp.where(i < nu[0], j, n_chunks - 1)

    return pl.pallas_call(
        functools.partial(_expert_ffn_kernel, n_chunks=n_chunks),
        grid_spec=pltpu.PrefetchScalarGridSpec(
            num_scalar_prefetch=2,
            grid=(rows // tm, n_chunks),
            in_specs=[pl.BlockSpec(io_block, lambda i, j, te, nu: (i, 0)),
                      pl.BlockSpec((None, D_MODEL, tf),
                                   lambda i, j, te, nu: (te[i], 0, chunk(i, j, nu))),
                      pl.BlockSpec((None, D_MODEL, tf),
                                   lambda i, j, te, nu: (te[i], 0, chunk(i, j, nu))),
                      pl.BlockSpec((None, tf, D_MODEL),
                                   lambda i, j, te, nu: (te[i], chunk(i, j, nu), 0))],
            out_specs=pl.BlockSpec(io_block, lambda i, j, te, nu: (i, 0)),
            scratch_shapes=[pltpu.VMEM((tm, D_MODEL), f32), pltpu.VMEM((tm, D_MODEL), bf16)]),
        out_shape=jax.ShapeDtypeStruct(x_rows.shape, f32),
        compiler_params=_params("parallel", "arbitrary"),
        name="ffn_experts",
    )(tile_expert, n_used, x_rows, w_gate, w_up, w_down)


def _dispatch_kernel(fill_ref, pos_ref, x_ref, xs_hbm, zero_buf, rows_buf, sem, *, tt, tm):
    step = pl.program_id(0)

    n_fill = fill_ref.shape[0] // 2

    @pl.when(step == 0)
    def _():
        zero_buf[...] = jnp.zeros_like(zero_buf)

        def fill_copy(f):
            start = pl.multiple_of(fill_ref[f] * ROW_TILE, ROW_TILE)
            return pltpu.make_async_copy(zero_buf, xs_hbm.at[pl.ds(start, tm * ROW_TILE)], sem)

        for f in range(n_fill):
            @pl.when(fill_ref[n_fill + f] > 0)
            def _():
                fill_copy(f).start()
        for f in range(n_fill):
            @pl.when(fill_ref[n_fill + f] > 0)
            def _():
                fill_copy(f).wait()

    _store_row_tiles(rows_buf, x_ref[...])

    def issue(t, carry):
        src = pl.multiple_of(t * ROW_TILE, ROW_TILE)
        for k in range(TOP_K):
            dst = pl.multiple_of(pos_ref[0, TOP_K * t + k] * ROW_TILE, ROW_TILE)
            pltpu.make_async_copy(rows_buf.at[pl.ds(src, ROW_TILE)],
                                  xs_hbm.at[pl.ds(dst, ROW_TILE)], sem).start(priority=k)
        return carry

    lax.fori_loop(0, tt, issue, 0, unroll=DMA_UNROLL)
    for _ in range(TOP_K):
        pltpu.make_async_copy(rows_buf, xs_hbm.at[pl.ds(0, tt * ROW_TILE)], sem).wait()


def _dispatch(x1, pos_blocks, fill, rows_sorted, *, tt, tm):
    n = x1.shape[0]
    return pl.pallas_call(
        functools.partial(_dispatch_kernel, tt=tt, tm=tm),
        grid_spec=pltpu.PrefetchScalarGridSpec(
            num_scalar_prefetch=1,
            grid=(n // tt,),
            in_specs=[pl.BlockSpec((None, 1, TOP_K * tt), lambda i, fill: (i, 0, 0),
                                   memory_space=pltpu.SMEM),
                      pl.BlockSpec((tt, D_MODEL), lambda i, fill: (i, 0))],
            out_specs=pl.BlockSpec(memory_space=pl.ANY),
            scratch_shapes=[pltpu.VMEM((tm * ROW_TILE, LANES), f32),
                            pltpu.VMEM((tt * ROW_TILE, LANES), f32),
                            pltpu.SemaphoreType.DMA(())]),
        out_shape=jax.ShapeDtypeStruct((rows_sorted * ROW_TILE, LANES), f32),
        compiler_params=_params("arbitrary"),
        name="moe_dispatch",
    )(fill, pos_blocks, x1)


def _combine_kernel(pos_ref, pos_next_ref, route_ref, x1_ref, ys_hbm, lng, lnb, out_ref, buf, sems,
                    *, tt, n_steps):
    step = pl.program_id(0)
    slot = step % 2

    def issue_tile(tile_pos_ref, into):
        def issue(t, carry):
            dst = pl.multiple_of(t * ROW_TILE, ROW_TILE)
            for k in range(TOP_K):
                src = pl.multiple_of(tile_pos_ref[0, TOP_K * t + k] * ROW_TILE, ROW_TILE)
                pltpu.make_async_copy(ys_hbm.at[pl.ds(src, ROW_TILE)],
                                      buf.at[into, k, pl.ds(dst, ROW_TILE)],
                                      sems.at[into]).start(priority=k)
            return carry

        lax.fori_loop(0, tt, issue, 0, unroll=DMA_UNROLL)

    @pl.when(step == 0)
    def _():
        issue_tile(pos_ref, 0)

    @pl.when(step + 1 < n_steps)
    def _():
        issue_tile(pos_next_ref, 1 - slot)

    for k in range(TOP_K):
        pltpu.make_async_copy(ys_hbm.at[pl.ds(0, tt * ROW_TILE)], buf.at[slot, k],
                              sems.at[slot]).wait()

    lane = lax.broadcasted_iota(jnp.int32, (1, LANES), 1)
    route = route_ref[...]
    p1 = jnp.sum(jnp.where(lane == 4, route, 0.0), axis=-1, keepdims=True)
    p2 = jnp.sum(jnp.where(lane == 5, route, 0.0), axis=-1, keepdims=True)
    y = p1 * _load_row_tiles(buf.at[slot, 0], tt) + p2 * _load_row_tiles(buf.at[slot, 1], tt)
    out_ref[...] = _layer_norm(DEEPNORM_ALPHA * x1_ref[...] + y, lng[...], lnb[...])


def _combine(pos_blocks, route, x1, y_sorted, ln_g, ln_b, *, tt):
    n = x1.shape[0]
    n_steps = n // tt
    pos_spec = lambda index: pl.BlockSpec((None, 1, TOP_K * tt), index, memory_space=pltpu.SMEM)
    return pl.pallas_call(
        functools.partial(_combine_kernel, tt=tt, n_steps=n_steps),
        grid=(n_steps,),
        in_specs=[pos_spec(lambda i: (i, 0, 0)),
                  pos_spec(lambda i: (jnp.minimum(i + 1, n_steps - 1), 0, 0)),
                  pl.BlockSpec((tt, LANES), lambda i: (i, 0)),
                  pl.BlockSpec((tt, D_MODEL), lambda i: (i, 0)),
                  pl.BlockSpec(memory_space=pl.ANY),
                  pl.BlockSpec((1, D_MODEL), lambda i: (0, 0)),
                  pl.BlockSpec((1, D_MODEL), lambda i: (0, 0))],
        out_specs=pl.BlockSpec((tt, D_MODEL), lambda i: (i, 0)),
        out_shape=jax.ShapeDtypeStruct((n, D_MODEL), f32),
        scratch_shapes=[pltpu.VMEM((2, TOP_K, tt * ROW_TILE, LANES), f32),
                        pltpu.SemaphoreType.DMA((2,))],
        compiler_params=_params("arbitrary"),
        name="moe_combine",
    )(pos_blocks, pos_blocks, route, x1, y_sorted, ln_g.reshape(1, D_MODEL).astype(f32),
      ln_b.reshape(1, D_MODEL).astype(f32))


def _moe(x1, route, counts, w_gate, w_up, w_down, ln_g, ln_b, *, tm, tf):
    n = x1.shape[0]
    i32 = jnp.int32
    counts = counts[0, :N_EXPERTS].astype(i32)
    padded = (counts + tm - 1) // tm * tm
    ends = jnp.cumsum(padded)
    offsets = ends - padded
    experts = route[:, 0:TOP_K].astype(i32)
    ranks = route[:, TOP_K:2 * TOP_K].astype(i32)
    pos = offsets[experts] + ranks
    pos_blocks = lambda tt: pos.reshape(n // tt, 1, TOP_K * tt)
    n_tiles = TOP_K * n // tm + N_EXPERTS
    n_used = (ends[-1] // tm).astype(i32).reshape(1)
    tile_ids = jnp.arange(n_tiles, dtype=i32)
    tile_expert = jnp.sum((tile_ids[:, None] >= (ends // tm)[None, :]).astype(i32), axis=1)
    last_expert = jnp.max(jnp.where(counts > 0, jnp.arange(N_EXPERTS, dtype=i32), 0))
    tile_expert = jnp.minimum(tile_expert, last_expert).astype(i32)
    tail_tiles = n_used[0] + jnp.arange(N_EXPERTS, dtype=i32)
    fill = jnp.concatenate([ends - tm, tail_tiles * tm,
                            (counts > 0).astype(i32), (tail_tiles < n_tiles).astype(i32)]).astype(i32)

    x_sorted = _dispatch(x1, pos_blocks(DISPATCH_ROWS), fill, n_tiles * tm, tt=DISPATCH_ROWS,
                         tm=tm)
    y_sorted = _expert_ffn(x_sorted, tile_expert, n_used, w_gate, w_up, w_down, tm=tm, tf=tf)
    return _combine(pos_blocks(COMBINE_ROWS), route, x1, y_sorted, ln_g, ln_b, tt=COMBINE_ROWS)


def kernel(x, w_in, conv_w, conv_b, w_a, b_a, w_x, b_x, lru_lambda, rel_bias, g_attn, g_lru, w_out, ln1_g, ln1_b, ln2_g, ln2_b, ffn_w_gate, ffn_w_up, ffn_w_down, router_w, moe_w_gate, moe_w_up, moe_w_down):
    batch, seq, _ = x.shape
    n = batch * seq
    h = x.reshape(n, D_MODEL).astype(f32)
    band_bias = jnp.stack([_band_bias(rel_bias, d) for _, d in DILATED_PATTERNS])
    for layer in range(DEPTH):
        qkv, ylru = _proj_lru(h, w_in[layer].astype(bf16), conv_w[layer], conv_b[layer],
                              w_a[layer], b_a[layer], w_x[layer], b_x[layer], lru_lambda[layer],
                              g_lru[layer], batch, seq)
        attn = _attention(qkv, band_bias, batch, seq)
        j = layer // 2
        w_o = w_out[layer].astype(bf16)
        if layer % 2 == 0:
            h = _mix_ffn_dense(attn, ylru, h, w_o, g_attn[layer], (ln1_g[layer], ln1_b[layer]),
                               ffn_w_gate[j].astype(bf16), ffn_w_up[j].astype(bf16),
                               ffn_w_down[j].astype(bf16), (ln2_g[layer], ln2_b[layer]))
        else:
            x1, route, counts = _mix_router(attn, ylru, h, w_o, g_attn[layer], ln1_g[layer],
                                            ln1_b[layer], router_w[j])
            h = _moe(x1, route, counts, moe_w_gate[j].astype(bf16), moe_w_up[j].astype(bf16),
                     moe_w_down[j].astype(bf16), ln2_g[layer], ln2_b[layer],
                     tm=EXPERT_ROWS, tf=EXPERT_FF_CHUNK)
    return h.reshape(batch, seq, D_MODEL).astype(x.dtype)
```

```python
import functools

import numpy as np
import jax
import jax.numpy as jnp
from jax import lax
from jax.experimental import pallas as pl
from jax.experimental.pallas import tpu as pltpu

D_MODEL = 1024
N_HEADS = 8
HEAD_DIM = 64
D_ATTN = N_HEADS * HEAD_DIM
D_LRU = 512
N_LRU_BLOCKS = 8
LRU_BLOCK = D_LRU // N_LRU_BLOCKS
CONV_WIDTH = 4
LRU_C = 8.0
DILATED_PATTERNS = ((128, 1), (512, 4), (2048, 16))
ATTN_BLOCK = 128
STAGE_DILATION = 4
N_BUCKETS = 32
MAX_DISTANCE = 2048
D_IN = 3 * D_ATTN + 2 * D_LRU
N_EXPERTS = 8
TOP_K = 2
DEPTH = 2
DEEPNORM_ALPHA = (2.0 * DEPTH) ** 0.25
LN_EPS = 1e-5
RMS_EPS = 1e-6
NEG_INF = -1e30
LOG2_E = float(np.log2(np.e))

LANES = 128
SUBLANES = 8
VMEM_LIMIT_BYTES = 56 * 1024 * 1024
MXU_DEPTH = 256

LRU_ROWS = 1024
ROUTER_ROWS = 1024
DENSE_FFN_ROWS = 1024
EXPERT_ROWS = 512
EXPERT_FF_CHUNK = 7 * MXU_DEPTH
DISPATCH_ROWS = 2048
COMBINE_ROWS = 256
DMA_UNROLL = 8
ATTN_UNROLL = 4

f32 = jnp.float32
bf16 = jnp.bfloat16


def _params(*semantics):
    return pltpu.CompilerParams(dimension_semantics=semantics,
                                vmem_limit_bytes=VMEM_LIMIT_BYTES)


def _t5_bucket(dist):
    max_exact = N_BUCKETS // 2
    d = np.maximum(dist, 1).astype(np.float32)
    large = max_exact + (np.log(d / max_exact) / np.log(MAX_DISTANCE / max_exact)
                         * (N_BUCKETS - max_exact)).astype(np.int32)
    large = np.minimum(large, N_BUCKETS - 1)
    return np.where(dist < max_exact, dist, large).astype(np.int32)


def _band_bias(rel_bias, dilation):
    nk = ATTN_BLOCK
    qi = np.arange(nk)[:, None]
    kj = np.arange(2 * nk)[None, :]
    delta = qi + nk - kj
    band = (delta >= 0) & (delta <= nk)
    bucket = _t5_bucket(np.clip(delta, 0, nk) * dilation)
    onehot = np.eye(N_BUCKETS, dtype=np.float32)[bucket.reshape(-1)]
    bias = jnp.dot(jnp.asarray(onehot), rel_bias.astype(f32), precision=lax.Precision.HIGHEST)
    bias = jnp.transpose(bias.reshape(nk, 2 * nk, N_HEADS), (2, 0, 1))
    valid = np.stack([band, band & (kj >= nk)])[:, None]
    bias = jnp.where(jnp.asarray(valid), bias[None], NEG_INF)
    return (bias * LOG2_E).reshape(2, N_HEADS // 2, 2 * nk, 2 * nk)


def _attn_kernel(q_ref, k_ref, v_ref, bias_ref, o_ref, q4, k4, v4, qs, ks, vs, s_buf, m_buf,
                 m_s, l_s, acc_s, *, seq):
    nk = ATTN_BLOCK
    n_blocks = seq // nk
    chunk = 2 * nk
    lane = lax.broadcasted_iota(jnp.int32, (1, LANES), 1)
    head0 = lane < HEAD_DIM

    ks[0:nk, :] = jnp.zeros((nk, LANES), bf16)
    vs[0:nk, 0:LANES] = jnp.zeros((nk, LANES), bf16)
    vs[:, LANES:2 * LANES] = jnp.ones((seq + nk, LANES), bf16)

    def stage(t, carry):
        chunks_per_residue = seq // STAGE_DILATION // chunk
        src = pl.ds(t // chunks_per_residue + STAGE_DILATION * chunk * (t % chunks_per_residue),
                    chunk, stride=STAGE_DILATION)
        dst = pl.ds(pl.multiple_of(t * chunk, chunk), chunk)
        q4[dst, :] = q_ref[src, :]
        k4[dst, :] = k_ref[src, :]
        v4[dst, :] = v_ref[src, :]
        return carry

    lax.fori_loop(0, seq // chunk, stage, 0)

    order = sorted(range(len(DILATED_PATTERNS)), key=lambda p: -DILATED_PATTERNS[p][1])
    assert DILATED_PATTERNS[order[-1]][1] == 1
    for p in order:
        d = DILATED_PATTERNS[p][1]
        is_first, is_last = p == order[0], p == order[-1]
        length = seq // d
        nb = length // nk
        chunks_per_residue = length // chunk

        def gather(t, carry):
            r = t // chunks_per_residue
            c = t % chunks_per_residue
            dst = pl.multiple_of(t * chunk, chunk)
            if d % STAGE_DILATION == 0:
                sub = d // STAGE_DILATION
                start = ((r % STAGE_DILATION) * (seq // STAGE_DILATION) + r // STAGE_DILATION
                         + sub * chunk * c)
                src = (pl.ds(start, chunk, stride=sub) if sub > 1
                       else pl.ds(pl.multiple_of(start, chunk), chunk))
                q, k, v = q4[src, :], k4[src, :], v4[src, :]
            else:
                src = pl.ds(r + d * chunk * c, chunk, stride=d) if d > 1 else pl.ds(dst, chunk)
                q, k, v = q_ref[src, :], k_ref[src, :], v_ref[src, :]
            q = q * (HEAD_DIM ** -0.5 * LOG2_E)
            q0 = jnp.where(head0, q, 0.0).astype(bf16)
            q1 = jnp.where(head0, 0.0, q).astype(bf16)
            for half in range(2):
                base = pl.multiple_of(2 * dst + half * chunk, chunk)
                qs[pl.ds(base, nk), :] = q0[half * nk:(half + 1) * nk]
                qs[pl.ds(base + nk, nk), :] = q1[half * nk:(half + 1) * nk]
            ks[pl.ds(nk + dst, chunk), :] = k.astype(bf16)
            vs[pl.ds(nk + dst, chunk), 0:LANES] = v.astype(bf16)
            return carry

        lax.fori_loop(0, seq // chunk, gather, 0)

        def scores(b, carry):
            first = jnp.asarray(b % nb == 0, jnp.int32)
            rows = pl.ds(pl.multiple_of(b * chunk, chunk), chunk)
            k = ks[pl.ds(pl.multiple_of(b * nk, nk), chunk), :]
            s_buf[rows, :] = lax.dot_general(qs[rows, :], k, (((1,), (1,)), ((), ())),
                                             preferred_element_type=f32) + bias_ref[p, first]
            return carry

        def rowmax(b, carry):
            rows = pl.ds(pl.multiple_of(b * chunk, chunk), chunk)
            m_buf[rows, :] = jnp.broadcast_to(jnp.max(s_buf[rows, :], axis=-1, keepdims=True),
                                              (chunk, LANES))
            return carry

        def block(b, carry):
            r = b // nb
            i = b % nb
            rows = pl.ds(pl.multiple_of(b * chunk, chunk), chunk)
            v = vs[pl.ds(pl.multiple_of(b * nk, nk), chunk), :]
            m = m_buf[rows, :]
            e = jnp.exp2(s_buf[rows, :] - jnp.concatenate([m, m], axis=1))
            pv = jnp.dot(e.astype(bf16), v, preferred_element_type=f32)
            m_blk = jnp.where(head0, m[0:nk], m[nk:chunk])
            l_blk = jnp.where(head0, pv[0:nk, LANES:], pv[nk:chunk, LANES:])
            pv_blk = jnp.where(head0, pv[0:nk, 0:LANES], pv[nk:chunk, 0:LANES])
            if d == 1:
                tok = pl.ds(pl.multiple_of(b * nk, nk), nk)
            else:
                tok = pl.ds(r + d * nk * i, nk, stride=d)
            if is_first:
                m_s[tok, :] = m_blk
                l_s[tok, :] = l_blk
                acc_s[tok, :] = pv_blk
                return carry
            m_old = m_s[tok, :]
            m_new = jnp.maximum(m_old, m_blk)
            w_old = jnp.exp2(m_old - m_new)
            w_blk = jnp.exp2(m_blk - m_new)
            l_new = w_old * l_s[tok, :] + w_blk * l_blk
            acc_new = w_old * acc_s[tok, :] + w_blk * pv_blk
            if is_last:
                o_ref[tok, :] = (acc_new / l_new).astype(o_ref.dtype)
            else:
                l_s[tok, :] = l_new
                acc_s[tok, :] = acc_new
                m_s[tok, :] = m_new
            return carry

        n_groups = n_blocks // ATTN_UNROLL
        assert n_groups >= 3

        def group(fn, g):
            for j in range(ATTN_UNROLL):
                fn(g * ATTN_UNROLL + j, 0)

        def pipelined(g, carry):
            group(block, g)
            group(rowmax, g + 1)
            group(scores, g + 2)
            return carry

        group(scores, 0)
        group(scores, 1)
        group(rowmax, 0)
        lax.fori_loop(0, n_groups - 2, pipelined, 0)
        group(block, n_groups - 2)
        group(rowmax, n_groups - 1)
        group(block, n_groups - 1)


def _attention(qkv, band_bias, batch, seq):
    view = qkv.reshape(batch, seq, 3 * D_ATTN)
    pairs = D_ATTN // LANES

    def spec(offset):
        return pl.BlockSpec((None, seq, LANES), lambda b, hp: (b, 0, offset * pairs + hp))

    n_pat = len(DILATED_PATTERNS)
    o = pl.pallas_call(
        functools.partial(_attn_kernel, seq=seq),
        grid=(batch, pairs),
        in_specs=[spec(0), spec(1), spec(2),
                  pl.BlockSpec((n_pat, 2, None, 2 * ATTN_BLOCK, 2 * ATTN_BLOCK),
                               lambda b, hp: (0, 0, hp, 0, 0))],
        out_specs=pl.BlockSpec((None, seq, LANES), lambda b, hp: (b, 0, hp)),
        out_shape=jax.ShapeDtypeStruct((batch, seq, D_ATTN), bf16),
        scratch_shapes=[pltpu.VMEM((seq, LANES), f32), pltpu.VMEM((seq, LANES), f32),
                        pltpu.VMEM((seq, LANES), f32),
                        pltpu.VMEM((2 * seq, LANES), bf16),
                        pltpu.VMEM((seq + ATTN_BLOCK, LANES), bf16),
                        pltpu.VMEM((seq + ATTN_BLOCK, 2 * LANES), bf16),
                        pltpu.VMEM((2 * seq, 2 * ATTN_BLOCK), f32),
                        pltpu.VMEM((2 * seq, LANES), f32),
                        pltpu.VMEM((seq, LANES), f32), pltpu.VMEM((seq, LANES), f32),
                        pltpu.VMEM((seq, LANES), f32)],
        compiler_params=_params("parallel", "parallel"),
        name="attention",
    )(view, view, view, band_bias)
    return o.reshape(batch * seq, D_ATTN)


def _gelu_tanh(x):
    return 0.5 * x * (1.0 + jnp.tanh(np.sqrt(2.0 / np.pi) * (x + 0.044715 * x * x * x)))


def _proj_lru_kernel(x_ref, w_ref, convw_ref, convb_ref, wg_ref, ba_ref, bx_ref, lam_ref, g_ref,
                     qkv_ref, y_ref, ubuf, hcarry, *, ts):
    pad = SUBLANES
    t = pl.program_id(1)
    n_qkv = qkv_ref.shape[1]
    xb = x_ref[...].astype(bf16)

    @pl.when(t == 0)
    def _():
        ubuf[0:pad, :] = jnp.zeros((pad, D_LRU), f32)
        hcarry[...] = jnp.zeros_like(hcarry)

    @pl.when(t > 0)
    def _():
        ubuf[0:pad, :] = ubuf[ts:ts + pad, :]

    ubuf[pad:pad + ts, :] = jnp.dot(xb, w_ref[:, n_qkv:n_qkv + D_LRU],
                                    preferred_element_type=f32)
    gate = jnp.dot(xb, w_ref[:, n_qkv + D_LRU:], preferred_element_type=f32)

    qkv_step = n_qkv // 3
    assert qkv_step == D_LRU

    def qkv_chunk(c):
        cols = slice(c * qkv_step, (c + 1) * qkv_step)
        p = jnp.dot(xb, w_ref[:, cols], preferred_element_type=f32)
        qkv_ref[:, cols] = p
        bits = lax.bitcast_convert_type(p, jnp.uint32).reshape(ts // SUBLANES, SUBLANES, qkv_step)
        folded = functools.reduce(jnp.bitwise_or, [bits[g] for g in range(ts // SUBLANES)])
        return ((folded >> 16) >> 16).astype(f32)

    u = convb_ref[...] + convw_ref[CONV_WIDTH - 1:CONV_WIDTH, :] * ubuf[pad:pad + ts, :]
    for w in range(CONV_WIDTH - 1):
        back = CONV_WIDTH - 1 - w
        u = u + convw_ref[w:w + 1, :] * ubuf[pad - back:pad - back + ts, :]

    gates = jnp.dot(u.astype(bf16), wg_ref[...], preferred_element_type=f32)
    r = jax.nn.sigmoid(gates[:, 0:D_LRU] + ba_ref[...])
    i = jax.nn.sigmoid(gates[:, D_LRU:2 * D_LRU] + bx_ref[...])
    neg_lam = -lam_ref[...]
    softplus = jnp.maximum(neg_lam, 0.0) + jnp.log1p(jnp.exp(-jnp.abs(neg_lam)))
    log_a = (-LRU_C) * r * softplus
    a = jnp.exp(log_a)
    b = jnp.sqrt(-jnp.tanh(log_a) * (1.0 + a * a)) * (i * u)
    zero0 = qkv_chunk(0)

    groups = ts // SUBLANES
    a = a.reshape(groups, SUBLANES, D_LRU)
    b = b.reshape(groups, SUBLANES, D_LRU) + zero0[None]
    sub = lax.broadcasted_iota(jnp.int32, (1, SUBLANES, 1), 1)
    shift = 1
    while shift < SUBLANES:
        live = sub >= shift
        a_prev = jnp.where(live, pltpu.roll(a, shift, 1), 1.0)
        b_prev = jnp.where(live, pltpu.roll(b, shift, 1), 0.0)
        b = a * b_prev + b
        a = a * a_prev
        shift *= 2
    h_prev = hcarry[...] + qkv_chunk(1)[0:1, :]
    h_groups = []
    for g in range(groups):
        h_g = a[g] * h_prev + b[g]
        h_prev = h_g[SUBLANES - 1:SUBLANES, :]
        h_groups.append(h_g)
    hcarry[...] = h_prev
    h = jnp.concatenate(h_groups, axis=0)
    zero2 = qkv_chunk(2)[0:1, :]

    y = _gelu_tanh(gate + zero2) * h
    y = y * lax.rsqrt(jnp.mean(y * y, axis=-1, keepdims=True) + RMS_EPS) * g_ref[...]
    y_ref[...] = y.astype(y_ref.dtype)


def _block_diag(w):
    g, i, j = w.shape
    eye = jnp.eye(g, dtype=w.dtype)
    return jnp.einsum('gij,gh->gihj', w, eye).reshape(g * i, g * j)


def _proj_lru(x2d, w_in_bf16, conv_w, conv_b, w_a, b_a, w_x, b_x, lam, g_lru, batch, seq,
              ts=LRU_ROWS):
    wg = jnp.concatenate([_block_diag(w_a), _block_diag(w_x)], axis=1).astype(bf16)
    row = lambda v: v.reshape(1, D_LRU).astype(f32)
    const = lambda shape: pl.BlockSpec(shape, lambda b, t: (0, 0))
    tile = lambda width: pl.BlockSpec((None, ts, width), lambda b, t: (b, t, 0))
    qkv, y = pl.pallas_call(
        functools.partial(_proj_lru_kernel, ts=ts),
        grid=(batch, seq // ts),
        in_specs=[tile(D_MODEL), const((D_MODEL, D_IN)),
                  const((CONV_WIDTH, D_LRU)), const((1, D_LRU)),
                  const((D_LRU, 2 * D_LRU)), const((1, D_LRU)), const((1, D_LRU)),
                  const((1, D_LRU)), const((1, D_LRU))],
        out_specs=[tile(3 * D_ATTN), tile(D_LRU)],
        out_shape=[jax.ShapeDtypeStruct((batch, seq, 3 * D_ATTN), f32),
                   jax.ShapeDtypeStruct((batch, seq, D_LRU), bf16)],
        scratch_shapes=[pltpu.VMEM((ts + 2 * SUBLANES, D_LRU), f32), pltpu.VMEM((1, D_LRU), f32)],
        compiler_params=_params("parallel", "arbitrary"),
        name="proj_rglru",
    )(x2d.reshape(batch, seq, D_MODEL), w_in_bf16,
      conv_w.reshape(CONV_WIDTH, D_LRU).astype(f32), row(conv_b), wg, row(b_a), row(b_x),
      row(lam), row(g_lru))
    return qkv.reshape(batch * seq, 3 * D_ATTN), y.reshape(batch * seq, D_LRU)


ROW_TILE = D_MODEL // LANES


def _store_row_tiles(ref, rows):
    t = rows.shape[0]
    for s in range(ROW_TILE):
        ref[pl.ds(s, t, stride=ROW_TILE), :] = rows[:, s * LANES:(s + 1) * LANES]


def _load_row_tiles(ref, t):
    return jnp.concatenate([ref[pl.ds(s, t, stride=ROW_TILE), :] for s in range(ROW_TILE)],
                           axis=-1)


def _layer_norm(z, g, b):
    mu = jnp.mean(z, axis=-1, keepdims=True)
    zc = z - mu
    var = jnp.mean(zc * zc, axis=-1, keepdims=True)
    return zc * lax.rsqrt(var + LN_EPS) * g + b


def _mixer_residual(attn_ref, ylru_ref, x_ref, w_ref, gattn, lng, lnb):
    attn = attn_ref[...].astype(f32)
    attn = attn * lax.rsqrt(jnp.mean(attn * attn, axis=-1, keepdims=True) + RMS_EPS) * gattn[...]
    y = jnp.dot(attn.astype(bf16), w_ref[0:D_ATTN, :], preferred_element_type=f32)
    y = y + jnp.dot(ylru_ref[...], w_ref[D_ATTN:, :], preferred_element_type=f32)
    return _layer_norm(DEEPNORM_ALPHA * x_ref[...] + y, lng[...], lnb[...])


def _swiglu(xb, wg, wu, wd):
    g = jnp.dot(xb, wg, preferred_element_type=f32)
    u = jnp.dot(xb, wu, preferred_element_type=f32)
    h = (g * jax.nn.sigmoid(g)) * u
    return jnp.dot(h.astype(bf16), wd, preferred_element_type=f32)


def _mix_ffn_kernel(attn_ref, ylru_ref, x_ref, wo_ref, gattn, ln1g, ln1b, wg_ref, wu_ref, wd_ref,
                    ln2g, ln2b, out_ref):
    x1 = _mixer_residual(attn_ref, ylru_ref, x_ref, wo_ref, gattn, ln1g, ln1b)
    f = _swiglu(x1.astype(bf16), wg_ref[...], wu_ref[...], wd_ref[...])
    out_ref[...] = _layer_norm(DEEPNORM_ALPHA * x1 + f, ln2g[...], ln2b[...])


def _mix_ffn_dense(attn, ylru, x2d, w_out_bf16, g_attn, ln1, w_gate, w_up, w_down, ln2,
                   tm=DENSE_FFN_ROWS):
    n = x2d.shape[0]
    d_ff = w_gate.shape[1]
    tile = lambda width: pl.BlockSpec((tm, width), lambda i: (i, 0))
    const = lambda shape: pl.BlockSpec(shape, lambda i: (0, 0), pipeline_mode=pl.Buffered(1))
    row = lambda v: v.reshape(1, -1).astype(f32)
    return pl.pallas_call(
        _mix_ffn_kernel,
        grid=(n // tm,),
        in_specs=[tile(D_ATTN), tile(D_LRU), tile(D_MODEL), const((D_MODEL, D_MODEL)),
                  const((1, D_ATTN)), const((1, D_MODEL)), const((1, D_MODEL)),
                  const((D_MODEL, d_ff)), const((D_MODEL, d_ff)), const((d_ff, D_MODEL)),
                  const((1, D_MODEL)), const((1, D_MODEL))],
        out_specs=tile(D_MODEL),
        out_shape=jax.ShapeDtypeStruct((n, D_MODEL), f32),
        compiler_params=_params("parallel"),
        name="mix_ffn_dense",
    )(attn, ylru, x2d, w_out_bf16, row(g_attn), row(ln1[0]), row(ln1[1]), w_gate, w_up, w_down,
      row(ln2[0]), row(ln2[1]))


def _mix_router_kernel(attn_ref, ylru, x_ref, w_ref, gattn, lng, lnb, rw_ref, tri_ref,
                       x1_ref, route_ref, count_ref):
    x1 = _mixer_residual(attn_ref, ylru, x_ref, w_ref, gattn, lng, lnb)
    x1_ref[...] = x1
    lane = lax.broadcasted_iota(jnp.int32, (1, LANES), 1).astype(f32)
    x_hi = x1.astype(bf16)
    x_lo = (x1 - x_hi.astype(f32)).astype(bf16)
    logits = (jnp.dot(x_hi, rw_ref[0], preferred_element_type=f32)
              + jnp.dot(x_lo, rw_ref[0], preferred_element_type=f32)
              + jnp.dot(x_hi, rw_ref[1], preferred_element_type=f32))
    logits = jnp.where(lane < N_EXPERTS, logits, -jnp.inf)
    v1 = jnp.max(logits, axis=-1, keepdims=True)
    i1 = jnp.min(jnp.where(logits == v1, lane, float(LANES)), axis=-1, keepdims=True)
    rest_logits = jnp.where(lane == i1, -jnp.inf, logits)
    v2 = jnp.max(rest_logits, axis=-1, keepdims=True)
    i2 = jnp.min(jnp.where(rest_logits == v2, lane, float(LANES)), axis=-1, keepdims=True)
    e2 = jnp.exp(v2 - v1)
    p1 = 1.0 / (1.0 + e2)
    p2 = e2 / (1.0 + e2)

    @pl.when(pl.program_id(0) == 0)
    def _():
        count_ref[...] = jnp.zeros_like(count_ref)

    tm = x1.shape[0]
    chosen = jnp.logical_or(lane == i1, lane == i2)
    rank = count_ref[...] + jnp.dot(tri_ref[...], chosen.astype(bf16),
                                    preferred_element_type=f32)
    count_ref[...] += jnp.sum(chosen.astype(f32), axis=0, keepdims=True)
    r1 = jnp.sum(jnp.where(lane == i1, rank, 0.0), axis=-1, keepdims=True)
    r2 = jnp.sum(jnp.where(lane == i2, rank, 0.0), axis=-1, keepdims=True)
    fields = (i1, i2, r1, r2, p1, p2)
    route = jnp.zeros((tm, LANES), f32)
    for k, val in enumerate(fields):
        route = jnp.where(lane == k, val, route)
    route_ref[...] = route


def _mix_router(attn, ylru, x2d, w_out_bf16, g_attn, ln_g, ln_b, router_w, tm=ROUTER_ROWS):
    n = x2d.shape[0]
    tile = lambda width: pl.BlockSpec((tm, width), lambda i: (i, 0))
    const = lambda shape: pl.BlockSpec(shape, lambda i: (0,) * len(shape))
    row = lambda v: v.reshape(1, -1).astype(f32)
    rw = jnp.zeros((D_MODEL, LANES), f32).at[:, :N_EXPERTS].set(router_w.astype(f32))
    rw_hi = rw.astype(bf16)
    rw_lo = (rw - rw_hi.astype(f32)).astype(bf16)
    strictly_lower = jnp.asarray(np.tri(tm, k=-1), bf16)
    return pl.pallas_call(
        _mix_router_kernel,
        grid=(n // tm,),
        in_specs=[tile(D_ATTN), tile(D_LRU), tile(D_MODEL), const((D_MODEL, D_MODEL)),
                  const((1, D_ATTN)), const((1, D_MODEL)), const((1, D_MODEL)),
                  const((2, D_MODEL, LANES)), const((tm, tm))],
        out_specs=[tile(D_MODEL), tile(LANES), const((1, LANES))],
        out_shape=[jax.ShapeDtypeStruct((n, D_MODEL), f32), jax.ShapeDtypeStruct((n, LANES), f32),
                   jax.ShapeDtypeStruct((1, LANES), f32)],
        compiler_params=_params("arbitrary"),
        name="mix_out_router",
    )(attn, ylru, x2d, w_out_bf16, row(g_attn), row(ln_g), row(ln_b), jnp.stack([rw_hi, rw_lo]),
      strictly_lower)


def _expert_ffn_kernel(tile_expert_ref, n_used_ref, x_ref, wg_ref, wu_ref, wd_ref, out_ref,
                       acc_ref, xb_ref, *, n_chunks):
    del tile_expert_ref
    j = pl.program_id(1)
    used = pl.program_id(0) < n_used_ref[0]
    last = n_chunks - 1
    assert n_chunks >= 2

    def chunk_out(xb):
        return _swiglu(xb, wg_ref[...], wu_ref[...], wd_ref[...])

    @pl.when(used & (j == 0))
    def _():
        xb = _load_row_tiles(x_ref, xb_ref.shape[0]).astype(bf16)
        xb_ref[...] = xb
        acc_ref[...] = chunk_out(xb)

    if n_chunks > 2:
        @pl.when(used & (j > 0) & (j < last))
        def _():
            acc_ref[...] += chunk_out(xb_ref[...])

    @pl.when(used & (j == last))
    def _():
        _store_row_tiles(out_ref, acc_ref[...] + chunk_out(xb_ref[...]))

    @pl.when(jnp.logical_not(used) & (j == last))
    def _():
        _store_row_tiles(out_ref, jnp.zeros(acc_ref.shape, f32))


def _expert_ffn(x_rows, tile_expert, n_used, w_gate, w_up, w_down, *, tm, tf):
    rows = x_rows.shape[0] // ROW_TILE
    n_chunks = w_gate.shape[2] // tf
    io_block = (tm * ROW_TILE, LANES)

    def chunk(i, j, nu):
        return jnp.where(i < nu[0], j, n_chunks - 1)

    return pl.pallas_call(
        functools.partial(_expert_ffn_kernel, n_chunks=n_chunks),
        grid_spec=pltpu.PrefetchScalarGridSpec(
            num_scalar_prefetch=2,
            grid=(rows // tm, n_chunks),
            in_specs=[pl.BlockSpec(io_block, lambda i, j, te, nu: (i, 0)),
                      pl.BlockSpec((None, D_MODEL, tf),
                                   lambda i, j, te, nu: (te[i], 0, chunk(i, j, nu))),
                      pl.BlockSpec((None, D_MODEL, tf),
                                   lambda i, j, te, nu: (te[i], 0, chunk(i, j, nu))),
                      pl.BlockSpec((None, tf, D_MODEL),
                                   lambda i, j, te, nu: (te[i], chunk(i, j, nu), 0))],
            out_specs=pl.BlockSpec(io_block, lambda i, j, te, nu: (i, 0)),
            scratch_shapes=[pltpu.VMEM((tm, D_MODEL), f32), pltpu.VMEM((tm, D_MODEL), bf16)]),
        out_shape=jax.ShapeDtypeStruct(x_rows.shape, f32),
        compiler_params=_params("parallel", "arbitrary"),
        name="ffn_experts",
    )(tile_expert, n_used, x_rows, w_gate, w_up, w_down)


def _dispatch_kernel(fill_ref, pos_ref, x_ref, xs_hbm, zero_buf, rows_buf, sem, *, tt, tm):
    step = pl.program_id(0)

    n_fill = fill_ref.shape[0] // 2

    @pl.when(step == 0)
    def _():
        zero_buf[...] = jnp.zeros_like(zero_buf)

        def fill_copy(f):
            start = pl.multiple_of(fill_ref[f] * ROW_TILE, ROW_TILE)
            return pltpu.make_async_copy(zero_buf, xs_hbm.at[pl.ds(start, tm * ROW_TILE)], sem)

        for f in range(n_fill):
            @pl.when(fill_ref[n_fill + f] > 0)
            def _():
                fill_copy(f).start()
        for f in range(n_fill):
            @pl.when(fill_ref[n_fill + f] > 0)
            def _():
                fill_copy(f).wait()

    _store_row_tiles(rows_buf, x_ref[...])

    def issue(t, carry):
        src = pl.multiple_of(t * ROW_TILE, ROW_TILE)
        for k in range(TOP_K):
            dst = pl.multiple_of(pos_ref[0, TOP_K * t + k] * ROW_TILE, ROW_TILE)
            pltpu.make_async_copy(rows_buf.at[pl.ds(src, ROW_TILE)],
                                  xs_hbm.at[pl.ds(dst, ROW_TILE)], sem).start(priority=k)
        return carry

    lax.fori_loop(0, tt, issue, 0, unroll=DMA_UNROLL)
    for _ in range(TOP_K):
        pltpu.make_async_copy(rows_buf, xs_hbm.at[pl.ds(0, tt * ROW_TILE)], sem).wait()


def _dispatch(x1, pos_blocks, fill, rows_sorted, *, tt, tm):
    n = x1.shape[0]
    return pl.pallas_call(
        functools.partial(_dispatch_kernel, tt=tt, tm=tm),
        grid_spec=pltpu.PrefetchScalarGridSpec(
            num_scalar_prefetch=1,
            grid=(n // tt,),
            in_specs=[pl.BlockSpec((None, 1, TOP_K * tt), lambda i, fill: (i, 0, 0),
                                   memory_space=pltpu.SMEM),
                      pl.BlockSpec((tt, D_MODEL), lambda i, fill: (i, 0))],
            out_specs=pl.BlockSpec(memory_space=pl.ANY),
            scratch_shapes=[pltpu.VMEM((tm * ROW_TILE, LANES), f32),
                            pltpu.VMEM((tt * ROW_TILE, LANES), f32),
                            pltpu.SemaphoreType.DMA(())]),
        out_shape=jax.ShapeDtypeStruct((rows_sorted * ROW_TILE, LANES), f32),
        compiler_params=_params("arbitrary"),
        name="moe_dispatch",
    )(fill, pos_blocks, x1)


def _combine_kernel(pos_ref, pos_next_ref, route_ref, x1_ref, ys_hbm, lng, lnb, out_ref, buf, sems,
                    *, tt, n_steps):
    step = pl.program_id(0)
    slot = step % 2

    def issue_tile(tile_pos_ref, into):
        def issue(t, carry):
            dst = pl.multiple_of(t * ROW_TILE, ROW_TILE)
            for k in range(TOP_K):
                src = pl.multiple_of(tile_pos_ref[0, TOP_K * t + k] * ROW_TILE, ROW_TILE)
                pltpu.make_async_copy(ys_hbm.at[pl.ds(src, ROW_TILE)],
                                      buf.at[into, k, pl.ds(dst, ROW_TILE)],
                                      sems.at[into]).start(priority=k)
            return carry

        lax.fori_loop(0, tt, issue, 0, unroll=DMA_UNROLL)

    @pl.when(step == 0)
    def _():
        issue_tile(pos_ref, 0)

    @pl.when(step + 1 < n_steps)
    def _():
        issue_tile(pos_next_ref, 1 - slot)

    for k in range(TOP_K):
        pltpu.make_async_copy(ys_hbm.at[pl.ds(0, tt * ROW_TILE)], buf.at[slot, k],
                              sems.at[slot]).wait()

    lane = lax.broadcasted_iota(jnp.int32, (1, LANES), 1)
    route = route_ref[...]
    p1 = jnp.sum(jnp.where(lane == 4, route, 0.0), axis=-1, keepdims=True)
    p2 = jnp.sum(jnp.where(lane == 5, route, 0.0), axis=-1, keepdims=True)
    y = p1 * _load_row_tiles(buf.at[slot, 0], tt) + p2 * _load_row_tiles(buf.at[slot, 1], tt)
    out_ref[...] = _layer_norm(DEEPNORM_ALPHA * x1_ref[...] + y, lng[...], lnb[...])


def _combine(pos_blocks, route, x1, y_sorted, ln_g, ln_b, *, tt):
    n = x1.shape[0]
    n_steps = n // tt
    pos_spec = lambda index: pl.BlockSpec((None, 1, TOP_K * tt), index, memory_space=pltpu.SMEM)
    return pl.pallas_call(
        functools.partial(_combine_kernel, tt=tt, n_steps=n_steps),
        grid=(n_steps,),
        in_specs=[pos_spec(lambda i: (i, 0, 0)),
                  pos_spec(lambda i: (jnp.minimum(i + 1, n_steps - 1), 0, 0)),
                  pl.BlockSpec((tt, LANES), lambda i: (i, 0)),
                  pl.BlockSpec((tt, D_MODEL), lambda i: (i, 0)),
                  pl.BlockSpec(memory_space=pl.ANY),
                  pl.BlockSpec((1, D_MODEL), lambda i: (0, 0)),
                  pl.BlockSpec((1, D_MODEL), lambda i: (0, 0))],
        out_specs=pl.BlockSpec((tt, D_MODEL), lambda i: (i, 0)),
        out_shape=jax.ShapeDtypeStruct((n, D_MODEL), f32),
        scratch_shapes=[pltpu.VMEM((2, TOP_K, tt * ROW_TILE, LANES), f32),
                        pltpu.SemaphoreType.DMA((2,))],
        compiler_params=_params("arbitrary"),
        name="moe_combine",
    )(pos_blocks, pos_blocks, route, x1, y_sorted, ln_g.reshape(1, D_MODEL).astype(f32),
      ln_b.reshape(1, D_MODEL).astype(f32))


def _moe(x1, route, counts, w_gate, w_up, w_down, ln_g, ln_b, *, tm, tf):
    n = x1.shape[0]
    i32 = jnp.int32
    counts = counts[0, :N_EXPERTS].astype(i32)
    padded = (counts + tm - 1) // tm * tm
    ends = jnp.cumsum(padded)
    offsets = ends - padded
    experts = route[:, 0:TOP_K].astype(i32)
    ranks = route[:, TOP_K:2 * TOP_K].astype(i32)
    pos = offsets[experts] + ranks
    pos_blocks = lambda tt: pos.reshape(n // tt, 1, TOP_K * tt)
    n_tiles = TOP_K * n // tm + N_EXPERTS
    n_used = (ends[-1] // tm).astype(i32).reshape(1)
    tile_ids = jnp.arange(n_tiles, dtype=i32)
    tile_expert = jnp.sum((tile_ids[:, None] >= (ends // tm)[None, :]).astype(i32), axis=1)
    last_expert = jnp.max(jnp.where(counts > 0, jnp.arange(N_EXPERTS, dtype=i32), 0))
    tile_expert = jnp.minimum(tile_expert, last_expert).astype(i32)
    tail_tiles = n_used[0] + jnp.arange(N_EXPERTS, dtype=i32)
    fill = jnp.concatenate([ends - tm, tail_tiles * tm,
                            (counts > 0).astype(i32), (tail_tiles < n_tiles).astype(i32)]).astype(i32)

    x_sorted = _dispatch(x1, pos_blocks(DISPATCH_ROWS), fill, n_tiles * tm, tt=DISPATCH_ROWS,
                         tm=tm)
    y_sorted = _expert_ffn(x_sorted, tile_expert, n_used, w_gate, w_up, w_down, tm=tm, tf=tf)
    return _combine(pos_blocks(COMBINE_ROWS), route, x1, y_sorted, ln_g, ln_b, tt=COMBINE_ROWS)


def kernel(x, w_in, conv_w, conv_b, w_a, b_a, w_x, b_x, lru_lambda, rel_bias, g_attn, g_lru, w_out, ln1_g, ln1_b, ln2_g, ln2_b, ffn_w_gate, ffn_w_up, ffn_w_down, router_w, moe_w_gate, moe_w_up, moe_w_down):
    batch, seq, _ = x.shape
    n = batch * seq
    h = x.reshape(n, D_MODEL).astype(f32)
    band_bias = jnp.stack([_band_bias(rel_bias, d) for _, d in DILATED_PATTERNS])
    for layer in range(DEPTH):
        qkv, ylru = _proj_lru(h, w_in[layer].astype(bf16), conv_w[layer], conv_b[layer],
                              w_a[layer], b_a[layer], w_x[layer], b_x[layer], lru_lambda[layer],
                              g_lru[layer], batch, seq)
        attn = _attention(qkv, band_bias, batch, seq)
        j = layer // 2
        w_o = w_out[layer].astype(bf16)
        if layer % 2 == 0:
            h = _mix_ffn_dense(attn, ylru, h, w_o, g_attn[layer], (ln1_g[layer], ln1_b[layer]),
                               ffn_w_gate[j].astype(bf16), ffn_w_up[j].astype(bf16),
                               ffn_w_down[j].astype(bf16), (ln2_g[layer], ln2_b[layer]))
        else:
            x1, route, counts = _mix_router(attn, ylru, h, w_o, g_attn[layer], ln1_g[layer],
                                            ln1_b[layer], router_w[j])
            h = _moe(x1, route, counts, moe_w_gate[j].astype(bf16), moe_w_up[j].astype(bf16),
                     moe_w_down[j].astype(bf16), ln2_g[layer], ln2_b[layer],
                     tm=EXPERT_ROWS, tf=EXPERT_FF_CHUNK)
    return h.reshape(batch, seq, D_MODEL).astype(x.dtype)
```

```python
import functools

import numpy as np
import jax
import jax.numpy as jnp
from jax import lax
from jax.experimental import pallas as pl
from jax.experimental.pallas import tpu as pltpu

D_MODEL = 1024
N_HEADS = 8
HEAD_DIM = 64
D_ATTN = N_HEADS * HEAD_DIM
D_LRU = 512
N_LRU_BLOCKS = 8
LRU_BLOCK = D_LRU // N_LRU_BLOCKS
CONV_WIDTH = 4
LRU_C = 8.0
DILATED_PATTERNS = ((128, 1), (512, 4), (2048, 16))
ATTN_BLOCK = 128
STAGE_DILATION = 4
N_BUCKETS = 32
MAX_DISTANCE = 2048
D_IN = 3 * D_ATTN + 2 * D_LRU
N_EXPERTS = 8
TOP_K = 2
DEPTH = 2
DEEPNORM_ALPHA = (2.0 * DEPTH) ** 0.25
LN_EPS = 1e-5
RMS_EPS = 1e-6
NEG_INF = -1e30
LOG2_E = float(np.log2(np.e))

LANES = 128
SUBLANES = 8
VMEM_LIMIT_BYTES = 56 * 1024 * 1024
MXU_DEPTH = 256

LRU_ROWS = 1024
ROUTER_ROWS = 1024
DENSE_FFN_ROWS = 1024
EXPERT_ROWS = 512
EXPERT_FF_CHUNK = 7 * MXU_DEPTH
DISPATCH_ROWS = 2048
COMBINE_ROWS = 256
DMA_UNROLL = 8
ATTN_UNROLL = 4

f32 = jnp.float32
bf16 = jnp.bfloat16


def _params(*semantics):
    return pltpu.CompilerParams(dimension_semantics=semantics,
                                vmem_limit_bytes=VMEM_LIMIT_BYTES)


def _t5_bucket(dist):
    max_exact = N_BUCKETS // 2
    d = np.maximum(dist, 1).astype(np.float32)
    large = max_exact + (np.log(d / max_exact) / np.log(MAX_DISTANCE / max_exact)
                         * (N_BUCKETS - max_exact)).astype(np.int32)
    large = np.minimum(large, N_BUCKETS - 1)
    return np.where(dist < max_exact, dist, large).astype(np.int32)


def _band_bias(rel_bias, dilation):
    nk = ATTN_BLOCK
    qi = np.arange(nk)[:, None]
    kj = np.arange(2 * nk)[None, :]
    delta = qi + nk - kj
    band = (delta >= 0) & (delta <= nk)
    bucket = _t5_bucket(np.clip(delta, 0, nk) * dilation)
    onehot = np.eye(N_BUCKETS, dtype=np.float32)[bucket.reshape(-1)]
    bias = jnp.dot(jnp.asarray(onehot), rel_bias.astype(f32), precision=lax.Precision.HIGHEST)
    bias = jnp.transpose(bias.reshape(nk, 2 * nk, N_HEADS), (2, 0, 1))
    valid = np.stack([band, band & (kj >= nk)])[:, None]
    bias = jnp.where(jnp.asarray(valid), bias[None], NEG_INF)
    return (bias * LOG2_E).reshape(2, N_HEADS // 2, 2 * nk, 2 * nk)


def _attn_kernel(q_ref, k_ref, v_ref, bias_ref, o_ref, q4, k4, v4, qs, ks, vs, s_buf, m_buf,
                 m_s, l_s, acc_s, *, seq):
    nk = ATTN_BLOCK
    n_blocks = seq // nk
    chunk = 2 * nk
    lane = lax.broadcasted_iota(jnp.int32, (1, LANES), 1)
    head0 = lane < HEAD_DIM

    ks[0:nk, :] = jnp.zeros((nk, LANES), bf16)
    vs[0:nk, 0:LANES] = jnp.zeros((nk, LANES), bf16)
    vs[:, LANES:2 * LANES] = jnp.ones((seq + nk, LANES), bf16)

    def stage(t, carry):
        chunks_per_residue = seq // STAGE_DILATION // chunk
        src = pl.ds(t // chunks_per_residue + STAGE_DILATION * chunk * (t % chunks_per_residue),
                    chunk, stride=STAGE_DILATION)
        dst = pl.ds(pl.multiple_of(t * chunk, chunk), chunk)
        q4[dst, :] = q_ref[src, :]
        k4[dst, :] = k_ref[src, :]
        v4[dst, :] = v_ref[src, :]
        return carry

    lax.fori_loop(0, seq // chunk, stage, 0)

    order = sorted(range(len(DILATED_PATTERNS)), key=lambda p: -DILATED_PATTERNS[p][1])
    assert DILATED_PATTERNS[order[-1]][1] == 1
    for p in order:
        d = DILATED_PATTERNS[p][1]
        is_first, is_last = p == order[0], p == order[-1]
        length = seq // d
        nb = length // nk
        chunks_per_residue = length // chunk

        def gather(t, carry):
            r = t // chunks_per_residue
            c = t % chunks_per_residue
            dst = pl.multiple_of(t * chunk, chunk)
            if d % STAGE_DILATION == 0:
                sub = d // STAGE_DILATION
                start = ((r % STAGE_DILATION) * (seq // STAGE_DILATION) + r // STAGE_DILATION
                         + sub * chunk * c)
                src = (pl.ds(start, chunk, stride=sub) if sub > 1
                       else pl.ds(pl.multiple_of(start, chunk), chunk))
                q, k, v = q4[src, :], k4[src, :], v4[src, :]
            else:
                src = pl.ds(r + d * chunk * c, chunk, stride=d) if d > 1 else pl.ds(dst, chunk)
                q, k, v = q_ref[src, :], k_ref[src, :], v_ref[src, :]
            q = q * (HEAD_DIM ** -0.5 * LOG2_E)
            q0 = jnp.where(head0, q, 0.0).astype(bf16)
            q1 = jnp.where(head0, 0.0, q).astype(bf16)
            for half in range(2):
                base = pl.multiple_of(2 * dst + half * chunk, chunk)
                qs[pl.ds(base, nk), :] = q0[half * nk:(half + 1) * nk]
                qs[pl.ds(base + nk, nk), :] = q1[half * nk:(half + 1) * nk]
            ks[pl.ds(nk + dst, chunk), :] = k.astype(bf16)
            vs[pl.ds(nk + dst, chunk), 0:LANES] = v.astype(bf16)
            return carry

        lax.fori_loop(0, seq // chunk, gather, 0)

        def scores(b, carry):
            first = jnp.asarray(b % nb == 0, jnp.int32)
            rows = pl.ds(pl.multiple_of(b * chunk, chunk), chunk)
            k = ks[pl.ds(pl.multiple_of(b * nk, nk), chunk), :]
            s_buf[rows, :] = lax.dot_general(qs[rows, :], k, (((1,), (1,)), ((), ())),
                                             preferred_element_type=f32) + bias_ref[p, first]
            return carry

        def rowmax(b, carry):
            rows = pl.ds(pl.multiple_of(b * chunk, chunk), chunk)
            m_buf[rows, :] = jnp.broadcast_to(jnp.max(s_buf[rows, :], axis=-1, keepdims=True),
                                              (chunk, LANES))
            return carry

        def block(b, carry):
            r = b // nb
            i = b % nb
            rows = pl.ds(pl.multiple_of(b * chunk, chunk), chunk)
            v = vs[pl.ds(pl.multiple_of(b * nk, nk), chunk), :]
            m = m_buf[rows, :]
            e = jnp.exp2(s_buf[rows, :] - jnp.concatenate([m, m], axis=1))
            pv = jnp.dot(e.astype(bf16), v, preferred_element_type=f32)
            m_blk = jnp.where(head0, m[0:nk], m[nk:chunk])
            l_blk = jnp.where(head0, pv[0:nk, LANES:], pv[nk:chunk, LANES:])
            pv_blk = jnp.where(head0, pv[0:nk, 0:LANES], pv[nk:chunk, 0:LANES])
            if d == 1:
                tok = pl.ds(pl.multiple_of(b * nk, nk), nk)
            else:
                tok = pl.ds(r + d * nk * i, nk, stride=d)
            if is_first:
                m_s[tok, :] = m_blk
                l_s[tok, :] = l_blk
                acc_s[tok, :] = pv_blk
                return carry
            m_old = m_s[tok, :]
            m_new = jnp.maximum(m_old, m_blk)
            w_old = jnp.exp2(m_old - m_new)
            w_blk = jnp.exp2(m_blk - m_new)
            l_new = w_old * l_s[tok, :] + w_blk * l_blk
            acc_new = w_old * acc_s[tok, :] + w_blk * pv_blk
            if is_last:
                o_ref[tok, :] = (acc_new / l_new).astype(o_ref.dtype)
            else:
                l_s[tok, :] = l_new
                acc_s[tok, :] = acc_new
                m_s[tok, :] = m_new
            return carry

        n_groups = n_blocks // ATTN_UNROLL
        assert n_groups >= 3

        def group(fn, g):
            for j in range(ATTN_UNROLL):
                fn(g * ATTN_UNROLL + j, 0)

        def pipelined(g, carry):
            group(block, g)
            group(rowmax, g + 1)
            group(scores, g + 2)
            return carry

        group(scores, 0)
        group(scores, 1)
        group(rowmax, 0)
        lax.fori_loop(0, n_groups - 2, pipelined, 0)
        group(block, n_groups - 2)
        group(rowmax, n_groups - 1)
        group(block, n_groups - 1)


def _attention(qkv, band_bias, batch, seq):
    view = qkv.reshape(batch, seq, 3 * D_ATTN)
    pairs = D_ATTN // LANES

    def spec(offset):
        return pl.BlockSpec((None, seq, LANES), lambda b, hp: (b, 0, offset * pairs + hp))

    n_pat = len(DILATED_PATTERNS)
    o = pl.pallas_call(
        functools.partial(_attn_kernel, seq=seq),
        grid=(batch, pairs),
        in_specs=[spec(0), spec(1), spec(2),
                  pl.BlockSpec((n_pat, 2, None, 2 * ATTN_BLOCK, 2 * ATTN_BLOCK),
                               lambda b, hp: (0, 0, hp, 0, 0))],
        out_specs=pl.BlockSpec((None, seq, LANES), lambda b, hp: (b, 0, hp)),
        out_shape=jax.ShapeDtypeStruct((batch, seq, D_ATTN), bf16),
        scratch_shapes=[pltpu.VMEM((seq, LANES), f32), pltpu.VMEM((seq, LANES), f32),
                        pltpu.VMEM((seq, LANES), f32),
                        pltpu.VMEM((2 * seq, LANES), bf16),
                        pltpu.VMEM((seq + ATTN_BLOCK, LANES), bf16),
                        pltpu.VMEM((seq + ATTN_BLOCK, 2 * LANES), bf16),
                        pltpu.VMEM((2 * seq, 2 * ATTN_BLOCK), f32),
                        pltpu.VMEM((2 * seq, LANES), f32),
                        pltpu.VMEM((seq, LANES), f32), pltpu.VMEM((seq, LANES), f32),
                        pltpu.VMEM((seq, LANES), f32)],
        compiler_params=_params("parallel", "parallel"),
        name="attention",
    )(view, view, view, band_bias)
    return o.reshape(batch * seq, D_ATTN)


def _gelu_tanh(x):
    return 0.5 * x * (1.0 + jnp.tanh(np.sqrt(2.0 / np.pi) * (x + 0.044715 * x * x * x)))


def _proj_lru_kernel(x_ref, w_ref, convw_ref, convb_ref, wg_ref, ba_ref, bx_ref, lam_ref, g_ref,
                     qkv_ref, y_ref, ubuf, hcarry, *, ts):
    pad = SUBLANES
    t = pl.program_id(1)
    n_qkv = qkv_ref.shape[1]
    xb = x_ref[...].astype(bf16)

    @pl.when(t == 0)
    def _():
        ubuf[0:pad, :] = jnp.zeros((pad, D_LRU), f32)
        hcarry[...] = jnp.zeros_like(hcarry)

    @pl.when(t > 0)
    def _():
        ubuf[0:pad, :] = ubuf[ts:ts + pad, :]

    ubuf[pad:pad + ts, :] = jnp.dot(xb, w_ref[:, n_qkv:n_qkv + D_LRU],
                                    preferred_element_type=f32)
    gate = jnp.dot(xb, w_ref[:, n_qkv + D_LRU:], preferred_element_type=f32)

    qkv_step = n_qkv // 3
    assert qkv_step == D_LRU

    def qkv_chunk(c):
        cols = slice(c * qkv_step, (c + 1) * qkv_step)
        p = jnp.dot(xb, w_ref[:, cols], preferred_element_type=f32)
        qkv_ref[:, cols] = p
        bits = lax.bitcast_convert_type(p, jnp.uint32).reshape(ts // SUBLANES, SUBLANES, qkv_step)
        folded = functools.reduce(jnp.bitwise_or, [bits[g] for g in range(ts // SUBLANES)])
        return ((folded >> 16) >> 16).astype(f32)

    u = convb_ref[...] + convw_ref[CONV_WIDTH - 1:CONV_WIDTH, :] * ubuf[pad:pad + ts, :]
    for w in range(CONV_WIDTH - 1):
        back = CONV_WIDTH - 1 - w
        u = u + convw_ref[w:w + 1, :] * ubuf[pad - back:pad - back + ts, :]

    gates = jnp.dot(u.astype(bf16), wg_ref[...], preferred_element_type=f32)
    r = jax.nn.sigmoid(gates[:, 0:D_LRU] + ba_ref[...])
    i = jax.nn.sigmoid(gates[:, D_LRU:2 * D_LRU] + bx_ref[...])
    neg_lam = -lam_ref[...]
    softplus = jnp.maximum(neg_lam, 0.0) + jnp.log1p(jnp.exp(-jnp.abs(neg_lam)))
    log_a = (-LRU_C) * r * softplus
    a = jnp.exp(log_a)
    b = jnp.sqrt(-jnp.tanh(log_a) * (1.0 + a * a)) * (i * u)
    zero0 = qkv_chunk(0)

    groups = ts // SUBLANES
    a = a.reshape(groups, SUBLANES, D_LRU)
    b = b.reshape(groups, SUBLANES, D_LRU) + zero0[None]
    sub = lax.broadcasted_iota(jnp.int32, (1, SUBLANES, 1), 1)
    shift = 1
    while shift < SUBLANES:
        live = sub >= shift
        a_prev = jnp.where(live, pltpu.roll(a, shift, 1), 1.0)
        b_prev = jnp.where(live, pltpu.roll(b, shift, 1), 0.0)
        b = a * b_prev + b
        a = a * a_prev
        shift *= 2
    h_prev = hcarry[...] + qkv_chunk(1)[0:1, :]
    h_groups = []
    for g in range(groups):
        h_g = a[g] * h_prev + b[g]
        h_prev = h_g[SUBLANES - 1:SUBLANES, :]
        h_groups.append(h_g)
    hcarry[...] = h_prev
    h = jnp.concatenate(h_groups, axis=0)
    zero2 = qkv_chunk(2)[0:1, :]

    y = _gelu_tanh(gate + zero2) * h
    y = y * lax.rsqrt(jnp.mean(y * y, axis=-1, keepdims=True) + RMS_EPS) * g_ref[...]
    y_ref[...] = y.astype(y_ref.dtype)


def _block_diag(w):
    g, i, j = w.shape
    eye = jnp.eye(g, dtype=w.dtype)
    return jnp.einsum('gij,gh->gihj', w, eye).reshape(g * i, g * j)


def _proj_lru(x2d, w_in_bf16, conv_w, conv_b, w_a, b_a, w_x, b_x, lam, g_lru, batch, seq,
              ts=LRU_ROWS):
    wg = jnp.concatenate([_block_diag(w_a), _block_diag(w_x)], axis=1).astype(bf16)
    row = lambda v: v.reshape(1, D_LRU).astype(f32)
    const = lambda shape: pl.BlockSpec(shape, lambda b, t: (0, 0))
    tile = lambda width: pl.BlockSpec((None, ts, width), lambda b, t: (b, t, 0))
    qkv, y = pl.pallas_call(
        functools.partial(_proj_lru_kernel, ts=ts),
        grid=(batch, seq // ts),
        in_specs=[tile(D_MODEL), const((D_MODEL, D_IN)),
                  const((CONV_WIDTH, D_LRU)), const((1, D_LRU)),
                  const((D_LRU, 2 * D_LRU)), const((1, D_LRU)), const((1, D_LRU)),
                  const((1, D_LRU)), const((1, D_LRU))],
        out_specs=[tile(3 * D_ATTN), tile(D_LRU)],
        out_shape=[jax.ShapeDtypeStruct((batch, seq, 3 * D_ATTN), f32),
                   jax.ShapeDtypeStruct((batch, seq, D_LRU), bf16)],
        scratch_shapes=[pltpu.VMEM((ts + 2 * SUBLANES, D_LRU), f32), pltpu.VMEM((1, D_LRU), f32)],
        compiler_params=_params("parallel", "arbitrary"),
        name="proj_rglru",
    )(x2d.reshape(batch, seq, D_MODEL), w_in_bf16,
      conv_w.reshape(CONV_WIDTH, D_LRU).astype(f32), row(conv_b), wg, row(b_a), row(b_x),
      row(lam), row(g_lru))
    return qkv.reshape(batch * seq, 3 * D_ATTN), y.reshape(batch * seq, D_LRU)


ROW_TILE = D_MODEL // LANES


def _store_row_tiles(ref, rows):
    t = rows.shape[0]
    for s in range(ROW_TILE):
        ref[pl.ds(s, t, stride=ROW_TILE), :] = rows[:, s * LANES:(s + 1) * LANES]


def _load_row_tiles(ref, t):
    return jnp.concatenate([ref[pl.ds(s, t, stride=ROW_TILE), :] for s in range(ROW_TILE)],
                           axis=-1)


def _layer_norm(z, g, b):
    mu = jnp.mean(z, axis=-1, keepdims=True)
    zc = z - mu
    var = jnp.mean(zc * zc, axis=-1, keepdims=True)
    return zc * lax.rsqrt(var + LN_EPS) * g + b


def _mixer_residual(attn_ref, ylru_ref, x_ref, w_ref, gattn, lng, lnb):
    attn = attn_ref[...].astype(f32)
    attn = attn * lax.rsqrt(jnp.mean(attn * attn, axis=-1, keepdims=True) + RMS_EPS) * gattn[...]
    y = jnp.dot(attn.astype(bf16), w_ref[0:D_ATTN, :], preferred_element_type=f32)
    y = y + jnp.dot(ylru_ref[...], w_ref[D_ATTN:, :], preferred_element_type=f32)
    return _layer_norm(DEEPNORM_ALPHA * x_ref[...] + y, lng[...], lnb[...])


def _swiglu(xb, wg, wu, wd):
    g = jnp.dot(xb, wg, preferred_element_type=f32)
    u = jnp.dot(xb, wu, preferred_element_type=f32)
    h = (g * jax.nn.sigmoid(g)) * u
    return jnp.dot(h.astype(bf16), wd, preferred_element_type=f32)


def _mix_ffn_kernel(attn_ref, ylru_ref, x_ref, wo_ref, gattn, ln1g, ln1b, wg_ref, wu_ref, wd_ref,
                    ln2g, ln2b, out_ref):
    x1 = _mixer_residual(attn_ref, ylru_ref, x_ref, wo_ref, gattn, ln1g, ln1b)
    f = _swiglu(x1.astype(bf16), wg_ref[...], wu_ref[...], wd_ref[...])
    out_ref[...] = _layer_norm(DEEPNORM_ALPHA * x1 + f, ln2g[...], ln2b[...])


def _mix_ffn_dense(attn, ylru, x2d, w_out_bf16, g_attn, ln1, w_gate, w_up, w_down, ln2,
                   tm=DENSE_FFN_ROWS):
    n = x2d.shape[0]
    d_ff = w_gate.shape[1]
    tile = lambda width: pl.BlockSpec((tm, width), lambda i: (i, 0))
    const = lambda shape: pl.BlockSpec(shape, lambda i: (0, 0), pipeline_mode=pl.Buffered(1))
    row = lambda v: v.reshape(1, -1).astype(f32)
    return pl.pallas_call(
        _mix_ffn_kernel,
        grid=(n // tm,),
        in_specs=[tile(D_ATTN), tile(D_LRU), tile(D_MODEL), const((D_MODEL, D_MODEL)),
                  const((1, D_ATTN)), const((1, D_MODEL)), const((1, D_MODEL)),
                  const((D_MODEL, d_ff)), const((D_MODEL, d_ff)), const((d_ff, D_MODEL)),
                  const((1, D_MODEL)), const((1, D_MODEL))],
        out_specs=tile(D_MODEL),
        out_shape=jax.ShapeDtypeStruct((n, D_MODEL), f32),
        compiler_params=_params("parallel"),
        name="mix_ffn_dense",
    )(attn, ylru, x2d, w_out_bf16, row(g_attn), row(ln1[0]), row(ln1[1]), w_gate, w_up, w_down,
      row(ln2[0]), row(ln2[1]))


def _mix_router_kernel(attn_ref, ylru, x_ref, w_ref, gattn, lng, lnb, rw_ref, tri_ref,
                       x1_ref, route_ref, count_ref):
    x1 = _mixer_residual(attn_ref, ylru, x_ref, w_ref, gattn, lng, lnb)
    x1_ref[...] = x1
    lane = lax.broadcasted_iota(jnp.int32, (1, LANES), 1).astype(f32)
    x_hi = x1.astype(bf16)
    x_lo = (x1 - x_hi.astype(f32)).astype(bf16)
    logits = (jnp.dot(x_hi, rw_ref[0], preferred_element_type=f32)
              + jnp.dot(x_lo, rw_ref[0], preferred_element_type=f32)
              + jnp.dot(x_hi, rw_ref[1], preferred_element_type=f32))
    logits = jnp.where(lane < N_EXPERTS, logits, -jnp.inf)
    v1 = jnp.max(logits, axis=-1, keepdims=True)
    i1 = jnp.min(jnp.where(logits == v1, lane, float(LANES)), axis=-1, keepdims=True)
    rest_logits = jnp.where(lane == i1, -jnp.inf, logits)
    v2 = jnp.max(rest_logits, axis=-1, keepdims=True)
    i2 = jnp.min(jnp.where(rest_logits == v2, lane, float(LANES)), axis=-1, keepdims=True)
    e2 = jnp.exp(v2 - v1)
    p1 = 1.0 / (1.0 + e2)
    p2 = e2 / (1.0 + e2)

    @pl.when(pl.program_id(0) == 0)
    def _():
        count_ref[...] = jnp.zeros_like(count_ref)

    tm = x1.shape[0]
    chosen = jnp.logical_or(lane == i1, lane == i2)
    rank = count_ref[...] + jnp.dot(tri_ref[...], chosen.astype(bf16),
                                    preferred_element_type=f32)
    count_ref[...] += jnp.sum(chosen.astype(f32), axis=0, keepdims=True)
    r1 = jnp.sum(jnp.where(lane == i1, rank, 0.0), axis=-1, keepdims=True)
    r2 = jnp.sum(jnp.where(lane == i2, rank, 0.0), axis=-1, keepdims=True)
    fields = (i1, i2, r1, r2, p1, p2)
    route = jnp.zeros((tm, LANES), f32)
    for k, val in enumerate(fields):
        route = jnp.where(lane == k, val, route)
    route_ref[...] = route


def _mix_router(attn, ylru, x2d, w_out_bf16, g_attn, ln_g, ln_b, router_w, tm=ROUTER_ROWS):
    n = x2d.shape[0]
    tile = lambda width: pl.BlockSpec((tm, width), lambda i: (i, 0))
    const = lambda shape: pl.BlockSpec(shape, lambda i: (0,) * len(shape))
    row = lambda v: v.reshape(1, -1).astype(f32)
    rw = jnp.zeros((D_MODEL, LANES), f32).at[:, :N_EXPERTS].set(router_w.astype(f32))
    rw_hi = rw.astype(bf16)
    rw_lo = (rw - rw_hi.astype(f32)).astype(bf16)
    strictly_lower = jnp.asarray(np.tri(tm, k=-1), bf16)
    return pl.pallas_call(
        _mix_router_kernel,
        grid=(n // tm,),
        in_specs=[tile(D_ATTN), tile(D_LRU), tile(D_MODEL), const((D_MODEL, D_MODEL)),
                  const((1, D_ATTN)), const((1, D_MODEL)), const((1, D_MODEL)),
                  const((2, D_MODEL, LANES)), const((tm, tm))],
        out_specs=[tile(D_MODEL), tile(LANES), const((1, LANES))],
        out_shape=[jax.ShapeDtypeStruct((n, D_MODEL), f32), jax.ShapeDtypeStruct((n, LANES), f32),
                   jax.ShapeDtypeStruct((1, LANES), f32)],
        compiler_params=_params("arbitrary"),
        name="mix_out_router",
    )(attn, ylru, x2d, w_out_bf16, row(g_attn), row(ln_g), row(ln_b), jnp.stack([rw_hi, rw_lo]),
      strictly_lower)


def _expert_ffn_kernel(tile_expert_ref, n_used_ref, x_ref, wg_ref, wu_ref, wd_ref, out_ref,
                       acc_ref, xb_ref, *, n_chunks):
    del tile_expert_ref
    j = pl.program_id(1)
    used = pl.program_id(0) < n_used_ref[0]
    last = n_chunks - 1
    assert n_chunks >= 2

    def chunk_out(xb):
        return _swiglu(xb, wg_ref[...].astype(bf16), wu_ref[...].astype(bf16), wd_ref[...])

    @pl.when(used & (j == 0))
    def _():
        xb = _load_row_tiles(x_ref, xb_ref.shape[0]).astype(bf16)
        xb_ref[...] = xb
        acc_ref[...] = chunk_out(xb)

    if n_chunks > 2:
        @pl.when(used & (j > 0) & (j < last))
        def _():
            acc_ref[...] += chunk_out(xb_ref[...])

    @pl.when(used & (j == last))
    def _():
        _store_row_tiles(out_ref, acc_ref[...] + chunk_out(xb_ref[...]))

    @pl.when(jnp.logical_not(used) & (j == last))
    def _():
        _store_row_tiles(out_ref, jnp.zeros(acc_ref.shape, f32))


def _expert_ffn(x_rows, tile_expert, n_used, w_gate, w_up, w_down, *, tm, tf):
    rows = x_rows.shape[0] // ROW_TILE
    n_chunks = w_gate.shape[2] // tf
    io_block = (tm * ROW_TILE, LANES)

    def chunk(i, j, nu):
        return jnp.where(i < nu[0], j, n_chunks - 1)

    return pl.pallas_call(
        functools.partial(_expert_ffn_kernel, n_chunks=n_chunks),
        grid_spec=pltpu.PrefetchScalarGridSpec(
            num_scalar_prefetch=2,
            grid=(rows // tm, n_chunks),
            in_specs=[pl.BlockSpec(io_block, lambda i, j, te, nu: (i, 0)),
                      pl.BlockSpec((None, D_MODEL, tf),
                                   lambda i, j, te, nu: (te[i], 0, chunk(i, j, nu))),
                      pl.BlockSpec((None, D_MODEL, tf),
                                   lambda i, j, te, nu: (te[i], 0, chunk(i, j, nu))),
                      pl.BlockSpec((None, tf, D_MODEL),
                                   lambda i, j, te, nu: (te[i], chunk(i, j, nu), 0))],
            out_specs=pl.BlockSpec(io_block, lambda i, j, te, nu: (i, 0)),
            scratch_shapes=[pltpu.VMEM((tm, D_MODEL), f32), pltpu.VMEM((tm, D_MODEL), bf16)]),
        out_shape=jax.ShapeDtypeStruct(x_rows.shape, f32),
        compiler_params=_params("parallel", "arbitrary"),
        name="ffn_experts",
    )(tile_expert, n_used, x_rows, w_gate, w_up, w_down)


def _dispatch_kernel(fill_ref, pos_ref, x_ref, xs_hbm, zero_buf, rows_buf, sem, *, tt, tm):
    step = pl.program_id(0)

    n_fill = fill_ref.shape[0] // 2

    @pl.when(step == 0)
    def _():
        zero_buf[...] = jnp.zeros_like(zero_buf)

        def fill_copy(f):
            start = pl.multiple_of(fill_ref[f] * ROW_TILE, ROW_TILE)
            return pltpu.make_async_copy(zero_buf, xs_hbm.at[pl.ds(start, tm * ROW_TILE)], sem)

        for f in range(n_fill):
            @pl.when(fill_ref[n_fill + f] > 0)
            def _():
                fill_copy(f).start()
        for f in range(n_fill):
            @pl.when(fill_ref[n_fill + f] > 0)
            def _():
                fill_copy(f).wait()

    _store_row_tiles(rows_buf, x_ref[...])

    def issue(t, carry):
        src = pl.multiple_of(t * ROW_TILE, ROW_TILE)
        for k in range(TOP_K):
            dst = pl.multiple_of(pos_ref[0, TOP_K * t + k] * ROW_TILE, ROW_TILE)
            pltpu.make_async_copy(rows_buf.at[pl.ds(src, ROW_TILE)],
                                  xs_hbm.at[pl.ds(dst, ROW_TILE)], sem).start(priority=k)
        return carry

    lax.fori_loop(0, tt, issue, 0, unroll=DMA_UNROLL)
    for _ in range(TOP_K):
        pltpu.make_async_copy(rows_buf, xs_hbm.at[pl.ds(0, tt * ROW_TILE)], sem).wait()


def _dispatch(x1, pos_blocks, fill, rows_sorted, *, tt, tm):
    n = x1.shape[0]
    return pl.pallas_call(
        functools.partial(_dispatch_kernel, tt=tt, tm=tm),
        grid_spec=pltpu.PrefetchScalarGridSpec(
            num_scalar_prefetch=1,
            grid=(n // tt,),
            in_specs=[pl.BlockSpec((None, 1, TOP_K * tt), lambda i, fill: (i, 0, 0),
                                   memory_space=pltpu.SMEM),
                      pl.BlockSpec((tt, D_MODEL), lambda i, fill: (i, 0))],
            out_specs=pl.BlockSpec(memory_space=pl.ANY),
            scratch_shapes=[pltpu.VMEM((tm * ROW_TILE, LANES), f32),
                            pltpu.VMEM((tt * ROW_TILE, LANES), f32),
                            pltpu.SemaphoreType.DMA(())]),
        out_shape=jax.ShapeDtypeStruct((rows_sorted * ROW_TILE, LANES), f32),
        compiler_params=_params("arbitrary"),
        name="moe_dispatch",
    )(fill, pos_blocks, x1)


def _combine_kernel(pos_ref, pos_next_ref, route_ref, x1_ref, ys_hbm, lng, lnb, out_ref, buf, sems,
                    *, tt, n_steps):
    step = pl.program_id(0)
    slot = step % 2

    def issue_tile(tile_pos_ref, into):
        def issue(t, carry):
            dst = pl.multiple_of(t * ROW_TILE, ROW_TILE)
            for k in range(TOP_K):
                src = pl.multiple_of(tile_pos_ref[0, TOP_K * t + k] * ROW_TILE, ROW_TILE)
                pltpu.make_async_copy(ys_hbm.at[pl.ds(src, ROW_TILE)],
                                      buf.at[into, k, pl.ds(dst, ROW_TILE)],
                                      sems.at[into]).start(priority=k)
            return carry

        lax.fori_loop(0, tt, issue, 0, unroll=DMA_UNROLL)

    @pl.when(step == 0)
    def _():
        issue_tile(pos_ref, 0)

    @pl.when(step + 1 < n_steps)
    def _():
        issue_tile(pos_next_ref, 1 - slot)

    for k in range(TOP_K):
        pltpu.make_async_copy(ys_hbm.at[pl.ds(0, tt * ROW_TILE)], buf.at[slot, k],
                              sems.at[slot]).wait()

    lane = lax.broadcasted_iota(jnp.int32, (1, LANES), 1)
    route = route_ref[...]
    p1 = jnp.sum(jnp.where(lane == 4, route, 0.0), axis=-1, keepdims=True)
    p2 = jnp.sum(jnp.where(lane == 5, route, 0.0), axis=-1, keepdims=True)
    y = p1 * _load_row_tiles(buf.at[slot, 0], tt) + p2 * _load_row_tiles(buf.at[slot, 1], tt)
    out_ref[...] = _layer_norm(DEEPNORM_ALPHA * x1_ref[...] + y, lng[...], lnb[...])


def _combine(pos_blocks, route, x1, y_sorted, ln_g, ln_b, *, tt):
    n = x1.shape[0]
    n_steps = n // tt
    pos_spec = lambda index: pl.BlockSpec((None, 1, TOP_K * tt), index, memory_space=pltpu.SMEM)
    return pl.pallas_call(
        functools.partial(_combine_kernel, tt=tt, n_steps=n_steps),
        grid=(n_steps,),
        in_specs=[pos_spec(lambda i: (i, 0, 0)),
                  pos_spec(lambda i: (jnp.minimum(i + 1, n_steps - 1), 0, 0)),
                  pl.BlockSpec((tt, LANES), lambda i: (i, 0)),
                  pl.BlockSpec((tt, D_MODEL), lambda i: (i, 0)),
                  pl.BlockSpec(memory_space=pl.ANY),
                  pl.BlockSpec((1, D_MODEL), lambda i: (0, 0)),
                  pl.BlockSpec((1, D_MODEL), lambda i: (0, 0))],
        out_specs=pl.BlockSpec((tt, D_MODEL), lambda i: (i, 0)),
        out_shape=jax.ShapeDtypeStruct((n, D_MODEL), f32),
        scratch_shapes=[pltpu.VMEM((2, TOP_K, tt * ROW_TILE, LANES), f32),
                        pltpu.SemaphoreType.DMA((2,))],
        compiler_params=_params("arbitrary"),
        name="moe_combine",
    )(pos_blocks, pos_blocks, route, x1, y_sorted, ln_g.reshape(1, D_MODEL).astype(f32),
      ln_b.reshape(1, D_MODEL).astype(f32))


def _moe(x1, route, counts, w_gate, w_up, w_down, ln_g, ln_b, *, tm, tf):
    n = x1.shape[0]
    i32 = jnp.int32
    counts = counts[0, :N_EXPERTS].astype(i32)
    padded = (counts + tm - 1) // tm * tm
    ends = jnp.cumsum(padded)
    offsets = ends - padded
    experts = route[:, 0:TOP_K].astype(i32)
    ranks = route[:, TOP_K:2 * TOP_K].astype(i32)
    pos = offsets[experts] + ranks
    pos_blocks = lambda tt: pos.reshape(n // tt, 1, TOP_K * tt)
    n_tiles = TOP_K * n // tm + N_EXPERTS
    n_used = (ends[-1] // tm).astype(i32).reshape(1)
    tile_ids = jnp.arange(n_tiles, dtype=i32)
    tile_expert = jnp.sum((tile_ids[:, None] >= (ends // tm)[None, :]).astype(i32), axis=1)
    last_expert = jnp.max(jnp.where(counts > 0, jnp.arange(N_EXPERTS, dtype=i32), 0))
    tile_expert = jnp.minimum(tile_expert, last_expert).astype(i32)
    tail_tiles = n_used[0] + jnp.arange(N_EXPERTS, dtype=i32)
    fill = jnp.concatenate([ends - tm, tail_tiles * tm,
                            (counts > 0).astype(i32), (tail_tiles < n_tiles).astype(i32)]).astype(i32)

    x_sorted = _dispatch(x1, pos_blocks(DISPATCH_ROWS), fill, n_tiles * tm, tt=DISPATCH_ROWS,
                         tm=tm)
    y_sorted = _expert_ffn(x_sorted, tile_expert, n_used, w_gate, w_up, w_down, tm=tm, tf=tf)
    return _combine(pos_blocks(COMBINE_ROWS), route, x1, y_sorted, ln_g, ln_b, tt=COMBINE_ROWS)


def kernel(x, w_in, conv_w, conv_b, w_a, b_a, w_x, b_x, lru_lambda, rel_bias, g_attn, g_lru, w_out, ln1_g, ln1_b, ln2_g, ln2_b, ffn_w_gate, ffn_w_up, ffn_w_down, router_w, moe_w_gate, moe_w_up, moe_w_down):
    batch, seq, _ = x.shape
    n = batch * seq
    h = x.reshape(n, D_MODEL).astype(f32)
    band_bias = jnp.stack([_band_bias(rel_bias, d) for _, d in DILATED_PATTERNS])
    for layer in range(DEPTH):
        qkv, ylru = _proj_lru(h, w_in[layer].astype(bf16), conv_w[layer], conv_b[layer],
                              w_a[layer], b_a[layer], w_x[layer], b_x[layer], lru_lambda[layer],
                              g_lru[layer], batch, seq)
        attn = _attention(qkv, band_bias, batch, seq)
        j = layer // 2
        w_o = w_out[layer].astype(bf16)
        if layer % 2 == 0:
            h = _mix_ffn_dense(attn, ylru, h, w_o, g_attn[layer], (ln1_g[layer], ln1_b[layer]),
                               ffn_w_gate[j].astype(bf16), ffn_w_up[j].astype(bf16),
                               ffn_w_down[j].astype(bf16), (ln2_g[layer], ln2_b[layer]))
        else:
            x1, route, counts = _mix_router(attn, ylru, h, w_o, g_attn[layer], ln1_g[layer],
                                            ln1_b[layer], router_w[j])
            h = _moe(x1, route, counts, moe_w_gate[j], moe_w_up[j],
                     moe_w_down[j].astype(bf16), ln2_g[layer], ln2_b[layer],
                     tm=EXPERT_ROWS, tf=EXPERT_FF_CHUNK)
    return h.reshape(batch, seq, D_MODEL).astype(x.dtype)
```

```python
import functools

import numpy as np
import jax
import jax.numpy as jnp
from jax import lax
from jax.experimental import pallas as pl
from jax.experimental.pallas import tpu as pltpu

D_MODEL = 1024
N_HEADS = 8
HEAD_DIM = 64
D_ATTN = N_HEADS * HEAD_DIM
D_LRU = 512
N_LRU_BLOCKS = 8
LRU_BLOCK = D_LRU // N_LRU_BLOCKS
CONV_WIDTH = 4
LRU_C = 8.0
DILATED_PATTERNS = ((128, 1), (512, 4), (2048, 16))
ATTN_BLOCK = 128
STAGE_DILATION = 4
N_BUCKETS = 32
MAX_DISTANCE = 2048
D_IN = 3 * D_ATTN + 2 * D_LRU
N_EXPERTS = 8
TOP_K = 2
DEPTH = 2
DEEPNORM_ALPHA = (2.0 * DEPTH) ** 0.25
LN_EPS = 1e-5
RMS_EPS = 1e-6
NEG_INF = -1e30
LOG2_E = float(np.log2(np.e))

LANES = 128
SUBLANES = 8
VMEM_LIMIT_BYTES = 56 * 1024 * 1024
MXU_DEPTH = 256

LRU_ROWS = 1024
ROUTER_ROWS = 1024
DENSE_FFN_ROWS = 1024
EXPERT_ROWS = 512
EXPERT_FF_CHUNK = 7 * MXU_DEPTH
DISPATCH_ROWS = 2048
COMBINE_ROWS = 256
DMA_UNROLL = 8
ATTN_UNROLL = 4

f32 = jnp.float32
bf16 = jnp.bfloat16


def _params(*semantics):
    return pltpu.CompilerParams(dimension_semantics=semantics,
                                vmem_limit_bytes=VMEM_LIMIT_BYTES)


def _t5_bucket(dist):
    max_exact = N_BUCKETS // 2
    d = np.maximum(dist, 1).astype(np.float32)
    large = max_exact + (np.log(d / max_exact) / np.log(MAX_DISTANCE / max_exact)
                         * (N_BUCKETS - max_exact)).astype(np.int32)
    large = np.minimum(large, N_BUCKETS - 1)
    return np.where(dist < max_exact, dist, large).astype(np.int32)


def _band_bias(rel_bias, dilation):
    nk = ATTN_BLOCK
    qi = np.arange(nk)[:, None]
    kj = np.arange(2 * nk)[None, :]
    delta = qi + nk - kj
    band = (delta >= 0) & (delta <= nk)
    bucket = _t5_bucket(np.clip(delta, 0, nk) * dilation)
    onehot = np.eye(N_BUCKETS, dtype=np.float32)[bucket.reshape(-1)]
    bias = jnp.dot(jnp.asarray(onehot), rel_bias.astype(f32), precision=lax.Precision.HIGHEST)
    bias = jnp.transpose(bias.reshape(nk, 2 * nk, N_HEADS), (2, 0, 1))
    valid = np.stack([band, band & (kj >= nk)])[:, None]
    bias = jnp.where(jnp.asarray(valid), bias[None], NEG_INF)
    return (bias * LOG2_E).reshape(2, N_HEADS // 2, 2 * nk, 2 * nk)


def _attn_kernel(q_ref, k_ref, v_ref, bias_ref, o_ref, q4, k4, v4, qs, ks, vs, s_buf, m_buf,
                 m_s, l_s, acc_s, *, seq):
    nk = ATTN_BLOCK
    n_blocks = seq // nk
    chunk = 2 * nk
    lane = lax.broadcasted_iota(jnp.int32, (1, LANES), 1)
    head0 = lane < HEAD_DIM

    ks[0:nk, :] = jnp.zeros((nk, LANES), bf16)
    vs[0:nk, 0:LANES] = jnp.zeros((nk, LANES), bf16)
    vs[:, LANES:2 * LANES] = jnp.ones((seq + nk, LANES), bf16)

    def stage(t, carry):
        chunks_per_residue = seq // STAGE_DILATION // chunk
        src = pl.ds(t // chunks_per_residue + STAGE_DILATION * chunk * (t % chunks_per_residue),
                    chunk, stride=STAGE_DILATION)
        dst = pl.ds(pl.multiple_of(t * chunk, chunk), chunk)
        q4[dst, :] = q_ref[src, :]
        k4[dst, :] = k_ref[src, :]
        v4[dst, :] = v_ref[src, :]
        return carry

    lax.fori_loop(0, seq // chunk, stage, 0)

    order = sorted(range(len(DILATED_PATTERNS)), key=lambda p: -DILATED_PATTERNS[p][1])
    assert DILATED_PATTERNS[order[-1]][1] == 1
    for p in order:
        d = DILATED_PATTERNS[p][1]
        is_first, is_last = p == order[0], p == order[-1]
        length = seq // d
        nb = length // nk
        chunks_per_residue = length // chunk

        def gather(t, carry):
            r = t // chunks_per_residue
            c = t % chunks_per_residue
            dst = pl.multiple_of(t * chunk, chunk)
            if d % STAGE_DILATION == 0:
                sub = d // STAGE_DILATION
                start = ((r % STAGE_DILATION) * (seq // STAGE_DILATION) + r // STAGE_DILATION
                         + sub * chunk * c)
                src = (pl.ds(start, chunk, stride=sub) if sub > 1
                       else pl.ds(pl.multiple_of(start, chunk), chunk))
                q, k, v = q4[src, :], k4[src, :], v4[src, :]
            else:
                src = pl.ds(r + d * chunk * c, chunk, stride=d) if d > 1 else pl.ds(dst, chunk)
                q, k, v = q_ref[src, :], k_ref[src, :], v_ref[src, :]
            q = q * (HEAD_DIM ** -0.5 * LOG2_E)
            q0 = jnp.where(head0, q, 0.0).astype(bf16)
            q1 = jnp.where(head0, 0.0, q).astype(bf16)
            for half in range(2):
                base = pl.multiple_of(2 * dst + half * chunk, chunk)
                qs[pl.ds(base, nk), :] = q0[half * nk:(half + 1) * nk]
                qs[pl.ds(base + nk, nk), :] = q1[half * nk:(half + 1) * nk]
            ks[pl.ds(nk + dst, chunk), :] = k.astype(bf16)
            vs[pl.ds(nk + dst, chunk), 0:LANES] = v.astype(bf16)
            return carry

        lax.fori_loop(0, seq // chunk, gather, 0)

        def scores(b, carry):
            first = jnp.asarray(b % nb == 0, jnp.int32)
            rows = pl.ds(pl.multiple_of(b * chunk, chunk), chunk)
            k = ks[pl.ds(pl.multiple_of(b * nk, nk), chunk), :]
            s_buf[rows, :] = lax.dot_general(qs[rows, :], k, (((1,), (1,)), ((), ())),
                                             preferred_element_type=f32) + bias_ref[p, first]
            return carry

        def rowmax(b, carry):
            rows = pl.ds(pl.multiple_of(b * chunk, chunk), chunk)
            m_buf[rows, :] = jnp.broadcast_to(jnp.max(s_buf[rows, :], axis=-1, keepdims=True),
                                              (chunk, LANES))
            return carry

        def block(b, carry):
            r = b // nb
            i = b % nb
            rows = pl.ds(pl.multiple_of(b * chunk, chunk), chunk)
            v = vs[pl.ds(pl.multiple_of(b * nk, nk), chunk), :]
            m = m_buf[rows, :]
            e = jnp.exp2(s_buf[rows, :] - jnp.concatenate([m, m], axis=1))
            pv = jnp.dot(e.astype(bf16), v, preferred_element_type=f32)
            m_blk = jnp.where(head0, m[0:nk], m[nk:chunk])
            l_blk = jnp.where(head0, pv[0:nk, LANES:], pv[nk:chunk, LANES:])
            pv_blk = jnp.where(head0, pv[0:nk, 0:LANES], pv[nk:chunk, 0:LANES])
            if d == 1:
                tok = pl.ds(pl.multiple_of(b * nk, nk), nk)
            else:
                tok = pl.ds(r + d * nk * i, nk, stride=d)
            if is_first:
                m_s[tok, :] = m_blk
                l_s[tok, :] = l_blk
                acc_s[tok, :] = pv_blk
                return carry
            m_old = m_s[tok, :]
            m_new = jnp.maximum(m_old, m_blk)
            w_old = jnp.exp2(m_old - m_new)
            w_blk = jnp.exp2(m_blk - m_new)
            l_new = w_old * l_s[tok, :] + w_blk * l_blk
            acc_new = w_old * acc_s[tok, :] + w_blk * pv_blk
            if is_last:
                o_ref[tok, :] = (acc_new / l_new).astype(o_ref.dtype)
            else:
                l_s[tok, :] = l_new
                acc_s[tok, :] = acc_new
                m_s[tok, :] = m_new
            return carry

        n_groups = n_blocks // ATTN_UNROLL
        assert n_groups >= 3

        def group(fn, g):
            for j in range(ATTN_UNROLL):
                fn(g * ATTN_UNROLL + j, 0)

        def pipelined(g, carry):
            group(block, g)
            group(rowmax, g + 1)
            group(scores, g + 2)
            return carry

        group(scores, 0)
        group(scores, 1)
        group(rowmax, 0)
        lax.fori_loop(0, n_groups - 2, pipelined, 0)
        group(block, n_groups - 2)
        group(rowmax, n_groups - 1)
        group(block, n_groups - 1)


def _attention(qkv, band_bias, batch, seq):
    view = qkv.reshape(batch, seq, 3 * D_ATTN)
    pairs = D_ATTN // LANES

    def spec(offset):
        return pl.BlockSpec((None, seq, LANES), lambda b, hp: (b, 0, offset * pairs + hp))

    n_pat = len(DILATED_PATTERNS)
    o = pl.pallas_call(
        functools.partial(_attn_kernel, seq=seq),
        grid=(batch, pairs),
        in_specs=[spec(0), spec(1), spec(2),
                  pl.BlockSpec((n_pat, 2, None, 2 * ATTN_BLOCK, 2 * ATTN_BLOCK),
                               lambda b, hp: (0, 0, hp, 0, 0))],
        out_specs=pl.BlockSpec((None, seq, LANES), lambda b, hp: (b, 0, hp)),
        out_shape=jax.ShapeDtypeStruct((batch, seq, D_ATTN), bf16),
        scratch_shapes=[pltpu.VMEM((seq, LANES), f32), pltpu.VMEM((seq, LANES), f32),
                        pltpu.VMEM((seq, LANES), f32),
                        pltpu.VMEM((2 * seq, LANES), bf16),
                        pltpu.VMEM((seq + ATTN_BLOCK, LANES), bf16),
                        pltpu.VMEM((seq + ATTN_BLOCK, 2 * LANES), bf16),
                        pltpu.VMEM((2 * seq, 2 * ATTN_BLOCK), f32),
                        pltpu.VMEM((2 * seq, LANES), f32),
                        pltpu.VMEM((seq, LANES), f32), pltpu.VMEM((seq, LANES), f32),
                        pltpu.VMEM((seq, LANES), f32)],
        compiler_params=_params("parallel", "parallel"),
        name="attention",
    )(view, view, view, band_bias)
    return o.reshape(batch * seq, D_ATTN)


def _gelu_tanh(x):
    return 0.5 * x * (1.0 + jnp.tanh(np.sqrt(2.0 / np.pi) * (x + 0.044715 * x * x * x)))


def _proj_lru_kernel(x_ref, w_ref, convw_ref, convb_ref, wg_ref, ba_ref, bx_ref, lam_ref, g_ref,
                     qkv_ref, y_ref, ubuf, hcarry, *, ts):
    pad = SUBLANES
    t = pl.program_id(1)
    n_qkv = qkv_ref.shape[1]
    xb = x_ref[...].astype(bf16)

    @pl.when(t == 0)
    def _():
        ubuf[0:pad, :] = jnp.zeros((pad, D_LRU), f32)
        hcarry[...] = jnp.zeros_like(hcarry)

    @pl.when(t > 0)
    def _():
        ubuf[0:pad, :] = ubuf[ts:ts + pad, :]

    ubuf[pad:pad + ts, :] = jnp.dot(xb, w_ref[:, n_qkv:n_qkv + D_LRU],
                                    preferred_element_type=f32)
    gate = jnp.dot(xb, w_ref[:, n_qkv + D_LRU:], preferred_element_type=f32)

    qkv_step = n_qkv // 3
    assert qkv_step == D_LRU

    def qkv_chunk(c):
        cols = slice(c * qkv_step, (c + 1) * qkv_step)
        p = jnp.dot(xb, w_ref[:, cols], preferred_element_type=f32)
        qkv_ref[:, cols] = p
        bits = lax.bitcast_convert_type(p, jnp.uint32).reshape(ts // SUBLANES, SUBLANES, qkv_step)
        folded = functools.reduce(jnp.bitwise_or, [bits[g] for g in range(ts // SUBLANES)])
        return ((folded >> 16) >> 16).astype(f32)

    u = convb_ref[...] + convw_ref[CONV_WIDTH - 1:CONV_WIDTH, :] * ubuf[pad:pad + ts, :]
    for w in range(CONV_WIDTH - 1):
        back = CONV_WIDTH - 1 - w
        u = u + convw_ref[w:w + 1, :] * ubuf[pad - back:pad - back + ts, :]

    gates = jnp.dot(u.astype(bf16), wg_ref[...], preferred_element_type=f32)
    r = jax.nn.sigmoid(gates[:, 0:D_LRU] + ba_ref[...])
    i = jax.nn.sigmoid(gates[:, D_LRU:2 * D_LRU] + bx_ref[...])
    neg_lam = -lam_ref[...]
    softplus = jnp.maximum(neg_lam, 0.0) + jnp.log1p(jnp.exp(-jnp.abs(neg_lam)))
    log_a = (-LRU_C) * r * softplus
    a = jnp.exp(log_a)
    b = jnp.sqrt(-jnp.tanh(log_a) * (1.0 + a * a)) * (i * u)
    zero0 = qkv_chunk(0)

    groups = ts // SUBLANES
    a = a.reshape(groups, SUBLANES, D_LRU)
    b = b.reshape(groups, SUBLANES, D_LRU) + zero0[None]
    sub = lax.broadcasted_iota(jnp.int32, (1, SUBLANES, 1), 1)
    shift = 1
    while shift < SUBLANES:
        live = sub >= shift
        a_prev = jnp.where(live, pltpu.roll(a, shift, 1), 1.0)
        b_prev = jnp.where(live, pltpu.roll(b, shift, 1), 0.0)
        b = a * b_prev + b
        a = a * a_prev
        shift *= 2
    h_prev = hcarry[...] + qkv_chunk(1)[0:1, :]
    h_groups = []
    for g in range(groups):
        h_g = a[g] * h_prev + b[g]
        h_prev = h_g[SUBLANES - 1:SUBLANES, :]
        h_groups.append(h_g)
    hcarry[...] = h_prev
    h = jnp.concatenate(h_groups, axis=0)
    zero2 = qkv_chunk(2)[0:1, :]

    y = _gelu_tanh(gate + zero2) * h
    y = y * lax.rsqrt(jnp.mean(y * y, axis=-1, keepdims=True) + RMS_EPS) * g_ref[...]
    y_ref[...] = y.astype(y_ref.dtype)


def _block_diag(w):
    g, i, j = w.shape
    eye = jnp.eye(g, dtype=w.dtype)
    return jnp.einsum('gij,gh->gihj', w, eye).reshape(g * i, g * j)


def _proj_lru(x2d, w_in_bf16, conv_w, conv_b, w_a, b_a, w_x, b_x, lam, g_lru, batch, seq,
              ts=LRU_ROWS):
    wg = jnp.concatenate([_block_diag(w_a), _block_diag(w_x)], axis=1).astype(bf16)
    row = lambda v: v.reshape(1, D_LRU).astype(f32)
    const = lambda shape: pl.BlockSpec(shape, lambda b, t: (0, 0))
    tile = lambda width: pl.BlockSpec((None, ts, width), lambda b, t: (b, t, 0))
    qkv, y = pl.pallas_call(
        functools.partial(_proj_lru_kernel, ts=ts),
        grid=(batch, seq // ts),
        in_specs=[tile(D_MODEL), const((D_MODEL, D_IN)),
                  const((CONV_WIDTH, D_LRU)), const((1, D_LRU)),
                  const((D_LRU, 2 * D_LRU)), const((1, D_LRU)), const((1, D_LRU)),
                  const((1, D_LRU)), const((1, D_LRU))],
        out_specs=[tile(3 * D_ATTN), tile(D_LRU)],
        out_shape=[jax.ShapeDtypeStruct((batch, seq, 3 * D_ATTN), f32),
                   jax.ShapeDtypeStruct((batch, seq, D_LRU), bf16)],
        scratch_shapes=[pltpu.VMEM((ts + 2 * SUBLANES, D_LRU), f32), pltpu.VMEM((1, D_LRU), f32)],
        compiler_params=_params("parallel", "arbitrary"),
        name="proj_rglru",
    )(x2d.reshape(batch, seq, D_MODEL), w_in_bf16,
      conv_w.reshape(CONV_WIDTH, D_LRU).astype(f32), row(conv_b), wg, row(b_a), row(b_x),
      row(lam), row(g_lru))
    return qkv.reshape(batch * seq, 3 * D_ATTN), y.reshape(batch * seq, D_LRU)


ROW_TILE = D_MODEL // LANES


def _store_row_tiles(ref, rows):
    t = rows.shape[0]
    for s in range(ROW_TILE):
        ref[pl.ds(s, t, stride=ROW_TILE), :] = rows[:, s * LANES:(s + 1) * LANES]


def _load_row_tiles(ref, t):
    return jnp.concatenate([ref[pl.ds(s, t, stride=ROW_TILE), :] for s in range(ROW_TILE)],
                           axis=-1)


def _layer_norm(z, g, b):
    mu = jnp.mean(z, axis=-1, keepdims=True)
    zc = z - mu
    var = jnp.mean(zc * zc, axis=-1, keepdims=True)
    return zc * lax.rsqrt(var + LN_EPS) * g + b


def _mixer_residual(attn_ref, ylru_ref, x_ref, w_ref, gattn, lng, lnb):
    attn = attn_ref[...].astype(f32)
    attn = attn * lax.rsqrt(jnp.mean(attn * attn, axis=-1, keepdims=True) + RMS_EPS) * gattn[...]
    y = jnp.dot(attn.astype(bf16), w_ref[0:D_ATTN, :], preferred_element_type=f32)
    y = y + jnp.dot(ylru_ref[...], w_ref[D_ATTN:, :], preferred_element_type=f32)
    return _layer_norm(DEEPNORM_ALPHA * x_ref[...] + y, lng[...], lnb[...])


def _swiglu(xb, wg, wu, wd):
    g = jnp.dot(xb, wg, preferred_element_type=f32)
    u = jnp.dot(xb, wu, preferred_element_type=f32)
    h = (g * jax.nn.sigmoid(g)) * u
    return jnp.dot(h.astype(bf16), wd, preferred_element_type=f32)


def _mix_ffn_kernel(attn_ref, ylru_ref, x_ref, wo_ref, gattn, ln1g, ln1b, wg_ref, wu_ref, wd_ref,
                    ln2g, ln2b, out_ref):
    x1 = _mixer_residual(attn_ref, ylru_ref, x_ref, wo_ref, gattn, ln1g, ln1b)
    f = _swiglu(x1.astype(bf16), wg_ref[...], wu_ref[...], wd_ref[...])
    out_ref[...] = _layer_norm(DEEPNORM_ALPHA * x1 + f, ln2g[...], ln2b[...])


def _mix_ffn_dense(attn, ylru, x2d, w_out_bf16, g_attn, ln1, w_gate, w_up, w_down, ln2,
                   tm=DENSE_FFN_ROWS):
    n = x2d.shape[0]
    d_ff = w_gate.shape[1]
    tile = lambda width: pl.BlockSpec((tm, width), lambda i: (i, 0))
    const = lambda shape: pl.BlockSpec(shape, lambda i: (0, 0), pipeline_mode=pl.Buffered(1))
    row = lambda v: v.reshape(1, -1).astype(f32)
    return pl.pallas_call(
        _mix_ffn_kernel,
        grid=(n // tm,),
        in_specs=[tile(D_ATTN), tile(D_LRU), tile(D_MODEL), const((D_MODEL, D_MODEL)),
                  const((1, D_ATTN)), const((1, D_MODEL)), const((1, D_MODEL)),
                  const((D_MODEL, d_ff)), const((D_MODEL, d_ff)), const((d_ff, D_MODEL)),
                  const((1, D_MODEL)), const((1, D_MODEL))],
        out_specs=tile(D_MODEL),
        out_shape=jax.ShapeDtypeStruct((n, D_MODEL), f32),
        compiler_params=_params("parallel"),
        name="mix_ffn_dense",
    )(attn, ylru, x2d, w_out_bf16, row(g_attn), row(ln1[0]), row(ln1[1]), w_gate, w_up, w_down,
      row(ln2[0]), row(ln2[1]))


def _mix_router_kernel(attn_ref, ylru, x_ref, w_ref, gattn, lng, lnb, rw_ref, tri_ref,
                       x1_ref, route_ref, count_ref):
    x1 = _mixer_residual(attn_ref, ylru, x_ref, w_ref, gattn, lng, lnb)
    x1_ref[...] = x1
    lane = lax.broadcasted_iota(jnp.int32, (1, LANES), 1).astype(f32)
    x_hi = x1.astype(bf16)
    x_lo = (x1 - x_hi.astype(f32)).astype(bf16)
    logits = (jnp.dot(x_hi, rw_ref[0], preferred_element_type=f32)
              + jnp.dot(x_lo, rw_ref[0], preferred_element_type=f32)
              + jnp.dot(x_hi, rw_ref[1], preferred_element_type=f32))
    logits = jnp.where(lane < N_EXPERTS, logits, -jnp.inf)
    v1 = jnp.max(logits, axis=-1, keepdims=True)
    i1 = jnp.min(jnp.where(logits == v1, lane, float(LANES)), axis=-1, keepdims=True)
    rest_logits = jnp.where(lane == i1, -jnp.inf, logits)
    v2 = jnp.max(rest_logits, axis=-1, keepdims=True)
    i2 = jnp.min(jnp.where(rest_logits == v2, lane, float(LANES)), axis=-1, keepdims=True)
    e2 = jnp.exp(v2 - v1)
    p1 = 1.0 / (1.0 + e2)
    p2 = e2 / (1.0 + e2)

    @pl.when(pl.program_id(0) == 0)
    def _():
        count_ref[...] = jnp.zeros_like(count_ref)

    tm = x1.shape[0]
    chosen = jnp.logical_or(lane == i1, lane == i2)
    rank = count_ref[...] + jnp.dot(tri_ref[...], chosen.astype(bf16),
                                    preferred_element_type=f32)
    count_ref[...] += jnp.sum(chosen.astype(f32), axis=0, keepdims=True)
    r1 = jnp.sum(jnp.where(lane == i1, rank, 0.0), axis=-1, keepdims=True)
    r2 = jnp.sum(jnp.where(lane == i2, rank, 0.0), axis=-1, keepdims=True)
    fields = (i1, i2, r1, r2, p1, p2)
    route = jnp.zeros((tm, LANES), f32)
    for k, val in enumerate(fields):
        route = jnp.where(lane == k, val, route)
    route_ref[...] = route


def _mix_router(attn, ylru, x2d, w_out_bf16, g_attn, ln_g, ln_b, router_w, tm=ROUTER_ROWS):
    n = x2d.shape[0]
    tile = lambda width: pl.BlockSpec((tm, width), lambda i: (i, 0))
    const = lambda shape: pl.BlockSpec(shape, lambda i: (0,) * len(shape))
    row = lambda v: v.reshape(1, -1).astype(f32)
    rw = jnp.zeros((D_MODEL, LANES), f32).at[:, :N_EXPERTS].set(router_w.astype(f32))
    rw_hi = rw.astype(bf16)
    rw_lo = (rw - rw_hi.astype(f32)).astype(bf16)
    strictly_lower = jnp.asarray(np.tri(tm, k=-1), bf16)
    return pl.pallas_call(
        _mix_router_kernel,
        grid=(n // tm,),
        in_specs=[tile(D_ATTN), tile(D_LRU), tile(D_MODEL), const((D_MODEL, D_MODEL)),
                  const((1, D_ATTN)), const((1, D_MODEL)), const((1, D_MODEL)),
                  const((2, D_MODEL, LANES)), const((tm, tm))],
        out_specs=[tile(D_MODEL), tile(LANES), const((1, LANES))],
        out_shape=[jax.ShapeDtypeStruct((n, D_MODEL), f32), jax.ShapeDtypeStruct((n, LANES), f32),
                   jax.ShapeDtypeStruct((1, LANES), f32)],
        compiler_params=_params("arbitrary"),
        name="mix_out_router",
    )(attn, ylru, x2d, w_out_bf16, row(g_attn), row(ln_g), row(ln_b), jnp.stack([rw_hi, rw_lo]),
      strictly_lower)


def _expert_ffn_kernel(tile_expert_ref, n_used_ref, x_ref, wg_ref, wu_ref, wd_ref, out_ref,
                       acc_ref, xb_ref, *, n_chunks):
    del tile_expert_ref
    j = pl.program_id(1)
    used = pl.program_id(0) < n_used_ref[0]
    last = n_chunks - 1
    assert n_chunks >= 2

    def chunk_out(xb):
        return _swiglu(xb, wg_ref[...], wu_ref[...], wd_ref[...])

    @pl.when(used & (j == 0))
    def _():
        xb = _load_row_tiles(x_ref, xb_ref.shape[0]).astype(bf16)
        xb_ref[...] = xb
        acc_ref[...] = chunk_out(xb)

    if n_chunks > 2:
        @pl.when(used & (j > 0) & (j < last))
        def _():
            acc_ref[...] += chunk_out(xb_ref[...])

    @pl.when(used & (j == last))
    def _():
        _store_row_tiles(out_ref, acc_ref[...] + chunk_out(xb_ref[...]))

    @pl.when(jnp.logical_not(used) & (j == last))
    def _():
        _store_row_tiles(out_ref, jnp.zeros(acc_ref.shape, f32))


def _expert_ffn(x_rows, tile_expert, n_used, w_gate, w_up, w_down, *, tm, tf):
    rows = x_rows.shape[0] // ROW_TILE
    n_chunks = w_gate.shape[2] // tf
    io_block = (tm * ROW_TILE, LANES)

    def chunk(i, j, nu):
        return jnp.where(i < nu[0], j, n_chunks - 1)

    return pl.pallas_call(
        functools.partial(_expert_ffn_kernel, n_chunks=n_chunks),
        grid_spec=pltpu.PrefetchScalarGridSpec(
            num_scalar_prefetch=2,
            grid=(rows // tm, n_chunks),
            in_specs=[pl.BlockSpec(io_block, lambda i, j, te, nu: (i, 0)),
                      pl.BlockSpec((None, D_MODEL, tf),
                                   lambda i, j, te, nu: (te[i], 0, chunk(i, j, nu))),
                      pl.BlockSpec((None, D_MODEL, tf),
                                   lambda i, j, te, nu: (te[i], 0, chunk(i, j, nu))),
                      pl.BlockSpec((None, tf, D_MODEL),
                                   lambda i, j, te, nu: (te[i], chunk(i, j, nu), 0))],
            out_specs=pl.BlockSpec(io_block, lambda i, j, te, nu: (i, 0)),
            scratch_shapes=[pltpu.VMEM((tm, D_MODEL), f32), pltpu.VMEM((tm, D_MODEL), bf16)]),
        out_shape=jax.ShapeDtypeStruct(x_rows.shape, f32),
        compiler_params=_params("parallel", "arbitrary"),
        name="ffn_experts",
    )(tile_expert, n_used, x_rows, w_gate, w_up, w_down)


def _expert_ffn_pass_kernel(tile_expert_ref, n_used_ref, x_ref, wg_ref, wu_ref, wd_ref, *rest):
    del tile_expert_ref
    part_ref, out_ref = rest if len(rest) == 2 else (None, rest[0])
    used = pl.program_id(0) < n_used_ref[0]
    tm = x_ref.shape[0] // ROW_TILE

    @pl.when(used)
    def _():
        xb = _load_row_tiles(x_ref, tm).astype(bf16)
        _store_row_tiles(out_ref, _swiglu(xb, wg_ref[...], wu_ref[...], wd_ref[...]))
        if part_ref is not None:
            out_ref[...] += part_ref[...]

    @pl.when(jnp.logical_not(used))
    def _():
        out_ref[...] = jnp.zeros(out_ref.shape, f32)


def _expert_ffn_stationary(x_rows, tile_expert, n_used, w_gate, w_up, w_down, *, tm, tf):
    rows = x_rows.shape[0] // ROW_TILE
    io_spec = pl.BlockSpec((tm * ROW_TILE, LANES), lambda i, te, nu: (i, 0))
    part = None
    for c in range(w_gate.shape[2] // tf):
        operands = (x_rows, w_gate, w_up, w_down) + (() if part is None else (part,))
        part = pl.pallas_call(
            _expert_ffn_pass_kernel,
            grid_spec=pltpu.PrefetchScalarGridSpec(
                num_scalar_prefetch=2,
                grid=(rows // tm,),
                in_specs=[io_spec,
                          pl.BlockSpec((None, D_MODEL, tf), lambda i, te, nu, c=c: (te[i], 0, c)),
                          pl.BlockSpec((None, D_MODEL, tf), lambda i, te, nu, c=c: (te[i], 0, c)),
                          pl.BlockSpec((None, tf, D_MODEL), lambda i, te, nu, c=c: (te[i], c, 0)),
                          ] + [io_spec] * (len(operands) - 4),
                out_specs=io_spec),
            out_shape=jax.ShapeDtypeStruct(x_rows.shape, f32),
            compiler_params=_params("arbitrary"),
            name="ffn_experts_pass",
        )(tile_expert, n_used, *operands)
    return part


def _dispatch_kernel(fill_ref, pos_ref, x_ref, xs_hbm, zero_buf, rows_buf, sem, *, tt, tm):
    step = pl.program_id(0)

    n_fill = fill_ref.shape[0] // 2

    @pl.when(step == 0)
    def _():
        zero_buf[...] = jnp.zeros_like(zero_buf)

        def fill_copy(f):
            start = pl.multiple_of(fill_ref[f] * ROW_TILE, ROW_TILE)
            return pltpu.make_async_copy(zero_buf, xs_hbm.at[pl.ds(start, tm * ROW_TILE)], sem)

        for f in range(n_fill):
            @pl.when(fill_ref[n_fill + f] > 0)
            def _():
                fill_copy(f).start()
        for f in range(n_fill):
            @pl.when(fill_ref[n_fill + f] > 0)
            def _():
                fill_copy(f).wait()

    _store_row_tiles(rows_buf, x_ref[...])

    def issue(t, carry):
        src = pl.multiple_of(t * ROW_TILE, ROW_TILE)
        for k in range(TOP_K):
            dst = pl.multiple_of(pos_ref[0, TOP_K * t + k] * ROW_TILE, ROW_TILE)
            pltpu.make_async_copy(rows_buf.at[pl.ds(src, ROW_TILE)],
                                  xs_hbm.at[pl.ds(dst, ROW_TILE)], sem).start(priority=k)
        return carry

    lax.fori_loop(0, tt, issue, 0, unroll=DMA_UNROLL)
    for _ in range(TOP_K):
        pltpu.make_async_copy(rows_buf, xs_hbm.at[pl.ds(0, tt * ROW_TILE)], sem).wait()


def _dispatch(x1, pos_blocks, fill, rows_sorted, *, tt, tm):
    n = x1.shape[0]
    return pl.pallas_call(
        functools.partial(_dispatch_kernel, tt=tt, tm=tm),
        grid_spec=pltpu.PrefetchScalarGridSpec(
            num_scalar_prefetch=1,
            grid=(n // tt,),
            in_specs=[pl.BlockSpec((None, 1, TOP_K * tt), lambda i, fill: (i, 0, 0),
                                   memory_space=pltpu.SMEM),
                      pl.BlockSpec((tt, D_MODEL), lambda i, fill: (i, 0))],
            out_specs=pl.BlockSpec(memory_space=pl.ANY),
            scratch_shapes=[pltpu.VMEM((tm * ROW_TILE, LANES), f32),
                            pltpu.VMEM((tt * ROW_TILE, LANES), f32),
                            pltpu.SemaphoreType.DMA(())]),
        out_shape=jax.ShapeDtypeStruct((rows_sorted * ROW_TILE, LANES), f32),
        compiler_params=_params("arbitrary"),
        name="moe_dispatch",
    )(fill, pos_blocks, x1)


def _combine_kernel(pos_ref, pos_next_ref, route_ref, x1_ref, ys_hbm, lng, lnb, out_ref, buf, sems,
                    *, tt, n_steps):
    step = pl.program_id(0)
    slot = step % 2

    def issue_tile(tile_pos_ref, into):
        def issue(t, carry):
            dst = pl.multiple_of(t * ROW_TILE, ROW_TILE)
            for k in range(TOP_K):
                src = pl.multiple_of(tile_pos_ref[0, TOP_K * t + k] * ROW_TILE, ROW_TILE)
                pltpu.make_async_copy(ys_hbm.at[pl.ds(src, ROW_TILE)],
                                      buf.at[into, k, pl.ds(dst, ROW_TILE)],
                                      sems.at[into]).start(priority=k)
            return carry

        lax.fori_loop(0, tt, issue, 0, unroll=DMA_UNROLL)

    @pl.when(step == 0)
    def _():
        issue_tile(pos_ref, 0)

    @pl.when(step + 1 < n_steps)
    def _():
        issue_tile(pos_next_ref, 1 - slot)

    for k in range(TOP_K):
        pltpu.make_async_copy(ys_hbm.at[pl.ds(0, tt * ROW_TILE)], buf.at[slot, k],
                              sems.at[slot]).wait()

    lane = lax.broadcasted_iota(jnp.int32, (1, LANES), 1)
    route = route_ref[...]
    p1 = jnp.sum(jnp.where(lane == 4, route, 0.0), axis=-1, keepdims=True)
    p2 = jnp.sum(jnp.where(lane == 5, route, 0.0), axis=-1, keepdims=True)
    y = p1 * _load_row_tiles(buf.at[slot, 0], tt) + p2 * _load_row_tiles(buf.at[slot, 1], tt)
    out_ref[...] = _layer_norm(DEEPNORM_ALPHA * x1_ref[...] + y, lng[...], lnb[...])


def _combine(pos_blocks, route, x1, y_sorted, ln_g, ln_b, *, tt):
    n = x1.shape[0]
    n_steps = n // tt
    pos_spec = lambda index: pl.BlockSpec((None, 1, TOP_K * tt), index, memory_space=pltpu.SMEM)
    return pl.pallas_call(
        functools.partial(_combine_kernel, tt=tt, n_steps=n_steps),
        grid=(n_steps,),
        in_specs=[pos_spec(lambda i: (i, 0, 0)),
                  pos_spec(lambda i: (jnp.minimum(i + 1, n_steps - 1), 0, 0)),
                  pl.BlockSpec((tt, LANES), lambda i: (i, 0)),
                  pl.BlockSpec((tt, D_MODEL), lambda i: (i, 0)),
                  pl.BlockSpec(memory_space=pl.ANY),
                  pl.BlockSpec((1, D_MODEL), lambda i: (0, 0)),
                  pl.BlockSpec((1, D_MODEL), lambda i: (0, 0))],
        out_specs=pl.BlockSpec((tt, D_MODEL), lambda i: (i, 0)),
        out_shape=jax.ShapeDtypeStruct((n, D_MODEL), f32),
        scratch_shapes=[pltpu.VMEM((2, TOP_K, tt * ROW_TILE, LANES), f32),
                        pltpu.SemaphoreType.DMA((2,))],
        compiler_params=_params("arbitrary"),
        name="moe_combine",
    )(pos_blocks, pos_blocks, route, x1, y_sorted, ln_g.reshape(1, D_MODEL).astype(f32),
      ln_b.reshape(1, D_MODEL).astype(f32))


def _moe(x1, route, counts, w_gate, w_up, w_down, ln_g, ln_b, *, tm, tf):
    n = x1.shape[0]
    i32 = jnp.int32
    counts = counts[0, :N_EXPERTS].astype(i32)
    padded = (counts + tm - 1) // tm * tm
    ends = jnp.cumsum(padded)
    offsets = ends - padded
    experts = route[:, 0:TOP_K].astype(i32)
    ranks = route[:, TOP_K:2 * TOP_K].astype(i32)
    pos = offsets[experts] + ranks
    pos_blocks = lambda tt: pos.reshape(n // tt, 1, TOP_K * tt)
    n_tiles = TOP_K * n // tm + N_EXPERTS
    n_used = (ends[-1] // tm).astype(i32).reshape(1)
    tile_ids = jnp.arange(n_tiles, dtype=i32)
    tile_expert = jnp.sum((tile_ids[:, None] >= (ends // tm)[None, :]).astype(i32), axis=1)
    last_expert = jnp.max(jnp.where(counts > 0, jnp.arange(N_EXPERTS, dtype=i32), 0))
    tile_expert = jnp.minimum(tile_expert, last_expert).astype(i32)
    tail_tiles = n_used[0] + jnp.arange(N_EXPERTS, dtype=i32)
    fill = jnp.concatenate([ends - tm, tail_tiles * tm,
                            (counts > 0).astype(i32), (tail_tiles < n_tiles).astype(i32)]).astype(i32)

    x_sorted = _dispatch(x1, pos_blocks(DISPATCH_ROWS), fill, n_tiles * tm, tt=DISPATCH_ROWS,
                         tm=tm)
    y_sorted = _expert_ffn_stationary(x_sorted, tile_expert, n_used, w_gate, w_up, w_down, tm=tm, tf=tf)
    return _combine(pos_blocks(COMBINE_ROWS), route, x1, y_sorted, ln_g, ln_b, tt=COMBINE_ROWS)


def kernel(x, w_in, conv_w, conv_b, w_a, b_a, w_x, b_x, lru_lambda, rel_bias, g_attn, g_lru, w_out, ln1_g, ln1_b, ln2_g, ln2_b, ffn_w_gate, ffn_w_up, ffn_w_down, router_w, moe_w_gate, moe_w_up, moe_w_down):
    batch, seq, _ = x.shape
    n = batch * seq
    h = x.reshape(n, D_MODEL).astype(f32)
    band_bias = jnp.stack([_band_bias(rel_bias, d) for _, d in DILATED_PATTERNS])
    for layer in range(DEPTH):
        qkv, ylru = _proj_lru(h, w_in[layer].astype(bf16), conv_w[layer], conv_b[layer],
                              w_a[layer], b_a[layer], w_x[layer], b_x[layer], lru_lambda[layer],
                              g_lru[layer], batch, seq)
        attn = _attention(qkv, band_bias, batch, seq)
        j = layer // 2
        w_o = w_out[layer].astype(bf16)
        if layer % 2 == 0:
            h = _mix_ffn_dense(attn, ylru, h, w_o, g_attn[layer], (ln1_g[layer], ln1_b[layer]),
                               ffn_w_gate[j].astype(bf16), ffn_w_up[j].astype(bf16),
                               ffn_w_down[j].astype(bf16), (ln2_g[layer], ln2_b[layer]))
        else:
            x1, route, counts = _mix_router(attn, ylru, h, w_o, g_attn[layer], ln1_g[layer],
                                            ln1_b[layer], router_w[j])
            h = _moe(x1, route, counts, moe_w_gate[j].astype(bf16), moe_w_up[j].astype(bf16),
                     moe_w_down[j].astype(bf16), ln2_g[layer], ln2_b[layer],
                     tm=EXPERT_ROWS, tf=EXPERT_FF_CHUNK)
    return h.reshape(batch, seq, D_MODEL).astype(x.dtype)
```
